```python
import math
import jax
import jax.numpy as jnp
from jax import lax
import numpy as np

D_MODEL = 1024
BATCH = 4
SEQ = 4096
DEPTH = 1
DEC_BATCH = 8
DEC_SEQ = 4096
PAST_LEN = 128

F32 = jnp.float32
HEAD_DIM = 64
A_Q_HEADS = 8
A_KV_HEADS = 2
A_GROUP = A_Q_HEADS // A_KV_HEADS
Q_BLOCK = 128
GRID_W = 64
ROPE_AXIS_DIM = HEAD_DIM // 2
ROPE_THETA = 10000.0
DILATED_PAIRS = ((128, 1), (512, 4), (2048, 16))
N_DIL_GROUPS = len(DILATED_PAIRS)
B_HEADS_PER_GROUP = 4
B_HEADS = B_HEADS_PER_GROUP * N_DIL_GROUPS
N_BUCKETS = 32
MAX_DISTANCE = 1024
A_WIDTH = A_Q_HEADS * HEAD_DIM
A_KV_WIDTH = A_KV_HEADS * HEAD_DIM
B_WIDTH = B_HEADS * HEAD_DIM
B_OUT_WIDTH = B_HEADS_PER_GROUP * HEAD_DIM
IN_SPLITS = (A_WIDTH,
             A_WIDTH + A_KV_WIDTH,
             A_WIDTH + 2 * A_KV_WIDTH,
             A_WIDTH + 2 * A_KV_WIDTH + B_WIDTH,
             A_WIDTH + 2 * A_KV_WIDTH + 2 * B_WIDTH,
             A_WIDTH + 2 * A_KV_WIDTH + 3 * B_WIDTH)
IN_COLS = A_WIDTH + 2 * A_KV_WIDTH + 3 * B_WIDTH + 2 * D_MODEL
N_GROUPS = 4
EXPERTS_PER_GROUP = 4
N_EXPERTS = N_GROUPS * EXPERTS_PER_GROUP
D_EXPERT = 512
TOP_K_INNER = 2
ALPHA = (2 * DEPTH) ** 0.25
BETA = (8 * DEPTH) ** -0.25
RMS_EPS = 1e-6
LN_EPS = 1e-5
NEG_INF = -1e30

kernel_name = "hybrid_gated_gqa_dilated_hmoe_encoder"


def layer_norm(x, g, b):
    xf = x.astype(F32)
    mu = jnp.mean(xf, axis=-1, keepdims=True)
    var = jnp.mean(jnp.square(xf - mu), axis=-1, keepdims=True)
    return ((xf - mu) * lax.rsqrt(var + LN_EPS) * g.astype(F32) + b.astype(F32)).astype(x.dtype)


def rms_heads(x, g):
    xf = x.astype(F32)
    return xf * lax.rsqrt(jnp.mean(jnp.square(xf), axis=-1, keepdims=True) + RMS_EPS) * g.astype(F32)


def axial_rope_tables(s):
    rows = s // GRID_W
    row_ids = jnp.repeat(jnp.arange(rows, dtype=F32), GRID_W)
    col_ids = jnp.tile(jnp.arange(GRID_W, dtype=F32), rows)
    inv = ROPE_THETA ** (-jnp.arange(0, ROPE_AXIS_DIM, 2, dtype=F32) / ROPE_AXIS_DIM)
    ang = jnp.concatenate([row_ids[:, None] * inv, col_ids[:, None] * inv], axis=-1)
    return jnp.cos(ang), jnp.sin(ang)


def apply_axial_rope(x, cos, sin):
    half = ROPE_AXIS_DIM // 2
    c = cos[None, :, None, :]
    s = sin[None, :, None, :]

    def rot(u, cc, ss):
        u1, u2 = u[..., :half], u[..., half:]
        return jnp.concatenate([u1 * cc - u2 * ss, u2 * cc + u1 * ss], axis=-1)

    return jnp.concatenate([rot(x[..., :ROPE_AXIS_DIM], c[..., :half], s[..., :half]),
                            rot(x[..., ROPE_AXIS_DIM:], c[..., half:], s[..., half:])], axis=-1)


def t5_bucket(rel):
    half = N_BUCKETS // 2
    exact = half // 2
    n = jnp.abs(rel)
    big = exact + (jnp.log(jnp.maximum(n, 1).astype(F32) / exact)
                   / math.log(MAX_DISTANCE / exact) * (half - exact)).astype(jnp.int32)
    big = jnp.minimum(big, half - 1)
    return jnp.where(rel > 0, half, 0) + jnp.where(n < exact, n, big)


def gqa_axial_attention(q, k, v):
    b, s = q.shape[0], q.shape[1]
    nq = s // Q_BLOCK
    qb = q.reshape(b, nq, Q_BLOCK, A_KV_HEADS, A_GROUP, HEAD_DIM).transpose(1, 0, 2, 3, 4, 5)
    scale = HEAD_DIM ** -0.5

    def one_block(q_blk):
        logits = jnp.einsum('bqkgd,bskd->bkgqs', q_blk, k).astype(F32) * scale
        p = jax.nn.softmax(logits, axis=-1).astype(v.dtype)
        return jnp.einsum('bkgqs,bskd->bqkgd', p, v)

    o = lax.map(one_block, qb)
    return o.transpose(1, 0, 2, 3, 4, 5).reshape(b, s, A_WIDTH)


def dilated_window_attention(q, k, v, bias_tab, window, dilation):
    b, s, h, dh = q.shape
    r = dilation
    n = window // (2 * r)
    L = s // r
    nblk = -(-L // n)
    lp = nblk * n

    def to_sub(a):
        return a.reshape(b, L, r, h, dh).transpose(0, 2, 1, 3, 4).reshape(b * r, L, h, dh)

    qs = jnp.pad(to_sub(q), ((0, 0), (0, lp - L), (0, 0), (0, 0)))
    ks = jnp.pad(to_sub(k), ((0, 0), (n, lp - L + n), (0, 0), (0, 0)))
    vs = jnp.pad(to_sub(v), ((0, 0), (n, lp - L + n), (0, 0), (0, 0)))
    qblk = qs.reshape(b * r, nblk, n, h, dh)

    def band(a):
        ab = a.reshape(b * r, nblk + 2, n, h, dh)
        return jnp.concatenate([ab[:, :-2], ab[:, 1:-1], ab[:, 2:]], axis=2)

    kb, vb = band(ks), band(vs)
    a_idx = jnp.arange(n)
    c_idx = jnp.arange(3 * n)
    rel = c_idx[None, :] - n - a_idx[:, None]
    in_band = jnp.abs(rel) <= n
    key_pos = jnp.arange(nblk)[:, None] * n + c_idx[None, :] - n
    key_ok = (key_pos >= 0) & (key_pos < L)
    mask = in_band[None, :, :] & key_ok[:, None, :]
    bias = bias_tab[t5_bucket(rel * r)].astype(F32).transpose(2, 0, 1)

    logits = jnp.einsum('bnqhd,bnkhd->bhnqk', qblk, kb).astype(F32) * (dh ** -0.5)
    logits = jnp.where(mask[None, None], logits + bias[None, :, None], NEG_INF)
    m = jnp.max(logits, axis=-1, keepdims=True)
    p = jnp.exp(logits - m)
    den = jnp.sum(p, axis=-1)
    o = jnp.einsum('bhnqk,bnkhd->bnqhd', p.astype(vb.dtype), vb).astype(F32)
    o = o / den.transpose(0, 2, 3, 1)[..., None]
    lse = (m[..., 0] + jnp.log(den)).transpose(0, 2, 3, 1)
    o = o.reshape(b, r, lp, h, dh)[:, :, :L].transpose(0, 2, 1, 3, 4).reshape(b, s, h, dh)
    lse = lse.reshape(b, r, lp, h)[:, :, :L].transpose(0, 2, 1, 3).reshape(b, s, h)
    return o, lse


def hierarchical_moe(x, w_rg, b_rg, w_re, b_re, w_ei, w_eo):
    b, s, d = x.shape
    t = x.reshape(b * s, d)
    p_group = jax.nn.softmax((t @ w_rg).astype(F32) + b_rg.astype(F32), axis=-1)
    top_p, top_g = lax.top_k(p_group, 1)
    group_onehot = jax.nn.one_hot(top_g[:, 0], N_GROUPS, dtype=F32)
    group_gate = group_onehot * top_p
    inner_logits = jnp.einsum('td,gde->tge', t, w_re).astype(F32) + b_re.astype(F32)
    sel_logits = jnp.einsum('tge,tg->te', inner_logits, group_onehot)
    top_l, top_e = lax.top_k(sel_logits, TOP_K_INNER)
    top_w = jax.nn.softmax(top_l, axis=-1)
    inner_gate = jnp.einsum('tk,tke->te', top_w, jax.nn.one_hot(top_e, EXPERTS_PER_GROUP, dtype=F32))
    gate = (group_gate[:, :, None] * inner_gate[:, None, :]).reshape(b * s, N_EXPERTS)
    y = jnp.zeros((b * s, d), F32)
    for e in range(N_EXPERTS):
        a, u = jnp.split(t @ w_ei[e], 2, axis=-1)
        y = y + gate[:, e:e + 1] * ((jax.nn.silu(a) * u) @ w_eo[e]).astype(F32)
    return y.astype(x.dtype).reshape(b, s, d)


def encoder_layer(x, w_in, b_gate, q_gain, k_gain, rel_bias, w_branch_a, w_branch_b, w_out,
                  ln1_g, ln1_b, w_route_group, b_route_group, w_route_expert, b_route_expert,
                  w_expert_in, w_expert_out, ln2_g, ln2_b):
    b, s, _ = x.shape
    proj = x @ w_in
    qa, ka, va, qb, kb, vb, gate_pre = jnp.split(proj, list(IN_SPLITS), axis=-1)

    cos, sin = axial_rope_tables(s)
    qa = apply_axial_rope(rms_heads(qa.reshape(b, s, A_Q_HEADS, HEAD_DIM), q_gain), cos, sin).astype(x.dtype)
    ka = apply_axial_rope(rms_heads(ka.reshape(b, s, A_KV_HEADS, HEAD_DIM), k_gain), cos, sin).astype(x.dtype)
    va = va.reshape(b, s, A_KV_HEADS, HEAD_DIM)
    ya = gqa_axial_attention(qa, ka, va) @ w_branch_a

    qb = qb.reshape(b, s, N_DIL_GROUPS, B_HEADS_PER_GROUP, HEAD_DIM)
    kb = kb.reshape(b, s, N_DIL_GROUPS, B_HEADS_PER_GROUP, HEAD_DIM)
    vb = vb.reshape(b, s, N_DIL_GROUPS, B_HEADS_PER_GROUP, HEAD_DIM)
    outs, lses = [], []
    for gi, (window, dil) in enumerate(DILATED_PAIRS):
        o, l = dilated_window_attention(qb[:, :, gi], kb[:, :, gi], vb[:, :, gi],
                                        rel_bias[:, gi * B_HEADS_PER_GROUP:(gi + 1) * B_HEADS_PER_GROUP],
                                        window, dil)
        outs.append(o)
        lses.append(l)
    wts = jax.nn.softmax(jnp.stack(lses, axis=0), axis=0)
    yb = jnp.sum(wts[..., None] * jnp.stack(outs, axis=0), axis=0)
    yb = yb.reshape(b, s, B_OUT_WIDTH).astype(x.dtype) @ w_branch_b

    gates = jax.nn.sigmoid(gate_pre.astype(F32) + b_gate.astype(F32))
    g_a, g_b = jnp.split(gates, 2, axis=-1)
    merged = (g_a * ya.astype(F32) + g_b * yb.astype(F32)).astype(x.dtype)
    x = layer_norm(ALPHA * x + merged @ w_out, ln1_g, ln1_b)

    moe = hierarchical_moe(x, w_route_group, b_route_group, w_route_expert, b_route_expert,
                           w_expert_in, w_expert_out)
    return layer_norm(ALPHA * x + moe, ln2_g, ln2_b)


def setup_inputs(seed: int = 0) -> dict:
    key = jax.random.key(seed)
    ks = jax.random.split(key, 20)

    def nrm(k, shape, scale):
        return jax.random.normal(k, shape, F32) * scale

    return {
        "x_prompt": nrm(ks[0], (BATCH, SEQ, D_MODEL), 1.0),
        "x_sample": nrm(ks[1], (DEC_BATCH, DEC_SEQ, D_MODEL), 1.0),
        "w_in": nrm(ks[2], (DEPTH, D_MODEL, IN_COLS), D_MODEL ** -0.5),
        "b_gate": nrm(ks[3], (DEPTH, 2 * D_MODEL), 0.02),
        "q_gain": 1.0 + nrm(ks[4], (DEPTH, HEAD_DIM), 0.02),
        "k_gain": 1.0 + nrm(ks[5], (DEPTH, HEAD_DIM), 0.02),
        "rel_bias": nrm(ks[6], (N_BUCKETS, B_HEADS), 0.5),
        "w_branch_a": nrm(ks[7], (DEPTH, A_WIDTH, D_MODEL), A_WIDTH ** -0.5),
        "w_branch_b": nrm(ks[8], (DEPTH, B_OUT_WIDTH, D_MODEL), B_OUT_WIDTH ** -0.5),
        "w_out": nrm(ks[9], (DEPTH, D_MODEL, D_MODEL), BETA * D_MODEL ** -0.5),
        "ln1_g": 1.0 + nrm(ks[10], (DEPTH, D_MODEL), 0.02),
        "ln1_b": nrm(ks[11], (DEPTH, D_MODEL), 0.02),
        "w_route_group": nrm(ks[12], (DEPTH, D_MODEL, N_GROUPS), D_MODEL ** -0.5),
        "b_route_group": nrm(ks[13], (DEPTH, N_GROUPS), 0.01),
        "w_route_expert": nrm(ks[14], (DEPTH, N_GROUPS, D_MODEL, EXPERTS_PER_GROUP), D_MODEL ** -0.5),
        "b_route_expert": nrm(ks[15], (DEPTH, N_GROUPS, EXPERTS_PER_GROUP), 0.01),
        "w_expert_in": nrm(ks[16], (DEPTH, N_EXPERTS, D_MODEL, 2 * D_EXPERT), D_MODEL ** -0.5),
        "w_expert_out": nrm(ks[17], (DEPTH, N_EXPERTS, D_EXPERT, D_MODEL), BETA * D_EXPERT ** -0.5),
        "ln2_g": 1.0 + nrm(ks[18], (DEPTH, D_MODEL), 0.02),
        "ln2_b": nrm(ks[19], (DEPTH, D_MODEL), 0.02),
    }


def reference(x_prompt, x_sample, w_in, b_gate, q_gain, k_gain, rel_bias, w_branch_a, w_branch_b,
              w_out, ln1_g, ln1_b, w_route_group, b_route_group, w_route_expert, b_route_expert,
              w_expert_in, w_expert_out, ln2_g, ln2_b):
    def trunk(x):
        for l in range(DEPTH):
            x = encoder_layer(x, w_in[l], b_gate[l], q_gain[l], k_gain[l], rel_bias,
                              w_branch_a[l], w_branch_b[l], w_out[l], ln1_g[l], ln1_b[l],
                              w_route_group[l], b_route_group[l], w_route_expert[l], b_route_expert[l],
                              w_expert_in[l], w_expert_out[l], ln2_g[l], ln2_b[l])
        return x

    y_prompt = trunk(x_prompt)
    y_sample = trunk(x_sample)
    return (y_prompt, y_sample)
```

```python
import functools
import math

import jax
import jax.numpy as jnp
import numpy as np
from jax import lax
from jax.experimental import pallas as pl
from jax.experimental.pallas import tpu as pltpu

F32 = jnp.float32
BF16 = jnp.bfloat16

D_MODEL = 1024
HEAD_DIM = 64
A_Q_HEADS = 8
A_KV_HEADS = 2
A_GROUP = A_Q_HEADS // A_KV_HEADS
GRID_W = 64
ROPE_AXIS_DIM = HEAD_DIM // 2
ROPE_THETA = 10000.0
DILATED_PAIRS = ((128, 1), (512, 4), (2048, 16))
N_DIL_GROUPS = len(DILATED_PAIRS)
B_HEADS_PER_GROUP = 4
N_BUCKETS = 32
MAX_DISTANCE = 1024
A_WIDTH = A_Q_HEADS * HEAD_DIM
A_KV_WIDTH = A_KV_HEADS * HEAD_DIM
B_WIDTH = B_HEADS_PER_GROUP * N_DIL_GROUPS * HEAD_DIM
B_OUT_WIDTH = B_HEADS_PER_GROUP * HEAD_DIM
DT_ROWS = A_WIDTH + 2 * A_KV_WIDTH
N_GROUPS = 4
EXPERTS_PER_GROUP = 4
N_EXPERTS = N_GROUPS * EXPERTS_PER_GROUP
D_EXPERT = 512
PAIRS = ((0, 1), (0, 2), (0, 3), (1, 2), (1, 3), (2, 3))
N_ROUTE_BUCKETS = N_GROUPS * len(PAIRS)
ALPHA = 2.0 ** 0.25
RMS_EPS = 1e-6
LN_EPS = 1e-5
NEG_INF = -1e30

LANES = 128
TM = 512
NQ = 256
HALO = 64
BAND = NQ + 2 * HALO
TMOE = 256
ROUTE_COLS = LANES
VMEM_LIMIT = 56 * 1024 * 1024


def _params(sem):
    return pltpu.CompilerParams(dimension_semantics=sem, vmem_limit_bytes=VMEM_LIMIT)


def _swap16(x):
    return jnp.concatenate([x[16:32], x[0:16], x[48:64], x[32:48]], axis=0)


def _norm_rope(xh, ta, tb):
    r = lax.rsqrt(jnp.mean(xh * xh, axis=0, keepdims=True) + RMS_EPS)
    return (xh * ta + _swap16(xh) * tb) * r


def _inproj_body(x_ref, wdt_ref, wb_ref, bg_ref, taq_ref, tbq_ref, tak_ref, tbk_ref,
                 qt_ref, k_ref, vt_ref, qb_ref, kb_ref, vb_ref, gates_ref):
    xb = x_ref[...].astype(BF16)
    dt = lax.dot_general(wdt_ref[...], xb, (((1,), (1,)), ((), ())),
                         preferred_element_type=F32)
    taq, tbq = taq_ref[...], tbq_ref[...]
    for h in range(A_Q_HEADS):
        sl = slice(h * HEAD_DIM, (h + 1) * HEAD_DIM)
        qt_ref[sl, :] = _norm_rope(dt[sl], taq, tbq).astype(BF16)
    tak, tbk = tak_ref[...], tbk_ref[...]
    kt = jnp.concatenate(
        [_norm_rope(dt[A_WIDTH + h * HEAD_DIM:A_WIDTH + (h + 1) * HEAD_DIM], tak, tbk)
         for h in range(A_KV_HEADS)], axis=0)
    k_ref[...] = kt.T.astype(BF16)
    vt_ref[...] = dt[A_WIDTH + A_KV_WIDTH:DT_ROWS].astype(BF16)
    for i, ref in enumerate((qb_ref, kb_ref, vb_ref)):
        ref[...] = jnp.dot(xb, wb_ref[:, i * B_WIDTH:(i + 1) * B_WIDTH],
                           preferred_element_type=F32).astype(BF16)
    g0 = 3 * B_WIDTH
    for i in range(2):
        cols = slice(g0 + i * D_MODEL, g0 + (i + 1) * D_MODEL)
        pre = jnp.dot(xb, wb_ref[:, cols], preferred_element_type=F32)
        gates_ref[:, i * D_MODEL:(i + 1) * D_MODEL] = jax.nn.sigmoid(
            pre + bg_ref[:, i * D_MODEL:(i + 1) * D_MODEL]).astype(BF16)


def _inproj_call(x, wdt, wb, bg, taq, tbq, tak, tbk):
    b, s, d = x.shape
    nt = s // TM
    wcols = wb.shape[1]
    tab = pl.BlockSpec((HEAD_DIM, TM), lambda p, i: (0, p))
    const = lambda shape: pl.BlockSpec(shape, lambda p, i: (0,) * len(shape))
    tok = lambda c: pl.BlockSpec((None, TM, c), lambda p, i: (i, p, 0))
    return pl.pallas_call(
        _inproj_body,
        grid=(nt, b),
        in_specs=[tok(d), const((DT_ROWS, d)), const((d, wcols)), const((1, 2 * D_MODEL)),
                  tab, tab, tab, tab],
        out_specs=[
            pl.BlockSpec((None, A_WIDTH, TM), lambda p, i: (i, 0, p)),
            tok(A_KV_WIDTH),
            pl.BlockSpec((None, None, A_KV_WIDTH, TM), lambda p, i: (i, p, 0, 0)),
            tok(B_WIDTH), tok(B_WIDTH), tok(B_WIDTH), tok(2 * D_MODEL),
        ],
        out_shape=[
            jax.ShapeDtypeStruct((b, A_WIDTH, s), BF16),
            jax.ShapeDtypeStruct((b, s, A_KV_WIDTH), BF16),
            jax.ShapeDtypeStruct((b, nt, A_KV_WIDTH, TM), BF16),
            jax.ShapeDtypeStruct((b, s, B_WIDTH), BF16),
            jax.ShapeDtypeStruct((b, s, B_WIDTH), BF16),
            jax.ShapeDtypeStruct((b, s, B_WIDTH), BF16),
            jax.ShapeDtypeStruct((b, s, 2 * D_MODEL), BF16),
        ],
        compiler_params=_params(("arbitrary", "arbitrary")),
        name="inproj",
    )(x, wdt, wb, bg, taq, tbq, tak, tbk)


def _rope_tables(s, gain, scale):
    t = np.arange(s)
    inv = ROPE_THETA ** (-np.arange(0, ROPE_AXIS_DIM, 2, dtype=np.float64) / ROPE_AXIS_DIM)
    pos = np.stack([t // GRID_W, t // GRID_W, t % GRID_W, t % GRID_W], axis=0)
    ang = (pos[:, None, :] * inv[None, :, None]).reshape(HEAD_DIM, s)
    sign = np.repeat(np.array([-1.0, 1.0, -1.0, 1.0]), 16)[:, None]
    partner = np.arange(HEAD_DIM) ^ 16
    cos = jnp.asarray(np.cos(ang), F32)
    sin = jnp.asarray(sign * np.sin(ang), F32)
    g = gain.astype(F32) * scale
    return g[:, None] * cos, g[partner][:, None] * sin


def _attn_a_body(qt_ref, k_ref, vt_ref, o_ref, ot_sc):
    kv = pl.program_id(1)
    nchunks = vt_ref.shape[0]
    tk = vt_ref.shape[2]
    row = lax.broadcasted_iota(jnp.int32, (A_KV_WIDTH, NQ), 0)
    own = (row // HEAD_DIM) == kv
    for g in range(A_GROUP):
        qg = qt_ref[g * HEAD_DIM:(g + 1) * HEAD_DIM, :]
        rhs = jnp.where(own, jnp.concatenate([qg, qg], axis=0), jnp.zeros((), BF16))

        def chunk(c, carry, rhs=rhs):
            m, l, acc = carry
            kc = k_ref[pl.ds(pl.multiple_of(c * tk, tk), tk), :]
            st = jnp.dot(kc, rhs, preferred_element_type=F32)
            m_new = jnp.maximum(m, jnp.max(st, axis=0, keepdims=True))
            alpha = jnp.exp(m - m_new)
            p = jnp.exp(st - m_new)
            l = alpha * l + jnp.sum(p, axis=0, keepdims=True)
            acc = alpha * acc + jnp.dot(vt_ref[c], p.astype(BF16),
                                        preferred_element_type=F32)
            return m_new, l, acc

        init = (jnp.full((1, NQ), NEG_INF, F32), jnp.zeros((1, NQ), F32),
                jnp.zeros((HEAD_DIM, NQ), F32))
        _, l, acc = lax.fori_loop(0, nchunks, chunk, init)
        ot_sc[g * HEAD_DIM:(g + 1) * HEAD_DIM, :] = acc / l
    o_ref[...] = ot_sc[...].T.astype(BF16)


def _attn_a_call(qt, k, vt):
    b, _, s = qt.shape
    nchunks, tk = vt.shape[1], vt.shape[3]
    gw = A_GROUP * HEAD_DIM
    return pl.pallas_call(
        _attn_a_body,
        grid=(b, A_KV_HEADS, s // NQ),
        in_specs=[
            pl.BlockSpec((None, gw, NQ), lambda i, h, j: (i, h, j)),
            pl.BlockSpec((None, s, A_KV_WIDTH), lambda i, h, j: (i, 0, 0)),
            pl.BlockSpec((None, nchunks, HEAD_DIM, tk), lambda i, h, j: (i, 0, h, 0)),
        ],
        out_specs=pl.BlockSpec((None, NQ, gw), lambda i, h, j: (i, j, h)),
        out_shape=jax.ShapeDtypeStruct((b, s, A_WIDTH), BF16),
        scratch_shapes=[pltpu.VMEM((gw, NQ), F32)],
        compiler_params=_params(("arbitrary", "arbitrary", "arbitrary")),
        name="attn_a",
    )(qt, k, vt)


def _attn_b_body(q_ref, k_ref, v_ref, bm_ref, o_ref, lse_ref, kp_sc, vp_sc, ot_sc, lt_sc):
    length = q_ref.shape[0]
    zeros = jnp.zeros((HALO, B_OUT_WIDTH), BF16)
    for sc, ref in ((kp_sc, k_ref), (vp_sc, v_ref)):
        sc[0:HALO, :] = zeros
        sc[HALO:HALO + length, :] = ref[...]
        sc[HALO + length:, :] = zeros
    row_head = lax.broadcasted_iota(jnp.int32, (B_OUT_WIDTH, NQ), 0) // HEAD_DIM
    band_row = lax.broadcasted_iota(jnp.int32, (BAND, NQ), 0)

    def block(i, carry):
        q0 = pl.multiple_of(i * NQ, NQ)
        qt = q_ref[pl.ds(q0, NQ), :].astype(F32).T.astype(BF16)
        kband = kp_sc[pl.ds(q0, BAND), :]
        vbt = vp_sc[pl.ds(q0, BAND), :].astype(F32).T.astype(BF16)
        key_pos = band_row + (q0 - HALO)
        pen = jnp.where((key_pos >= 0) & (key_pos < length), 0.0, NEG_INF)
        for h in range(B_HEADS_PER_GROUP):
            rhs = jnp.where(row_head == h, qt, jnp.zeros((), BF16))
            st = jnp.dot(kband, rhs, preferred_element_type=F32)
            st = st + bm_ref[h] + pen
            m = jnp.max(st, axis=0, keepdims=True)
            p = jnp.exp(st - m)
            l = jnp.sum(p, axis=0, keepdims=True)
            sl = slice(h * HEAD_DIM, (h + 1) * HEAD_DIM)
            ot = jnp.dot(vbt[sl], p.astype(BF16), preferred_element_type=F32)
            ot_sc[sl, :] = ot / l
            lt_sc[sl, :] = jnp.broadcast_to(m + jnp.log(l), (HEAD_DIM, NQ))
        o_ref[pl.ds(q0, NQ), :] = ot_sc[...].T
        lse_ref[pl.ds(q0, NQ), :] = lt_sc[...].T
        return carry

    lax.fori_loop(0, length // NQ, block, 0)


def _attn_b_call(qb, kb, vb, bm, gi, r):
    b, s, _ = qb.shape
    length = s // r
    nblk_in = B_WIDTH // B_OUT_WIDTH
    view_in = lambda a: a.reshape(b, length, r * B_WIDTH)
    in_spec = pl.BlockSpec((None, length, B_OUT_WIDTH), lambda i, j: (i, 0, nblk_in * j + gi))
    out_spec = pl.BlockSpec((None, length, B_OUT_WIDTH), lambda i, j: (i, 0, j))
    out_sds = jax.ShapeDtypeStruct((b, length, r * B_OUT_WIDTH), F32)
    o, lse = pl.pallas_call(
        _attn_b_body,
        grid=(b, r),
        in_specs=[in_spec, in_spec, in_spec,
                  pl.BlockSpec((B_HEADS_PER_GROUP, BAND, NQ), lambda i, j: (0, 0, 0))],
        out_specs=[out_spec, out_spec],
        out_shape=[out_sds, out_sds],
        scratch_shapes=[pltpu.VMEM((length + 2 * HALO, B_OUT_WIDTH), BF16),
                        pltpu.VMEM((length + 2 * HALO, B_OUT_WIDTH), BF16),
                        pltpu.VMEM((B_OUT_WIDTH, NQ), F32),
                        pltpu.VMEM((B_OUT_WIDTH, NQ), F32)],
        compiler_params=_params(("arbitrary", "arbitrary")),
        name=f"attn_b_r{r}",
    )(view_in(qb), view_in(kb), view_in(vb), bm)
    return o.reshape(b, s, B_OUT_WIDTH), lse.reshape(b, s, B_OUT_WIDTH)


def _t5_bucket_np(rel):
    half = N_BUCKETS // 2
    exact = half // 2
    n = np.abs(rel)
    big = exact + (np.log(np.maximum(n, 1).astype(np.float64) / exact)
                   / math.log(MAX_DISTANCE / exact) * (half - exact)).astype(np.int64)
    big = np.minimum(big, half - 1)
    return np.where(rel > 0, half, 0) + np.where(n < exact, n, big)


def _bias_mask_table(rel_bias, gi, r):
    rel = np.arange(BAND)[:, None] - HALO - np.arange(NQ)[None, :]
    bucket = jnp.asarray(_t5_bucket_np(rel * r), jnp.int32)
    tab = rel_bias[:, gi * B_HEADS_PER_GROUP:(gi + 1) * B_HEADS_PER_GROUP].astype(F32)
    bias = jnp.transpose(tab[bucket], (2, 0, 1))
    return jnp.where(jnp.asarray(np.abs(rel) <= HALO)[None], bias, NEG_INF)


def _layer_norm(h, g, b):
    mu = jnp.mean(h, axis=-1, keepdims=True)
    c = h - mu
    var = jnp.mean(c * c, axis=-1, keepdims=True)
    return c * lax.rsqrt(var + LN_EPS) * g + b


def _split_bf16(a):
    hi = a.astype(BF16)
    return hi, (a - hi.astype(F32)).astype(BF16)


def _merge_body(x_ref, ya_ref, o1_ref, o2_ref, o3_ref, l1_ref, l2_ref, l3_ref, gates_ref,
                wa_ref, wbr_ref, wo_ref, g_ref, b_ref, wrh_ref, wrl_ref, br_ref, out_ref):
    l1, l2, l3 = l1_ref[...], l2_ref[...], l3_ref[...]
    mx = jnp.maximum(jnp.maximum(l1, l2), l3)
    e1, e2, e3 = jnp.exp(l1 - mx), jnp.exp(l2 - mx), jnp.exp(l3 - mx)
    yb = (e1 * o1_ref[...] + e2 * o2_ref[...] + e3 * o3_ref[...]) / (e1 + e2 + e3)
    ya_p = jnp.dot(ya_ref[...], wa_ref[...], preferred_element_type=F32)
    yb_p = jnp.dot(yb.astype(BF16), wbr_ref[...], preferred_element_type=F32)
    merged = (gates_ref[:, 0:D_MODEL].astype(F32) * ya_p
              + gates_ref[:, D_MODEL:2 * D_MODEL].astype(F32) * yb_p)
    h = ALPHA * x_ref[...] + jnp.dot(merged.astype(BF16), wo_ref[...], preferred_element_type=F32)
    x1 = _layer_norm(h, g_ref[...], b_ref[...])
    out_ref[:, 0:D_MODEL] = x1

    xh, xl = _split_bf16(x1)
    logits = (jnp.dot(xh, wrh_ref[...], preferred_element_type=F32)
              + jnp.dot(xl, wrh_ref[...], preferred_element_type=F32)
              + jnp.dot(xh, wrl_ref[...], preferred_element_type=F32)) + br_ref[...]
    col = lambda i: logits[:, i:i + 1]
    lg = [col(i) for i in range(N_GROUPS)]
    gmax = functools.reduce(jnp.maximum, lg)
    top_p = 1.0 / functools.reduce(lambda a, c: a + c, [jnp.exp(v - gmax) for v in lg])
    gsel = jnp.full_like(gmax, N_GROUPS - 1).astype(jnp.int32)
    for i in reversed(range(N_GROUPS - 1)):
        gsel = jnp.where(lg[i] >= gmax, i, gsel)
    sel = []
    for e in range(EXPERTS_PER_GROUP):
        v = col(N_GROUPS + (N_GROUPS - 1) * EXPERTS_PER_GROUP + e)
        for gi in reversed(range(N_GROUPS - 1)):
            v = jnp.where(gsel == gi, col(N_GROUPS + gi * EXPERTS_PER_GROUP + e), v)
        sel.append(v)
    rank = []
    for e in range(EXPERTS_PER_GROUP):
        rk = jnp.zeros_like(gsel)
        for j in range(EXPERTS_PER_GROUP):
            if j != e:
                ahead = (sel[j] > sel[e]) | ((sel[j] == sel[e]) & (j < e))
                rk = rk + ahead.astype(jnp.int32)
        rank.append(rk)
    pick = lambda k: functools.reduce(
        lambda a, c: a + c, [jnp.where(rank[e] == k, sel[e], 0.0) for e in range(EXPERTS_PER_GROUP)])
    t = jnp.exp(pick(1) - pick(0))
    w_first = top_p / (1.0 + t)
    w_second = top_p * t / (1.0 + t)
    chosen = [rank[e] < 2 for e in range(EXPERTS_PER_GROUP)]
    weight = [jnp.where(rank[e] == 0, w_first, w_second) for e in range(EXPERTS_PER_GROUP)]
    pair = jnp.zeros_like(gsel)
    w_lo = jnp.zeros_like(top_p)
    w_hi = jnp.zeros_like(top_p)
    for pi, (a, c) in enumerate(PAIRS):
        hit = chosen[a] & chosen[c]
        pair = jnp.where(hit, pi, pair)
        w_lo = jnp.where(hit, weight[a], w_lo)
        w_hi = jnp.where(hit, weight[c], w_hi)
    bucket = (gsel * len(PAIRS) + pair).astype(F32)
    lane = lax.broadcasted_iota(jnp.int32, (x1.shape[0], ROUTE_COLS), 1)
    out_ref[:, D_MODEL:] = jnp.where(lane == 0, bucket,
                                     jnp.where(lane == 1, w_lo, jnp.where(lane == 2, w_hi, 0.0)))


def _merge_call(x, ya, os, ls, gates, wa, wbr, wo, g, b, wrh, wrl, br):
    t, d = x.shape
    tok = lambda c: pl.BlockSpec((TM, c), lambda i: (i, 0))
    const = lambda a: pl.BlockSpec(a.shape, lambda i: (0,) * a.ndim)
    consts = (wa, wbr, wo, g, b, wrh, wrl, br)
    return pl.pallas_call(
        _merge_body,
        grid=(t // TM,),
        in_specs=[tok(d), tok(A_WIDTH)] + [tok(B_OUT_WIDTH)] * 6 + [tok(2 * D_MODEL)]
        + [const(a) for a in consts],
        out_specs=tok(d + ROUTE_COLS),
        out_shape=jax.ShapeDtypeStruct((t, d + ROUTE_COLS), F32),
        compiler_params=_params(("arbitrary",)),
        name="merge_ln1_route",
    )(x, ya, *os, *ls, gates, *consts)


def _moe_body(nvalid_ref, elo_ref, ehi_ref, ids_ref, x_hbm, wil_ref, wih_ref, wol_ref, woh_ref,
              g_ref, b_ref, out_hbm, xbuf, obuf, sem_in, sem_out):
    del elo_ref, ehi_ref
    n = nvalid_ref[pl.program_id(0)]

    def row_in(k):
        return pltpu.make_async_copy(x_hbm.at[pl.ds(ids_ref[0, k], 1)], xbuf.at[pl.ds(k, 1)], sem_in)

    def row_out(k):
        return pltpu.make_async_copy(obuf.at[pl.ds(k, 1)], out_hbm.at[pl.ds(ids_ref[0, k], 1)], sem_out)

    def for_rows(fn):
        lax.fori_loop(0, n, lambda k, c: (fn(k), c)[1], 0)

    @pl.when(pl.program_id(0) == 0)
    def _():
        xbuf[...] = jnp.zeros_like(xbuf)

    @pl.when(n > 0)
    def _():
        for_rows(lambda k: row_in(k).start())
        for_rows(lambda k: row_in(k).wait())
        xt = xbuf[:, 0:D_MODEL]
        xb = xt.astype(BF16)

        def expert(wi_ref, wo_ref):
            hcat = jnp.dot(xb, wi_ref[...], preferred_element_type=F32)
            a, u = hcat[:, 0:D_EXPERT], hcat[:, D_EXPERT:]
            act = (a * jax.nn.sigmoid(a) * u).astype(BF16)
            return jnp.dot(act, wo_ref[...], preferred_element_type=F32)

        moe = (xbuf[:, D_MODEL + 1:D_MODEL + 2] * expert(wil_ref, wol_ref)
               + xbuf[:, D_MODEL + 2:D_MODEL + 3] * expert(wih_ref, woh_ref))
        obuf[...] = _layer_norm(ALPHA * xt + moe, g_ref[...], b_ref[...])
        for_rows(lambda k: row_out(k).start())
        for_rows(lambda k: row_out(k).wait())


def _moe_call(x1ext, nvalid, elo, ehi, ids, wei, weo, g, b):
    t = x1ext.shape[0]
    ntiles = nvalid.shape[0]
    wi_spec = lambda e: pl.BlockSpec((None, D_MODEL, 2 * D_EXPERT), lambda i, nv, lo, hi: (e(lo, hi)[i], 0, 0))
    wo_spec = lambda e: pl.BlockSpec((None, D_EXPERT, D_MODEL), lambda i, nv, lo, hi: (e(lo, hi)[i], 0, 0))
    first = lambda lo, hi: lo
    second = lambda lo, hi: hi
    vec = pl.BlockSpec((1, D_MODEL), lambda i, nv, lo, hi: (0, 0))
    grid_spec = pltpu.PrefetchScalarGridSpec(
        num_scalar_prefetch=3,
        grid=(ntiles,),
        in_specs=[
            pl.BlockSpec((None, 1, TMOE), lambda i, nv, lo, hi: (i, 0, 0), memory_space=pltpu.SMEM),
            pl.BlockSpec(memory_space=pl.ANY),
            wi_spec(first), wi_spec(second), wo_spec(first), wo_spec(second), vec, vec,
        ],
        out_specs=pl.BlockSpec(memory_space=pl.ANY),
        scratch_shapes=[pltpu.VMEM((TMOE, D_MODEL + ROUTE_COLS), F32),
                        pltpu.VMEM((TMOE, D_MODEL), F32),
                        pltpu.SemaphoreType.DMA(()),
                        pltpu.SemaphoreType.DMA(())],
    )
    return pl.pallas_call(
        _moe_body,
        grid_spec=grid_spec,
        out_shape=jax.ShapeDtypeStruct((t, D_MODEL), F32),
        compiler_params=_params(("arbitrary",)),
        name="moe_ln2",
    )(nvalid, elo, ehi, ids, x1ext, wei, wei, weo, weo, g, b)


def _route_tiles(bucket):
    t = bucket.shape[0]
    ntiles = t // TMOE + N_ROUTE_BUCKETS
    order = jnp.argsort(bucket, stable=True).astype(jnp.int32)
    counts = jnp.sum(bucket[:, None] == jnp.arange(N_ROUTE_BUCKETS, dtype=jnp.int32)[None, :],
                     axis=0, dtype=jnp.int32)
    row_start = jnp.cumsum(counts) - counts
    tiles_per = (counts + TMOE - 1) // TMOE
    tile_end = jnp.cumsum(tiles_per)
    total = tile_end[-1]
    tile = jnp.arange(ntiles, dtype=jnp.int32)
    used = tile < total
    bkt = jnp.searchsorted(tile_end, jnp.minimum(tile, total - 1), side="right").astype(jnp.int32)
    local = tile - (tile_end - tiles_per)[bkt]
    row0 = row_start[bkt] + local * TMOE
    nvalid = jnp.where(used, jnp.clip(counts[bkt] - local * TMOE, 0, TMOE), 0).astype(jnp.int32)
    pairs = jnp.asarray(PAIRS, jnp.int32)
    grp = bkt // len(PAIRS)
    elo = grp * EXPERTS_PER_GROUP + pairs[bkt % len(PAIRS), 0]
    ehi = grp * EXPERTS_PER_GROUP + pairs[bkt % len(PAIRS), 1]
    idx = jnp.minimum(row0[:, None] + jnp.arange(TMOE, dtype=jnp.int32)[None, :], t - 1)
    ids = order[idx].reshape(ntiles, 1, TMOE)
    return nvalid, elo.astype(jnp.int32), ehi.astype(jnp.int32), ids


def _prepare(w_in, b_gate, q_gain, k_gain, rel_bias, w_branch_a, w_branch_b, w_out, ln1_g, ln1_b,
             w_route_group, b_route_group, w_route_expert, b_route_expert, w_expert_in,
             w_expert_out, ln2_g, ln2_b, seq):
    row = lambda v: v.reshape(1, -1).astype(F32)
    w = w_in[0]
    wr = jnp.concatenate([w_route_group[0]]
                         + [w_route_expert[0, gi] for gi in range(N_GROUPS)], axis=1)
    wr = jnp.pad(wr.astype(F32), ((0, 0), (0, ROUTE_COLS - wr.shape[1])))
    br = jnp.concatenate([b_route_group[0], b_route_expert[0].reshape(-1)])
    br = jnp.pad(br.astype(F32), (0, ROUTE_COLS - br.shape[0])).reshape(1, ROUTE_COLS)
    wrh, wrl = _split_bf16(wr)
    scale = HEAD_DIM ** -0.5
    wb = w[:, DT_ROWS:]
    wb = jnp.concatenate([wb[:, 0:B_WIDTH] * scale, wb[:, B_WIDTH:]], axis=1)
    return dict(
        wdt=w[:, 0:DT_ROWS].T.astype(BF16), wb=wb.astype(BF16), bg=row(b_gate[0]),
        rope_q=_rope_tables(seq, q_gain[0], scale), rope_k=_rope_tables(seq, k_gain[0], 1.0),
        bm=[_bias_mask_table(rel_bias, gi, r) for gi, (_, r) in enumerate(DILATED_PAIRS)],
        wa=w_branch_a[0].astype(BF16), wbr=w_branch_b[0].astype(BF16), wo=w_out[0].astype(BF16),
        ln1=(row(ln1_g[0]), row(ln1_b[0])), wrh=wrh, wrl=wrl, br=br,
        wei=w_expert_in[0].astype(BF16), weo=w_expert_out[0].astype(BF16),
        ln2=(row(ln2_g[0]), row(ln2_b[0])),
    )


def _trunk(x, p):
    b, s, d = x.shape
    qt, k, vt, qb, kb, vb, gates = _inproj_call(x, p["wdt"], p["wb"], p["bg"], *p["rope_q"], *p["rope_k"])
    ya = _attn_a_call(qt, k, vt)
    os, ls = [], []
    for gi, (_, r) in enumerate(DILATED_PAIRS):
        o, lse = _attn_b_call(qb, kb, vb, p["bm"][gi], gi, r)
        os.append(o.reshape(b * s, B_OUT_WIDTH))
        ls.append(lse.reshape(b * s, B_OUT_WIDTH))
    x1ext = _merge_call(x.reshape(b * s, d), ya.reshape(b * s, A_WIDTH), os, ls,
                        gates.reshape(b * s, 2 * D_MODEL), p["wa"], p["wbr"], p["wo"], *p["ln1"],
                        p["wrh"], p["wrl"], p["br"])
    nvalid, elo, ehi, ids = _route_tiles(x1ext[:, D_MODEL].astype(jnp.int32))
    y = _moe_call(x1ext, nvalid, elo, ehi, ids, p["wei"], p["weo"], *p["ln2"])
    return y.reshape(b, s, d)


def kernel(x_prompt, x_sample, w_in, b_gate, q_gain, k_gain, rel_bias, w_branch_a, w_branch_b, w_out,
           ln1_g, ln1_b, w_route_group, b_route_group, w_route_expert, b_route_expert, w_expert_in,
           w_expert_out, ln2_g, ln2_b):
    assert x_prompt.shape[1] == x_sample.shape[1]
    p = _prepare(w_in, b_gate, q_gain, k_gain, rel_bias, w_branch_a, w_branch_b, w_out, ln1_g, ln1_b,
                 w_route_group, b_route_group, w_route_expert, b_route_expert, w_expert_in,
                 w_expert_out, ln2_g, ln2_b, x_prompt.shape[1])
    return (_trunk(x_prompt, p), _trunk(x_sample, p))
```

```python
import functools
import math

import jax
import jax.numpy as jnp
import numpy as np
from jax import lax
from jax.experimental import pallas as pl
from jax.experimental.pallas import tpu as pltpu

F32 = jnp.float32
BF16 = jnp.bfloat16

D_MODEL = 1024
HEAD_DIM = 64
A_Q_HEADS = 8
A_KV_HEADS = 2
A_GROUP = A_Q_HEADS // A_KV_HEADS
GRID_W = 64
ROPE_AXIS_DIM = HEAD_DIM // 2
ROPE_THETA = 10000.0
DILATED_PAIRS = ((128, 1), (512, 4), (2048, 16))
N_DIL_GROUPS = len(DILATED_PAIRS)
B_HEADS_PER_GROUP = 4
N_BUCKETS = 32
MAX_DISTANCE = 1024
A_WIDTH = A_Q_HEADS * HEAD_DIM
A_KV_WIDTH = A_KV_HEADS * HEAD_DIM
B_WIDTH = B_HEADS_PER_GROUP * N_DIL_GROUPS * HEAD_DIM
B_OUT_WIDTH = B_HEADS_PER_GROUP * HEAD_DIM
DT_ROWS = A_WIDTH + 2 * A_KV_WIDTH
N_GROUPS = 4
EXPERTS_PER_GROUP = 4
N_EXPERTS = N_GROUPS * EXPERTS_PER_GROUP
D_EXPERT = 512
PAIRS = ((0, 1), (0, 2), (0, 3), (1, 2), (1, 3), (2, 3))
N_ROUTE_BUCKETS = N_GROUPS * len(PAIRS)
ALPHA = 2.0 ** 0.25
RMS_EPS = 1e-6
LN_EPS = 1e-5
NEG_INF = -1e30

LANES = 128
TM = 512
NQ = 256
BF16_SUBLANES = 16
VT_ROWS = HEAD_DIM + BF16_SUBLANES
LOG2E = 1.4426950408889634
MAX_SAFE_SHIFT = 40.0
HALO = 64
BAND = NQ + 2 * HALO
TMOE = 256
ROUTE_COLS = LANES
VMEM_LIMIT = 56 * 1024 * 1024


def _params(sem):
    return pltpu.CompilerParams(dimension_semantics=sem, vmem_limit_bytes=VMEM_LIMIT)


def _swap16(x):
    return jnp.concatenate([x[16:32], x[0:16], x[48:64], x[32:48]], axis=0)


def _norm_rope(xh, ta, tb):
    r = lax.rsqrt(jnp.mean(xh * xh, axis=0, keepdims=True) + RMS_EPS)
    return (xh * ta + _swap16(xh) * tb) * r


def _inproj_body(x_ref, wdt_ref, wb_ref, bg_ref, taq_ref, tbq_ref, tak_ref, tbk_ref,
                 qt_ref, k_ref, vt_ref, qb_ref, kb_ref, vb_ref, gates_ref):
    xb = x_ref[...].astype(BF16)
    dt = lax.dot_general(wdt_ref[...], xb, (((1,), (1,)), ((), ())),
                         preferred_element_type=F32)
    taq, tbq = taq_ref[...], tbq_ref[...]
    for h in range(A_Q_HEADS):
        sl = slice(h * HEAD_DIM, (h + 1) * HEAD_DIM)
        qt_ref[sl, :] = _norm_rope(dt[sl], taq, tbq).astype(BF16)
    tak, tbk = tak_ref[...], tbk_ref[...]
    kt = jnp.concatenate(
        [_norm_rope(dt[A_WIDTH + h * HEAD_DIM:A_WIDTH + (h + 1) * HEAD_DIM], tak, tbk)
         for h in range(A_KV_HEADS)], axis=0)
    k_ref[...] = kt.T.astype(BF16)
    for h in range(A_KV_HEADS):
        v0 = A_WIDTH + A_KV_WIDTH + h * HEAD_DIM
        vt_ref[h, 0:HEAD_DIM, :] = dt[v0:v0 + HEAD_DIM].astype(BF16)
        vt_ref[h, HEAD_DIM:, :] = jnp.ones((VT_ROWS - HEAD_DIM, TM), BF16)
    for i, ref in enumerate((qb_ref, kb_ref, vb_ref)):
        ref[...] = jnp.dot(xb, wb_ref[:, i * B_WIDTH:(i + 1) * B_WIDTH],
                           preferred_element_type=F32).astype(BF16)
    g0 = 3 * B_WIDTH
    for i in range(2):
        cols = slice(g0 + i * D_MODEL, g0 + (i + 1) * D_MODEL)
        pre = jnp.dot(xb, wb_ref[:, cols], preferred_element_type=F32)
        gates_ref[:, i * D_MODEL:(i + 1) * D_MODEL] = jax.nn.sigmoid(
            pre + bg_ref[:, i * D_MODEL:(i + 1) * D_MODEL]).astype(BF16)


def _inproj_call(x, wdt, wb, bg, taq, tbq, tak, tbk):
    b, s, d = x.shape
    nt = s // TM
    wcols = wb.shape[1]
    tab = pl.BlockSpec((HEAD_DIM, TM), lambda p, i: (0, p))
    const = lambda shape: pl.BlockSpec(shape, lambda p, i: (0,) * len(shape))
    tok = lambda c: pl.BlockSpec((None, TM, c), lambda p, i: (i, p, 0))
    return pl.pallas_call(
        _inproj_body,
        grid=(nt, b),
        in_specs=[tok(d), const((DT_ROWS, d)), const((d, wcols)), const((1, 2 * D_MODEL)),
                  tab, tab, tab, tab],
        out_specs=[
            pl.BlockSpec((None, A_WIDTH, TM), lambda p, i: (i, 0, p)),
            tok(A_KV_WIDTH),
            pl.BlockSpec((None, None, A_KV_HEADS, VT_ROWS, TM), lambda p, i: (i, p, 0, 0, 0)),
            tok(B_WIDTH), tok(B_WIDTH), tok(B_WIDTH), tok(2 * D_MODEL),
        ],
        out_shape=[
            jax.ShapeDtypeStruct((b, A_WIDTH, s), BF16),
            jax.ShapeDtypeStruct((b, s, A_KV_WIDTH), BF16),
            jax.ShapeDtypeStruct((b, nt, A_KV_HEADS, VT_ROWS, TM), BF16),
            jax.ShapeDtypeStruct((b, s, B_WIDTH), BF16),
            jax.ShapeDtypeStruct((b, s, B_WIDTH), BF16),
            jax.ShapeDtypeStruct((b, s, B_WIDTH), BF16),
            jax.ShapeDtypeStruct((b, s, 2 * D_MODEL), BF16),
        ],
        compiler_params=_params(("arbitrary", "arbitrary")),
        name="inproj",
    )(x, wdt, wb, bg, taq, tbq, tak, tbk)


def _rope_tables(s, gain, scale):
    t = np.arange(s)
    inv = ROPE_THETA ** (-np.arange(0, ROPE_AXIS_DIM, 2, dtype=np.float64) / ROPE_AXIS_DIM)
    pos = np.stack([t // GRID_W, t // GRID_W, t % GRID_W, t % GRID_W], axis=0)
    ang = (pos[:, None, :] * inv[None, :, None]).reshape(HEAD_DIM, s)
    sign = np.repeat(np.array([-1.0, 1.0, -1.0, 1.0]), 16)[:, None]
    partner = np.arange(HEAD_DIM) ^ 16
    cos = jnp.asarray(np.cos(ang), F32)
    sin = jnp.asarray(sign * np.sin(ang), F32)
    g = gain.astype(F32) * scale
    return g[:, None] * cos, g[partner][:, None] * sin


def _attn_a_body(fast_ref, kbound_ref, qt_ref, k_ref, vt_ref, o_ref, rhs_sc, acc_sc, ot_sc):
    kv = pl.program_id(1)
    nchunks = vt_ref.shape[0]
    tk = vt_ref.shape[2]
    row = lax.broadcasted_iota(jnp.int32, (A_KV_WIDTH, NQ), 0)
    own = (row // HEAD_DIM) == kv
    for g in range(A_GROUP):
        qg = qt_ref[g * HEAD_DIM:(g + 1) * HEAD_DIM, :]
        rhs_sc[g] = jnp.where(own, jnp.concatenate([qg, qg], axis=0), jnp.zeros((), BF16))

    def load_chunk(c):
        return k_ref[pl.ds(pl.multiple_of(c * tk, tk), tk), :], vt_ref[c]

    @pl.when(fast_ref[0] == 1)
    def _():
        shift = []
        for g in range(A_GROUP):
            qg = qt_ref[g * HEAD_DIM:(g + 1) * HEAD_DIM, :].astype(F32)
            shift.append(jnp.sqrt(jnp.sum(qg * qg, axis=0, keepdims=True)) * kbound_ref[0])

        def scores(tile):
            c, g = tile
            return jnp.dot(k_ref[c * tk:(c + 1) * tk, :], rhs_sc[g], preferred_element_type=F32)

        tiles = [(c, g) for c in range(nchunks) for g in range(A_GROUP)]
        accs = [None] * A_GROUP
        st = scores(tiles[0])
        for i, (c, g) in enumerate(tiles):
            st_next = scores(tiles[i + 1]) if i + 1 < len(tiles) else None
            p = jnp.exp2(st - shift[g]).astype(BF16)
            pv = jnp.dot(vt_ref[c], p, preferred_element_type=F32)
            accs[g] = pv if accs[g] is None else accs[g] + pv
            st = st_next
        for g in range(A_GROUP):
            acc_sc[g] = accs[g]

    @pl.when(fast_ref[0] == 0)
    def _():
        acc_sc[...] = jnp.zeros_like(acc_sc)

        def chunk(c, ms):
            kc, vc = load_chunk(c)
            new_ms = []
            for g in range(A_GROUP):
                st = jnp.dot(kc, rhs_sc[g], preferred_element_type=F32)
                m_new = jnp.maximum(ms[g], jnp.max(st, axis=0, keepdims=True))
                alpha = jnp.exp2(ms[g] - m_new)
                p = jnp.exp2(st - m_new).astype(BF16)
                acc_sc[g] = alpha * acc_sc[g] + jnp.dot(vc, p, preferred_element_type=F32)
                new_ms.append(m_new)
            return tuple(new_ms)

        lax.fori_loop(0, nchunks, chunk,
                      tuple(jnp.full((1, NQ), NEG_INF, F32) for _ in range(A_GROUP)))

    for g in range(A_GROUP):
        ot_sc[g * HEAD_DIM:(g + 1) * HEAD_DIM, :] = (
            acc_sc[g, 0:HEAD_DIM, :] / acc_sc[g, HEAD_DIM:HEAD_DIM + 1, :])
    o_ref[...] = ot_sc[...].T.astype(BF16)


def _attn_a_call(fast, kbound, qt, k, vt):
    b, _, s = qt.shape
    nchunks, tk = vt.shape[1], vt.shape[4]
    gw = A_GROUP * HEAD_DIM
    grid_spec = pltpu.PrefetchScalarGridSpec(
        num_scalar_prefetch=2,
        grid=(b, A_KV_HEADS, s // NQ),
        in_specs=[
            pl.BlockSpec((None, gw, NQ), lambda i, h, j, *_: (i, h, j)),
            pl.BlockSpec((None, s, A_KV_WIDTH), lambda i, h, j, *_: (i, 0, 0)),
            pl.BlockSpec((None, nchunks, None, VT_ROWS, tk), lambda i, h, j, *_: (i, 0, h, 0, 0)),
        ],
        out_specs=pl.BlockSpec((None, NQ, gw), lambda i, h, j, *_: (i, j, h)),
        scratch_shapes=[pltpu.VMEM((A_GROUP, A_KV_WIDTH, NQ), BF16),
                        pltpu.VMEM((A_GROUP, VT_ROWS, NQ), F32),
                        pltpu.VMEM((gw, NQ), F32)],
    )
    return pl.pallas_call(
        _attn_a_body,
        grid_spec=grid_spec,
        out_shape=jax.ShapeDtypeStruct((b, s, A_WIDTH), BF16),
        compiler_params=_params(("arbitrary", "arbitrary", "arbitrary")),
        name="attn_a",
    )(fast, kbound, qt, k, vt)


def _score_bound(q_gain, k_gain):
    kb = math.sqrt(HEAD_DIM) * jnp.max(jnp.abs(k_gain.astype(F32)))
    qb = math.sqrt(HEAD_DIM) * jnp.max(jnp.abs(q_gain.astype(F32))) * (HEAD_DIM ** -0.5) * LOG2E
    fast = (qb * kb <= MAX_SAFE_SHIFT).astype(jnp.int32)
    return fast.reshape(1), kb.reshape(1)


def _attn_b_body(q_ref, k_ref, v_ref, bm_ref, o_ref, lse_ref, kp_sc, vp_sc, ot_sc, lt_sc):
    length = q_ref.shape[0]
    zeros = jnp.zeros((HALO, B_OUT_WIDTH), BF16)
    for sc, ref in ((kp_sc, k_ref), (vp_sc, v_ref)):
        sc[0:HALO, :] = zeros
        sc[HALO:HALO + length, :] = ref[...]
        sc[HALO + length:, :] = zeros
    row_head = lax.broadcasted_iota(jnp.int32, (B_OUT_WIDTH, NQ), 0) // HEAD_DIM
    band_row = lax.broadcasted_iota(jnp.int32, (BAND, NQ), 0)

    def block(i, carry):
        q0 = pl.multiple_of(i * NQ, NQ)
        qt = q_ref[pl.ds(q0, NQ), :].astype(F32).T.astype(BF16)
        kband = kp_sc[pl.ds(q0, BAND), :]
        vbt = vp_sc[pl.ds(q0, BAND), :].astype(F32).T.astype(BF16)
        key_pos = band_row + (q0 - HALO)
        pen = jnp.where((key_pos >= 0) & (key_pos < length), 0.0, NEG_INF)
        for h in range(B_HEADS_PER_GROUP):
            rhs = jnp.where(row_head == h, qt, jnp.zeros((), BF16))
            st = jnp.dot(kband, rhs, preferred_element_type=F32)
            st = st + bm_ref[h] + pen
            m = jnp.max(st, axis=0, keepdims=True)
            p = jnp.exp(st - m)
            l = jnp.sum(p, axis=0, keepdims=True)
            sl = slice(h * HEAD_DIM, (h + 1) * HEAD_DIM)
            ot = jnp.dot(vbt[sl], p.astype(BF16), preferred_element_type=F32)
            ot_sc[sl, :] = ot / l
            lt_sc[sl, :] = jnp.broadcast_to(m + jnp.log(l), (HEAD_DIM, NQ))
        o_ref[pl.ds(q0, NQ), :] = ot_sc[...].T
        lse_ref[pl.ds(q0, NQ), :] = lt_sc[...].T
        return carry

    lax.fori_loop(0, length // NQ, block, 0)


def _attn_b_call(qb, kb, vb, bm, gi, r):
    b, s, _ = qb.shape
    length = s // r
    nblk_in = B_WIDTH // B_OUT_WIDTH
    view_in = lambda a: a.reshape(b, length, r * B_WIDTH)
    in_spec = pl.BlockSpec((None, length, B_OUT_WIDTH), lambda i, j: (i, 0, nblk_in * j + gi))
    out_spec = pl.BlockSpec((None, length, B_OUT_WIDTH), lambda i, j: (i, 0, j))
    out_sds = jax.ShapeDtypeStruct((b, length, r * B_OUT_WIDTH), F32)
    o, lse = pl.pallas_call(
        _attn_b_body,
        grid=(b, r),
        in_specs=[in_spec, in_spec, in_spec,
                  pl.BlockSpec((B_HEADS_PER_GROUP, BAND, NQ), lambda i, j: (0, 0, 0))],
        out_specs=[out_spec, out_spec],
        out_shape=[out_sds, out_sds],
        scratch_shapes=[pltpu.VMEM((length + 2 * HALO, B_OUT_WIDTH), BF16),
                        pltpu.VMEM((length + 2 * HALO, B_OUT_WIDTH), BF16),
                        pltpu.VMEM((B_OUT_WIDTH, NQ), F32),
                        pltpu.VMEM((B_OUT_WIDTH, NQ), F32)],
        compiler_params=_params(("arbitrary", "arbitrary")),
        name=f"attn_b_r{r}",
    )(view_in(qb), view_in(kb), view_in(vb), bm)
    return o.reshape(b, s, B_OUT_WIDTH), lse.reshape(b, s, B_OUT_WIDTH)


def _t5_bucket_np(rel):
    half = N_BUCKETS // 2
    exact = half // 2
    n = np.abs(rel)
    big = exact + (np.log(np.maximum(n, 1).astype(np.float64) / exact)
                   / math.log(MAX_DISTANCE / exact) * (half - exact)).astype(np.int64)
    big = np.minimum(big, half - 1)
    return np.where(rel > 0, half, 0) + np.where(n < exact, n, big)


def _bias_mask_table(rel_bias, gi, r):
    span = BAND + NQ - 1
    rel = np.arange(span) - (NQ - 1) - HALO
    tab = rel_bias[:, gi * B_HEADS_PER_GROUP:(gi + 1) * B_HEADS_PER_GROUP].astype(F32)
    u = jnp.where(jnp.asarray(np.abs(rel) <= HALO)[:, None],
                  tab[jnp.asarray(_t5_bucket_np(rel * r), jnp.int32)], NEG_INF).T
    hankel = jnp.tile(u, (1, NQ + 1))[:, :NQ * (span + 1)].reshape(-1, NQ, span + 1)
    return jnp.transpose(hankel[:, ::-1, :BAND], (0, 2, 1))


def _layer_norm(h, g, b):
    mu = jnp.mean(h, axis=-1, keepdims=True)
    c = h - mu
    var = jnp.mean(c * c, axis=-1, keepdims=True)
    return c * lax.rsqrt(var + LN_EPS) * g + b


def _split_bf16(a):
    hi = a.astype(BF16)
    return hi, (a - hi.astype(F32)).astype(BF16)


def _merge_body(x_ref, ya_ref, o1_ref, o2_ref, o3_ref, l1_ref, l2_ref, l3_ref, gates_ref,
                wa_ref, wbr_ref, wo_ref, g_ref, b_ref, wrh_ref, wrl_ref, br_ref, out_ref):
    l1, l2, l3 = l1_ref[...], l2_ref[...], l3_ref[...]
    mx = jnp.maximum(jnp.maximum(l1, l2), l3)
    e1, e2, e3 = jnp.exp(l1 - mx), jnp.exp(l2 - mx), jnp.exp(l3 - mx)
    yb = (e1 * o1_ref[...] + e2 * o2_ref[...] + e3 * o3_ref[...]) / (e1 + e2 + e3)
    ya_p = jnp.dot(ya_ref[...], wa_ref[...], preferred_element_type=F32)
    yb_p = jnp.dot(yb.astype(BF16), wbr_ref[...], preferred_element_type=F32)
    merged = (gates_ref[:, 0:D_MODEL].astype(F32) * ya_p
              + gates_ref[:, D_MODEL:2 * D_MODEL].astype(F32) * yb_p)
    h = ALPHA * x_ref[...] + jnp.dot(merged.astype(BF16), wo_ref[...], preferred_element_type=F32)
    x1 = _layer_norm(h, g_ref[...], b_ref[...])
    out_ref[:, 0:D_MODEL] = x1

    xh, xl = _split_bf16(x1)
    logits = (jnp.dot(xh, wrh_ref[...], preferred_element_type=F32)
              + jnp.dot(xl, wrh_ref[...], preferred_element_type=F32)
              + jnp.dot(xh, wrl_ref[...], preferred_element_type=F32)) + br_ref[...]
    col = lambda i: logits[:, i:i + 1]
    lg = [col(i) for i in range(N_GROUPS)]
    gmax = functools.reduce(jnp.maximum, lg)
    top_p = 1.0 / functools.reduce(lambda a, c: a + c, [jnp.exp(v - gmax) for v in lg])
    gsel = jnp.full_like(gmax, N_GROUPS - 1).astype(jnp.int32)
    for i in reversed(range(N_GROUPS - 1)):
        gsel = jnp.where(lg[i] >= gmax, i, gsel)
    sel = []
    for e in range(EXPERTS_PER_GROUP):
        v = col(N_GROUPS + (N_GROUPS - 1) * EXPERTS_PER_GROUP + e)
        for gi in reversed(range(N_GROUPS - 1)):
            v = jnp.where(gsel == gi, col(N_GROUPS + gi * EXPERTS_PER_GROUP + e), v)
        sel.append(v)
    rank = []
    for e in range(EXPERTS_PER_GROUP):
        rk = jnp.zeros_like(gsel)
        for j in range(EXPERTS_PER_GROUP):
            if j != e:
                ahead = (sel[j] > sel[e]) | ((sel[j] == sel[e]) & (j < e))
                rk = rk + ahead.astype(jnp.int32)
        rank.append(rk)
    pick = lambda k: functools.reduce(
        lambda a, c: a + c, [jnp.where(rank[e] == k, sel[e], 0.0) for e in range(EXPERTS_PER_GROUP)])
    t = jnp.exp(pick(1) - pick(0))
    w_first = top_p / (1.0 + t)
    w_second = top_p * t / (1.0 + t)
    chosen = [rank[e] < 2 for e in range(EXPERTS_PER_GROUP)]
    weight = [jnp.where(rank[e] == 0, w_first, w_second) for e in range(EXPERTS_PER_GROUP)]
    pair = jnp.zeros_like(gsel)
    w_lo = jnp.zeros_like(top_p)
    w_hi = jnp.zeros_like(top_p)
    for pi, (a, c) in enumerate(PAIRS):
        hit = chosen[a] & chosen[c]
        pair = jnp.where(hit, pi, pair)
        w_lo = jnp.where(hit, weight[a], w_lo)
        w_hi = jnp.where(hit, weight[c], w_hi)
    bucket = (gsel * len(PAIRS) + pair).astype(F32)
    lane = lax.broadcasted_iota(jnp.int32, (x1.shape[0], ROUTE_COLS), 1)
    out_ref[:, D_MODEL:] = jnp.where(lane == 0, bucket,
                                     jnp.where(lane == 1, w_lo, jnp.where(lane == 2, w_hi, 0.0)))


def _merge_call(x, ya, os, ls, gates, wa, wbr, wo, g, b, wrh, wrl, br):
    t, d = x.shape
    tok = lambda c: pl.BlockSpec((TM, c), lambda i: (i, 0))
    const = lambda a: pl.BlockSpec(a.shape, lambda i: (0,) * a.ndim)
    consts = (wa, wbr, wo, g, b, wrh, wrl, br)
    return pl.pallas_call(
        _merge_body,
        grid=(t // TM,),
        in_specs=[tok(d), tok(A_WIDTH)] + [tok(B_OUT_WIDTH)] * 6 + [tok(2 * D_MODEL)]
        + [const(a) for a in consts],
        out_specs=tok(d + ROUTE_COLS),
        out_shape=jax.ShapeDtypeStruct((t, d + ROUTE_COLS), F32),
        compiler_params=_params(("arbitrary",)),
        name="merge_ln1_route",
    )(x, ya, *os, *ls, gates, *consts)


def _moe_body(nvalid_ref, elo_ref, ehi_ref, ids_ref, x_hbm, wil_ref, wih_ref, wol_ref, woh_ref,
              g_ref, b_ref, out_hbm, xbuf, obuf, sem_in, sem_out):
    del elo_ref, ehi_ref
    n = nvalid_ref[pl.program_id(0)]

    def row_in(k):
        return pltpu.make_async_copy(x_hbm.at[pl.ds(ids_ref[0, k], 1)], xbuf.at[pl.ds(k, 1)], sem_in)

    def row_out(k):
        return pltpu.make_async_copy(obuf.at[pl.ds(k, 1)], out_hbm.at[pl.ds(ids_ref[0, k], 1)], sem_out)

    def for_rows(fn):
        lax.fori_loop(0, n, lambda k, c: (fn(k), c)[1], 0)

    @pl.when(pl.program_id(0) == 0)
    def _():
        xbuf[...] = jnp.zeros_like(xbuf)

    @pl.when(n > 0)
    def _():
        for_rows(lambda k: row_in(k).start())
        for_rows(lambda k: row_in(k).wait())
        xt = xbuf[:, 0:D_MODEL]
        xb = xt.astype(BF16)

        def expert(wi_ref, wo_ref):
            hcat = jnp.dot(xb, wi_ref[...], preferred_element_type=F32)
            a, u = hcat[:, 0:D_EXPERT], hcat[:, D_EXPERT:]
            act = (a * jax.nn.sigmoid(a) * u).astype(BF16)
            return jnp.dot(act, wo_ref[...], preferred_element_type=F32)

        moe = (xbuf[:, D_MODEL + 1:D_MODEL + 2] * expert(wil_ref, wol_ref)
               + xbuf[:, D_MODEL + 2:D_MODEL + 3] * expert(wih_ref, woh_ref))
        obuf[...] = _layer_norm(ALPHA * xt + moe, g_ref[...], b_ref[...])
        for_rows(lambda k: row_out(k).start())
        for_rows(lambda k: row_out(k).wait())


def _moe_call(x1ext, nvalid, elo, ehi, ids, wei, weo, g, b):
    t = x1ext.shape[0]
    ntiles = nvalid.shape[0]
    wi_spec = lambda e: pl.BlockSpec((None, D_MODEL, 2 * D_EXPERT), lambda i, nv, lo, hi: (e(lo, hi)[i], 0, 0))
    wo_spec = lambda e: pl.BlockSpec((None, D_EXPERT, D_MODEL), lambda i, nv, lo, hi: (e(lo, hi)[i], 0, 0))
    first = lambda lo, hi: lo
    second = lambda lo, hi: hi
    vec = pl.BlockSpec((1, D_MODEL), lambda i, nv, lo, hi: (0, 0))
    grid_spec = pltpu.PrefetchScalarGridSpec(
        num_scalar_prefetch=3,
        grid=(ntiles,),
        in_specs=[
            pl.BlockSpec((None, 1, TMOE), lambda i, nv, lo, hi: (i, 0, 0), memory_space=pltpu.SMEM),
            pl.BlockSpec(memory_space=pl.ANY),
            wi_spec(first), wi_spec(second), wo_spec(first), wo_spec(second), vec, vec,
        ],
        out_specs=pl.BlockSpec(memory_space=pl.ANY),
        scratch_shapes=[pltpu.VMEM((TMOE, D_MODEL + ROUTE_COLS), F32),
                        pltpu.VMEM((TMOE, D_MODEL), F32),
                        pltpu.SemaphoreType.DMA(()),
                        pltpu.SemaphoreType.DMA(())],
    )
    return pl.pallas_call(
        _moe_body,
        grid_spec=grid_spec,
        out_shape=jax.ShapeDtypeStruct((t, D_MODEL), F32),
        compiler_params=_params(("arbitrary",)),
        name="moe_ln2",
    )(nvalid, elo, ehi, ids, x1ext, wei, wei, weo, weo, g, b)


def _route_tiles(bucket):
    t = bucket.shape[0]
    ntiles = t // TMOE + N_ROUTE_BUCKETS
    order = jnp.argsort(bucket, stable=True).astype(jnp.int32)
    counts = jnp.sum(bucket[:, None] == jnp.arange(N_ROUTE_BUCKETS, dtype=jnp.int32)[None, :],
                     axis=0, dtype=jnp.int32)
    row_start = jnp.cumsum(counts) - counts
    tiles_per = (counts + TMOE - 1) // TMOE
    tile_end = jnp.cumsum(tiles_per)
    total = tile_end[-1]
    tile = jnp.arange(ntiles, dtype=jnp.int32)
    used = tile < total
    bkt = jnp.searchsorted(tile_end, jnp.minimum(tile, total - 1), side="right").astype(jnp.int32)
    local = tile - (tile_end - tiles_per)[bkt]
    row0 = row_start[bkt] + local * TMOE
    nvalid = jnp.where(used, jnp.clip(counts[bkt] - local * TMOE, 0, TMOE), 0).astype(jnp.int32)
    pairs = jnp.asarray(PAIRS, jnp.int32)
    grp = bkt // len(PAIRS)
    elo = grp * EXPERTS_PER_GROUP + pairs[bkt % len(PAIRS), 0]
    ehi = grp * EXPERTS_PER_GROUP + pairs[bkt % len(PAIRS), 1]
    idx = jnp.minimum(row0[:, None] + jnp.arange(TMOE, dtype=jnp.int32)[None, :], t - 1)
    ids = order[idx].reshape(ntiles, 1, TMOE)
    return nvalid, elo.astype(jnp.int32), ehi.astype(jnp.int32), ids


def _prepare(w_in, b_gate, q_gain, k_gain, rel_bias, w_branch_a, w_branch_b, w_out, ln1_g, ln1_b,
             w_route_group, b_route_group, w_route_expert, b_route_expert, w_expert_in,
             w_expert_out, ln2_g, ln2_b, seq):
    row = lambda v: v.reshape(1, -1).astype(F32)
    w = w_in[0]
    wr = jnp.concatenate([w_route_group[0]]
                         + [w_route_expert[0, gi] for gi in range(N_GROUPS)], axis=1)
    wr = jnp.pad(wr.astype(F32), ((0, 0), (0, ROUTE_COLS - wr.shape[1])))
    br = jnp.concatenate([b_route_group[0], b_route_expert[0].reshape(-1)])
    br = jnp.pad(br.astype(F32), (0, ROUTE_COLS - br.shape[0])).reshape(1, ROUTE_COLS)
    wrh, wrl = _split_bf16(wr)
    scale = HEAD_DIM ** -0.5
    wb = w[:, DT_ROWS:]
    wb = jnp.concatenate([wb[:, 0:B_WIDTH] * scale, wb[:, B_WIDTH:]], axis=1)
    return dict(
        wdt=w[:, 0:DT_ROWS].T.astype(BF16), wb=wb.astype(BF16), bg=row(b_gate[0]),
        rope_q=_rope_tables(seq, q_gain[0], scale * LOG2E), rope_k=_rope_tables(seq, k_gain[0], 1.0),
        score_bound=_score_bound(q_gain[0], k_gain[0]),
        bm=[_bias_mask_table(rel_bias, gi, r) for gi, (_, r) in enumerate(DILATED_PAIRS)],
        wa=w_branch_a[0].astype(BF16), wbr=w_branch_b[0].astype(BF16), wo=w_out[0].astype(BF16),
        ln1=(row(ln1_g[0]), row(ln1_b[0])), wrh=wrh, wrl=wrl, br=br,
        wei=w_expert_in[0].astype(BF16), weo=w_expert_out[0].astype(BF16),
        ln2=(row(ln2_g[0]), row(ln2_b[0])),
    )


def _trunk(x, p):
    b, s, d = x.shape
    qt, k, vt, qb, kb, vb, gates = _inproj_call(x, p["wdt"], p["wb"], p["bg"], *p["rope_q"], *p["rope_k"])
    ya = _attn_a_call(*p["score_bound"], qt, k, vt)
    os, ls = [], []
    for gi, (_, r) in enumerate(DILATED_PAIRS):
        o, lse = _attn_b_call(qb, kb, vb, p["bm"][gi], gi, r)
        os.append(o.reshape(b * s, B_OUT_WIDTH))
        ls.append(lse.reshape(b * s, B_OUT_WIDTH))
    x1ext = _merge_call(x.reshape(b * s, d), ya.reshape(b * s, A_WIDTH), os, ls,
                        gates.reshape(b * s, 2 * D_MODEL), p["wa"], p["wbr"], p["wo"], *p["ln1"],
                        p["wrh"], p["wrl"], p["br"])
    nvalid, elo, ehi, ids = _route_tiles(x1ext[:, D_MODEL].astype(jnp.int32))
    y = _moe_call(x1ext, nvalid, elo, ehi, ids, p["wei"], p["weo"], *p["ln2"])
    return y.reshape(b, s, d)


def kernel(x_prompt, x_sample, w_in, b_gate, q_gain, k_gain, rel_bias, w_branch_a, w_branch_b, w_out,
           ln1_g, ln1_b, w_route_group, b_route_group, w_route_expert, b_route_expert, w_expert_in,
           w_expert_out, ln2_g, ln2_b):
    assert x_prompt.shape[1] == x_sample.shape[1]
    p = _prepare(w_in, b_gate, q_gain, k_gain, rel_bias, w_branch_a, w_branch_b, w_out, ln1_g, ln1_b,
                 w_route_group, b_route_group, w_route_expert, b_route_expert, w_expert_in,
                 w_expert_out, ln2_g, ln2_b, x_prompt.shape[1])
    return (_trunk(x_prompt, p), _trunk(x_sample, p))
```

```python
import functools
import math

import jax
import jax.numpy as jnp
import numpy as np
from jax import lax
from jax.experimental import pallas as pl
from jax.experimental.pallas import tpu as pltpu

F32 = jnp.float32
BF16 = jnp.bfloat16

D_MODEL = 1024
HEAD_DIM = 64
A_Q_HEADS = 8
A_KV_HEADS = 2
A_GROUP = A_Q_HEADS // A_KV_HEADS
GRID_W = 64
ROPE_AXIS_DIM = HEAD_DIM // 2
ROPE_THETA = 10000.0
DILATED_PAIRS = ((128, 1), (512, 4), (2048, 16))
N_DIL_GROUPS = len(DILATED_PAIRS)
B_HEADS_PER_GROUP = 4
N_BUCKETS = 32
MAX_DISTANCE = 1024
A_WIDTH = A_Q_HEADS * HEAD_DIM
A_KV_WIDTH = A_KV_HEADS * HEAD_DIM
B_WIDTH = B_HEADS_PER_GROUP * N_DIL_GROUPS * HEAD_DIM
B_OUT_WIDTH = B_HEADS_PER_GROUP * HEAD_DIM
DT_ROWS = A_WIDTH + 2 * A_KV_WIDTH
N_GROUPS = 4
EXPERTS_PER_GROUP = 4
N_EXPERTS = N_GROUPS * EXPERTS_PER_GROUP
D_EXPERT = 512
PAIRS = ((0, 1), (0, 2), (0, 3), (1, 2), (1, 3), (2, 3))
N_ROUTE_BUCKETS = N_GROUPS * len(PAIRS)
ALPHA = 2.0 ** 0.25
RMS_EPS = 1e-6
LN_EPS = 1e-5
NEG_INF = -1e30

LANES = 128
TM = 512
NQ = 256
BF16_SUBLANES = 16
F32_SUBLANES = 8
VT_ROWS = HEAD_DIM + BF16_SUBLANES
LOG2E = 1.4426950408889634
LN2 = 0.6931471805599453
MAX_SAFE_SHIFT = 40.0
HALO = 64
BAND = NQ + 2 * HALO
TMOE = 256
DMA_UNROLL = 8
ROUTE_COLS = LANES
VMEM_LIMIT = 56 * 1024 * 1024


def _params(sem):
    return pltpu.CompilerParams(dimension_semantics=sem, vmem_limit_bytes=VMEM_LIMIT)


def _swap16(x):
    return jnp.concatenate([x[16:32], x[0:16], x[48:64], x[32:48]], axis=0)


def _norm_rope(xh, ta, tb):
    r = lax.rsqrt(jnp.mean(xh * xh, axis=0, keepdims=True) + RMS_EPS)
    return (xh * ta + _swap16(xh) * tb) * r


def _inproj_body(x_ref, wdt_ref, wb_ref, bg_ref, taq_ref, tbq_ref, tak_ref, tbk_ref,
                 qt_ref, k_ref, vt_ref, *rest):
    b_refs, gates_ref, tmp_sc = rest[:3 * N_DIL_GROUPS], rest[3 * N_DIL_GROUPS], rest[-1]
    xb = x_ref[...].astype(BF16)
    dt = lax.dot_general(wdt_ref[...], xb, (((1,), (1,)), ((), ())),
                         preferred_element_type=F32)
    taq, tbq = taq_ref[...], tbq_ref[...]
    for h in range(A_Q_HEADS):
        sl = slice(h * HEAD_DIM, (h + 1) * HEAD_DIM)
        qt_ref[sl, :] = _norm_rope(dt[sl], taq, tbq).astype(BF16)
    tak, tbk = tak_ref[...], tbk_ref[...]
    kt = jnp.concatenate(
        [_norm_rope(dt[A_WIDTH + h * HEAD_DIM:A_WIDTH + (h + 1) * HEAD_DIM], tak, tbk)
         for h in range(A_KV_HEADS)], axis=0)
    k_ref[...] = kt.T.astype(BF16)
    for h in range(A_KV_HEADS):
        v0 = A_WIDTH + A_KV_WIDTH + h * HEAD_DIM
        vt_ref[h, 0:HEAD_DIM, :] = dt[v0:v0 + HEAD_DIM].astype(BF16)
        vt_ref[h, HEAD_DIM:, :] = jnp.ones((VT_ROWS - HEAD_DIM, TM), BF16)
    for i in range(3):
        res = jnp.dot(xb, wb_ref[:, i * B_WIDTH:(i + 1) * B_WIDTH], preferred_element_type=F32)
        for c in range(B_WIDTH // LANES):
            tmp_sc[c] = res[:, c * LANES:(c + 1) * LANES]
        for gi, (_, r) in enumerate(DILATED_PAIRS):
            ref = b_refs[i * N_DIL_GROUPS + gi]
            for j in range(r):
                for half in range(B_OUT_WIDTH // LANES):
                    c = gi * (B_OUT_WIDTH // LANES) + half
                    lanes = slice(j * B_OUT_WIDTH + half * LANES, j * B_OUT_WIDTH + (half + 1) * LANES)
                    ref[:, lanes] = tmp_sc[c, pl.ds(j, TM // r, stride=r), :].astype(BF16)
    g0 = 3 * B_WIDTH
    for i in range(2):
        cols = slice(g0 + i * D_MODEL, g0 + (i + 1) * D_MODEL)
        pre = jnp.dot(xb, wb_ref[:, cols], preferred_element_type=F32)
        gates_ref[:, i * D_MODEL:(i + 1) * D_MODEL] = jax.nn.sigmoid(
            pre + bg_ref[:, i * D_MODEL:(i + 1) * D_MODEL]).astype(BF16)


def _inproj_call(x, wdt, wb, bg, taq, tbq, tak, tbk):
    b, s, d = x.shape
    nt = s // TM
    wcols = wb.shape[1]
    tab = pl.BlockSpec((HEAD_DIM, TM), lambda p, i: (0, p))
    const = lambda shape: pl.BlockSpec(shape, lambda p, i: (0,) * len(shape))
    tok = lambda c: pl.BlockSpec((None, TM, c), lambda p, i: (i, p, 0))
    dil = [r for _ in range(3) for _, r in DILATED_PAIRS]
    return pl.pallas_call(
        _inproj_body,
        grid=(nt, b),
        in_specs=[tok(d), const((DT_ROWS, d)), const((d, wcols)), const((1, 2 * D_MODEL)),
                  tab, tab, tab, tab],
        out_specs=[
            pl.BlockSpec((None, A_WIDTH, TM), lambda p, i: (i, 0, p)),
            tok(A_KV_WIDTH),
            pl.BlockSpec((None, None, A_KV_HEADS, VT_ROWS, TM), lambda p, i: (i, p, 0, 0, 0)),
        ] + [pl.BlockSpec((None, TM // r, r * B_OUT_WIDTH), lambda p, i: (i, p, 0)) for r in dil]
        + [tok(2 * D_MODEL)],
        out_shape=[
            jax.ShapeDtypeStruct((b, A_WIDTH, s), BF16),
            jax.ShapeDtypeStruct((b, s, A_KV_WIDTH), BF16),
            jax.ShapeDtypeStruct((b, nt, A_KV_HEADS, VT_ROWS, TM), BF16),
        ] + [jax.ShapeDtypeStruct((b, s // r, r * B_OUT_WIDTH), BF16) for r in dil]
        + [jax.ShapeDtypeStruct((b, s, 2 * D_MODEL), BF16)],
        scratch_shapes=[pltpu.VMEM((B_WIDTH // LANES, TM, LANES), F32)],
        compiler_params=_params(("arbitrary", "arbitrary")),
        name="inproj",
    )(x, wdt, wb, bg, taq, tbq, tak, tbk)


def _rope_tables(s, gain, scale):
    t = np.arange(s)
    inv = ROPE_THETA ** (-np.arange(0, ROPE_AXIS_DIM, 2, dtype=np.float64) / ROPE_AXIS_DIM)
    pos = np.stack([t // GRID_W, t // GRID_W, t % GRID_W, t % GRID_W], axis=0)
    ang = (pos[:, None, :] * inv[None, :, None]).reshape(HEAD_DIM, s)
    sign = np.repeat(np.array([-1.0, 1.0, -1.0, 1.0]), 16)[:, None]
    partner = np.arange(HEAD_DIM) ^ 16
    cos = jnp.asarray(np.cos(ang), F32)
    sin = jnp.asarray(sign * np.sin(ang), F32)
    g = gain.astype(F32) * scale
    return g[:, None] * cos, g[partner][:, None] * sin


def _attn_a_body(fast_ref, kbound_ref, qt_ref, k_ref, vt_ref, o_ref, rhs_sc, acc_sc, ot_sc):
    kv = pl.program_id(1)
    nchunks = vt_ref.shape[0]
    tk = vt_ref.shape[2]
    row = lax.broadcasted_iota(jnp.int32, (A_KV_WIDTH, NQ), 0)
    own = (row // HEAD_DIM) == kv
    for g in range(A_GROUP):
        qg = qt_ref[g * HEAD_DIM:(g + 1) * HEAD_DIM, :]
        rhs_sc[g] = jnp.where(own, jnp.concatenate([qg, qg], axis=0), jnp.zeros((), BF16))

    def load_chunk(c):
        return k_ref[pl.ds(pl.multiple_of(c * tk, tk), tk), :], vt_ref[c]

    @pl.when(fast_ref[0] == 1)
    def _():
        shift = []
        for g in range(A_GROUP):
            qg = qt_ref[g * HEAD_DIM:(g + 1) * HEAD_DIM, :].astype(F32)
            shift.append(jnp.sqrt(jnp.sum(qg * qg, axis=0, keepdims=True)) * kbound_ref[0])

        def scores(tile):
            c, g = tile
            return jnp.dot(k_ref[c * tk:(c + 1) * tk, :], rhs_sc[g], preferred_element_type=F32)

        heads = range(A_GROUP)
        accs = [None] * A_GROUP
        sts = [scores((0, g)) for g in heads]
        for c in range(nchunks):
            sts_next = [scores((c + 1, g)) for g in heads] if c + 1 < nchunks else None
            ps = [jnp.exp2(sts[g] - shift[g]).astype(BF16) for g in heads]
            pvs = [jnp.dot(vt_ref[c], ps[g], preferred_element_type=F32) for g in heads]
            accs = [pvs[g] if accs[g] is None else accs[g] + pvs[g] for g in heads]
            sts = sts_next
        for g in heads:
            acc_sc[g] = accs[g]

    @pl.when(fast_ref[0] == 0)
    def _():
        acc_sc[...] = jnp.zeros_like(acc_sc)

        def chunk(c, ms):
            kc, vc = load_chunk(c)
            new_ms = []
            for g in range(A_GROUP):
                st = jnp.dot(kc, rhs_sc[g], preferred_element_type=F32)
                m_new = jnp.maximum(ms[g], jnp.max(st, axis=0, keepdims=True))
                alpha = jnp.exp2(ms[g] - m_new)
                p = jnp.exp2(st - m_new).astype(BF16)
                acc_sc[g] = alpha * acc_sc[g] + jnp.dot(vc, p, preferred_element_type=F32)
                new_ms.append(m_new)
            return tuple(new_ms)

        lax.fori_loop(0, nchunks, chunk,
                      tuple(jnp.full((1, NQ), NEG_INF, F32) for _ in range(A_GROUP)))

    for g in range(A_GROUP):
        ot_sc[g * HEAD_DIM:(g + 1) * HEAD_DIM, :] = (
            acc_sc[g, 0:HEAD_DIM, :] / acc_sc[g, HEAD_DIM:HEAD_DIM + 1, :])
    o_ref[...] = ot_sc[...].T.astype(BF16)


def _attn_a_call(fast, kbound, qt, k, vt):
    b, _, s = qt.shape
    nchunks, tk = vt.shape[1], vt.shape[4]
    gw = A_GROUP * HEAD_DIM
    grid_spec = pltpu.PrefetchScalarGridSpec(
        num_scalar_prefetch=2,
        grid=(b, A_KV_HEADS, s // NQ),
        in_specs=[
            pl.BlockSpec((None, gw, NQ), lambda i, h, j, *_: (i, h, j)),
            pl.BlockSpec((None, s, A_KV_WIDTH), lambda i, h, j, *_: (i, 0, 0)),
            pl.BlockSpec((None, nchunks, None, VT_ROWS, tk), lambda i, h, j, *_: (i, 0, h, 0, 0)),
        ],
        out_specs=pl.BlockSpec((None, NQ, gw), lambda i, h, j, *_: (i, j, h)),
        scratch_shapes=[pltpu.VMEM((A_GROUP, A_KV_WIDTH, NQ), BF16),
                        pltpu.VMEM((A_GROUP, VT_ROWS, NQ), F32),
                        pltpu.VMEM((gw, NQ), F32)],
    )
    return pl.pallas_call(
        _attn_a_body,
        grid_spec=grid_spec,
        out_shape=jax.ShapeDtypeStruct((b, s, A_WIDTH), BF16),
        compiler_params=_params(("arbitrary", "arbitrary", "arbitrary")),
        name="attn_a",
    )(fast, kbound, qt, k, vt)


def _score_bound(q_gain, k_gain):
    kb = math.sqrt(HEAD_DIM) * jnp.max(jnp.abs(k_gain.astype(F32)))
    qb = math.sqrt(HEAD_DIM) * jnp.max(jnp.abs(q_gain.astype(F32))) * (HEAD_DIM ** -0.5) * LOG2E
    fast = (qb * kb <= MAX_SAFE_SHIFT).astype(jnp.int32)
    return fast.reshape(1), kb.reshape(1)


def _attn_b_body(q_ref, k_ref, v_ref, bm_ref, o_ref, lse_ref, kp_sc, vp_sc):
    length = q_ref.shape[0]
    zeros = jnp.zeros((HALO, B_OUT_WIDTH), BF16)
    for sc, ref in ((kp_sc, k_ref), (vp_sc, v_ref)):
        sc[0:HALO, :] = zeros
        sc[HALO:HALO + length, :] = ref[...]
        sc[HALO + length:, :] = zeros
    row_head = lax.broadcasted_iota(jnp.int32, (B_OUT_WIDTH, NQ), 0) // HEAD_DIM
    ones = jnp.ones((VT_ROWS - HEAD_DIM, BAND), BF16)
    blocks = {}

    def block(i):
        if i not in blocks:
            q0 = i * NQ
            qt = q_ref[q0:q0 + NQ, :].astype(F32).T.astype(BF16)
            kband = kp_sc[q0:q0 + BAND, :]
            vbt = vp_sc[q0:q0 + BAND, :].astype(F32).T.astype(BF16)
            pen = None
            if q0 - HALO < 0 or q0 - HALO + BAND > length:
                key_pos = lax.broadcasted_iota(jnp.int32, (BAND, NQ), 0) + (q0 - HALO)
                pen = jnp.where((key_pos >= 0) & (key_pos < length), 0.0, NEG_INF)
            blocks[i] = (qt, kband, vbt, pen)
        return blocks[i]

    def scores(tile):
        i, h = tile
        qt, kband, _, _ = block(i)
        rhs = jnp.where(row_head == h, qt, jnp.zeros((), BF16))
        return jnp.dot(kband, rhs, preferred_element_type=F32)

    heads = range(B_HEADS_PER_GROUP)
    nblk = length // NQ
    sts = [scores((0, h)) for h in heads]
    for i in range(nblk):
        sts_next = [scores((i + 1, h)) for h in heads] if i + 1 < nblk else None
        _, _, vbt, pen = block(i)
        sts = [sts[h] + bm_ref[h] for h in heads]
        if pen is not None:
            sts = [st + pen for st in sts]
        ms = [jnp.max(st, axis=0, keepdims=True) for st in sts]
        ps = [jnp.exp2(st - m).astype(BF16) for st, m in zip(sts, ms)]
        accs = [jnp.dot(jnp.concatenate([vbt[h * HEAD_DIM:(h + 1) * HEAD_DIM], ones], axis=0), ps[h],
                        preferred_element_type=F32) for h in heads]
        ls = [acc[HEAD_DIM:HEAD_DIM + 1] for acc in accs]
        rows = slice(i * NQ, (i + 1) * NQ)
        o_ref[rows, :] = jnp.concatenate([acc[0:HEAD_DIM] / l for acc, l in zip(accs, ls)], axis=0).T
        lse_ref[rows, :] = jnp.concatenate(
            [jnp.broadcast_to((m + jnp.log2(l)) * LN2, (HEAD_DIM, NQ)) for m, l in zip(ms, ls)], axis=0).T
        del blocks[i]
        sts = sts_next


def _attn_b_call(qg, kg, vg, bm, r):
    b, length, _ = qg.shape
    spec = pl.BlockSpec((None, length, B_OUT_WIDTH), lambda i, j: (i, 0, j))
    out_sds = jax.ShapeDtypeStruct((b, length, r * B_OUT_WIDTH), F32)
    return pl.pallas_call(
        _attn_b_body,
        grid=(b, r),
        in_specs=[spec, spec, spec,
                  pl.BlockSpec((B_HEADS_PER_GROUP, BAND, NQ), lambda i, j: (0, 0, 0))],
        out_specs=[spec, spec],
        out_shape=[out_sds, out_sds],
        scratch_shapes=[pltpu.VMEM((length + 2 * HALO, B_OUT_WIDTH), BF16),
                        pltpu.VMEM((length + 2 * HALO, B_OUT_WIDTH), BF16)],
        compiler_params=_params(("arbitrary", "arbitrary")),
        name=f"attn_b_r{r}",
    )(qg, kg, vg, bm)


def _t5_bucket_np(rel):
    half = N_BUCKETS // 2
    exact = half // 2
    n = np.abs(rel)
    big = exact + (np.log(np.maximum(n, 1).astype(np.float64) / exact)
                   / math.log(MAX_DISTANCE / exact) * (half - exact)).astype(np.int64)
    big = np.minimum(big, half - 1)
    return np.where(rel > 0, half, 0) + np.where(n < exact, n, big)


def _bias_mask_table(rel_bias, gi, r):
    span = BAND + NQ - 1
    rel = np.arange(span) - (NQ - 1) - HALO
    tab = rel_bias[:, gi * B_HEADS_PER_GROUP:(gi + 1) * B_HEADS_PER_GROUP].astype(F32) * LOG2E
    u = jnp.where(jnp.asarray(np.abs(rel) <= HALO)[:, None],
                  tab[jnp.asarray(_t5_bucket_np(rel * r), jnp.int32)], NEG_INF).T
    hankel = jnp.tile(u, (1, NQ + 1))[:, :NQ * (span + 1)].reshape(-1, NQ, span + 1)
    return jnp.transpose(hankel[:, ::-1, :BAND], (0, 2, 1))


def _layer_norm(h, g, b):
    mu = jnp.mean(h, axis=-1, keepdims=True)
    c = h - mu
    var = jnp.mean(c * c, axis=-1, keepdims=True)
    return c * lax.rsqrt(var + LN_EPS) * g + b


def _split_bf16(a):
    hi = a.astype(BF16)
    return hi, (a - hi.astype(F32)).astype(BF16)


def _merge_body(x_ref, ya_ref, o1_ref, o2_ref, o3_ref, l1_ref, l2_ref, l3_ref, gates_ref,
                wa_ref, wbr_ref, wo_ref, g_ref, b_ref, wrh_ref, wrl_ref, br_ref, out_ref, *tok_sc):
    def token_major(ref, sc, r):
        if r == 1:
            return ref[...]
        slabs = B_OUT_WIDTH // LANES
        for j in range(r):
            for c in range(slabs):
                lanes = slice(j * B_OUT_WIDTH + c * LANES, j * B_OUT_WIDTH + (c + 1) * LANES)
                sc[c, pl.ds(j, TM // r, stride=r), :] = ref[:, lanes]
        return jnp.concatenate([sc[c] for c in range(slabs)], axis=1)

    dil = [r for _, r in DILATED_PAIRS]
    o1, o2, o3 = (token_major(ref, sc, r) for ref, sc, r in zip((o1_ref, o2_ref, o3_ref), tok_sc[0:3], dil))
    l1, l2, l3 = (token_major(ref, sc, r) for ref, sc, r in zip((l1_ref, l2_ref, l3_ref), tok_sc[3:6], dil))
    mx = jnp.maximum(jnp.maximum(l1, l2), l3)
    e1, e2, e3 = jnp.exp(l1 - mx), jnp.exp(l2 - mx), jnp.exp(l3 - mx)
    yb = (e1 * o1 + e2 * o2 + e3 * o3) / (e1 + e2 + e3)
    ya_p = jnp.dot(ya_ref[...], wa_ref[...], preferred_element_type=F32)
    yb_p = jnp.dot(yb.astype(BF16), wbr_ref[...], preferred_element_type=F32)
    merged = (gates_ref[:, 0:D_MODEL].astype(F32) * ya_p
              + gates_ref[:, D_MODEL:2 * D_MODEL].astype(F32) * yb_p)
    h = ALPHA * x_ref[...] + jnp.dot(merged.astype(BF16), wo_ref[...], preferred_element_type=F32)
    x1 = _layer_norm(h, g_ref[...], b_ref[...])
    out_ref[:, 0:D_MODEL] = x1

    xh, xl = _split_bf16(x1)
    logits = (jnp.dot(xh, wrh_ref[...], preferred_element_type=F32)
              + jnp.dot(xl, wrh_ref[...], preferred_element_type=F32)
              + jnp.dot(xh, wrl_ref[...], preferred_element_type=F32)) + br_ref[...]
    col = lambda i: logits[:, i:i + 1]
    lg = [col(i) for i in range(N_GROUPS)]
    gmax = functools.reduce(jnp.maximum, lg)
    top_p = 1.0 / functools.reduce(lambda a, c: a + c, [jnp.exp(v - gmax) for v in lg])
    gsel = jnp.full_like(gmax, N_GROUPS - 1).astype(jnp.int32)
    for i in reversed(range(N_GROUPS - 1)):
        gsel = jnp.where(lg[i] >= gmax, i, gsel)
    sel = []
    for e in range(EXPERTS_PER_GROUP):
        v = col(N_GROUPS + (N_GROUPS - 1) * EXPERTS_PER_GROUP + e)
        for gi in reversed(range(N_GROUPS - 1)):
            v = jnp.where(gsel == gi, col(N_GROUPS + gi * EXPERTS_PER_GROUP + e), v)
        sel.append(v)
    rank = []
    for e in range(EXPERTS_PER_GROUP):
        rk = jnp.zeros_like(gsel)
        for j in range(EXPERTS_PER_GROUP):
            if j != e:
                ahead = (sel[j] > sel[e]) | ((sel[j] == sel[e]) & (j < e))
                rk = rk + ahead.astype(jnp.int32)
        rank.append(rk)
    pick = lambda k: functools.reduce(
        lambda a, c: a + c, [jnp.where(rank[e] == k, sel[e], 0.0) for e in range(EXPERTS_PER_GROUP)])
    t = jnp.exp(pick(1) - pick(0))
    w_first = top_p / (1.0 + t)
    w_second = top_p * t / (1.0 + t)
    chosen = [rank[e] < 2 for e in range(EXPERTS_PER_GROUP)]
    weight = [jnp.where(rank[e] == 0, w_first, w_second) for e in range(EXPERTS_PER_GROUP)]
    pair = jnp.zeros_like(gsel)
    w_lo = jnp.zeros_like(top_p)
    w_hi = jnp.zeros_like(top_p)
    for pi, (a, c) in enumerate(PAIRS):
        hit = chosen[a] & chosen[c]
        pair = jnp.where(hit, pi, pair)
        w_lo = jnp.where(hit, weight[a], w_lo)
        w_hi = jnp.where(hit, weight[c], w_hi)
    bucket = (gsel * len(PAIRS) + pair).astype(F32)
    lane = lax.broadcasted_iota(jnp.int32, (x1.shape[0], ROUTE_COLS), 1)
    out_ref[:, D_MODEL:] = jnp.where(lane == 0, bucket,
                                     jnp.where(lane == 1, w_lo, jnp.where(lane == 2, w_hi, 0.0)))


def _merge_call(x, ya, os, ls, gates, wa, wbr, wo, g, b, wrh, wrl, br):
    t, d = x.shape
    tok = lambda c: pl.BlockSpec((TM, c), lambda i: (i, 0))
    const = lambda a: pl.BlockSpec(a.shape, lambda i: (0,) * a.ndim)
    consts = (wa, wbr, wo, g, b, wrh, wrl, br)
    dil = [pl.BlockSpec((TM // r, r * B_OUT_WIDTH), lambda i: (i, 0)) for _, r in DILATED_PAIRS]
    return pl.pallas_call(
        _merge_body,
        grid=(t // TM,),
        in_specs=[tok(d), tok(A_WIDTH)] + dil + dil + [tok(2 * D_MODEL)]
        + [const(a) for a in consts],
        out_specs=tok(d + ROUTE_COLS),
        out_shape=jax.ShapeDtypeStruct((t, d + ROUTE_COLS), F32),
        scratch_shapes=[pltpu.VMEM((B_OUT_WIDTH // LANES, TM, LANES), F32)] * (2 * N_DIL_GROUPS),
        compiler_params=_params(("arbitrary",)),
        name="merge_ln1_route",
    )(x, ya, *os, *ls, gates, *consts)


def _moe_body(nvalid_ref, elo_ref, ehi_ref, ids_ref, ids_next_ref, x_hbm, wil_ref, wih_ref, wol_ref,
              woh_ref, g_ref, b_ref, out_hbm, xbuf, obuf, sem_in, sem_out):
    del elo_ref, ehi_ref
    i = pl.program_id(0)
    ntiles = pl.num_programs(0)
    slot = i % 2
    n = nvalid_ref[i]

    def start_gather(ids, dst_slot):
        def body(kk, c):
            for u in range(DMA_UNROLL):
                k = kk * DMA_UNROLL + u
                pltpu.make_async_copy(x_hbm.at[pl.ds(ids[0, k], 1)], xbuf.at[dst_slot, pl.ds(k, 1)],
                                      sem_in.at[dst_slot]).start()
            return c
        lax.fori_loop(0, TMOE // DMA_UNROLL, body, 0)

    def scatter_row(k):
        return pltpu.make_async_copy(obuf.at[slot, pl.ds(k, 1)], out_hbm.at[pl.ds(ids_ref[0, k], 1)],
                                     sem_out.at[slot])

    def start_scatter(rows):
        def body(kk, c):
            for u in range(DMA_UNROLL):
                scatter_row(kk * DMA_UNROLL + u).start()
            return c
        full = rows // DMA_UNROLL
        lax.fori_loop(0, full, body, 0)
        lax.fori_loop(full * DMA_UNROLL, rows, lambda k, c: (scatter_row(k).start(), c)[1], 0)

    def wait_scatter(s, rows):
        def wait_rows(count):
            pltpu.make_async_copy(obuf.at[s, pl.ds(0, count)], out_hbm.at[pl.ds(0, count)],
                                  sem_out.at[s]).wait()
        full = pl.multiple_of((rows // F32_SUBLANES) * F32_SUBLANES, F32_SUBLANES)

        @pl.when(full > 0)
        def _():
            wait_rows(full)

        lax.fori_loop(full, rows, lambda k, c: (wait_rows(1), c)[1], 0)

    @pl.when((i == 0) & (n > 0))
    def _():
        start_gather(ids_ref, 0)

    @pl.when(n > 0)
    def _():
        pltpu.make_async_copy(x_hbm.at[pl.ds(0, TMOE)], xbuf.at[slot], sem_in.at[slot]).wait()

    @pl.when(i + 1 < ntiles)
    def _():
        @pl.when(nvalid_ref[i + 1] > 0)
        def _():
            start_gather(ids_next_ref, 1 - slot)

    @pl.when(i >= 2)
    def _():
        rows = nvalid_ref[i - 2]

        @pl.when(rows > 0)
        def _():
            wait_scatter(slot, rows)

    @pl.when(n > 0)
    def _():
        xt = xbuf[slot, :, 0:D_MODEL]
        xb = xt.astype(BF16)

        def expert(wi_ref, wo_ref):
            hcat = jnp.dot(xb, wi_ref[...], preferred_element_type=F32)
            a, u = hcat[:, 0:D_EXPERT], hcat[:, D_EXPERT:]
            act = (a * jax.nn.sigmoid(a) * u).astype(BF16)
            return jnp.dot(act, wo_ref[...], preferred_element_type=F32)

        moe = (xbuf[slot, :, D_MODEL + 1:D_MODEL + 2] * expert(wil_ref, wol_ref)
               + xbuf[slot, :, D_MODEL + 2:D_MODEL + 3] * expert(wih_ref, woh_ref))
        obuf[slot] = _layer_norm(ALPHA * xt + moe, g_ref[...], b_ref[...])
        start_scatter(n)

    @pl.when(i == ntiles - 1)
    def _():
        @pl.when(i >= 1)
        def _():
            rows = nvalid_ref[i - 1]

            @pl.when(rows > 0)
            def _():
                wait_scatter(1 - slot, rows)

        @pl.when(n > 0)
        def _():
            wait_scatter(slot, n)


def _moe_call(x1ext, nvalid, elo, ehi, ids, wei, weo, g, b):
    t = x1ext.shape[0]
    ntiles = nvalid.shape[0]
    wi_spec = lambda e: pl.BlockSpec((None, D_MODEL, 2 * D_EXPERT), lambda i, nv, lo, hi: (e(lo, hi)[i], 0, 0))
    wo_spec = lambda e: pl.BlockSpec((None, D_EXPERT, D_MODEL), lambda i, nv, lo, hi: (e(lo, hi)[i], 0, 0))
    first = lambda lo, hi: lo
    second = lambda lo, hi: hi
    vec = pl.BlockSpec((1, D_MODEL), lambda i, nv, lo, hi: (0, 0))
    grid_spec = pltpu.PrefetchScalarGridSpec(
        num_scalar_prefetch=3,
        grid=(ntiles,),
        in_specs=[
            pl.BlockSpec((None, 1, TMOE), lambda i, nv, lo, hi: (i, 0, 0), memory_space=pltpu.SMEM),
            pl.BlockSpec((None, 1, TMOE), lambda i, nv, lo, hi: (jnp.minimum(i + 1, ntiles - 1), 0, 0),
                         memory_space=pltpu.SMEM),
            pl.BlockSpec(memory_space=pl.ANY),
            wi_spec(first), wi_spec(second), wo_spec(first), wo_spec(second), vec, vec,
        ],
        out_specs=pl.BlockSpec(memory_space=pl.ANY),
        scratch_shapes=[pltpu.VMEM((2, TMOE, D_MODEL + ROUTE_COLS), F32),
                        pltpu.VMEM((2, TMOE, D_MODEL), F32),
                        pltpu.SemaphoreType.DMA((2,)),
                        pltpu.SemaphoreType.DMA((2,))],
    )
    return pl.pallas_call(
        _moe_body,
        grid_spec=grid_spec,
        out_shape=jax.ShapeDtypeStruct((t, D_MODEL), F32),
        compiler_params=_params(("arbitrary",)),
        name="moe_ln2",
    )(nvalid, elo, ehi, ids, ids, x1ext, wei, wei, weo, weo, g, b)


def _route_tiles(bucket):
    t = bucket.shape[0]
    ntiles = t // TMOE + N_ROUTE_BUCKETS
    order = jnp.argsort(bucket, stable=True).astype(jnp.int32)
    counts = jnp.sum(bucket[:, None] == jnp.arange(N_ROUTE_BUCKETS, dtype=jnp.int32)[None, :],
                     axis=0, dtype=jnp.int32)
    row_start = jnp.cumsum(counts) - counts
    tiles_per = (counts + TMOE - 1) // TMOE
    tile_end = jnp.cumsum(tiles_per)
    total = tile_end[-1]
    tile = jnp.arange(ntiles, dtype=jnp.int32)
    used = tile < total
    bkt = jnp.searchsorted(tile_end, jnp.minimum(tile, total - 1), side="right").astype(jnp.int32)
    local = tile - (tile_end - tiles_per)[bkt]
    row0 = row_start[bkt] + local * TMOE
    nvalid = jnp.where(used, jnp.clip(counts[bkt] - local * TMOE, 0, TMOE), 0).astype(jnp.int32)
    pairs = jnp.asarray(PAIRS, jnp.int32)
    grp = bkt // len(PAIRS)
    elo = grp * EXPERTS_PER_GROUP + pairs[bkt % len(PAIRS), 0]
    ehi = grp * EXPERTS_PER_GROUP + pairs[bkt % len(PAIRS), 1]
    idx = jnp.minimum(row0[:, None] + jnp.arange(TMOE, dtype=jnp.int32)[None, :], t - 1)
    ids = order[idx].reshape(ntiles, 1, TMOE)
    return nvalid, elo.astype(jnp.int32), ehi.astype(jnp.int32), ids


def _prepare(w_in, b_gate, q_gain, k_gain, rel_bias, w_branch_a, w_branch_b, w_out, ln1_g, ln1_b,
             w_route_group, b_route_group, w_route_expert, b_route_expert, w_expert_in,
             w_expert_out, ln2_g, ln2_b, seq):
    row = lambda v: v.reshape(1, -1).astype(F32)
    w = w_in[0]
    wr = jnp.concatenate([w_route_group[0]]
                         + [w_route_expert[0, gi] for gi in range(N_GROUPS)], axis=1)
    wr = jnp.pad(wr.astype(F32), ((0, 0), (0, ROUTE_COLS - wr.shape[1])))
    br = jnp.concatenate([b_route_group[0], b_route_expert[0].reshape(-1)])
    br = jnp.pad(br.astype(F32), (0, ROUTE_COLS - br.shape[0])).reshape(1, ROUTE_COLS)
    wrh, wrl = _split_bf16(wr)
    scale = HEAD_DIM ** -0.5
    wb = w[:, DT_ROWS:]
    wb = jnp.concatenate([wb[:, 0:B_WIDTH] * (scale * LOG2E), wb[:, B_WIDTH:]], axis=1)
    return dict(
        wdt=w[:, 0:DT_ROWS].T.astype(BF16), wb=wb.astype(BF16), bg=row(b_gate[0]),
        rope_q=_rope_tables(seq, q_gain[0], scale * LOG2E), rope_k=_rope_tables(seq, k_gain[0], 1.0),
        score_bound=_score_bound(q_gain[0], k_gain[0]),
        bm=[_bias_mask_table(rel_bias, gi, r) for gi, (_, r) in enumerate(DILATED_PAIRS)],
        wa=w_branch_a[0].astype(BF16), wbr=w_branch_b[0].astype(BF16), wo=w_out[0].astype(BF16),
        ln1=(row(ln1_g[0]), row(ln1_b[0])), wrh=wrh, wrl=wrl, br=br,
        wei=w_expert_in[0].astype(BF16), weo=w_expert_out[0].astype(BF16),
        ln2=(row(ln2_g[0]), row(ln2_b[0])),
    )


def _trunk(x, p):
    b, s, d = x.shape
    qt, k, vt, *qkv_b, gates = _inproj_call(x, p["wdt"], p["wb"], p["bg"], *p["rope_q"], *p["rope_k"])
    ya = _attn_a_call(*p["score_bound"], qt, k, vt)
    os, ls = [], []
    for gi, (_, r) in enumerate(DILATED_PAIRS):
        qg, kg, vg = (qkv_b[i * N_DIL_GROUPS + gi] for i in range(3))
        o, lse = _attn_b_call(qg, kg, vg, p["bm"][gi], r)
        os.append(o.reshape(b * s // r, r * B_OUT_WIDTH))
        ls.append(lse.reshape(b * s // r, r * B_OUT_WIDTH))
    x1ext = _merge_call(x.reshape(b * s, d), ya.reshape(b * s, A_WIDTH), os, ls,
                        gates.reshape(b * s, 2 * D_MODEL), p["wa"], p["wbr"], p["wo"], *p["ln1"],
                        p["wrh"], p["wrl"], p["br"])
    nvalid, elo, ehi, ids = _route_tiles(x1ext[:, D_MODEL].astype(jnp.int32))
    y = _moe_call(x1ext, nvalid, elo, ehi, ids, p["wei"], p["weo"], *p["ln2"])
    return y.reshape(b, s, d)


def kernel(x_prompt, x_sample, w_in, b_gate, q_gain, k_gain, rel_bias, w_branch_a, w_branch_b, w_out,
           ln1_g, ln1_b, w_route_group, b_route_group, w_route_expert, b_route_expert, w_expert_in,
           w_expert_out, ln2_g, ln2_b):
    assert x_prompt.shape[1] == x_sample.shape[1]
    p = _prepare(w_in, b_gate, q_gain, k_gain, rel_bias, w_branch_a, w_branch_b, w_out, ln1_g, ln1_b,
                 w_route_group, b_route_group, w_route_expert, b_route_expert, w_expert_in,
                 w_expert_out, ln2_g, ln2_b, x_prompt.shape[1])
    return (_trunk(x_prompt, p), _trunk(x_sample, p))
```

```python
import functools
import math

import jax
import jax.numpy as jnp
import numpy as np
from jax import lax
from jax.experimental import pallas as pl
from jax.experimental.pallas import tpu as pltpu

F32 = jnp.float32
BF16 = jnp.bfloat16

D_MODEL = 1024
HEAD_DIM = 64
A_Q_HEADS = 8
A_KV_HEADS = 2
A_GROUP = A_Q_HEADS // A_KV_HEADS
GRID_W = 64
ROPE_AXIS_DIM = HEAD_DIM // 2
ROPE_THETA = 10000.0
DILATED_PAIRS = ((128, 1), (512, 4), (2048, 16))
N_DIL_GROUPS = len(DILATED_PAIRS)
B_HEADS_PER_GROUP = 4
N_BUCKETS = 32
MAX_DISTANCE = 1024
A_WIDTH = A_Q_HEADS * HEAD_DIM
A_KV_WIDTH = A_KV_HEADS * HEAD_DIM
B_WIDTH = B_HEADS_PER_GROUP * N_DIL_GROUPS * HEAD_DIM
B_OUT_WIDTH = B_HEADS_PER_GROUP * HEAD_DIM
DT_ROWS = A_WIDTH + 2 * A_KV_WIDTH
N_GROUPS = 4
EXPERTS_PER_GROUP = 4
N_EXPERTS = N_GROUPS * EXPERTS_PER_GROUP
D_EXPERT = 512
PAIRS = ((0, 1), (0, 2), (0, 3), (1, 2), (1, 3), (2, 3))
N_ROUTE_BUCKETS = N_GROUPS * len(PAIRS)
ALPHA = 2.0 ** 0.25
RMS_EPS = 1e-6
LN_EPS = 1e-5
NEG_INF = -1e30

LANES = 128
TM = 512
TMG = 512
TSUB = 128
NQ = 256
BF16_SUBLANES = 16
F32_SUBLANES = 8
VT_ROWS = HEAD_DIM + BF16_SUBLANES
LOG2E = 1.4426950408889634
LN2 = 0.6931471805599453
MAX_SAFE_SHIFT = 40.0
HALO = 64
BAND = NQ + 2 * HALO
TMOE = 256
DMA_UNROLL = 8
ROUTE_COLS = LANES
VMEM_LIMIT = 56 * 1024 * 1024


def _params(sem):
    return pltpu.CompilerParams(dimension_semantics=sem, vmem_limit_bytes=VMEM_LIMIT)


def _swap16(x):
    return jnp.concatenate([x[16:32], x[0:16], x[48:64], x[32:48]], axis=0)


def _norm_rope(xh, ta, tb):
    r = lax.rsqrt(jnp.mean(xh * xh, axis=0, keepdims=True) + RMS_EPS)
    return (xh * ta + _swap16(xh) * tb) * r


def _inproj_body(x_ref, wdt_ref, wb_ref, bg_ref, taq_ref, tbq_ref, tak_ref, tbk_ref,
                 qt_ref, k_ref, vt_ref, *rest):
    b_refs, gates_ref, tmp_sc = rest[:3 * N_DIL_GROUPS], rest[3 * N_DIL_GROUPS], rest[-1]
    xb = x_ref[...].astype(BF16)
    dt = lax.dot_general(wdt_ref[...], xb, (((1,), (1,)), ((), ())),
                         preferred_element_type=F32)
    taq, tbq = taq_ref[...], tbq_ref[...]
    for h in range(A_Q_HEADS):
        sl = slice(h * HEAD_DIM, (h + 1) * HEAD_DIM)
        qt_ref[sl, :] = _norm_rope(dt[sl], taq, tbq).astype(BF16)
    tak, tbk = tak_ref[...], tbk_ref[...]
    kt = jnp.concatenate(
        [_norm_rope(dt[A_WIDTH + h * HEAD_DIM:A_WIDTH + (h + 1) * HEAD_DIM], tak, tbk)
         for h in range(A_KV_HEADS)], axis=0)
    k_ref[...] = kt.T.astype(BF16)
    for h in range(A_KV_HEADS):
        v0 = A_WIDTH + A_KV_WIDTH + h * HEAD_DIM
        vt_ref[h, 0:HEAD_DIM, :] = dt[v0:v0 + HEAD_DIM].astype(BF16)
        vt_ref[h, HEAD_DIM:, :] = jnp.ones((VT_ROWS - HEAD_DIM, TM), BF16)
    for i in range(3):
        res = jnp.dot(xb, wb_ref[:, i * B_WIDTH:(i + 1) * B_WIDTH], preferred_element_type=F32)
        for c in range(B_WIDTH // LANES):
            tmp_sc[c] = res[:, c * LANES:(c + 1) * LANES]
        for gi, (_, r) in enumerate(DILATED_PAIRS):
            ref = b_refs[i * N_DIL_GROUPS + gi]
            for j in range(r):
                for half in range(B_OUT_WIDTH // LANES):
                    c = gi * (B_OUT_WIDTH // LANES) + half
                    lanes = slice(j * B_OUT_WIDTH + half * LANES, j * B_OUT_WIDTH + (half + 1) * LANES)
                    ref[:, lanes] = tmp_sc[c, pl.ds(j, TM // r, stride=r), :].astype(BF16)
    g0 = 3 * B_WIDTH
    for i in range(2):
        cols = slice(g0 + i * D_MODEL, g0 + (i + 1) * D_MODEL)
        pre = jnp.dot(xb, wb_ref[:, cols], preferred_element_type=F32)
        gates_ref[:, i * D_MODEL:(i + 1) * D_MODEL] = jax.nn.sigmoid(
            pre + bg_ref[:, i * D_MODEL:(i + 1) * D_MODEL]).astype(BF16)


def _inproj_call(x, wdt, wb, bg, taq, tbq, tak, tbk):
    b, s, d = x.shape
    nt = s // TM
    wcols = wb.shape[1]
    tab = pl.BlockSpec((HEAD_DIM, TM), lambda p, i: (0, p))
    const = lambda shape: pl.BlockSpec(shape, lambda p, i: (0,) * len(shape))
    tok = lambda c: pl.BlockSpec((None, TM, c), lambda p, i: (i, p, 0))
    dil = [r for _ in range(3) for _, r in DILATED_PAIRS]
    return pl.pallas_call(
        _inproj_body,
        grid=(nt, b),
        in_specs=[tok(d), const((DT_ROWS, d)), const((d, wcols)), const((1, 2 * D_MODEL)),
                  tab, tab, tab, tab],
        out_specs=[
            pl.BlockSpec((None, A_WIDTH, TM), lambda p, i: (i, 0, p)),
            tok(A_KV_WIDTH),
            pl.BlockSpec((None, None, A_KV_HEADS, VT_ROWS, TM), lambda p, i: (i, p, 0, 0, 0)),
        ] + [pl.BlockSpec((None, TM // r, r * B_OUT_WIDTH), lambda p, i: (i, p, 0)) for r in dil]
        + [tok(2 * D_MODEL)],
        out_shape=[
            jax.ShapeDtypeStruct((b, A_WIDTH, s), BF16),
            jax.ShapeDtypeStruct((b, s, A_KV_WIDTH), BF16),
            jax.ShapeDtypeStruct((b, nt, A_KV_HEADS, VT_ROWS, TM), BF16),
        ] + [jax.ShapeDtypeStruct((b, s // r, r * B_OUT_WIDTH), BF16) for r in dil]
        + [jax.ShapeDtypeStruct((b, s, 2 * D_MODEL), BF16)],
        scratch_shapes=[pltpu.VMEM((B_WIDTH // LANES, TM, LANES), F32)],
        compiler_params=_params(("arbitrary", "arbitrary")),
        name="inproj",
    )(x, wdt, wb, bg, taq, tbq, tak, tbk)


def _rope_tables(s, gain, scale):
    t = np.arange(s)
    inv = ROPE_THETA ** (-np.arange(0, ROPE_AXIS_DIM, 2, dtype=np.float64) / ROPE_AXIS_DIM)
    pos = np.stack([t // GRID_W, t // GRID_W, t % GRID_W, t % GRID_W], axis=0)
    ang = (pos[:, None, :] * inv[None, :, None]).reshape(HEAD_DIM, s)
    sign = np.repeat(np.array([-1.0, 1.0, -1.0, 1.0]), 16)[:, None]
    partner = np.arange(HEAD_DIM) ^ 16
    cos = jnp.asarray(np.cos(ang), F32)
    sin = jnp.asarray(sign * np.sin(ang), F32)
    g = gain.astype(F32) * scale
    return g[:, None] * cos, g[partner][:, None] * sin


def _attn_a_body(fast_ref, kbound_ref, qt_ref, k_ref, vt_ref, o_ref, rhs_sc, acc_sc, ot_sc):
    kv = pl.program_id(1)
    nchunks = vt_ref.shape[0]
    tk = vt_ref.shape[2]
    row = lax.broadcasted_iota(jnp.int32, (A_KV_WIDTH, NQ), 0)
    own = (row // HEAD_DIM) == kv
    for g in range(A_GROUP):
        qg = qt_ref[g * HEAD_DIM:(g + 1) * HEAD_DIM, :]
        rhs_sc[g] = jnp.where(own, jnp.concatenate([qg, qg], axis=0), jnp.zeros((), BF16))

    def load_chunk(c):
        return k_ref[pl.ds(pl.multiple_of(c * tk, tk), tk), :], vt_ref[c]

    @pl.when(fast_ref[0] == 1)
    def _():
        shift = []
        for g in range(A_GROUP):
            qg = qt_ref[g * HEAD_DIM:(g + 1) * HEAD_DIM, :].astype(F32)
            shift.append(jnp.sqrt(jnp.sum(qg * qg, axis=0, keepdims=True)) * kbound_ref[0])

        def scores(tile):
            c, g = tile
            return jnp.dot(k_ref[c * tk:(c + 1) * tk, :], rhs_sc[g], preferred_element_type=F32)

        heads = range(A_GROUP)
        accs = [None] * A_GROUP
        sts = [scores((0, g)) for g in heads]
        for c in range(nchunks):
            sts_next = [scores((c + 1, g)) for g in heads] if c + 1 < nchunks else None
            ps = [jnp.exp2(sts[g] - shift[g]).astype(BF16) for g in heads]
            pvs = [jnp.dot(vt_ref[c], ps[g], preferred_element_type=F32) for g in heads]
            accs = [pvs[g] if accs[g] is None else accs[g] + pvs[g] for g in heads]
            sts = sts_next
        for g in heads:
            acc_sc[g] = accs[g]

    @pl.when(fast_ref[0] == 0)
    def _():
        acc_sc[...] = jnp.zeros_like(acc_sc)

        def chunk(c, ms):
            kc, vc = load_chunk(c)
            new_ms = []
            for g in range(A_GROUP):
                st = jnp.dot(kc, rhs_sc[g], preferred_element_type=F32)
                m_new = jnp.maximum(ms[g], jnp.max(st, axis=0, keepdims=True))
                alpha = jnp.exp2(ms[g] - m_new)
                p = jnp.exp2(st - m_new).astype(BF16)
                acc_sc[g] = alpha * acc_sc[g] + jnp.dot(vc, p, preferred_element_type=F32)
                new_ms.append(m_new)
            return tuple(new_ms)

        lax.fori_loop(0, nchunks, chunk,
                      tuple(jnp.full((1, NQ), NEG_INF, F32) for _ in range(A_GROUP)))

    for g in range(A_GROUP):
        ot_sc[g * HEAD_DIM:(g + 1) * HEAD_DIM, :] = (
            acc_sc[g, 0:HEAD_DIM, :] / acc_sc[g, HEAD_DIM:HEAD_DIM + 1, :])
    o_ref[...] = ot_sc[...].T.astype(BF16)


def _attn_a_call(fast, kbound, qt, k, vt):
    b, _, s = qt.shape
    nchunks, tk = vt.shape[1], vt.shape[4]
    gw = A_GROUP * HEAD_DIM
    grid_spec = pltpu.PrefetchScalarGridSpec(
        num_scalar_prefetch=2,
        grid=(b, A_KV_HEADS, s // NQ),
        in_specs=[
            pl.BlockSpec((None, gw, NQ), lambda i, h, j, *_: (i, h, j)),
            pl.BlockSpec((None, s, A_KV_WIDTH), lambda i, h, j, *_: (i, 0, 0)),
            pl.BlockSpec((None, nchunks, None, VT_ROWS, tk), lambda i, h, j, *_: (i, 0, h, 0, 0)),
        ],
        out_specs=pl.BlockSpec((None, NQ, gw), lambda i, h, j, *_: (i, j, h)),
        scratch_shapes=[pltpu.VMEM((A_GROUP, A_KV_WIDTH, NQ), BF16),
                        pltpu.VMEM((A_GROUP, VT_ROWS, NQ), F32),
                        pltpu.VMEM((gw, NQ), F32)],
    )
    return pl.pallas_call(
        _attn_a_body,
        grid_spec=grid_spec,
        out_shape=jax.ShapeDtypeStruct((b, s, A_WIDTH), BF16),
        compiler_params=_params(("arbitrary", "arbitrary", "arbitrary")),
        name="attn_a",
    )(fast, kbound, qt, k, vt)


def _score_bound(q_gain, k_gain):
    kb = math.sqrt(HEAD_DIM) * jnp.max(jnp.abs(k_gain.astype(F32)))
    qb = math.sqrt(HEAD_DIM) * jnp.max(jnp.abs(q_gain.astype(F32))) * (HEAD_DIM ** -0.5) * LOG2E
    fast = (qb * kb <= MAX_SAFE_SHIFT).astype(jnp.int32)
    return fast.reshape(1), kb.reshape(1)


def _attn_b_body(q_ref, k_ref, v_ref, bm_ref, o_ref, lse_ref, kp_sc, vp_sc):
    length = q_ref.shape[0]
    zeros = jnp.zeros((HALO, B_OUT_WIDTH), BF16)
    for sc, ref in ((kp_sc, k_ref), (vp_sc, v_ref)):
        sc[0:HALO, :] = zeros
        sc[HALO:HALO + length, :] = ref[...]
        sc[HALO + length:, :] = zeros
    row_head = lax.broadcasted_iota(jnp.int32, (B_OUT_WIDTH, NQ), 0) // HEAD_DIM
    ones = jnp.ones((VT_ROWS - HEAD_DIM, BAND), BF16)
    blocks = {}

    def block(i):
        if i not in blocks:
            q0 = i * NQ
            qt = q_ref[q0:q0 + NQ, :].astype(F32).T.astype(BF16)
            kband = kp_sc[q0:q0 + BAND, :]
            vbt = vp_sc[q0:q0 + BAND, :].astype(F32).T.astype(BF16)
            pen = None
            if q0 - HALO < 0 or q0 - HALO + BAND > length:
                key_pos = lax.broadcasted_iota(jnp.int32, (BAND, NQ), 0) + (q0 - HALO)
                pen = jnp.where((key_pos >= 0) & (key_pos < length), 0.0, NEG_INF)
            blocks[i] = (qt, kband, vbt, pen)
        return blocks[i]

    def scores(tile):
        i, h = tile
        qt, kband, _, _ = block(i)
        rhs = jnp.where(row_head == h, qt, jnp.zeros((), BF16))
        return jnp.dot(kband, rhs, preferred_element_type=F32)

    heads = range(B_HEADS_PER_GROUP)
    nblk = length // NQ
    sts = [scores((0, h)) for h in heads]
    for i in range(nblk):
        sts_next = [scores((i + 1, h)) for h in heads] if i + 1 < nblk else None
        _, _, vbt, pen = block(i)
        sts = [sts[h] + bm_ref[h] for h in heads]
        if pen is not None:
            sts = [st + pen for st in sts]
        ms = [jnp.max(st, axis=0, keepdims=True) for st in sts]
        ps = [jnp.exp2(st - m).astype(BF16) for st, m in zip(sts, ms)]
        accs = [jnp.dot(jnp.concatenate([vbt[h * HEAD_DIM:(h + 1) * HEAD_DIM], ones], axis=0), ps[h],
                        preferred_element_type=F32) for h in heads]
        ls = [acc[HEAD_DIM:HEAD_DIM + 1] for acc in accs]
        rows = slice(i * NQ, (i + 1) * NQ)
        o_ref[rows, :] = jnp.concatenate([acc[0:HEAD_DIM] / l for acc, l in zip(accs, ls)], axis=0).T
        lse_ref[rows, :] = jnp.concatenate(
            [jnp.broadcast_to((m + jnp.log2(l)) * LN2, (HEAD_DIM, NQ)) for m, l in zip(ms, ls)], axis=0).T
        del blocks[i]
        sts = sts_next


def _attn_b_call(qg, kg, vg, bm, r):
    b, length, _ = qg.shape
    spec = pl.BlockSpec((None, length, B_OUT_WIDTH), lambda i, j: (i, 0, j))
    out_sds = jax.ShapeDtypeStruct((b, length, r * B_OUT_WIDTH), F32)
    return pl.pallas_call(
        _attn_b_body,
        grid=(b, r),
        in_specs=[spec, spec, spec,
                  pl.BlockSpec((B_HEADS_PER_GROUP, BAND, NQ), lambda i, j: (0, 0, 0))],
        out_specs=[spec, spec],
        out_shape=[out_sds, out_sds],
        scratch_shapes=[pltpu.VMEM((length + 2 * HALO, B_OUT_WIDTH), BF16),
                        pltpu.VMEM((length + 2 * HALO, B_OUT_WIDTH), BF16)],
        compiler_params=_params(("arbitrary", "arbitrary")),
        name=f"attn_b_r{r}",
    )(qg, kg, vg, bm)


def _t5_bucket_np(rel):
    half = N_BUCKETS // 2
    exact = half // 2
    n = np.abs(rel)
    big = exact + (np.log(np.maximum(n, 1).astype(np.float64) / exact)
                   / math.log(MAX_DISTANCE / exact) * (half - exact)).astype(np.int64)
    big = np.minimum(big, half - 1)
    return np.where(rel > 0, half, 0) + np.where(n < exact, n, big)


def _bias_mask_table(rel_bias, gi, r):
    span = BAND + NQ - 1
    rel = np.arange(span) - (NQ - 1) - HALO
    tab = rel_bias[:, gi * B_HEADS_PER_GROUP:(gi + 1) * B_HEADS_PER_GROUP].astype(F32) * LOG2E
    u = jnp.where(jnp.asarray(np.abs(rel) <= HALO)[:, None],
                  tab[jnp.asarray(_t5_bucket_np(rel * r), jnp.int32)], NEG_INF).T
    hankel = jnp.tile(u, (1, NQ + 1))[:, :NQ * (span + 1)].reshape(-1, NQ, span + 1)
    return jnp.transpose(hankel[:, ::-1, :BAND], (0, 2, 1))


def _layer_norm(h, g, b):
    mu = jnp.mean(h, axis=-1, keepdims=True)
    c = h - mu
    var = jnp.mean(c * c, axis=-1, keepdims=True)
    return c * lax.rsqrt(var + LN_EPS) * g + b


def _split_bf16(a):
    hi = a.astype(BF16)
    return hi, (a - hi.astype(F32)).astype(BF16)


def _merge_body(x_ref, ya_ref, o1_ref, o2_ref, o3_ref, l1_ref, l2_ref, l3_ref, gates_ref,
                wa_ref, wbr_ref, wo_ref, g_ref, b_ref, wr2_ref, br_ref, out_ref, *tok_sc):
    slabs = B_OUT_WIDTH // LANES
    dil = [r for _, r in DILATED_PAIRS]
    b_refs = (o1_ref, o2_ref, o3_ref, l1_ref, l2_ref, l3_ref)
    for ref, sc, r in zip(b_refs, tok_sc, dil + dil):
        if r > 1:
            for j in range(r):
                for c in range(slabs):
                    lanes = slice(j * B_OUT_WIDTH + c * LANES, j * B_OUT_WIDTH + (c + 1) * LANES)
                    sc[c, pl.ds(j, TMG // r, stride=r), :] = ref[:, lanes]

    def token_major(which, rows):
        ref, sc, r = b_refs[which], tok_sc[which], (dil + dil)[which]
        if r == 1:
            return ref[rows, :]
        return jnp.concatenate([sc[c, rows, :] for c in range(slabs)], axis=1)

    subs = [slice(i * TSUB, (i + 1) * TSUB) for i in range(TMG // TSUB)]
    ybs = []
    for rows in subs:
        o1, o2, o3, l1, l2, l3 = (token_major(w, rows) for w in range(6))
        mx = jnp.maximum(jnp.maximum(l1, l2), l3)
        e1, e2, e3 = jnp.exp(l1 - mx), jnp.exp(l2 - mx), jnp.exp(l3 - mx)
        ybs.append(((e1 * o1 + e2 * o2 + e3 * o3) / (e1 + e2 + e3)).astype(BF16))
    ya_ps = [jnp.dot(ya_ref[rows, :], wa_ref[...], preferred_element_type=F32) for rows in subs]
    yb_ps = [jnp.dot(yb, wbr_ref[...], preferred_element_type=F32) for yb in ybs]
    mergeds = [(gates_ref[rows, 0:D_MODEL].astype(F32) * ya_p
                + gates_ref[rows, D_MODEL:2 * D_MODEL].astype(F32) * yb_p).astype(BF16)
               for rows, ya_p, yb_p in zip(subs, ya_ps, yb_ps)]
    hs = [ALPHA * x_ref[rows, :] + jnp.dot(m, wo_ref[...], preferred_element_type=F32)
          for rows, m in zip(subs, mergeds)]
    x1s = [_layer_norm(h, g_ref[...], b_ref[...]) for h in hs]
    for rows, x1 in zip(subs, x1s):
        out_ref[rows, 0:D_MODEL] = x1

    splits = [_split_bf16(x1) for x1 in x1s]
    boths = [jnp.dot(xh, wr2_ref[...], preferred_element_type=F32)
             + jnp.dot(xl, wr2_ref[...], preferred_element_type=F32) for xh, xl in splits]
    logits = jnp.concatenate(
        [(both[:, 0:ROUTE_COLS] + both[:, ROUTE_COLS:] + br_ref[...]).T for both in boths], axis=1)
    col = lambda i: logits[i:i + 1, :]
    lg = [col(i) for i in range(N_GROUPS)]
    gmax = functools.reduce(jnp.maximum, lg)
    top_p = 1.0 / functools.reduce(lambda a, c: a + c, [jnp.exp(v - gmax) for v in lg])
    gsel = jnp.full_like(gmax, N_GROUPS - 1).astype(jnp.int32)
    for i in reversed(range(N_GROUPS - 1)):
        gsel = jnp.where(lg[i] >= gmax, i, gsel)
    sel = []
    for e in range(EXPERTS_PER_GROUP):
        v = col(N_GROUPS + (N_GROUPS - 1) * EXPERTS_PER_GROUP + e)
        for gi in reversed(range(N_GROUPS - 1)):
            v = jnp.where(gsel == gi, col(N_GROUPS + gi * EXPERTS_PER_GROUP + e), v)
        sel.append(v)
    rank = []
    for e in range(EXPERTS_PER_GROUP):
        rk = jnp.zeros_like(gsel)
        for j in range(EXPERTS_PER_GROUP):
            if j != e:
                ahead = (sel[j] > sel[e]) | ((sel[j] == sel[e]) & (j < e))
                rk = rk + ahead.astype(jnp.int32)
        rank.append(rk)
    pick = lambda k: functools.reduce(
        lambda a, c: a + c, [jnp.where(rank[e] == k, sel[e], 0.0) for e in range(EXPERTS_PER_GROUP)])
    t = jnp.exp(pick(1) - pick(0))
    w_first = top_p / (1.0 + t)
    w_second = top_p * t / (1.0 + t)
    chosen = [rank[e] < 2 for e in range(EXPERTS_PER_GROUP)]
    weight = [jnp.where(rank[e] == 0, w_first, w_second) for e in range(EXPERTS_PER_GROUP)]
    pair = jnp.zeros_like(gsel)
    w_lo = jnp.zeros_like(top_p)
    w_hi = jnp.zeros_like(top_p)
    for pi, (a, c) in enumerate(PAIRS):
        hit = chosen[a] & chosen[c]
        pair = jnp.where(hit, pi, pair)
        w_lo = jnp.where(hit, weight[a], w_lo)
        w_hi = jnp.where(hit, weight[c], w_hi)
    bucket = (gsel * len(PAIRS) + pair).astype(F32)
    row = lax.broadcasted_iota(jnp.int32, (F32_SUBLANES, TMG), 0)
    head = jnp.where(row == 0, bucket, jnp.where(row == 1, w_lo, jnp.where(row == 2, w_hi, 0.0)))
    route_t = jnp.concatenate([head, jnp.zeros((ROUTE_COLS - F32_SUBLANES, TMG), F32)], axis=0)
    out_ref[:, D_MODEL:] = route_t.T


def _merge_call(x, ya, os, ls, gates, wa, wbr, wo, g, b, wr2, br):
    t, d = x.shape
    tok = lambda c: pl.BlockSpec((TMG, c), lambda i: (i, 0))
    const = lambda a: pl.BlockSpec(a.shape, lambda i: (0,) * a.ndim)
    consts = (wa, wbr, wo, g, b, wr2, br)
    dil = [pl.BlockSpec((TMG // r, r * B_OUT_WIDTH), lambda i: (i, 0)) for _, r in DILATED_PAIRS]
    return pl.pallas_call(
        _merge_body,
        grid=(t // TMG,),
        in_specs=[tok(d), tok(A_WIDTH)] + dil + dil + [tok(2 * D_MODEL)]
        + [const(a) for a in consts],
        out_specs=tok(d + ROUTE_COLS),
        out_shape=jax.ShapeDtypeStruct((t, d + ROUTE_COLS), F32),
        scratch_shapes=[pltpu.VMEM((B_OUT_WIDTH // LANES, TMG, LANES), F32)] * (2 * N_DIL_GROUPS),
        compiler_params=_params(("arbitrary",)),
        name="merge_ln1_route",
    )(x, ya, *os, *ls, gates, *consts)


def _moe_body(nvalid_ref, elo_ref, ehi_ref, ids_ref, ids_next_ref, x_hbm, wil_ref, wih_ref, wol_ref,
              woh_ref, g_ref, b_ref, out_hbm, xbuf, obuf, sem_in, sem_out):
    del elo_ref, ehi_ref
    i = pl.program_id(0)
    ntiles = pl.num_programs(0)
    slot = i % 2
    n = nvalid_ref[i]

    def start_gather(ids, dst_slot):
        def body(kk, c):
            for u in range(DMA_UNROLL):
                k = kk * DMA_UNROLL + u
                pltpu.make_async_copy(x_hbm.at[pl.ds(ids[0, k], 1)], xbuf.at[dst_slot, pl.ds(k, 1)],
                                      sem_in.at[dst_slot]).start()
            return c
        lax.fori_loop(0, TMOE // DMA_UNROLL, body, 0)

    def scatter_row(k):
        return pltpu.make_async_copy(obuf.at[slot, pl.ds(k, 1)], out_hbm.at[pl.ds(ids_ref[0, k], 1)],
                                     sem_out.at[slot])

    def start_scatter(rows):
        def body(kk, c):
            for u in range(DMA_UNROLL):
                scatter_row(kk * DMA_UNROLL + u).start()
            return c
        full = rows // DMA_UNROLL
        lax.fori_loop(0, full, body, 0)
        lax.fori_loop(full * DMA_UNROLL, rows, lambda k, c: (scatter_row(k).start(), c)[1], 0)

    def wait_scatter(s, rows):
        def wait_rows(count):
            pltpu.make_async_copy(obuf.at[s, pl.ds(0, count)], out_hbm.at[pl.ds(0, count)],
                                  sem_out.at[s]).wait()
        full = pl.multiple_of((rows // F32_SUBLANES) * F32_SUBLANES, F32_SUBLANES)

        @pl.when(full > 0)
        def _():
            wait_rows(full)

        lax.fori_loop(full, rows, lambda k, c: (wait_rows(1), c)[1], 0)

    @pl.when((i == 0) & (n > 0))
    def _():
        start_gather(ids_ref, 0)

    @pl.when(n > 0)
    def _():
        pltpu.make_async_copy(x_hbm.at[pl.ds(0, TMOE)], xbuf.at[slot], sem_in.at[slot]).wait()

    @pl.when(i + 1 < ntiles)
    def _():
        @pl.when(nvalid_ref[i + 1] > 0)
        def _():
            start_gather(ids_next_ref, 1 - slot)

    @pl.when(i >= 2)
    def _():
        rows = nvalid_ref[i - 2]

        @pl.when(rows > 0)
        def _():
            wait_scatter(slot, rows)

    @pl.when(n > 0)
    def _():
        xt = xbuf[slot, :, 0:D_MODEL]
        xb = xt.astype(BF16)

        hcats = [jnp.dot(xb, wi_ref[...], preferred_element_type=F32)
                 for wi_ref in (wil_ref, wih_ref)]
        acts = [(h[:, 0:D_EXPERT] * jax.nn.sigmoid(h[:, 0:D_EXPERT]) * h[:, D_EXPERT:]).astype(BF16)
                for h in hcats]
        y_lo, y_hi = (jnp.dot(act, wo_ref[...], preferred_element_type=F32)
                      for act, wo_ref in zip(acts, (wol_ref, woh_ref)))
        moe = (xbuf[slot, :, D_MODEL + 1:D_MODEL + 2] * y_lo
               + xbuf[slot, :, D_MODEL + 2:D_MODEL + 3] * y_hi)
        obuf[slot] = _layer_norm(ALPHA * xt + moe, g_ref[...], b_ref[...])
        start_scatter(n)

    @pl.when(i == ntiles - 1)
    def _():
        @pl.when(i >= 1)
        def _():
            rows = nvalid_ref[i - 1]

            @pl.when(rows > 0)
            def _():
                wait_scatter(1 - slot, rows)

        @pl.when(n > 0)
        def _():
            wait_scatter(slot, n)


def _moe_call(x1ext, nvalid, elo, ehi, ids, wei, weo, g, b):
    t = x1ext.shape[0]
    ntiles = nvalid.shape[0]
    wi_spec = lambda e: pl.BlockSpec((None, D_MODEL, 2 * D_EXPERT), lambda i, nv, lo, hi: (e(lo, hi)[i], 0, 0))
    wo_spec = lambda e: pl.BlockSpec((None, D_EXPERT, D_MODEL), lambda i, nv, lo, hi: (e(lo, hi)[i], 0, 0))
    first = lambda lo, hi: lo
    second = lambda lo, hi: hi
    vec = pl.BlockSpec((1, D_MODEL), lambda i, nv, lo, hi: (0, 0))
    grid_spec = pltpu.PrefetchScalarGridSpec(
        num_scalar_prefetch=3,
        grid=(ntiles,),
        in_specs=[
            pl.BlockSpec((None, 1, TMOE), lambda i, nv, lo, hi: (i, 0, 0), memory_space=pltpu.SMEM),
            pl.BlockSpec((None, 1, TMOE), lambda i, nv, lo, hi: (jnp.minimum(i + 1, ntiles - 1), 0, 0),
                         memory_space=pltpu.SMEM),
            pl.BlockSpec(memory_space=pl.ANY),
            wi_spec(first), wi_spec(second), wo_spec(first), wo_spec(second), vec, vec,
        ],
        out_specs=pl.BlockSpec(memory_space=pl.ANY),
        scratch_shapes=[pltpu.VMEM((2, TMOE, D_MODEL + ROUTE_COLS), F32),
                        pltpu.VMEM((2, TMOE, D_MODEL), F32),
                        pltpu.SemaphoreType.DMA((2,)),
                        pltpu.SemaphoreType.DMA((2,))],
    )
    return pl.pallas_call(
        _moe_body,
        grid_spec=grid_spec,
        out_shape=jax.ShapeDtypeStruct((t, D_MODEL), F32),
        compiler_params=_params(("arbitrary",)),
        name="moe_ln2",
    )(nvalid, elo, ehi, ids, ids, x1ext, wei, wei, weo, weo, g, b)


def _route_tiles(bucket):
    t = bucket.shape[0]
    ntiles = t // TMOE + N_ROUTE_BUCKETS
    order = jnp.argsort(bucket, stable=True).astype(jnp.int32)
    counts = jnp.sum(bucket[:, None] == jnp.arange(N_ROUTE_BUCKETS, dtype=jnp.int32)[None, :],
                     axis=0, dtype=jnp.int32)
    row_start = jnp.cumsum(counts) - counts
    tiles_per = (counts + TMOE - 1) // TMOE
    tile_end = jnp.cumsum(tiles_per)
    total = tile_end[-1]
    tile = jnp.arange(ntiles, dtype=jnp.int32)
    used = tile < total
    bkt = jnp.sum(jnp.minimum(tile, total - 1)[:, None] >= tile_end[None, :], axis=1, dtype=jnp.int32)
    local = tile - (tile_end - tiles_per)[bkt]
    row0 = row_start[bkt] + local * TMOE
    nvalid = jnp.where(used, jnp.clip(counts[bkt] - local * TMOE, 0, TMOE), 0).astype(jnp.int32)
    pairs = jnp.asarray(PAIRS, jnp.int32)
    grp = bkt // len(PAIRS)
    elo = grp * EXPERTS_PER_GROUP + pairs[bkt % len(PAIRS), 0]
    ehi = grp * EXPERTS_PER_GROUP + pairs[bkt % len(PAIRS), 1]
    idx = jnp.minimum(row0[:, None] + jnp.arange(TMOE, dtype=jnp.int32)[None, :], t - 1)
    ids = order[idx].reshape(ntiles, 1, TMOE)
    return nvalid, elo.astype(jnp.int32), ehi.astype(jnp.int32), ids


def _prepare(w_in, b_gate, q_gain, k_gain, rel_bias, w_branch_a, w_branch_b, w_out, ln1_g, ln1_b,
             w_route_group, b_route_group, w_route_expert, b_route_expert, w_expert_in,
             w_expert_out, ln2_g, ln2_b, seq):
    row = lambda v: v.reshape(1, -1).astype(F32)
    w = w_in[0]
    wr = jnp.concatenate([w_route_group[0]]
                         + [w_route_expert[0, gi] for gi in range(N_GROUPS)], axis=1)
    wr = jnp.pad(wr.astype(F32), ((0, 0), (0, ROUTE_COLS - wr.shape[1])))
    br = jnp.concatenate([b_route_group[0], b_route_expert[0].reshape(-1)])
    br = jnp.pad(br.astype(F32), (0, ROUTE_COLS - br.shape[0])).reshape(1, ROUTE_COLS)
    wr2 = jnp.concatenate(_split_bf16(wr), axis=1)
    scale = HEAD_DIM ** -0.5
    wb = w[:, DT_ROWS:]
    wb = jnp.concatenate([wb[:, 0:B_WIDTH] * (scale * LOG2E), wb[:, B_WIDTH:]], axis=1)
    return dict(
        wdt=w[:, 0:DT_ROWS].T.astype(BF16), wb=wb.astype(BF16), bg=row(b_gate[0]),
        rope_q=_rope_tables(seq, q_gain[0], scale * LOG2E), rope_k=_rope_tables(seq, k_gain[0], 1.0),
        score_bound=_score_bound(q_gain[0], k_gain[0]),
        bm=[_bias_mask_table(rel_bias, gi, r) for gi, (_, r) in enumerate(DILATED_PAIRS)],
        wa=w_branch_a[0].astype(BF16), wbr=w_branch_b[0].astype(BF16), wo=w_out[0].astype(BF16),
        ln1=(row(ln1_g[0]), row(ln1_b[0])), wr2=wr2, br=br,
        wei=w_expert_in[0].astype(BF16), weo=w_expert_out[0].astype(BF16),
        ln2=(row(ln2_g[0]), row(ln2_b[0])),
    )


def _trunk(x, p):
    b, s, d = x.shape
    qt, k, vt, *qkv_b, gates = _inproj_call(x, p["wdt"], p["wb"], p["bg"], *p["rope_q"], *p["rope_k"])
    ya = _attn_a_call(*p["score_bound"], qt, k, vt)
    os, ls = [], []
    for gi, (_, r) in enumerate(DILATED_PAIRS):
        qg, kg, vg = (qkv_b[i * N_DIL_GROUPS + gi] for i in range(3))
        o, lse = _attn_b_call(qg, kg, vg, p["bm"][gi], r)
        os.append(o.reshape(b * s // r, r * B_OUT_WIDTH))
        ls.append(lse.reshape(b * s // r, r * B_OUT_WIDTH))
    x1ext = _merge_call(x.reshape(b * s, d), ya.reshape(b * s, A_WIDTH), os, ls,
                        gates.reshape(b * s, 2 * D_MODEL), p["wa"], p["wbr"], p["wo"], *p["ln1"],
                        p["wr2"], p["br"])
    nvalid, elo, ehi, ids = _route_tiles(x1ext[:, D_MODEL].astype(jnp.int32))
    y = _moe_call(x1ext, nvalid, elo, ehi, ids, p["wei"], p["weo"], *p["ln2"])
    return y.reshape(b, s, d)


def kernel(x_prompt, x_sample, w_in, b_gate, q_gain, k_gain, rel_bias, w_branch_a, w_branch_b, w_out,
           ln1_g, ln1_b, w_route_group, b_route_group, w_route_expert, b_route_expert, w_expert_in,
           w_expert_out, ln2_g, ln2_b):
    assert x_prompt.shape[1] == x_sample.shape[1]
    p = _prepare(w_in, b_gate, q_gain, k_gain, rel_bias, w_branch_a, w_branch_b, w_out, ln1_g, ln1_b,
                 w_route_group, b_route_group, w_route_expert, b_route_expert, w_expert_in,
                 w_expert_out, ln2_g, ln2_b, x_prompt.shape[1])
    return (_trunk(x_prompt, p), _trunk(x_sample, p))
```

```python
import functools
import math

import jax
import jax.numpy as jnp
import numpy as np
from jax import lax
from jax.experimental import pallas as pl
from jax.experimental.pallas import tpu as pltpu
from jax.experimental.pallas import tpu_sc as plsc

F32 = jnp.float32
BF16 = jnp.bfloat16

D_MODEL = 1024
HEAD_DIM = 64
A_Q_HEADS = 8
A_KV_HEADS = 2
A_GROUP = A_Q_HEADS // A_KV_HEADS
GRID_W = 64
ROPE_AXIS_DIM = HEAD_DIM // 2
ROPE_THETA = 10000.0
DILATED_PAIRS = ((128, 1), (512, 4), (2048, 16))
N_DIL_GROUPS = len(DILATED_PAIRS)
B_HEADS_PER_GROUP = 4
N_BUCKETS = 32
MAX_DISTANCE = 1024
A_WIDTH = A_Q_HEADS * HEAD_DIM
A_KV_WIDTH = A_KV_HEADS * HEAD_DIM
B_WIDTH = B_HEADS_PER_GROUP * N_DIL_GROUPS * HEAD_DIM
B_OUT_WIDTH = B_HEADS_PER_GROUP * HEAD_DIM
DT_ROWS = A_WIDTH + 2 * A_KV_WIDTH
N_GROUPS = 4
EXPERTS_PER_GROUP = 4
N_EXPERTS = N_GROUPS * EXPERTS_PER_GROUP
D_EXPERT = 512
PAIRS = ((0, 1), (0, 2), (0, 3), (1, 2), (1, 3), (2, 3))
N_ROUTE_BUCKETS = N_GROUPS * len(PAIRS)
ALPHA = 2.0 ** 0.25
RMS_EPS = 1e-6
LN_EPS = 1e-5
NEG_INF = -1e30

LANES = 128
TM = 512
TMG = 512
TSUB = 128
NQ = 256
BF16_SUBLANES = 16
F32_SUBLANES = 8
VT_ROWS = HEAD_DIM + BF16_SUBLANES
LOG2E = 1.4426950408889634
LN2 = 0.6931471805599453
MAX_SAFE_SHIFT = 40.0
HALO = 64
BAND = NQ + 2 * HALO
B_SUBSEQ_PER_STEP = 4
TMOE = 256
DMA_UNROLL = 8
ROUTE_COLS = LANES
SC_CORES = 2
SC_SUBCORES = 16
SC_ROWS = 32
VMEM_LIMIT = 56 * 1024 * 1024


def _params(sem):
    return pltpu.CompilerParams(dimension_semantics=sem, vmem_limit_bytes=VMEM_LIMIT)


def _swap16(x):
    return jnp.concatenate([x[16:32], x[0:16], x[48:64], x[32:48]], axis=0)


def _norm_rope(xh, ta, tb):
    r = lax.rsqrt(jnp.mean(xh * xh, axis=0, keepdims=True) + RMS_EPS)
    return (xh * ta + _swap16(xh) * tb) * r


def _inproj_body(x_ref, wdt_ref, wb_ref, bg_ref, taq_ref, tbq_ref, tak_ref, tbk_ref,
                 qt_ref, k_ref, vt_ref, *rest):
    b_refs, gates_ref, tmp_sc = rest[:3 * N_DIL_GROUPS], rest[3 * N_DIL_GROUPS], rest[-1]
    xb = x_ref[...].astype(BF16)
    dt = lax.dot_general(wdt_ref[...], xb, (((1,), (1,)), ((), ())),
                         preferred_element_type=F32)
    taq, tbq = taq_ref[...], tbq_ref[...]
    for h in range(A_Q_HEADS):
        sl = slice(h * HEAD_DIM, (h + 1) * HEAD_DIM)
        qt_ref[sl, :] = _norm_rope(dt[sl], taq, tbq).astype(BF16)
    tak, tbk = tak_ref[...], tbk_ref[...]
    kt = jnp.concatenate(
        [_norm_rope(dt[A_WIDTH + h * HEAD_DIM:A_WIDTH + (h + 1) * HEAD_DIM], tak, tbk)
         for h in range(A_KV_HEADS)], axis=0)
    k_ref[...] = kt.T.astype(BF16)
    for h in range(A_KV_HEADS):
        v0 = A_WIDTH + A_KV_WIDTH + h * HEAD_DIM
        vt_ref[h, 0:HEAD_DIM, :] = dt[v0:v0 + HEAD_DIM].astype(BF16)
        vt_ref[h, HEAD_DIM:, :] = jnp.ones((VT_ROWS - HEAD_DIM, TM), BF16)
    for i in range(3):
        res = jnp.dot(xb, wb_ref[:, i * B_WIDTH:(i + 1) * B_WIDTH], preferred_element_type=F32)
        for c in range(B_WIDTH // LANES):
            tmp_sc[c] = res[:, c * LANES:(c + 1) * LANES]
        for gi, (_, r) in enumerate(DILATED_PAIRS):
            ref = b_refs[i * N_DIL_GROUPS + gi]
            for j in range(r):
                for half in range(B_OUT_WIDTH // LANES):
                    c = gi * (B_OUT_WIDTH // LANES) + half
                    lanes = slice(j * B_OUT_WIDTH + half * LANES, j * B_OUT_WIDTH + (half + 1) * LANES)
                    ref[:, lanes] = tmp_sc[c, pl.ds(j, TM // r, stride=r), :].astype(BF16)
    g0 = 3 * B_WIDTH
    for i in range(2):
        cols = slice(g0 + i * D_MODEL, g0 + (i + 1) * D_MODEL)
        pre = jnp.dot(xb, wb_ref[:, cols], preferred_element_type=F32)
        gates_ref[:, i * D_MODEL:(i + 1) * D_MODEL] = jax.nn.sigmoid(
            pre + bg_ref[:, i * D_MODEL:(i + 1) * D_MODEL]).astype(BF16)


def _inproj_call(x, wdt, wb, bg, taq, tbq, tak, tbk):
    b, s, d = x.shape
    nt = s // TM
    wcols = wb.shape[1]
    tab = pl.BlockSpec((HEAD_DIM, TM), lambda p, i: (0, p))
    const = lambda shape: pl.BlockSpec(shape, lambda p, i: (0,) * len(shape))
    tok = lambda c: pl.BlockSpec((None, TM, c), lambda p, i: (i, p, 0))
    dil = [r for _ in range(3) for _, r in DILATED_PAIRS]
    return pl.pallas_call(
        _inproj_body,
        grid=(nt, b),
        in_specs=[tok(d), const((DT_ROWS, d)), const((d, wcols)), const((1, 2 * D_MODEL)),
                  tab, tab, tab, tab],
        out_specs=[
            pl.BlockSpec((None, A_WIDTH, TM), lambda p, i: (i, 0, p)),
            tok(A_KV_WIDTH),
            pl.BlockSpec((None, None, A_KV_HEADS, VT_ROWS, TM), lambda p, i: (i, p, 0, 0, 0)),
        ] + [pl.BlockSpec((None, TM // r, r * B_OUT_WIDTH), lambda p, i: (i, p, 0)) for r in dil]
        + [tok(2 * D_MODEL)],
        out_shape=[
            jax.ShapeDtypeStruct((b, A_WIDTH, s), BF16),
            jax.ShapeDtypeStruct((b, s, A_KV_WIDTH), BF16),
            jax.ShapeDtypeStruct((b, nt, A_KV_HEADS, VT_ROWS, TM), BF16),
        ] + [jax.ShapeDtypeStruct((b, s // r, r * B_OUT_WIDTH), BF16) for r in dil]
        + [jax.ShapeDtypeStruct((b, s, 2 * D_MODEL), BF16)],
        scratch_shapes=[pltpu.VMEM((B_WIDTH // LANES, TM, LANES), F32)],
        compiler_params=_params(("arbitrary", "arbitrary")),
        name="inproj",
    )(x, wdt, wb, bg, taq, tbq, tak, tbk)


def _rope_tables(s, gain, scale):
    t = np.arange(s)
    inv = ROPE_THETA ** (-np.arange(0, ROPE_AXIS_DIM, 2, dtype=np.float64) / ROPE_AXIS_DIM)
    pos = np.stack([t // GRID_W, t // GRID_W, t % GRID_W, t % GRID_W], axis=0)
    ang = (pos[:, None, :] * inv[None, :, None]).reshape(HEAD_DIM, s)
    sign = np.repeat(np.array([-1.0, 1.0, -1.0, 1.0]), 16)[:, None]
    partner = np.arange(HEAD_DIM) ^ 16
    cos = jnp.asarray(np.cos(ang), F32)
    sin = jnp.asarray(sign * np.sin(ang), F32)
    g = gain.astype(F32) * scale
    return g[:, None] * cos, g[partner][:, None] * sin


def _attn_a_body(fast_ref, kbound_ref, qt_ref, k_ref, vt_ref, o_ref, rhs_sc, acc_sc, ot_sc):
    kv = pl.program_id(1)
    nchunks = vt_ref.shape[0]
    tk = vt_ref.shape[2]
    row = lax.broadcasted_iota(jnp.int32, (A_KV_WIDTH, NQ), 0)
    own = (row // HEAD_DIM) == kv
    for g in range(A_GROUP):
        qg = qt_ref[g * HEAD_DIM:(g + 1) * HEAD_DIM, :]
        rhs_sc[g] = jnp.where(own, jnp.concatenate([qg, qg], axis=0), jnp.zeros((), BF16))

    def load_chunk(c):
        return k_ref[pl.ds(pl.multiple_of(c * tk, tk), tk), :], vt_ref[c]

    @pl.when(fast_ref[0] == 1)
    def _():
        shift = []
        for g in range(A_GROUP):
            qg = qt_ref[g * HEAD_DIM:(g + 1) * HEAD_DIM, :].astype(F32)
            shift.append(jnp.sqrt(jnp.sum(qg * qg, axis=0, keepdims=True)) * kbound_ref[0])

        def scores(tile):
            c, g = tile
            return jnp.dot(k_ref[c * tk:(c + 1) * tk, :], rhs_sc[g], preferred_element_type=F32)

        heads = range(A_GROUP)
        accs = [None] * A_GROUP
        sts = [scores((0, g)) for g in heads]
        for c in range(nchunks):
            sts_next = [scores((c + 1, g)) for g in heads] if c + 1 < nchunks else None
            ps = [jnp.exp2(sts[g] - shift[g]).astype(BF16) for g in heads]
            pvs = [jnp.dot(vt_ref[c], ps[g], preferred_element_type=F32) for g in heads]
            accs = [pvs[g] if accs[g] is None else accs[g] + pvs[g] for g in heads]
            sts = sts_next
        for g in heads:
            acc_sc[g] = accs[g]

    @pl.when(fast_ref[0] == 0)
    def _():
        acc_sc[...] = jnp.zeros_like(acc_sc)

        def chunk(c, ms):
            kc, vc = load_chunk(c)
            new_ms = []
            for g in range(A_GROUP):
                st = jnp.dot(kc, rhs_sc[g], preferred_element_type=F32)
                m_new = jnp.maximum(ms[g], jnp.max(st, axis=0, keepdims=True))
                alpha = jnp.exp2(ms[g] - m_new)
                p = jnp.exp2(st - m_new).astype(BF16)
                acc_sc[g] = alpha * acc_sc[g] + jnp.dot(vc, p, preferred_element_type=F32)
                new_ms.append(m_new)
            return tuple(new_ms)

        lax.fori_loop(0, nchunks, chunk,
                      tuple(jnp.full((1, NQ), NEG_INF, F32) for _ in range(A_GROUP)))

    for g in range(A_GROUP):
        ot_sc[g * HEAD_DIM:(g + 1) * HEAD_DIM, :] = (
            acc_sc[g, 0:HEAD_DIM, :] / acc_sc[g, HEAD_DIM:HEAD_DIM + 1, :])
    o_ref[...] = ot_sc[...].T.astype(BF16)


def _attn_a_call(fast, kbound, qt, k, vt):
    b, _, s = qt.shape
    nchunks, tk = vt.shape[1], vt.shape[4]
    gw = A_GROUP * HEAD_DIM
    grid_spec = pltpu.PrefetchScalarGridSpec(
        num_scalar_prefetch=2,
        grid=(b, A_KV_HEADS, s // NQ),
        in_specs=[
            pl.BlockSpec((None, gw, NQ), lambda i, h, j, *_: (i, h, j)),
            pl.BlockSpec((None, s, A_KV_WIDTH), lambda i, h, j, *_: (i, 0, 0)),
            pl.BlockSpec((None, nchunks, None, VT_ROWS, tk), lambda i, h, j, *_: (i, 0, h, 0, 0)),
        ],
        out_specs=pl.BlockSpec((None, NQ, gw), lambda i, h, j, *_: (i, j, h)),
        scratch_shapes=[pltpu.VMEM((A_GROUP, A_KV_WIDTH, NQ), BF16),
                        pltpu.VMEM((A_GROUP, VT_ROWS, NQ), F32),
                        pltpu.VMEM((gw, NQ), F32)],
    )
    return pl.pallas_call(
        _attn_a_body,
        grid_spec=grid_spec,
        out_shape=jax.ShapeDtypeStruct((b, s, A_WIDTH), BF16),
        compiler_params=_params(("arbitrary", "arbitrary", "arbitrary")),
        name="attn_a",
    )(fast, kbound, qt, k, vt)


def _score_bound(q_gain, k_gain):
    kb = math.sqrt(HEAD_DIM) * jnp.max(jnp.abs(k_gain.astype(F32)))
    qb = math.sqrt(HEAD_DIM) * jnp.max(jnp.abs(q_gain.astype(F32))) * (HEAD_DIM ** -0.5) * LOG2E
    fast = (qb * kb <= MAX_SAFE_SHIFT).astype(jnp.int32)
    return fast.reshape(1), kb.reshape(1)


def _attn_b_body(q_ref, k_ref, v_ref, bm_ref, o_ref, lse_ref, kp_sc, vp_sc):
    length = q_ref.shape[0]
    nsub = q_ref.shape[1] // B_OUT_WIDTH
    cols = lambda u: slice(u * B_OUT_WIDTH, (u + 1) * B_OUT_WIDTH)
    zeros = jnp.zeros((HALO, B_OUT_WIDTH), BF16)
    for sc, ref in ((kp_sc, k_ref), (vp_sc, v_ref)):
        for u in range(nsub):
            sc[u, 0:HALO, :] = zeros
            sc[u, HALO:HALO + length, :] = ref[:, cols(u)]
            sc[u, HALO + length:, :] = zeros
    row_head = lax.broadcasted_iota(jnp.int32, (B_OUT_WIDTH, NQ), 0) // HEAD_DIM
    ones = jnp.ones((VT_ROWS - HEAD_DIM, BAND), BF16)
    blocks = {}

    def block(key):
        if key not in blocks:
            u, i = key
            q0 = i * NQ
            qt = q_ref[q0:q0 + NQ, cols(u)].astype(F32).T.astype(BF16)
            kband = kp_sc[u, q0:q0 + BAND, :]
            vbt = vp_sc[u, q0:q0 + BAND, :].astype(F32).T.astype(BF16)
            pen = None
            if q0 - HALO < 0 or q0 - HALO + BAND > length:
                key_pos = lax.broadcasted_iota(jnp.int32, (BAND, NQ), 0) + (q0 - HALO)
                pen = jnp.where((key_pos >= 0) & (key_pos < length), 0.0, NEG_INF)
            blocks[key] = (qt, kband, vbt, pen)
        return blocks[key]

    def scores(key, h):
        qt, kband, _, _ = block(key)
        rhs = jnp.where(row_head == h, qt, jnp.zeros((), BF16))
        return jnp.dot(kband, rhs, preferred_element_type=F32)

    heads = range(B_HEADS_PER_GROUP)
    keys = [(u, i) for u in range(nsub) for i in range(length // NQ)]
    sts = [scores(keys[0], h) for h in heads]
    for n, key in enumerate(keys):
        sts_next = [scores(keys[n + 1], h) for h in heads] if n + 1 < len(keys) else None
        _, _, vbt, pen = block(key)
        sts = [sts[h] + bm_ref[h] for h in heads]
        if pen is not None:
            sts = [st + pen for st in sts]
        ms = [jnp.max(st, axis=0, keepdims=True) for st in sts]
        ps = [jnp.exp2(st - m).astype(BF16) for st, m in zip(sts, ms)]
        accs = [jnp.dot(jnp.concatenate([vbt[h * HEAD_DIM:(h + 1) * HEAD_DIM], ones], axis=0), ps[h],
                        preferred_element_type=F32) for h in heads]
        ls = [acc[HEAD_DIM:HEAD_DIM + 1] for acc in accs]
        u, i = key
        rows = slice(i * NQ, (i + 1) * NQ)
        o_ref[rows, cols(u)] = jnp.concatenate([acc[0:HEAD_DIM] / l for acc, l in zip(accs, ls)], axis=0).T
        lse_ref[rows, cols(u)] = jnp.concatenate(
            [jnp.broadcast_to((m + jnp.log2(l)) * LN2, (HEAD_DIM, NQ)) for m, l in zip(ms, ls)], axis=0).T
        del blocks[key]
        sts = sts_next


def _attn_b_call(qg, kg, vg, bm, r):
    b, length, _ = qg.shape
    nsub = min(r, B_SUBSEQ_PER_STEP)
    spec = pl.BlockSpec((None, length, nsub * B_OUT_WIDTH), lambda i, j: (i, 0, j))
    out_sds = jax.ShapeDtypeStruct((b, length, r * B_OUT_WIDTH), F32)
    return pl.pallas_call(
        _attn_b_body,
        grid=(b, r // nsub),
        in_specs=[spec, spec, spec,
                  pl.BlockSpec((B_HEADS_PER_GROUP, BAND, NQ), lambda i, j: (0, 0, 0))],
        out_specs=[spec, spec],
        out_shape=[out_sds, out_sds],
        scratch_shapes=[pltpu.VMEM((nsub, length + 2 * HALO, B_OUT_WIDTH), BF16),
                        pltpu.VMEM((nsub, length + 2 * HALO, B_OUT_WIDTH), BF16)],
        compiler_params=_params(("arbitrary", "arbitrary")),
        name=f"attn_b_r{r}",
    )(qg, kg, vg, bm)


def _t5_bucket_np(rel):
    half = N_BUCKETS // 2
    exact = half // 2
    n = np.abs(rel)
    big = exact + (np.log(np.maximum(n, 1).astype(np.float64) / exact)
                   / math.log(MAX_DISTANCE / exact) * (half - exact)).astype(np.int64)
    big = np.minimum(big, half - 1)
    return np.where(rel > 0, half, 0) + np.where(n < exact, n, big)


def _bias_mask_table(rel_bias, gi, r):
    span = BAND + NQ - 1
    rel = np.arange(span) - (NQ - 1) - HALO
    tab = rel_bias[:, gi * B_HEADS_PER_GROUP:(gi + 1) * B_HEADS_PER_GROUP].astype(F32) * LOG2E
    u = jnp.where(jnp.asarray(np.abs(rel) <= HALO)[:, None],
                  tab[jnp.asarray(_t5_bucket_np(rel * r), jnp.int32)], NEG_INF).T
    hankel = jnp.tile(u, (1, NQ + 1))[:, :NQ * (span + 1)].reshape(-1, NQ, span + 1)
    return jnp.transpose(hankel[:, ::-1, :BAND], (0, 2, 1))


def _layer_norm(h, g, b):
    mu = jnp.mean(h, axis=-1, keepdims=True)
    c = h - mu
    var = jnp.mean(c * c, axis=-1, keepdims=True)
    return c * lax.rsqrt(var + LN_EPS) * g + b


def _split_bf16(a):
    hi = a.astype(BF16)
    return hi, (a - hi.astype(F32)).astype(BF16)


def _merge_body(x_ref, ya_ref, o1_ref, o2_ref, o3_ref, l1_ref, l2_ref, l3_ref, gates_ref,
                wa_ref, wbr_ref, wo_ref, g_ref, b_ref, wr2_ref, br_ref, out_ref, *tok_sc):
    slabs = B_OUT_WIDTH // LANES
    dil = [r for _, r in DILATED_PAIRS]
    b_refs = (o1_ref, o2_ref, o3_ref, l1_ref, l2_ref, l3_ref)
    for ref, sc, r in zip(b_refs, tok_sc, dil + dil):
        if r > 1:
            for j in range(r):
                for c in range(slabs):
                    lanes = slice(j * B_OUT_WIDTH + c * LANES, j * B_OUT_WIDTH + (c + 1) * LANES)
                    sc[c, pl.ds(j, TMG // r, stride=r), :] = ref[:, lanes]

    def token_major(which, rows):
        ref, sc, r = b_refs[which], tok_sc[which], (dil + dil)[which]
        if r == 1:
            return ref[rows, :]
        return jnp.concatenate([sc[c, rows, :] for c in range(slabs)], axis=1)

    subs = [slice(i * TSUB, (i + 1) * TSUB) for i in range(TMG // TSUB)]
    ybs = []
    for rows in subs:
        o1, o2, o3, l1, l2, l3 = (token_major(w, rows) for w in range(6))
        mx = jnp.maximum(jnp.maximum(l1, l2), l3)
        e1, e2, e3 = jnp.exp(l1 - mx), jnp.exp(l2 - mx), jnp.exp(l3 - mx)
        ybs.append(((e1 * o1 + e2 * o2 + e3 * o3) / (e1 + e2 + e3)).astype(BF16))
    ya_ps = [jnp.dot(ya_ref[rows, :], wa_ref[...], preferred_element_type=F32) for rows in subs]
    yb_ps = [jnp.dot(yb, wbr_ref[...], preferred_element_type=F32) for yb in ybs]
    mergeds = [(gates_ref[rows, 0:D_MODEL].astype(F32) * ya_p
                + gates_ref[rows, D_MODEL:2 * D_MODEL].astype(F32) * yb_p).astype(BF16)
               for rows, ya_p, yb_p in zip(subs, ya_ps, yb_ps)]
    hs = [ALPHA * x_ref[rows, :] + jnp.dot(m, wo_ref[...], preferred_element_type=F32)
          for rows, m in zip(subs, mergeds)]
    x1s = [_layer_norm(h, g_ref[...], b_ref[...]) for h in hs]
    for rows, x1 in zip(subs, x1s):
        out_ref[rows, 0:D_MODEL] = x1

    splits = [_split_bf16(x1) for x1 in x1s]
    boths = [jnp.dot(xh, wr2_ref[...], preferred_element_type=F32)
             + jnp.dot(xl, wr2_ref[...], preferred_element_type=F32) for xh, xl in splits]
    logits = jnp.concatenate(
        [(both[:, 0:ROUTE_COLS] + both[:, ROUTE_COLS:] + br_ref[...]).T for both in boths], axis=1)
    col = lambda i: logits[i:i + 1, :]
    lg = [col(i) for i in range(N_GROUPS)]
    gmax = functools.reduce(jnp.maximum, lg)
    top_p = 1.0 / functools.reduce(lambda a, c: a + c, [jnp.exp(v - gmax) for v in lg])
    gsel = jnp.full_like(gmax, N_GROUPS - 1).astype(jnp.int32)
    for i in reversed(range(N_GROUPS - 1)):
        gsel = jnp.where(lg[i] >= gmax, i, gsel)
    sel = []
    for e in range(EXPERTS_PER_GROUP):
        v = col(N_GROUPS + (N_GROUPS - 1) * EXPERTS_PER_GROUP + e)
        for gi in reversed(range(N_GROUPS - 1)):
            v = jnp.where(gsel == gi, col(N_GROUPS + gi * EXPERTS_PER_GROUP + e), v)
        sel.append(v)
    rank = []
    for e in range(EXPERTS_PER_GROUP):
        rk = jnp.zeros_like(gsel)
        for j in range(EXPERTS_PER_GROUP):
            if j != e:
                ahead = (sel[j] > sel[e]) | ((sel[j] == sel[e]) & (j < e))
                rk = rk + ahead.astype(jnp.int32)
        rank.append(rk)
    pick = lambda k: functools.reduce(
        lambda a, c: a + c, [jnp.where(rank[e] == k, sel[e], 0.0) for e in range(EXPERTS_PER_GROUP)])
    t = jnp.exp(pick(1) - pick(0))
    w_first = top_p / (1.0 + t)
    w_second = top_p * t / (1.0 + t)
    chosen = [rank[e] < 2 for e in range(EXPERTS_PER_GROUP)]
    weight = [jnp.where(rank[e] == 0, w_first, w_second) for e in range(EXPERTS_PER_GROUP)]
    pair = jnp.zeros_like(gsel)
    w_lo = jnp.zeros_like(top_p)
    w_hi = jnp.zeros_like(top_p)
    for pi, (a, c) in enumerate(PAIRS):
        hit = chosen[a] & chosen[c]
        pair = jnp.where(hit, pi, pair)
        w_lo = jnp.where(hit, weight[a], w_lo)
        w_hi = jnp.where(hit, weight[c], w_hi)
    bucket = (gsel * len(PAIRS) + pair).astype(F32)
    row = lax.broadcasted_iota(jnp.int32, (F32_SUBLANES, TMG), 0)
    head = jnp.where(row == 0, bucket, jnp.where(row == 1, w_lo, jnp.where(row == 2, w_hi, 0.0)))
    route_t = jnp.concatenate([head, jnp.zeros((ROUTE_COLS - F32_SUBLANES, TMG), F32)], axis=0)
    out_ref[:, D_MODEL:] = route_t.T


def _merge_call(x, ya, os, ls, gates, wa, wbr, wo, g, b, wr2, br):
    t, d = x.shape
    tok = lambda c: pl.BlockSpec((TMG, c), lambda i: (i, 0))
    const = lambda a: pl.BlockSpec(a.shape, lambda i: (0,) * a.ndim)
    consts = (wa, wbr, wo, g, b, wr2, br)
    dil = [pl.BlockSpec((TMG // r, r * B_OUT_WIDTH), lambda i: (i, 0)) for _, r in DILATED_PAIRS]
    return pl.pallas_call(
        _merge_body,
        grid=(t // TMG,),
        in_specs=[tok(d), tok(A_WIDTH)] + dil + dil + [tok(2 * D_MODEL)]
        + [const(a) for a in consts],
        out_specs=tok(d + ROUTE_COLS),
        out_shape=jax.ShapeDtypeStruct((t, d + ROUTE_COLS), F32),
        scratch_shapes=[pltpu.VMEM((B_OUT_WIDTH // LANES, TMG, LANES), F32)] * (2 * N_DIL_GROUPS),
        compiler_params=_params(("arbitrary",)),
        name="merge_ln1_route",
    )(x, ya, *os, *ls, gates, *consts)


def _moe_body(nvalid_ref, elo_ref, ehi_ref, ids_ref, ids_next_ref, x_hbm, wil_ref, wih_ref, wol_ref,
              woh_ref, g_ref, b_ref, out_hbm, xbuf, obuf, sem_in, sem_out):
    del elo_ref, ehi_ref
    i = pl.program_id(0)
    ntiles = pl.num_programs(0)
    slot = i % 2
    n = nvalid_ref[i]

    def start_gather(ids, dst_slot):
        def body(kk, c):
            for u in range(DMA_UNROLL):
                k = kk * DMA_UNROLL + u
                pltpu.make_async_copy(x_hbm.at[pl.ds(ids[0, k], 1)], xbuf.at[dst_slot, pl.ds(k, 1)],
                                      sem_in.at[dst_slot]).start()
            return c
        lax.fori_loop(0, TMOE // DMA_UNROLL, body, 0)

    def scatter_row(k):
        return pltpu.make_async_copy(obuf.at[slot, pl.ds(k, 1)], out_hbm.at[pl.ds(ids_ref[0, k], 1)],
                                     sem_out.at[slot])

    def start_scatter(rows):
        def body(kk, c):
            for u in range(DMA_UNROLL):
                scatter_row(kk * DMA_UNROLL + u).start()
            return c
        full = rows // DMA_UNROLL
        lax.fori_loop(0, full, body, 0)
        lax.fori_loop(full * DMA_UNROLL, rows, lambda k, c: (scatter_row(k).start(), c)[1], 0)

    def wait_scatter(s, rows):
        def wait_rows(count):
            pltpu.make_async_copy(obuf.at[s, pl.ds(0, count)], out_hbm.at[pl.ds(0, count)],
                                  sem_out.at[s]).wait()
        full = pl.multiple_of((rows // F32_SUBLANES) * F32_SUBLANES, F32_SUBLANES)

        @pl.when(full > 0)
        def _():
            wait_rows(full)

        lax.fori_loop(full, rows, lambda k, c: (wait_rows(1), c)[1], 0)

    @pl.when((i == 0) & (n > 0))
    def _():
        start_gather(ids_ref, 0)

    @pl.when(n > 0)
    def _():
        pltpu.make_async_copy(x_hbm.at[pl.ds(0, TMOE)], xbuf.at[slot], sem_in.at[slot]).wait()

    @pl.when(i + 1 < ntiles)
    def _():
        @pl.when(nvalid_ref[i + 1] > 0)
        def _():
            start_gather(ids_next_ref, 1 - slot)

    @pl.when(i >= 2)
    def _():
        rows = nvalid_ref[i - 2]

        @pl.when(rows > 0)
        def _():
            wait_scatter(slot, rows)

    @pl.when(n > 0)
    def _():
        xt = xbuf[slot, :, 0:D_MODEL]
        xb = xt.astype(BF16)

        hcats = [jnp.dot(xb, wi_ref[...], preferred_element_type=F32)
                 for wi_ref in (wil_ref, wih_ref)]
        acts = [(h[:, 0:D_EXPERT] * jax.nn.sigmoid(h[:, 0:D_EXPERT]) * h[:, D_EXPERT:]).astype(BF16)
                for h in hcats]
        y_lo, y_hi = (jnp.dot(act, wo_ref[...], preferred_element_type=F32)
                      for act, wo_ref in zip(acts, (wol_ref, woh_ref)))
        moe = (xbuf[slot, :, D_MODEL + 1:D_MODEL + 2] * y_lo
               + xbuf[slot, :, D_MODEL + 2:D_MODEL + 3] * y_hi)
        obuf[slot] = _layer_norm(ALPHA * xt + moe, g_ref[...], b_ref[...])
        start_scatter(n)

    @pl.when(i == ntiles - 1)
    def _():
        @pl.when(i >= 1)
        def _():
            rows = nvalid_ref[i - 1]

            @pl.when(rows > 0)
            def _():
                wait_scatter(1 - slot, rows)

        @pl.when(n > 0)
        def _():
            wait_scatter(slot, n)


def _moe_call(x1ext, nvalid, elo, ehi, ids, wei, weo, g, b):
    t = x1ext.shape[0]
    ntiles = nvalid.shape[0]
    wi_spec = lambda e: pl.BlockSpec((None, D_MODEL, 2 * D_EXPERT), lambda i, nv, lo, hi: (e(lo, hi)[i], 0, 0))
    wo_spec = lambda e: pl.BlockSpec((None, D_EXPERT, D_MODEL), lambda i, nv, lo, hi: (e(lo, hi)[i], 0, 0))
    first = lambda lo, hi: lo
    second = lambda lo, hi: hi
    vec = pl.BlockSpec((1, D_MODEL), lambda i, nv, lo, hi: (0, 0))
    grid_spec = pltpu.PrefetchScalarGridSpec(
        num_scalar_prefetch=3,
        grid=(ntiles,),
        in_specs=[
            pl.BlockSpec((None, 1, TMOE), lambda i, nv, lo, hi: (i, 0, 0), memory_space=pltpu.SMEM),
            pl.BlockSpec((None, 1, TMOE), lambda i, nv, lo, hi: (jnp.minimum(i + 1, ntiles - 1), 0, 0),
                         memory_space=pltpu.SMEM),
            pl.BlockSpec(memory_space=pl.ANY),
            wi_spec(first), wi_spec(second), wo_spec(first), wo_spec(second), vec, vec,
        ],
        out_specs=pl.BlockSpec(memory_space=pl.ANY),
        scratch_shapes=[pltpu.VMEM((2, TMOE, D_MODEL + ROUTE_COLS), F32),
                        pltpu.VMEM((2, TMOE, D_MODEL), F32),
                        pltpu.SemaphoreType.DMA((2,)),
                        pltpu.SemaphoreType.DMA((2,))],
    )
    return pl.pallas_call(
        _moe_body,
        grid_spec=grid_spec,
        out_shape=jax.ShapeDtypeStruct((t, D_MODEL), F32),
        compiler_params=_params(("arbitrary",)),
        name="moe_ln2",
    )(nvalid, elo, ehi, ids, ids, x1ext, wei, wei, weo, weo, g, b)


def _route_tiles(bucket):
    t = bucket.shape[0]
    ntiles = t // TMOE + N_ROUTE_BUCKETS
    order = jnp.argsort(bucket, stable=True).astype(jnp.int32)
    counts = jnp.sum(bucket[:, None] == jnp.arange(N_ROUTE_BUCKETS, dtype=jnp.int32)[None, :],
                     axis=0, dtype=jnp.int32)
    row_start = jnp.cumsum(counts) - counts
    tiles_per = (counts + TMOE - 1) // TMOE
    tile_end = jnp.cumsum(tiles_per)
    total = tile_end[-1]
    tile = jnp.arange(ntiles, dtype=jnp.int32)
    used = tile < total
    bkt = jnp.sum(jnp.minimum(tile, total - 1)[:, None] >= tile_end[None, :], axis=1, dtype=jnp.int32)
    local = tile - (tile_end - tiles_per)[bkt]
    row0 = row_start[bkt] + local * TMOE
    nvalid = jnp.where(used, jnp.clip(counts[bkt] - local * TMOE, 0, TMOE), 0).astype(jnp.int32)
    pairs = jnp.asarray(PAIRS, jnp.int32)
    grp = bkt // len(PAIRS)
    elo = grp * EXPERTS_PER_GROUP + pairs[bkt % len(PAIRS), 0]
    ehi = grp * EXPERTS_PER_GROUP + pairs[bkt % len(PAIRS), 1]
    idx = jnp.minimum(row0[:, None] + jnp.arange(TMOE, dtype=jnp.int32)[None, :], t - 1)
    ids = order[idx].reshape(ntiles, 1, TMOE)
    return nvalid, elo.astype(jnp.int32), ehi.astype(jnp.int32), ids


def _sc_gather(table, idx):
    nrows, d = idx.shape[0], table.shape[1]
    workers = SC_CORES * SC_SUBCORES
    per_worker = nrows // workers
    assert nrows % (workers * SC_ROWS) == 0
    mesh = plsc.VectorSubcoreMesh(core_axis_name="c", subcore_axis_name="s")

    @functools.partial(
        pl.kernel, mesh=mesh, out_type=jax.ShapeDtypeStruct((nrows, d), table.dtype),
        scratch_types=[pltpu.VMEM((SC_ROWS,), jnp.int32), pltpu.VMEM((SC_ROWS, d), table.dtype),
                       pltpu.SemaphoreType.DMA])
    def gather(table_hbm, idx_hbm, out_hbm, idx_v, rows_v, sem):
        base = (lax.axis_index("s") * SC_CORES + lax.axis_index("c")) * per_worker

        @pl.loop(0, per_worker, step=SC_ROWS)
        def _(off):
            start = pl.multiple_of(base + off, SC_ROWS)
            pltpu.sync_copy(idx_hbm.at[pl.ds(start, SC_ROWS)], idx_v)
            pltpu.async_copy(table_hbm.at[idx_v], rows_v, sem).wait()
            pltpu.sync_copy(rows_v, out_hbm.at[pl.ds(start, SC_ROWS)])

    return gather(table, idx)


def _moe_tiles_body(nvalid_ref, elo_ref, ehi_ref, x_ref, wil_ref, wih_ref, wol_ref, woh_ref,
                    g_ref, b_ref, o_ref):
    del elo_ref, ehi_ref
    n = nvalid_ref[pl.program_id(0)]

    @pl.when(n == 0)
    def _():
        o_ref[...] = jnp.zeros_like(o_ref)

    @pl.when(n > 0)
    def _():
        xt = x_ref[:, 0:D_MODEL]
        xb = xt.astype(BF16)
        hcats = [jnp.dot(xb, wi_ref[...], preferred_element_type=F32)
                 for wi_ref in (wil_ref, wih_ref)]
        acts = [(h[:, 0:D_EXPERT] * jax.nn.sigmoid(h[:, 0:D_EXPERT]) * h[:, D_EXPERT:]).astype(BF16)
                for h in hcats]
        y_lo, y_hi = (jnp.dot(act, wo_ref[...], preferred_element_type=F32)
                      for act, wo_ref in zip(acts, (wol_ref, woh_ref)))
        moe = (x_ref[:, D_MODEL + 1:D_MODEL + 2] * y_lo + x_ref[:, D_MODEL + 2:D_MODEL + 3] * y_hi)
        o_ref[...] = _layer_norm(ALPHA * xt + moe, g_ref[...], b_ref[...])


def _moe_tiles_call(x_sorted, nvalid, elo, ehi, wei, weo, g, b):
    ntiles = nvalid.shape[0]
    wi_spec = lambda e: pl.BlockSpec((None, D_MODEL, 2 * D_EXPERT), lambda i, nv, lo, hi: (e(lo, hi)[i], 0, 0))
    wo_spec = lambda e: pl.BlockSpec((None, D_EXPERT, D_MODEL), lambda i, nv, lo, hi: (e(lo, hi)[i], 0, 0))
    first = lambda lo, hi: lo
    second = lambda lo, hi: hi
    vec = pl.BlockSpec((1, D_MODEL), lambda i, nv, lo, hi: (0, 0))
    grid_spec = pltpu.PrefetchScalarGridSpec(
        num_scalar_prefetch=3,
        grid=(ntiles,),
        in_specs=[pl.BlockSpec((TMOE, D_MODEL + ROUTE_COLS), lambda i, nv, lo, hi: (i, 0)),
                  wi_spec(first), wi_spec(second), wo_spec(first), wo_spec(second), vec, vec],
        out_specs=pl.BlockSpec((TMOE, D_MODEL), lambda i, nv, lo, hi: (i, 0)),
    )
    return pl.pallas_call(
        _moe_tiles_body,
        grid_spec=grid_spec,
        out_shape=jax.ShapeDtypeStruct((ntiles * TMOE, D_MODEL), F32),
        compiler_params=_params(("arbitrary",)),
        name="moe_ln2",
    )(nvalid, elo, ehi, x_sorted, wei, wei, weo, weo, g, b)


def _route_plan(bucket):
    t = bucket.shape[0]
    ntiles = t // TMOE + N_ROUTE_BUCKETS
    order = jnp.argsort(bucket, stable=True).astype(jnp.int32)
    counts = jnp.sum(bucket[:, None] == jnp.arange(N_ROUTE_BUCKETS, dtype=jnp.int32)[None, :],
                     axis=0, dtype=jnp.int32)
    row_start = jnp.cumsum(counts) - counts
    tiles_per = (counts + TMOE - 1) // TMOE
    tile_end = jnp.cumsum(tiles_per)
    tile_start = tile_end - tiles_per
    total = tile_end[-1]
    tile = jnp.arange(ntiles, dtype=jnp.int32)
    used = tile < total
    bkt = jnp.sum(jnp.minimum(tile, total - 1)[:, None] >= tile_end[None, :], axis=1, dtype=jnp.int32)
    local = tile - tile_start[bkt]
    row0 = row_start[bkt] + local * TMOE
    nvalid = jnp.where(used, jnp.clip(counts[bkt] - local * TMOE, 0, TMOE), 0).astype(jnp.int32)
    pairs = jnp.asarray(PAIRS, jnp.int32)
    grp = bkt // len(PAIRS)
    elo = grp * EXPERTS_PER_GROUP + pairs[bkt % len(PAIRS), 0]
    ehi = grp * EXPERTS_PER_GROUP + pairs[bkt % len(PAIRS), 1]
    idx = jnp.minimum(row0[:, None] + jnp.arange(TMOE, dtype=jnp.int32)[None, :], t - 1)
    ids = order[idx].reshape(ntiles * TMOE)
    sorted_bkt = bucket[order]
    padded_row = tile_start[sorted_bkt] * TMOE + (jnp.arange(t, dtype=jnp.int32) - row_start[sorted_bkt])
    pos = jnp.zeros((t,), jnp.int32).at[order].set(padded_row)
    return nvalid, elo.astype(jnp.int32), ehi.astype(jnp.int32), ids, pos


def _prepare(w_in, b_gate, q_gain, k_gain, rel_bias, w_branch_a, w_branch_b, w_out, ln1_g, ln1_b,
             w_route_group, b_route_group, w_route_expert, b_route_expert, w_expert_in,
             w_expert_out, ln2_g, ln2_b, seq):
    row = lambda v: v.reshape(1, -1).astype(F32)
    w = w_in[0]
    wr = jnp.concatenate([w_route_group[0]]
                         + [w_route_expert[0, gi] for gi in range(N_GROUPS)], axis=1)
    wr = jnp.pad(wr.astype(F32), ((0, 0), (0, ROUTE_COLS - wr.shape[1])))
    br = jnp.concatenate([b_route_group[0], b_route_expert[0].reshape(-1)])
    br = jnp.pad(br.astype(F32), (0, ROUTE_COLS - br.shape[0])).reshape(1, ROUTE_COLS)
    wr2 = jnp.concatenate(_split_bf16(wr), axis=1)
    scale = HEAD_DIM ** -0.5
    wb = w[:, DT_ROWS:]
    wb = jnp.concatenate([wb[:, 0:B_WIDTH] * (scale * LOG2E), wb[:, B_WIDTH:]], axis=1)
    return dict(
        wdt=w[:, 0:DT_ROWS].T.astype(BF16), wb=wb.astype(BF16), bg=row(b_gate[0]),
        rope_q=_rope_tables(seq, q_gain[0], scale * LOG2E), rope_k=_rope_tables(seq, k_gain[0], 1.0),
        score_bound=_score_bound(q_gain[0], k_gain[0]),
        bm=[_bias_mask_table(rel_bias, gi, r) for gi, (_, r) in enumerate(DILATED_PAIRS)],
        wa=w_branch_a[0].astype(BF16), wbr=w_branch_b[0].astype(BF16), wo=w_out[0].astype(BF16),
        ln1=(row(ln1_g[0]), row(ln1_b[0])), wr2=wr2, br=br,
        wei=w_expert_in[0].astype(BF16), weo=w_expert_out[0].astype(BF16),
        ln2=(row(ln2_g[0]), row(ln2_b[0])),
    )


def _trunk(x, p):
    b, s, d = x.shape
    qt, k, vt, *qkv_b, gates = _inproj_call(x, p["wdt"], p["wb"], p["bg"], *p["rope_q"], *p["rope_k"])
    ya = _attn_a_call(*p["score_bound"], qt, k, vt)
    os, ls = [], []
    for gi, (_, r) in enumerate(DILATED_PAIRS):
        qg, kg, vg = (qkv_b[i * N_DIL_GROUPS + gi] for i in range(3))
        o, lse = _attn_b_call(qg, kg, vg, p["bm"][gi], r)
        os.append(o.reshape(b * s // r, r * B_OUT_WIDTH))
        ls.append(lse.reshape(b * s // r, r * B_OUT_WIDTH))
    x1ext = _merge_call(x.reshape(b * s, d), ya.reshape(b * s, A_WIDTH), os, ls,
                        gates.reshape(b * s, 2 * D_MODEL), p["wa"], p["wbr"], p["wo"], *p["ln1"],
                        p["wr2"], p["br"])
    return x1ext


def _experts(x1ext, p, shape):
    nvalid, elo, ehi, ids, pos = _route_plan(x1ext[:, D_MODEL].astype(jnp.int32))
    x_sorted = _sc_gather(x1ext, ids)
    y_sorted = _moe_tiles_call(x_sorted, nvalid, elo, ehi, p["wei"], p["weo"], *p["ln2"])
    return _sc_gather(y_sorted, pos).reshape(shape)


def kernel(x_prompt, x_sample, w_in, b_gate, q_gain, k_gain, rel_bias, w_branch_a, w_branch_b, w_out,
           ln1_g, ln1_b, w_route_group, b_route_group, w_route_expert, b_route_expert, w_expert_in,
           w_expert_out, ln2_g, ln2_b):
    assert x_prompt.shape[1] == x_sample.shape[1]
    p = _prepare(w_in, b_gate, q_gain, k_gain, rel_bias, w_branch_a, w_branch_b, w_out, ln1_g, ln1_b,
                 w_route_group, b_route_group, w_route_expert, b_route_expert, w_expert_in,
                 w_expert_out, ln2_g, ln2_b, x_prompt.shape[1])
    x1_prompt = _trunk(x_prompt, p)
    x1_sample = _trunk(x_sample, p)
    return (_experts(x1_prompt, p, x_prompt.shape), _experts(x1_sample, p, x_sample.shape))
```

```python
import functools
import math

import jax
import jax.numpy as jnp
import numpy as np
from jax import lax
from jax.experimental import pallas as pl
from jax.experimental.pallas import tpu as pltpu
from jax.experimental.pallas import tpu_sc as plsc

F32 = jnp.float32
BF16 = jnp.bfloat16

D_MODEL = 1024
HEAD_DIM = 64
A_Q_HEADS = 8
A_KV_HEADS = 2
A_GROUP = A_Q_HEADS // A_KV_HEADS
GRID_W = 64
ROPE_AXIS_DIM = HEAD_DIM // 2
ROPE_THETA = 10000.0
DILATED_PAIRS = ((128, 1), (512, 4), (2048, 16))
N_DIL_GROUPS = len(DILATED_PAIRS)
B_HEADS_PER_GROUP = 4
N_BUCKETS = 32
MAX_DISTANCE = 1024
A_WIDTH = A_Q_HEADS * HEAD_DIM
A_KV_WIDTH = A_KV_HEADS * HEAD_DIM
B_WIDTH = B_HEADS_PER_GROUP * N_DIL_GROUPS * HEAD_DIM
B_OUT_WIDTH = B_HEADS_PER_GROUP * HEAD_DIM
DT_ROWS = A_WIDTH + 2 * A_KV_WIDTH
N_GROUPS = 4
EXPERTS_PER_GROUP = 4
N_EXPERTS = N_GROUPS * EXPERTS_PER_GROUP
D_EXPERT = 512
PAIRS = ((0, 1), (0, 2), (0, 3), (1, 2), (1, 3), (2, 3))
N_ROUTE_BUCKETS = N_GROUPS * len(PAIRS)
ALPHA = 2.0 ** 0.25
RMS_EPS = 1e-6
LN_EPS = 1e-5
NEG_INF = -1e30

LANES = 128
TM = 512
TMG = 512
TSUB = 128
NQ = 256
BF16_SUBLANES = 16
F32_SUBLANES = 8
VT_ROWS = HEAD_DIM + BF16_SUBLANES
LOG2E = 1.4426950408889634
LN2 = 0.6931471805599453
MAX_SAFE_SHIFT = 40.0
HALO = 64
BAND = NQ + 2 * HALO
B_SUBSEQ_PER_STEP = 4
TMOE = 256
ROUTE_COLS = LANES
SC_CORES = 2
SC_SUBCORES = 16
SC_ROWS = 32
VMEM_LIMIT = 56 * 1024 * 1024


def _params(sem):
    return pltpu.CompilerParams(dimension_semantics=sem, vmem_limit_bytes=VMEM_LIMIT)


def _swap16(x):
    return jnp.concatenate([x[16:32], x[0:16], x[48:64], x[32:48]], axis=0)


def _norm_rope(xh, ta, tb):
    r = lax.rsqrt(jnp.mean(xh * xh, axis=0, keepdims=True) + RMS_EPS)
    return (xh * ta + _swap16(xh) * tb) * r


def _inproj_body(x_ref, wdt_ref, wb_ref, bg_ref, taq_ref, tbq_ref, tak_ref, tbk_ref,
                 qt_ref, k_ref, vt_ref, *rest):
    b_refs, gates_ref, tmp_sc = rest[:3 * N_DIL_GROUPS], rest[3 * N_DIL_GROUPS], rest[-1]
    xb = x_ref[...].astype(BF16)
    dt = lax.dot_general(wdt_ref[...], xb, (((1,), (1,)), ((), ())),
                         preferred_element_type=F32)
    taq, tbq = taq_ref[...], tbq_ref[...]
    for h in range(A_Q_HEADS):
        sl = slice(h * HEAD_DIM, (h + 1) * HEAD_DIM)
        qt_ref[sl, :] = _norm_rope(dt[sl], taq, tbq).astype(BF16)
    tak, tbk = tak_ref[...], tbk_ref[...]
    kt = jnp.concatenate(
        [_norm_rope(dt[A_WIDTH + h * HEAD_DIM:A_WIDTH + (h + 1) * HEAD_DIM], tak, tbk)
         for h in range(A_KV_HEADS)], axis=0)
    k_ref[...] = kt.T.astype(BF16)
    for h in range(A_KV_HEADS):
        v0 = A_WIDTH + A_KV_WIDTH + h * HEAD_DIM
        vt_ref[h, 0:HEAD_DIM, :] = dt[v0:v0 + HEAD_DIM].astype(BF16)
        vt_ref[h, HEAD_DIM:, :] = jnp.ones((VT_ROWS - HEAD_DIM, TM), BF16)
    for i in range(3):
        res = jnp.dot(xb, wb_ref[:, i * B_WIDTH:(i + 1) * B_WIDTH], preferred_element_type=F32)
        for c in range(B_WIDTH // LANES):
            tmp_sc[c] = res[:, c * LANES:(c + 1) * LANES]
        for gi, (_, r) in enumerate(DILATED_PAIRS):
            ref = b_refs[i * N_DIL_GROUPS + gi]
            for j in range(r):
                for half in range(B_OUT_WIDTH // LANES):
                    c = gi * (B_OUT_WIDTH // LANES) + half
                    lanes = slice(j * B_OUT_WIDTH + half * LANES, j * B_OUT_WIDTH + (half + 1) * LANES)
                    ref[:, lanes] = tmp_sc[c, pl.ds(j, TM // r, stride=r), :].astype(BF16)
    g0 = 3 * B_WIDTH
    for i in range(2):
        cols = slice(g0 + i * D_MODEL, g0 + (i + 1) * D_MODEL)
        pre = jnp.dot(xb, wb_ref[:, cols], preferred_element_type=F32)
        gates_ref[:, i * D_MODEL:(i + 1) * D_MODEL] = jax.nn.sigmoid(
            pre + bg_ref[:, i * D_MODEL:(i + 1) * D_MODEL]).astype(BF16)


def _inproj_call(x, wdt, wb, bg, taq, tbq, tak, tbk):
    b, s, d = x.shape
    nt = s // TM
    wcols = wb.shape[1]
    tab = pl.BlockSpec((HEAD_DIM, TM), lambda p, i: (0, p))
    const = lambda shape: pl.BlockSpec(shape, lambda p, i: (0,) * len(shape))
    tok = lambda c: pl.BlockSpec((None, TM, c), lambda p, i: (i, p, 0))
    dil = [r for _ in range(3) for _, r in DILATED_PAIRS]
    return pl.pallas_call(
        _inproj_body,
        grid=(nt, b),
        in_specs=[tok(d), const((DT_ROWS, d)), const((d, wcols)), const((1, 2 * D_MODEL)),
                  tab, tab, tab, tab],
        out_specs=[
            pl.BlockSpec((None, A_WIDTH, TM), lambda p, i: (i, 0, p)),
            tok(A_KV_WIDTH),
            pl.BlockSpec((None, None, A_KV_HEADS, VT_ROWS, TM), lambda p, i: (i, p, 0, 0, 0)),
        ] + [pl.BlockSpec((None, TM // r, r * B_OUT_WIDTH), lambda p, i: (i, p, 0)) for r in dil]
        + [tok(2 * D_MODEL)],
        out_shape=[
            jax.ShapeDtypeStruct((b, A_WIDTH, s), BF16),
            jax.ShapeDtypeStruct((b, s, A_KV_WIDTH), BF16),
            jax.ShapeDtypeStruct((b, nt, A_KV_HEADS, VT_ROWS, TM), BF16),
        ] + [jax.ShapeDtypeStruct((b, s // r, r * B_OUT_WIDTH), BF16) for r in dil]
        + [jax.ShapeDtypeStruct((b, s, 2 * D_MODEL), BF16)],
        scratch_shapes=[pltpu.VMEM((B_WIDTH // LANES, TM, LANES), F32)],
        compiler_params=_params(("arbitrary", "arbitrary")),
        name="inproj",
    )(x, wdt, wb, bg, taq, tbq, tak, tbk)


def _rope_tables(s, gain, scale):
    t = np.arange(s)
    inv = ROPE_THETA ** (-np.arange(0, ROPE_AXIS_DIM, 2, dtype=np.float64) / ROPE_AXIS_DIM)
    pos = np.stack([t // GRID_W, t // GRID_W, t % GRID_W, t % GRID_W], axis=0)
    ang = (pos[:, None, :] * inv[None, :, None]).reshape(HEAD_DIM, s)
    sign = np.repeat(np.array([-1.0, 1.0, -1.0, 1.0]), 16)[:, None]
    partner = np.arange(HEAD_DIM) ^ 16
    cos = jnp.asarray(np.cos(ang), F32)
    sin = jnp.asarray(sign * np.sin(ang), F32)
    g = gain.astype(F32) * scale
    return g[:, None] * cos, g[partner][:, None] * sin


def _attn_a_body(fast_ref, kbound_ref, qt_ref, k_ref, vt_ref, o_ref, rhs_sc, acc_sc, ot_sc):
    kv = pl.program_id(1)
    nchunks = vt_ref.shape[0]
    tk = vt_ref.shape[2]
    row = lax.broadcasted_iota(jnp.int32, (A_KV_WIDTH, NQ), 0)
    own = (row // HEAD_DIM) == kv
    for g in range(A_GROUP):
        qg = qt_ref[g * HEAD_DIM:(g + 1) * HEAD_DIM, :]
        rhs_sc[g] = jnp.where(own, jnp.concatenate([qg, qg], axis=0), jnp.zeros((), BF16))

    def load_chunk(c):
        return k_ref[pl.ds(pl.multiple_of(c * tk, tk), tk), :], vt_ref[c]

    @pl.when(fast_ref[0] == 1)
    def _():
        shift = []
        for g in range(A_GROUP):
            qg = qt_ref[g * HEAD_DIM:(g + 1) * HEAD_DIM, :].astype(F32)
            shift.append(jnp.sqrt(jnp.sum(qg * qg, axis=0, keepdims=True)) * kbound_ref[0])

        def scores(tile):
            c, g = tile
            return jnp.dot(k_ref[c * tk:(c + 1) * tk, :], rhs_sc[g], preferred_element_type=F32)

        heads = range(A_GROUP)
        accs = [None] * A_GROUP
        sts = [scores((0, g)) for g in heads]
        for c in range(nchunks):
            sts_next = [scores((c + 1, g)) for g in heads] if c + 1 < nchunks else None
            ps = [jnp.exp2(sts[g] - shift[g]).astype(BF16) for g in heads]
            pvs = [jnp.dot(vt_ref[c], ps[g], preferred_element_type=F32) for g in heads]
            accs = [pvs[g] if accs[g] is None else accs[g] + pvs[g] for g in heads]
            sts = sts_next
        for g in heads:
            acc_sc[g] = accs[g]

    @pl.when(fast_ref[0] == 0)
    def _():
        acc_sc[...] = jnp.zeros_like(acc_sc)

        def chunk(c, ms):
            kc, vc = load_chunk(c)
            new_ms = []
            for g in range(A_GROUP):
                st = jnp.dot(kc, rhs_sc[g], preferred_element_type=F32)
                m_new = jnp.maximum(ms[g], jnp.max(st, axis=0, keepdims=True))
                alpha = jnp.exp2(ms[g] - m_new)
                p = jnp.exp2(st - m_new).astype(BF16)
                acc_sc[g] = alpha * acc_sc[g] + jnp.dot(vc, p, preferred_element_type=F32)
                new_ms.append(m_new)
            return tuple(new_ms)

        lax.fori_loop(0, nchunks, chunk,
                      tuple(jnp.full((1, NQ), NEG_INF, F32) for _ in range(A_GROUP)))

    for g in range(A_GROUP):
        ot_sc[g * HEAD_DIM:(g + 1) * HEAD_DIM, :] = (
            acc_sc[g, 0:HEAD_DIM, :] / acc_sc[g, HEAD_DIM:HEAD_DIM + 1, :])
    o_ref[...] = ot_sc[...].T.astype(BF16)


def _attn_a_call(fast, kbound, qt, k, vt):
    b, _, s = qt.shape
    nchunks, tk = vt.shape[1], vt.shape[4]
    gw = A_GROUP * HEAD_DIM
    grid_spec = pltpu.PrefetchScalarGridSpec(
        num_scalar_prefetch=2,
        grid=(b, A_KV_HEADS, s // NQ),
        in_specs=[
            pl.BlockSpec((None, gw, NQ), lambda i, h, j, *_: (i, h, j)),
            pl.BlockSpec((None, s, A_KV_WIDTH), lambda i, h, j, *_: (i, 0, 0)),
            pl.BlockSpec((None, nchunks, None, VT_ROWS, tk), lambda i, h, j, *_: (i, 0, h, 0, 0)),
        ],
        out_specs=pl.BlockSpec((None, NQ, gw), lambda i, h, j, *_: (i, j, h)),
        scratch_shapes=[pltpu.VMEM((A_GROUP, A_KV_WIDTH, NQ), BF16),
                        pltpu.VMEM((A_GROUP, VT_ROWS, NQ), F32),
                        pltpu.VMEM((gw, NQ), F32)],
    )
    return pl.pallas_call(
        _attn_a_body,
        grid_spec=grid_spec,
        out_shape=jax.ShapeDtypeStruct((b, s, A_WIDTH), BF16),
        compiler_params=_params(("arbitrary", "arbitrary", "arbitrary")),
        name="attn_a",
    )(fast, kbound, qt, k, vt)


def _score_bound(q_gain, k_gain):
    kb = math.sqrt(HEAD_DIM) * jnp.max(jnp.abs(k_gain.astype(F32)))
    qb = math.sqrt(HEAD_DIM) * jnp.max(jnp.abs(q_gain.astype(F32))) * (HEAD_DIM ** -0.5) * LOG2E
    fast = (qb * kb <= MAX_SAFE_SHIFT).astype(jnp.int32)
    return fast.reshape(1), kb.reshape(1)


def _attn_b_body(q_ref, k_ref, v_ref, bm_ref, o_ref, lse_ref, kp_sc, vp_sc):
    length = q_ref.shape[0]
    nsub = q_ref.shape[1] // B_OUT_WIDTH
    cols = lambda u: slice(u * B_OUT_WIDTH, (u + 1) * B_OUT_WIDTH)
    zeros = jnp.zeros((HALO, B_OUT_WIDTH), BF16)
    for sc, ref in ((kp_sc, k_ref), (vp_sc, v_ref)):
        for u in range(nsub):
            sc[u, 0:HALO, :] = zeros
            sc[u, HALO:HALO + length, :] = ref[:, cols(u)]
            sc[u, HALO + length:, :] = zeros
    row_head = lax.broadcasted_iota(jnp.int32, (B_OUT_WIDTH, NQ), 0) // HEAD_DIM
    ones = jnp.ones((VT_ROWS - HEAD_DIM, BAND), BF16)
    blocks = {}

    def block(key):
        if key not in blocks:
            u, i = key
            q0 = i * NQ
            qt = q_ref[q0:q0 + NQ, cols(u)].astype(F32).T.astype(BF16)
            kband = kp_sc[u, q0:q0 + BAND, :]
            vbt = vp_sc[u, q0:q0 + BAND, :].astype(F32).T.astype(BF16)
            pen = None
            if q0 - HALO < 0 or q0 - HALO + BAND > length:
                key_pos = lax.broadcasted_iota(jnp.int32, (BAND, NQ), 0) + (q0 - HALO)
                pen = jnp.where((key_pos >= 0) & (key_pos < length), 0.0, NEG_INF)
            blocks[key] = (qt, kband, vbt, pen)
        return blocks[key]

    def scores(key, h):
        qt, kband, _, _ = block(key)
        rhs = jnp.where(row_head == h, qt, jnp.zeros((), BF16))
        return jnp.dot(kband, rhs, preferred_element_type=F32)

    heads = range(B_HEADS_PER_GROUP)
    keys = [(u, i) for u in range(nsub) for i in range(length // NQ)]
    sts = [scores(keys[0], h) for h in heads]
    for n, key in enumerate(keys):
        sts_next = [scores(keys[n + 1], h) for h in heads] if n + 1 < len(keys) else None
        _, _, vbt, pen = block(key)
        sts = [sts[h] + bm_ref[h] for h in heads]
        if pen is not None:
            sts = [st + pen for st in sts]
        ms = [jnp.max(st, axis=0, keepdims=True) for st in sts]
        ps = [jnp.exp2(st - m).astype(BF16) for st, m in zip(sts, ms)]
        accs = [jnp.dot(jnp.concatenate([vbt[h * HEAD_DIM:(h + 1) * HEAD_DIM], ones], axis=0), ps[h],
                        preferred_element_type=F32) for h in heads]
        ls = [acc[HEAD_DIM:HEAD_DIM + 1] for acc in accs]
        u, i = key
        rows = slice(i * NQ, (i + 1) * NQ)
        o_ref[rows, cols(u)] = jnp.concatenate([acc[0:HEAD_DIM] / l for acc, l in zip(accs, ls)], axis=0).T
        lse_ref[rows, cols(u)] = jnp.concatenate(
            [jnp.broadcast_to((m + jnp.log2(l)) * LN2, (HEAD_DIM, NQ)) for m, l in zip(ms, ls)], axis=0).T
        del blocks[key]
        sts = sts_next


def _attn_b_call(qg, kg, vg, bm, r):
    b, length, _ = qg.shape
    nsub = min(r, B_SUBSEQ_PER_STEP)
    spec = pl.BlockSpec((None, length, nsub * B_OUT_WIDTH), lambda i, j: (i, 0, j))
    out_sds = jax.ShapeDtypeStruct((b, length, r * B_OUT_WIDTH), F32)
    return pl.pallas_call(
        _attn_b_body,
        grid=(b, r // nsub),
        in_specs=[spec, spec, spec,
                  pl.BlockSpec((B_HEADS_PER_GROUP, BAND, NQ), lambda i, j: (0, 0, 0))],
        out_specs=[spec, spec],
        out_shape=[out_sds, out_sds],
        scratch_shapes=[pltpu.VMEM((nsub, length + 2 * HALO, B_OUT_WIDTH), BF16),
                        pltpu.VMEM((nsub, length + 2 * HALO, B_OUT_WIDTH), BF16)],
        compiler_params=_params(("arbitrary", "arbitrary")),
        name=f"attn_b_r{r}",
    )(qg, kg, vg, bm)


def _t5_bucket_np(rel):
    half = N_BUCKETS // 2
    exact = half // 2
    n = np.abs(rel)
    big = exact + (np.log(np.maximum(n, 1).astype(np.float64) / exact)
                   / math.log(MAX_DISTANCE / exact) * (half - exact)).astype(np.int64)
    big = np.minimum(big, half - 1)
    return np.where(rel > 0, half, 0) + np.where(n < exact, n, big)


def _bias_mask_table(rel_bias, gi, r):
    span = BAND + NQ
    rel = np.arange(span) - (NQ - 1) - HALO
    tab = rel_bias[:, gi * B_HEADS_PER_GROUP:(gi + 1) * B_HEADS_PER_GROUP].astype(F32) * LOG2E
    u = jnp.where(jnp.asarray(np.abs(rel) <= HALO)[:, None],
                  tab[jnp.asarray(_t5_bucket_np(rel * r), jnp.int32)], NEG_INF).T
    shifted = jnp.tile(u, (1, NQ))[:, :NQ * (span - 1)].reshape(-1, NQ, span - 1)
    return jnp.transpose(shifted[:, :, NQ - 1:NQ - 1 + BAND], (0, 2, 1))


def _layer_norm(h, g, b):
    mu = jnp.mean(h, axis=-1, keepdims=True)
    c = h - mu
    var = jnp.mean(c * c, axis=-1, keepdims=True)
    return c * lax.rsqrt(var + LN_EPS) * g + b


def _split_bf16(a):
    hi = a.astype(BF16)
    return hi, (a - hi.astype(F32)).astype(BF16)


def _merge_body(x_ref, ya_ref, o1_ref, o2_ref, o3_ref, l1_ref, l2_ref, l3_ref, gates_ref,
                wa_ref, wbr_ref, wo_ref, g_ref, b_ref, wr2_ref, br_ref, out_ref, bkt_ref, *tok_sc):
    slabs = B_OUT_WIDTH // LANES
    dil = [r for _, r in DILATED_PAIRS]
    b_refs = (o1_ref, o2_ref, o3_ref, l1_ref, l2_ref, l3_ref)
    for ref, sc, r in zip(b_refs, tok_sc, dil + dil):
        if r > 1:
            for j in range(r):
                for c in range(slabs):
                    lanes = slice(j * B_OUT_WIDTH + c * LANES, j * B_OUT_WIDTH + (c + 1) * LANES)
                    sc[c, pl.ds(j, TMG // r, stride=r), :] = ref[:, lanes]

    def token_major(which, rows):
        ref, sc, r = b_refs[which], tok_sc[which], (dil + dil)[which]
        if r == 1:
            return ref[rows, :]
        return jnp.concatenate([sc[c, rows, :] for c in range(slabs)], axis=1)

    subs = [slice(i * TSUB, (i + 1) * TSUB) for i in range(TMG // TSUB)]
    ybs = []
    for rows in subs:
        o1, o2, o3, l1, l2, l3 = (token_major(w, rows) for w in range(6))
        mx = jnp.maximum(jnp.maximum(l1, l2), l3)
        e1, e2, e3 = jnp.exp(l1 - mx), jnp.exp(l2 - mx), jnp.exp(l3 - mx)
        ybs.append(((e1 * o1 + e2 * o2 + e3 * o3) / (e1 + e2 + e3)).astype(BF16))
    ya_ps = [jnp.dot(ya_ref[rows, :], wa_ref[...], preferred_element_type=F32) for rows in subs]
    yb_ps = [jnp.dot(yb, wbr_ref[...], preferred_element_type=F32) for yb in ybs]
    mergeds = [(gates_ref[rows, 0:D_MODEL].astype(F32) * ya_p
                + gates_ref[rows, D_MODEL:2 * D_MODEL].astype(F32) * yb_p).astype(BF16)
               for rows, ya_p, yb_p in zip(subs, ya_ps, yb_ps)]
    hs = [ALPHA * x_ref[rows, :] + jnp.dot(m, wo_ref[...], preferred_element_type=F32)
          for rows, m in zip(subs, mergeds)]
    x1s = [_layer_norm(h, g_ref[...], b_ref[...]) for h in hs]
    for rows, x1 in zip(subs, x1s):
        out_ref[rows, 0:D_MODEL] = x1

    splits = [_split_bf16(x1) for x1 in x1s]
    boths = [jnp.dot(xh, wr2_ref[...], preferred_element_type=F32)
             + jnp.dot(xl, wr2_ref[...], preferred_element_type=F32) for xh, xl in splits]
    logits = jnp.concatenate(
        [(both[:, 0:ROUTE_COLS] + both[:, ROUTE_COLS:] + br_ref[...]).T for both in boths], axis=1)
    col = lambda i: logits[i:i + 1, :]
    lg = [col(i) for i in range(N_GROUPS)]
    gmax = functools.reduce(jnp.maximum, lg)
    top_p = 1.0 / functools.reduce(lambda a, c: a + c, [jnp.exp(v - gmax) for v in lg])
    gsel = jnp.full_like(gmax, N_GROUPS - 1).astype(jnp.int32)
    for i in reversed(range(N_GROUPS - 1)):
        gsel = jnp.where(lg[i] >= gmax, i, gsel)
    sel = []
    for e in range(EXPERTS_PER_GROUP):
        v = col(N_GROUPS + (N_GROUPS - 1) * EXPERTS_PER_GROUP + e)
        for gi in reversed(range(N_GROUPS - 1)):
            v = jnp.where(gsel == gi, col(N_GROUPS + gi * EXPERTS_PER_GROUP + e), v)
        sel.append(v)
    rank = []
    for e in range(EXPERTS_PER_GROUP):
        rk = jnp.zeros_like(gsel)
        for j in range(EXPERTS_PER_GROUP):
            if j != e:
                ahead = (sel[j] > sel[e]) | ((sel[j] == sel[e]) & (j < e))
                rk = rk + ahead.astype(jnp.int32)
        rank.append(rk)
    pick = lambda k: functools.reduce(
        lambda a, c: a + c, [jnp.where(rank[e] == k, sel[e], 0.0) for e in range(EXPERTS_PER_GROUP)])
    t = jnp.exp(pick(1) - pick(0))
    w_first = top_p / (1.0 + t)
    w_second = top_p * t / (1.0 + t)
    chosen = [rank[e] < 2 for e in range(EXPERTS_PER_GROUP)]
    weight = [jnp.where(rank[e] == 0, w_first, w_second) for e in range(EXPERTS_PER_GROUP)]
    pair = jnp.zeros_like(gsel)
    w_lo = jnp.zeros_like(top_p)
    w_hi = jnp.zeros_like(top_p)
    for pi, (a, c) in enumerate(PAIRS):
        hit = chosen[a] & chosen[c]
        pair = jnp.where(hit, pi, pair)
        w_lo = jnp.where(hit, weight[a], w_lo)
        w_hi = jnp.where(hit, weight[c], w_hi)
    bkt_ref[...] = gsel * len(PAIRS) + pair
    row = lax.broadcasted_iota(jnp.int32, (F32_SUBLANES, TMG), 0)
    head = jnp.where(row == 1, w_lo, jnp.where(row == 2, w_hi, 0.0))
    route_t = jnp.concatenate([head, jnp.zeros((ROUTE_COLS - F32_SUBLANES, TMG), F32)], axis=0)
    out_ref[:, D_MODEL:] = route_t.T


def _merge_call(x, ya, os, ls, gates, wa, wbr, wo, g, b, wr2, br):
    t, d = x.shape
    tok = lambda c: pl.BlockSpec((TMG, c), lambda i: (i, 0))
    const = lambda a: pl.BlockSpec(a.shape, lambda i: (0,) * a.ndim)
    consts = (wa, wbr, wo, g, b, wr2, br)
    dil = [pl.BlockSpec((TMG // r, r * B_OUT_WIDTH), lambda i: (i, 0)) for _, r in DILATED_PAIRS]
    return pl.pallas_call(
        _merge_body,
        grid=(t // TMG,),
        in_specs=[tok(d), tok(A_WIDTH)] + dil + dil + [tok(2 * D_MODEL)]
        + [const(a) for a in consts],
        out_specs=[tok(d + ROUTE_COLS), pl.BlockSpec((1, TMG), lambda i: (0, i))],
        out_shape=[jax.ShapeDtypeStruct((t, d + ROUTE_COLS), F32), jax.ShapeDtypeStruct((1, t), jnp.int32)],
        scratch_shapes=[pltpu.VMEM((B_OUT_WIDTH // LANES, TMG, LANES), F32)] * (2 * N_DIL_GROUPS),
        compiler_params=_params(("arbitrary",)),
        name="merge_ln1_route",
    )(x, ya, *os, *ls, gates, *consts)


def _sc_gather(table, idx):
    nrows, d = idx.shape[0], table.shape[1]
    workers = SC_CORES * SC_SUBCORES
    per_worker = nrows // workers
    assert nrows % (workers * SC_ROWS) == 0
    mesh = plsc.VectorSubcoreMesh(core_axis_name="c", subcore_axis_name="s")

    @functools.partial(
        pl.kernel, mesh=mesh, out_type=jax.ShapeDtypeStruct((nrows, d), table.dtype),
        scratch_types=[pltpu.VMEM((SC_ROWS,), jnp.int32), pltpu.VMEM((SC_ROWS, d), table.dtype),
                       pltpu.SemaphoreType.DMA])
    def gather(table_hbm, idx_hbm, out_hbm, idx_v, rows_v, sem):
        base = (lax.axis_index("s") * SC_CORES + lax.axis_index("c")) * per_worker

        @pl.loop(0, per_worker, step=SC_ROWS)
        def _(off):
            start = pl.multiple_of(base + off, SC_ROWS)
            pltpu.sync_copy(idx_hbm.at[pl.ds(start, SC_ROWS)], idx_v)
            pltpu.async_copy(table_hbm.at[idx_v], rows_v, sem).wait()
            pltpu.sync_copy(rows_v, out_hbm.at[pl.ds(start, SC_ROWS)])

    return gather(table, idx)


def _moe_tiles_body(nvalid_ref, elo_ref, ehi_ref, x_ref, wil_ref, wih_ref, wol_ref, woh_ref,
                    g_ref, b_ref, o_ref):
    del elo_ref, ehi_ref
    n = nvalid_ref[pl.program_id(0)]

    @pl.when(n == 0)
    def _():
        o_ref[...] = jnp.zeros_like(o_ref)

    @pl.when(n > 0)
    def _():
        xt = x_ref[:, 0:D_MODEL]
        xb = xt.astype(BF16)
        hcats = [jnp.dot(xb, wi_ref[...], preferred_element_type=F32)
                 for wi_ref in (wil_ref, wih_ref)]
        acts = [(h[:, 0:D_EXPERT] * jax.nn.sigmoid(h[:, 0:D_EXPERT]) * h[:, D_EXPERT:]).astype(BF16)
                for h in hcats]
        y_lo, y_hi = (jnp.dot(act, wo_ref[...], preferred_element_type=F32)
                      for act, wo_ref in zip(acts, (wol_ref, woh_ref)))
        moe = (x_ref[:, D_MODEL + 1:D_MODEL + 2] * y_lo + x_ref[:, D_MODEL + 2:D_MODEL + 3] * y_hi)
        o_ref[...] = _layer_norm(ALPHA * xt + moe, g_ref[...], b_ref[...])


def _moe_tiles_call(x_sorted, nvalid, elo, ehi, wei, weo, g, b):
    ntiles = nvalid.shape[0]
    wi_spec = lambda e: pl.BlockSpec((None, D_MODEL, 2 * D_EXPERT), lambda i, nv, lo, hi: (e(lo, hi)[i], 0, 0))
    wo_spec = lambda e: pl.BlockSpec((None, D_EXPERT, D_MODEL), lambda i, nv, lo, hi: (e(lo, hi)[i], 0, 0))
    first = lambda lo, hi: lo
    second = lambda lo, hi: hi
    vec = pl.BlockSpec((1, D_MODEL), lambda i, nv, lo, hi: (0, 0))
    grid_spec = pltpu.PrefetchScalarGridSpec(
        num_scalar_prefetch=3,
        grid=(ntiles,),
        in_specs=[pl.BlockSpec((TMOE, D_MODEL + ROUTE_COLS), lambda i, nv, lo, hi: (i, 0)),
                  wi_spec(first), wi_spec(second), wo_spec(first), wo_spec(second), vec, vec],
        out_specs=pl.BlockSpec((TMOE, D_MODEL), lambda i, nv, lo, hi: (i, 0)),
    )
    return pl.pallas_call(
        _moe_tiles_body,
        grid_spec=grid_spec,
        out_shape=jax.ShapeDtypeStruct((ntiles * TMOE, D_MODEL), F32),
        compiler_params=_params(("arbitrary",)),
        name="moe_ln2",
    )(nvalid, elo, ehi, x_sorted, wei, wei, weo, weo, g, b)


def _route_plan(bucket):
    t = bucket.shape[0]
    ntiles = t // TMOE + N_ROUTE_BUCKETS
    token = jnp.arange(t, dtype=jnp.int32)
    sorted_bkt, order = lax.sort((bucket, token), num_keys=1, is_stable=True)
    counts = jnp.sum(bucket[None, :] == jnp.arange(N_ROUTE_BUCKETS, dtype=jnp.int32)[:, None],
                     axis=1, dtype=jnp.int32)
    row_start = jnp.cumsum(counts) - counts
    tiles_per = (counts + TMOE - 1) // TMOE
    tile_end = jnp.cumsum(tiles_per)
    tile_start = tile_end - tiles_per
    total = tile_end[-1]
    tile = jnp.arange(ntiles, dtype=jnp.int32)
    used = tile < total
    bkt = jnp.sum(jnp.minimum(tile, total - 1)[:, None] >= tile_end[None, :], axis=1, dtype=jnp.int32)
    local = tile - tile_start[bkt]
    row0 = row_start[bkt] + local * TMOE
    nvalid = jnp.where(used, jnp.clip(counts[bkt] - local * TMOE, 0, TMOE), 0).astype(jnp.int32)
    pairs = jnp.asarray(PAIRS, jnp.int32)
    grp = bkt // len(PAIRS)
    elo = grp * EXPERTS_PER_GROUP + pairs[bkt % len(PAIRS), 0]
    ehi = grp * EXPERTS_PER_GROUP + pairs[bkt % len(PAIRS), 1]
    idx = jnp.minimum(row0[:, None] + jnp.arange(TMOE, dtype=jnp.int32)[None, :], t - 1)
    ids = order[idx].reshape(ntiles * TMOE)
    padded_row = tile_start[sorted_bkt] * TMOE + (token - row_start[sorted_bkt])
    _, pos = lax.sort((order, padded_row), num_keys=1)
    return nvalid, elo.astype(jnp.int32), ehi.astype(jnp.int32), ids, pos


def _prepare(w_in, b_gate, q_gain, k_gain, rel_bias, w_branch_a, w_branch_b, w_out, ln1_g, ln1_b,
             w_route_group, b_route_group, w_route_expert, b_route_expert, w_expert_in,
             w_expert_out, ln2_g, ln2_b, seq):
    row = lambda v: v.reshape(1, -1).astype(F32)
    w = w_in[0]
    wr = jnp.concatenate([w_route_group[0]]
                         + [w_route_expert[0, gi] for gi in range(N_GROUPS)], axis=1)
    wr = jnp.pad(wr.astype(F32), ((0, 0), (0, ROUTE_COLS - wr.shape[1])))
    br = jnp.concatenate([b_route_group[0], b_route_expert[0].reshape(-1)])
    br = jnp.pad(br.astype(F32), (0, ROUTE_COLS - br.shape[0])).reshape(1, ROUTE_COLS)
    wr2 = jnp.concatenate(_split_bf16(wr), axis=1)
    scale = HEAD_DIM ** -0.5
    wb = w[:, DT_ROWS:]
    wb = jnp.concatenate([wb[:, 0:B_WIDTH] * (scale * LOG2E), wb[:, B_WIDTH:]], axis=1)
    return dict(
        wdt=w[:, 0:DT_ROWS].T.astype(BF16), wb=wb.astype(BF16), bg=row(b_gate[0]),
        rope_q=_rope_tables(seq, q_gain[0], scale * LOG2E), rope_k=_rope_tables(seq, k_gain[0], 1.0),
        score_bound=_score_bound(q_gain[0], k_gain[0]),
        bm=[_bias_mask_table(rel_bias, gi, r) for gi, (_, r) in enumerate(DILATED_PAIRS)],
        wa=w_branch_a[0].astype(BF16), wbr=w_branch_b[0].astype(BF16), wo=w_out[0].astype(BF16),
        ln1=(row(ln1_g[0]), row(ln1_b[0])), wr2=wr2, br=br,
        wei=w_expert_in[0].astype(BF16), weo=w_expert_out[0].astype(BF16),
        ln2=(row(ln2_g[0]), row(ln2_b[0])),
    )


def _trunk(x, p):
    b, s, d = x.shape
    qt, k, vt, *qkv_b, gates = _inproj_call(x, p["wdt"], p["wb"], p["bg"], *p["rope_q"], *p["rope_k"])
    ya = _attn_a_call(*p["score_bound"], qt, k, vt)
    os, ls = [], []
    for gi, (_, r) in enumerate(DILATED_PAIRS):
        qg, kg, vg = (qkv_b[i * N_DIL_GROUPS + gi] for i in range(3))
        o, lse = _attn_b_call(qg, kg, vg, p["bm"][gi], r)
        os.append(o.reshape(b * s // r, r * B_OUT_WIDTH))
        ls.append(lse.reshape(b * s // r, r * B_OUT_WIDTH))
    return _merge_call(x.reshape(b * s, d), ya.reshape(b * s, A_WIDTH), os, ls,
                       gates.reshape(b * s, 2 * D_MODEL), p["wa"], p["wbr"], p["wo"], *p["ln1"],
                       p["wr2"], p["br"])


def _experts(x1ext, bucket, p, shape):
    nvalid, elo, ehi, ids, pos = _route_plan(bucket.reshape(-1))
    x_sorted = _sc_gather(x1ext, ids)
    y_sorted = _moe_tiles_call(x_sorted, nvalid, elo, ehi, p["wei"], p["weo"], *p["ln2"])
    return _sc_gather(y_sorted, pos).reshape(shape)


def kernel(x_prompt, x_sample, w_in, b_gate, q_gain, k_gain, rel_bias, w_branch_a, w_branch_b, w_out,
           ln1_g, ln1_b, w_route_group, b_route_group, w_route_expert, b_route_expert, w_expert_in,
           w_expert_out, ln2_g, ln2_b):
    assert x_prompt.shape[1] == x_sample.shape[1]
    p = _prepare(w_in, b_gate, q_gain, k_gain, rel_bias, w_branch_a, w_branch_b, w_out, ln1_g, ln1_b,
                 w_route_group, b_route_group, w_route_expert, b_route_expert, w_expert_in,
                 w_expert_out, ln2_g, ln2_b, x_prompt.shape[1])
    mid_prompt = _trunk(x_prompt, p)
    mid_sample = _trunk(x_sample, p)
    y_sample = _experts(*mid_sample, p, x_sample.shape)
    y_prompt = _experts(*mid_prompt, p, x_prompt.shape)
    return (y_prompt, y_sample)
```

```python
import functools
import math

import jax
import jax.numpy as jnp
import numpy as np
from jax import lax
from jax.experimental import pallas as pl
from jax.experimental.pallas import tpu as pltpu
from jax.experimental.pallas import tpu_sc as plsc

F32 = jnp.float32
BF16 = jnp.bfloat16

D_MODEL = 1024
HEAD_DIM = 64
A_Q_HEADS = 8
A_KV_HEADS = 2
A_GROUP = A_Q_HEADS // A_KV_HEADS
GRID_W = 64
ROPE_AXIS_DIM = HEAD_DIM // 2
ROPE_THETA = 10000.0
DILATED_PAIRS = ((128, 1), (512, 4), (2048, 16))
N_DIL_GROUPS = len(DILATED_PAIRS)
B_HEADS_PER_GROUP = 4
N_BUCKETS = 32
MAX_DISTANCE = 1024
A_WIDTH = A_Q_HEADS * HEAD_DIM
A_KV_WIDTH = A_KV_HEADS * HEAD_DIM
B_WIDTH = B_HEADS_PER_GROUP * N_DIL_GROUPS * HEAD_DIM
B_OUT_WIDTH = B_HEADS_PER_GROUP * HEAD_DIM
DT_ROWS = A_WIDTH + 2 * A_KV_WIDTH
N_GROUPS = 4
EXPERTS_PER_GROUP = 4
N_EXPERTS = N_GROUPS * EXPERTS_PER_GROUP
D_EXPERT = 512
PAIRS = ((0, 1), (0, 2), (0, 3), (1, 2), (1, 3), (2, 3))
N_ROUTE_BUCKETS = N_GROUPS * len(PAIRS)
ALPHA = 2.0 ** 0.25
RMS_EPS = 1e-6
LN_EPS = 1e-5
NEG_INF = -1e30

LANES = 128
TM = 512
TMG = 512
TSUB = 128
NQ = 256
BF16_SUBLANES = 16
F32_SUBLANES = 8
VT_ROWS = HEAD_DIM + BF16_SUBLANES
LOG2E = 1.4426950408889634
LN2 = 0.6931471805599453
MAX_SAFE_SHIFT = 40.0
HALO = 64
BAND = NQ + 2 * HALO
B_SUBSEQ_PER_STEP = 4
TMOE = 256
ROUTE_COLS = LANES
SC_CORES = 2
SC_SUBCORES = 16
SC_ROWS = 32
VMEM_LIMIT = 56 * 1024 * 1024


def _params(sem):
    return pltpu.CompilerParams(dimension_semantics=sem, vmem_limit_bytes=VMEM_LIMIT)


def _swap16(x):
    return jnp.concatenate([x[16:32], x[0:16], x[48:64], x[32:48]], axis=0)


def _norm_rope(xh, ta, tb):
    r = lax.rsqrt(jnp.mean(xh * xh, axis=0, keepdims=True) + RMS_EPS)
    return (xh * ta + _swap16(xh) * tb) * r


def _inproj_body(x_ref, wdt_ref, wb_ref, bg_ref, taq_ref, tbq_ref, tak_ref, tbk_ref,
                 qt_ref, k_ref, vt_ref, *rest):
    b_refs, gates_ref, tmp_sc = rest[:3 * N_DIL_GROUPS], rest[3 * N_DIL_GROUPS], rest[-1]
    xb = x_ref[...].astype(BF16)
    dt = lax.dot_general(wdt_ref[...], xb, (((1,), (1,)), ((), ())),
                         preferred_element_type=F32)
    taq, tbq = taq_ref[...], tbq_ref[...]
    for h in range(A_Q_HEADS):
        sl = slice(h * HEAD_DIM, (h + 1) * HEAD_DIM)
        qt_ref[sl, :] = _norm_rope(dt[sl], taq, tbq).astype(BF16)
    tak, tbk = tak_ref[...], tbk_ref[...]
    kt = jnp.concatenate(
        [_norm_rope(dt[A_WIDTH + h * HEAD_DIM:A_WIDTH + (h + 1) * HEAD_DIM], tak, tbk)
         for h in range(A_KV_HEADS)], axis=0)
    k_ref[...] = kt.T.astype(BF16)
    for h in range(A_KV_HEADS):
        v0 = A_WIDTH + A_KV_WIDTH + h * HEAD_DIM
        vt_ref[h, 0:HEAD_DIM, :] = dt[v0:v0 + HEAD_DIM].astype(BF16)
        vt_ref[h, HEAD_DIM:, :] = jnp.ones((VT_ROWS - HEAD_DIM, TM), BF16)
    for i in range(3):
        res = jnp.dot(xb, wb_ref[:, i * B_WIDTH:(i + 1) * B_WIDTH], preferred_element_type=F32)
        for c in range(B_WIDTH // LANES):
            tmp_sc[c] = res[:, c * LANES:(c + 1) * LANES]
        for gi, (_, r) in enumerate(DILATED_PAIRS):
            ref = b_refs[i * N_DIL_GROUPS + gi]
            for j in range(r):
                for half in range(B_OUT_WIDTH // LANES):
                    c = gi * (B_OUT_WIDTH // LANES) + half
                    lanes = slice(j * B_OUT_WIDTH + half * LANES, j * B_OUT_WIDTH + (half + 1) * LANES)
                    ref[:, lanes] = tmp_sc[c, pl.ds(j, TM // r, stride=r), :].astype(BF16)
    g0 = 3 * B_WIDTH
    for i in range(2):
        cols = slice(g0 + i * D_MODEL, g0 + (i + 1) * D_MODEL)
        pre = jnp.dot(xb, wb_ref[:, cols], preferred_element_type=F32)
        gates_ref[:, i * D_MODEL:(i + 1) * D_MODEL] = jax.nn.sigmoid(
            pre + bg_ref[:, i * D_MODEL:(i + 1) * D_MODEL]).astype(BF16)


def _inproj_call(x, wdt, wb, bg, taq, tbq, tak, tbk):
    b, s, d = x.shape
    nt = s // TM
    wcols = wb.shape[1]
    tab = pl.BlockSpec((HEAD_DIM, TM), lambda p, i: (0, p))
    const = lambda shape: pl.BlockSpec(shape, lambda p, i: (0,) * len(shape))
    tok = lambda c: pl.BlockSpec((None, TM, c), lambda p, i: (i, p, 0))
    dil = [r for _ in range(3) for _, r in DILATED_PAIRS]
    return pl.pallas_call(
        _inproj_body,
        grid=(nt, b),
        in_specs=[tok(d), const((DT_ROWS, d)), const((d, wcols)), const((1, 2 * D_MODEL)),
                  tab, tab, tab, tab],
        out_specs=[
            pl.BlockSpec((None, A_WIDTH, TM), lambda p, i: (i, 0, p)),
            tok(A_KV_WIDTH),
            pl.BlockSpec((None, None, A_KV_HEADS, VT_ROWS, TM), lambda p, i: (i, p, 0, 0, 0)),
        ] + [pl.BlockSpec((None, TM // r, r * B_OUT_WIDTH), lambda p, i: (i, p, 0)) for r in dil]
        + [tok(2 * D_MODEL)],
        out_shape=[
            jax.ShapeDtypeStruct((b, A_WIDTH, s), BF16),
            jax.ShapeDtypeStruct((b, s, A_KV_WIDTH), BF16),
            jax.ShapeDtypeStruct((b, nt, A_KV_HEADS, VT_ROWS, TM), BF16),
        ] + [jax.ShapeDtypeStruct((b, s // r, r * B_OUT_WIDTH), BF16) for r in dil]
        + [jax.ShapeDtypeStruct((b, s, 2 * D_MODEL), BF16)],
        scratch_shapes=[pltpu.VMEM((B_WIDTH // LANES, TM, LANES), F32)],
        compiler_params=_params(("arbitrary", "arbitrary")),
        name="inproj",
    )(x, wdt, wb, bg, taq, tbq, tak, tbk)


def _rope_tables(s, gain, scale):
    t = np.arange(s)
    inv = ROPE_THETA ** (-np.arange(0, ROPE_AXIS_DIM, 2, dtype=np.float64) / ROPE_AXIS_DIM)
    pos = np.stack([t // GRID_W, t // GRID_W, t % GRID_W, t % GRID_W], axis=0)
    ang = (pos[:, None, :] * inv[None, :, None]).reshape(HEAD_DIM, s)
    sign = np.repeat(np.array([-1.0, 1.0, -1.0, 1.0]), 16)[:, None]
    partner = np.arange(HEAD_DIM) ^ 16
    cos = jnp.asarray(np.cos(ang), F32)
    sin = jnp.asarray(sign * np.sin(ang), F32)
    g = gain.astype(F32) * scale
    return g[:, None] * cos, g[partner][:, None] * sin


def _attn_a_body(fast_ref, kbound_ref, qt_ref, k_ref, vt_ref, o_ref, rhs_sc, acc_sc, ot_sc):
    kv = pl.program_id(1)
    nchunks = vt_ref.shape[0]
    tk = vt_ref.shape[2]
    row = lax.broadcasted_iota(jnp.int32, (A_KV_WIDTH, NQ), 0)
    own = (row // HEAD_DIM) == kv
    for g in range(A_GROUP):
        qg = qt_ref[g * HEAD_DIM:(g + 1) * HEAD_DIM, :]
        rhs_sc[g] = jnp.where(own, jnp.concatenate([qg, qg], axis=0), jnp.zeros((), BF16))

    def load_chunk(c):
        return k_ref[pl.ds(pl.multiple_of(c * tk, tk), tk), :], vt_ref[c]

    @pl.when(fast_ref[0] == 1)
    def _():
        shift = []
        for g in range(A_GROUP):
            qg = qt_ref[g * HEAD_DIM:(g + 1) * HEAD_DIM, :].astype(F32)
            shift.append(jnp.sqrt(jnp.sum(qg * qg, axis=0, keepdims=True)) * kbound_ref[0])

        def scores(tile):
            c, g = tile
            return jnp.dot(k_ref[c * tk:(c + 1) * tk, :], rhs_sc[g], preferred_element_type=F32)

        heads = range(A_GROUP)
        accs = [None] * A_GROUP
        sts = [scores((0, g)) for g in heads]
        for c in range(nchunks):
            sts_next = [scores((c + 1, g)) for g in heads] if c + 1 < nchunks else None
            ps = [jnp.exp2(sts[g] - shift[g]).astype(BF16) for g in heads]
            pvs = [jnp.dot(vt_ref[c], ps[g], preferred_element_type=F32) for g in heads]
            accs = [pvs[g] if accs[g] is None else accs[g] + pvs[g] for g in heads]
            sts = sts_next
        for g in heads:
            acc_sc[g] = accs[g]

    @pl.when(fast_ref[0] == 0)
    def _():
        acc_sc[...] = jnp.zeros_like(acc_sc)

        def chunk(c, ms):
            kc, vc = load_chunk(c)
            new_ms = []
            for g in range(A_GROUP):
                st = jnp.dot(kc, rhs_sc[g], preferred_element_type=F32)
                m_new = jnp.maximum(ms[g], jnp.max(st, axis=0, keepdims=True))
                alpha = jnp.exp2(ms[g] - m_new)
                p = jnp.exp2(st - m_new).astype(BF16)
                acc_sc[g] = alpha * acc_sc[g] + jnp.dot(vc, p, preferred_element_type=F32)
                new_ms.append(m_new)
            return tuple(new_ms)

        lax.fori_loop(0, nchunks, chunk,
                      tuple(jnp.full((1, NQ), NEG_INF, F32) for _ in range(A_GROUP)))

    for g in range(A_GROUP):
        ot_sc[g * HEAD_DIM:(g + 1) * HEAD_DIM, :] = (
            acc_sc[g, 0:HEAD_DIM, :] / acc_sc[g, HEAD_DIM:HEAD_DIM + 1, :])
    o_ref[...] = ot_sc[...].T.astype(BF16)


def _attn_a_call(fast, kbound, qt, k, vt):
    b, _, s = qt.shape
    nchunks, tk = vt.shape[1], vt.shape[4]
    gw = A_GROUP * HEAD_DIM
    grid_spec = pltpu.PrefetchScalarGridSpec(
        num_scalar_prefetch=2,
        grid=(b, A_KV_HEADS, s // NQ),
        in_specs=[
            pl.BlockSpec((None, gw, NQ), lambda i, h, j, *_: (i, h, j)),
            pl.BlockSpec((None, s, A_KV_WIDTH), lambda i, h, j, *_: (i, 0, 0)),
            pl.BlockSpec((None, nchunks, None, VT_ROWS, tk), lambda i, h, j, *_: (i, 0, h, 0, 0)),
        ],
        out_specs=pl.BlockSpec((None, NQ, gw), lambda i, h, j, *_: (i, j, h)),
        scratch_shapes=[pltpu.VMEM((A_GROUP, A_KV_WIDTH, NQ), BF16),
                        pltpu.VMEM((A_GROUP, VT_ROWS, NQ), F32),
                        pltpu.VMEM((gw, NQ), F32)],
    )
    return pl.pallas_call(
        _attn_a_body,
        grid_spec=grid_spec,
        out_shape=jax.ShapeDtypeStruct((b, s, A_WIDTH), BF16),
        compiler_params=_params(("arbitrary", "arbitrary", "arbitrary")),
        name="attn_a",
    )(fast, kbound, qt, k, vt)


def _score_bound(q_gain, k_gain):
    kb = math.sqrt(HEAD_DIM) * jnp.max(jnp.abs(k_gain.astype(F32)))
    qb = math.sqrt(HEAD_DIM) * jnp.max(jnp.abs(q_gain.astype(F32))) * (HEAD_DIM ** -0.5) * LOG2E
    fast = (qb * kb <= MAX_SAFE_SHIFT).astype(jnp.int32)
    return fast.reshape(1), kb.reshape(1)


def _attn_b_body(q_ref, k_ref, v_ref, bm_ref, o_ref, lse_ref, kp_sc, vp_sc):
    length = q_ref.shape[0]
    nsub = q_ref.shape[1] // B_OUT_WIDTH
    cols = lambda u: slice(u * B_OUT_WIDTH, (u + 1) * B_OUT_WIDTH)
    zeros = jnp.zeros((HALO, B_OUT_WIDTH), BF16)
    for sc, ref in ((kp_sc, k_ref), (vp_sc, v_ref)):
        for u in range(nsub):
            sc[u, 0:HALO, :] = zeros
            sc[u, HALO:HALO + length, :] = ref[:, cols(u)]
            sc[u, HALO + length:, :] = zeros
    row_head = lax.broadcasted_iota(jnp.int32, (B_OUT_WIDTH, NQ), 0) // HEAD_DIM
    ones = jnp.ones((VT_ROWS - HEAD_DIM, BAND), BF16)
    blocks = {}

    def block(key):
        if key not in blocks:
            u, i = key
            q0 = i * NQ
            qt = q_ref[q0:q0 + NQ, cols(u)].astype(F32).T.astype(BF16)
            kband = kp_sc[u, q0:q0 + BAND, :]
            vbt = vp_sc[u, q0:q0 + BAND, :].astype(F32).T.astype(BF16)
            pen = None
            if q0 - HALO < 0 or q0 - HALO + BAND > length:
                key_pos = lax.broadcasted_iota(jnp.int32, (BAND, NQ), 0) + (q0 - HALO)
                pen = jnp.where((key_pos >= 0) & (key_pos < length), 0.0, NEG_INF)
            blocks[key] = (qt, kband, vbt, pen)
        return blocks[key]

    def scores(key, h):
        qt, kband, _, _ = block(key)
        rhs = jnp.where(row_head == h, qt, jnp.zeros((), BF16))
        return jnp.dot(kband, rhs, preferred_element_type=F32)

    heads = range(B_HEADS_PER_GROUP)
    keys = [(u, i) for u in range(nsub) for i in range(length // NQ)]
    sts = [scores(keys[0], h) for h in heads]
    for n, key in enumerate(keys):
        sts_next = [scores(keys[n + 1], h) for h in heads] if n + 1 < len(keys) else None
        _, _, vbt, pen = block(key)
        sts = [sts[h] + bm_ref[h] for h in heads]
        if pen is not None:
            sts = [st + pen for st in sts]
        ms = [jnp.max(st, axis=0, keepdims=True) for st in sts]
        ps = [jnp.exp2(st - m).astype(BF16) for st, m in zip(sts, ms)]
        accs = [jnp.dot(jnp.concatenate([vbt[h * HEAD_DIM:(h + 1) * HEAD_DIM], ones], axis=0), ps[h],
                        preferred_element_type=F32) for h in heads]
        ls = [acc[HEAD_DIM:HEAD_DIM + 1] for acc in accs]
        u, i = key
        rows = slice(i * NQ, (i + 1) * NQ)
        o_ref[rows, cols(u)] = jnp.concatenate([acc[0:HEAD_DIM] / l for acc, l in zip(accs, ls)], axis=0).T
        lse_ref[rows, cols(u)] = jnp.concatenate(
            [jnp.broadcast_to((m + jnp.log2(l)) * LN2, (HEAD_DIM, NQ)) for m, l in zip(ms, ls)], axis=0).T
        del blocks[key]
        sts = sts_next


def _attn_b_call(qg, kg, vg, bm, r):
    b, length, _ = qg.shape
    nsub = min(r, B_SUBSEQ_PER_STEP)
    spec = pl.BlockSpec((None, length, nsub * B_OUT_WIDTH), lambda i, j: (i, 0, j))
    out_sds = jax.ShapeDtypeStruct((b, length, r * B_OUT_WIDTH), F32)
    return pl.pallas_call(
        _attn_b_body,
        grid=(b, r // nsub),
        in_specs=[spec, spec, spec,
                  pl.BlockSpec((B_HEADS_PER_GROUP, BAND, NQ), lambda i, j: (0, 0, 0))],
        out_specs=[spec, spec],
        out_shape=[out_sds, out_sds],
        scratch_shapes=[pltpu.VMEM((nsub, length + 2 * HALO, B_OUT_WIDTH), BF16),
                        pltpu.VMEM((nsub, length + 2 * HALO, B_OUT_WIDTH), BF16)],
        compiler_params=_params(("arbitrary", "arbitrary")),
        name=f"attn_b_r{r}",
    )(qg, kg, vg, bm)


def _t5_bucket_np(rel):
    half = N_BUCKETS // 2
    exact = half // 2
    n = np.abs(rel)
    big = exact + (np.log(np.maximum(n, 1).astype(np.float64) / exact)
                   / math.log(MAX_DISTANCE / exact) * (half - exact)).astype(np.int64)
    big = np.minimum(big, half - 1)
    return np.where(rel > 0, half, 0) + np.where(n < exact, n, big)


def _bias_mask_table(rel_bias, gi, r):
    span = BAND + NQ
    rel = np.arange(span) - (NQ - 1) - HALO
    tab = rel_bias[:, gi * B_HEADS_PER_GROUP:(gi + 1) * B_HEADS_PER_GROUP].astype(F32) * LOG2E
    u = jnp.where(jnp.asarray(np.abs(rel) <= HALO)[:, None],
                  tab[jnp.asarray(_t5_bucket_np(rel * r), jnp.int32)], NEG_INF).T
    shifted = jnp.tile(u, (1, NQ))[:, :NQ * (span - 1)].reshape(-1, NQ, span - 1)
    return jnp.transpose(shifted[:, :, NQ - 1:NQ - 1 + BAND], (0, 2, 1))


def _layer_norm(h, g, b):
    mu = jnp.mean(h, axis=-1, keepdims=True)
    c = h - mu
    var = jnp.mean(c * c, axis=-1, keepdims=True)
    return c * lax.rsqrt(var + LN_EPS) * g + b


def _split_bf16(a):
    hi = a.astype(BF16)
    return hi, (a - hi.astype(F32)).astype(BF16)


def _merge_body(x_ref, ya_ref, o1_ref, o2_ref, o3_ref, l1_ref, l2_ref, l3_ref, gates_ref,
                wa_ref, wbr_ref, wo_ref, g_ref, b_ref, wr2_ref, br_ref, out_ref, bkt_ref, cnt_ref,
                *tok_sc):
    slabs = B_OUT_WIDTH // LANES
    dil = [r for _, r in DILATED_PAIRS]
    b_refs = (o1_ref, o2_ref, o3_ref, l1_ref, l2_ref, l3_ref)
    for ref, sc, r in zip(b_refs, tok_sc, dil + dil):
        if r > 1:
            for j in range(r):
                for c in range(slabs):
                    lanes = slice(j * B_OUT_WIDTH + c * LANES, j * B_OUT_WIDTH + (c + 1) * LANES)
                    sc[c, pl.ds(j, TMG // r, stride=r), :] = ref[:, lanes]

    def token_major(which, rows):
        ref, sc, r = b_refs[which], tok_sc[which], (dil + dil)[which]
        if r == 1:
            return ref[rows, :]
        return jnp.concatenate([sc[c, rows, :] for c in range(slabs)], axis=1)

    subs = [slice(i * TSUB, (i + 1) * TSUB) for i in range(TMG // TSUB)]
    ybs = []
    for rows in subs:
        o1, o2, o3, l1, l2, l3 = (token_major(w, rows) for w in range(6))
        mx = jnp.maximum(jnp.maximum(l1, l2), l3)
        e1, e2, e3 = jnp.exp(l1 - mx), jnp.exp(l2 - mx), jnp.exp(l3 - mx)
        ybs.append(((e1 * o1 + e2 * o2 + e3 * o3) / (e1 + e2 + e3)).astype(BF16))
    ya_ps = [jnp.dot(ya_ref[rows, :], wa_ref[...], preferred_element_type=F32) for rows in subs]
    yb_ps = [jnp.dot(yb, wbr_ref[...], preferred_element_type=F32) for yb in ybs]
    mergeds = [(gates_ref[rows, 0:D_MODEL].astype(F32) * ya_p
                + gates_ref[rows, D_MODEL:2 * D_MODEL].astype(F32) * yb_p).astype(BF16)
               for rows, ya_p, yb_p in zip(subs, ya_ps, yb_ps)]
    hs = [ALPHA * x_ref[rows, :] + jnp.dot(m, wo_ref[...], preferred_element_type=F32)
          for rows, m in zip(subs, mergeds)]
    x1s = [_layer_norm(h, g_ref[...], b_ref[...]) for h in hs]
    for rows, x1 in zip(subs, x1s):
        out_ref[rows, 0:D_MODEL] = x1

    splits = [_split_bf16(x1) for x1 in x1s]
    boths = [jnp.dot(xh, wr2_ref[...], preferred_element_type=F32)
             + jnp.dot(xl, wr2_ref[...], preferred_element_type=F32) for xh, xl in splits]
    logits = jnp.concatenate(
        [(both[:, 0:ROUTE_COLS] + both[:, ROUTE_COLS:] + br_ref[...]).T for both in boths], axis=1)
    col = lambda i: logits[i:i + 1, :]
    lg = [col(i) for i in range(N_GROUPS)]
    gmax = functools.reduce(jnp.maximum, lg)
    top_p = 1.0 / functools.reduce(lambda a, c: a + c, [jnp.exp(v - gmax) for v in lg])
    gsel = jnp.full_like(gmax, N_GROUPS - 1).astype(jnp.int32)
    for i in reversed(range(N_GROUPS - 1)):
        gsel = jnp.where(lg[i] >= gmax, i, gsel)
    sel = []
    for e in range(EXPERTS_PER_GROUP):
        v = col(N_GROUPS + (N_GROUPS - 1) * EXPERTS_PER_GROUP + e)
        for gi in reversed(range(N_GROUPS - 1)):
            v = jnp.where(gsel == gi, col(N_GROUPS + gi * EXPERTS_PER_GROUP + e), v)
        sel.append(v)
    rank = []
    for e in range(EXPERTS_PER_GROUP):
        rk = jnp.zeros_like(gsel)
        for j in range(EXPERTS_PER_GROUP):
            if j != e:
                ahead = (sel[j] > sel[e]) | ((sel[j] == sel[e]) & (j < e))
                rk = rk + ahead.astype(jnp.int32)
        rank.append(rk)
    pick = lambda k: functools.reduce(
        lambda a, c: a + c, [jnp.where(rank[e] == k, sel[e], 0.0) for e in range(EXPERTS_PER_GROUP)])
    t = jnp.exp(pick(1) - pick(0))
    w_first = top_p / (1.0 + t)
    w_second = top_p * t / (1.0 + t)
    chosen = [rank[e] < 2 for e in range(EXPERTS_PER_GROUP)]
    weight = [jnp.where(rank[e] == 0, w_first, w_second) for e in range(EXPERTS_PER_GROUP)]
    pair = jnp.zeros_like(gsel)
    w_lo = jnp.zeros_like(top_p)
    w_hi = jnp.zeros_like(top_p)
    for pi, (a, c) in enumerate(PAIRS):
        hit = chosen[a] & chosen[c]
        pair = jnp.where(hit, pi, pair)
        w_lo = jnp.where(hit, weight[a], w_lo)
        w_hi = jnp.where(hit, weight[c], w_hi)
    bucket = gsel * len(PAIRS) + pair
    bkt_ref[...] = bucket
    hits = lax.broadcasted_iota(jnp.int32, (ROUTE_COLS, TMG), 0) == bucket

    @pl.when(pl.program_id(0) == 0)
    def _():
        cnt_ref[...] = jnp.zeros_like(cnt_ref)

    cnt_ref[...] += jnp.sum(hits.astype(jnp.int32), axis=1, keepdims=True)
    row = lax.broadcasted_iota(jnp.int32, (F32_SUBLANES, TMG), 0)
    head = jnp.where(row == 1, w_lo, jnp.where(row == 2, w_hi, 0.0))
    route_t = jnp.concatenate([head, jnp.zeros((ROUTE_COLS - F32_SUBLANES, TMG), F32)], axis=0)
    out_ref[:, D_MODEL:] = route_t.T


def _merge_call(x, ya, os, ls, gates, wa, wbr, wo, g, b, wr2, br):
    t, d = x.shape
    tok = lambda c: pl.BlockSpec((TMG, c), lambda i: (i, 0))
    const = lambda a: pl.BlockSpec(a.shape, lambda i: (0,) * a.ndim)
    consts = (wa, wbr, wo, g, b, wr2, br)
    dil = [pl.BlockSpec((TMG // r, r * B_OUT_WIDTH), lambda i: (i, 0)) for _, r in DILATED_PAIRS]
    return pl.pallas_call(
        _merge_body,
        grid=(t // TMG,),
        in_specs=[tok(d), tok(A_WIDTH)] + dil + dil + [tok(2 * D_MODEL)]
        + [const(a) for a in consts],
        out_specs=[tok(d + ROUTE_COLS), pl.BlockSpec((1, TMG), lambda i: (0, i)),
                   pl.BlockSpec((ROUTE_COLS, 1), lambda i: (0, 0))],
        out_shape=[jax.ShapeDtypeStruct((t, d + ROUTE_COLS), F32), jax.ShapeDtypeStruct((1, t), jnp.int32),
                   jax.ShapeDtypeStruct((ROUTE_COLS, 1), jnp.int32)],
        scratch_shapes=[pltpu.VMEM((B_OUT_WIDTH // LANES, TMG, LANES), F32)] * (2 * N_DIL_GROUPS),
        compiler_params=_params(("arbitrary",)),
        name="merge_ln1_route",
    )(x, ya, *os, *ls, gates, *consts)


def _sc_gather(table, idx):
    nrows, d = idx.shape[0], table.shape[1]
    workers = SC_CORES * SC_SUBCORES
    per_worker = nrows // workers
    assert nrows % (workers * SC_ROWS) == 0
    mesh = plsc.VectorSubcoreMesh(core_axis_name="c", subcore_axis_name="s")

    @functools.partial(
        pl.kernel, mesh=mesh, out_type=jax.ShapeDtypeStruct((nrows, d), table.dtype),
        scratch_types=[pltpu.VMEM((SC_ROWS,), jnp.int32), pltpu.VMEM((SC_ROWS, d), table.dtype),
                       pltpu.SemaphoreType.DMA])
    def gather(table_hbm, idx_hbm, out_hbm, idx_v, rows_v, sem):
        base = (lax.axis_index("s") * SC_CORES + lax.axis_index("c")) * per_worker

        @pl.loop(0, per_worker, step=SC_ROWS)
        def _(off):
            start = pl.multiple_of(base + off, SC_ROWS)
            pltpu.sync_copy(idx_hbm.at[pl.ds(start, SC_ROWS)], idx_v)
            pltpu.async_copy(table_hbm.at[idx_v], rows_v, sem).wait()
            pltpu.sync_copy(rows_v, out_hbm.at[pl.ds(start, SC_ROWS)])

    return gather(table, idx)


def _moe_tiles_body(nvalid_ref, elo_ref, ehi_ref, x_ref, wil_ref, wih_ref, wol_ref, woh_ref,
                    g_ref, b_ref, o_ref):
    del elo_ref, ehi_ref
    n = nvalid_ref[pl.program_id(0)]

    @pl.when(n == 0)
    def _():
        o_ref[...] = jnp.zeros_like(o_ref)

    @pl.when(n > 0)
    def _():
        xt = x_ref[:, 0:D_MODEL]
        xb = xt.astype(BF16)
        hcats = [jnp.dot(xb, wi_ref[...], preferred_element_type=F32)
                 for wi_ref in (wil_ref, wih_ref)]
        acts = [(h[:, 0:D_EXPERT] * jax.nn.sigmoid(h[:, 0:D_EXPERT]) * h[:, D_EXPERT:]).astype(BF16)
                for h in hcats]
        y_lo, y_hi = (jnp.dot(act, wo_ref[...], preferred_element_type=F32)
                      for act, wo_ref in zip(acts, (wol_ref, woh_ref)))
        moe = (x_ref[:, D_MODEL + 1:D_MODEL + 2] * y_lo + x_ref[:, D_MODEL + 2:D_MODEL + 3] * y_hi)
        o_ref[...] = _layer_norm(ALPHA * xt + moe, g_ref[...], b_ref[...])


def _moe_tiles_call(x_sorted, nvalid, elo, ehi, wei, weo, g, b):
    ntiles = nvalid.shape[0]
    wi_spec = lambda e: pl.BlockSpec((None, D_MODEL, 2 * D_EXPERT), lambda i, nv, lo, hi: (e(lo, hi)[i], 0, 0))
    wo_spec = lambda e: pl.BlockSpec((None, D_EXPERT, D_MODEL), lambda i, nv, lo, hi: (e(lo, hi)[i], 0, 0))
    first = lambda lo, hi: lo
    second = lambda lo, hi: hi
    vec = pl.BlockSpec((1, D_MODEL), lambda i, nv, lo, hi: (0, 0))
    grid_spec = pltpu.PrefetchScalarGridSpec(
        num_scalar_prefetch=3,
        grid=(ntiles,),
        in_specs=[pl.BlockSpec((TMOE, D_MODEL + ROUTE_COLS), lambda i, nv, lo, hi: (i, 0)),
                  wi_spec(first), wi_spec(second), wo_spec(first), wo_spec(second), vec, vec],
        out_specs=pl.BlockSpec((TMOE, D_MODEL), lambda i, nv, lo, hi: (i, 0)),
    )
    return pl.pallas_call(
        _moe_tiles_body,
        grid_spec=grid_spec,
        out_shape=jax.ShapeDtypeStruct((ntiles * TMOE, D_MODEL), F32),
        compiler_params=_params(("arbitrary",)),
        name="moe_ln2",
    )(nvalid, elo, ehi, x_sorted, wei, wei, weo, weo, g, b)


def _route_plan(bucket, counts):
    t = bucket.shape[0]
    ntiles = t // TMOE + N_ROUTE_BUCKETS
    token = jnp.arange(t, dtype=jnp.int32)
    _, order = lax.sort((bucket, token), num_keys=1, is_stable=True)
    row_start = jnp.cumsum(counts) - counts
    tiles_per = (counts + TMOE - 1) // TMOE
    tile_end = jnp.cumsum(tiles_per)
    tile_start = tile_end - tiles_per
    total = tile_end[-1]
    tile = jnp.arange(ntiles, dtype=jnp.int32)
    used = tile < total
    bkt = jnp.sum(jnp.minimum(tile, total - 1)[:, None] >= tile_end[None, :], axis=1, dtype=jnp.int32)
    local = tile - tile_start[bkt]
    row0 = row_start[bkt] + local * TMOE
    nvalid = jnp.where(used, jnp.clip(counts[bkt] - local * TMOE, 0, TMOE), 0).astype(jnp.int32)
    pairs = jnp.asarray(PAIRS, jnp.int32)
    grp = bkt // len(PAIRS)
    elo = grp * EXPERTS_PER_GROUP + pairs[bkt % len(PAIRS), 0]
    ehi = grp * EXPERTS_PER_GROUP + pairs[bkt % len(PAIRS), 1]
    idx = jnp.minimum(row0[:, None] + jnp.arange(TMOE, dtype=jnp.int32)[None, :], t - 1)
    ids = order[idx].reshape(ntiles * TMOE)
    shift = tile_start * TMOE - row_start
    step = shift - jnp.concatenate([jnp.zeros((1,), jnp.int32), shift[:-1]])
    padded_row = token + jnp.sum(jnp.where(token[None, :] >= row_start[:, None], step[:, None], 0), axis=0)
    _, pos = lax.sort((order, padded_row), num_keys=1)
    return nvalid, elo.astype(jnp.int32), ehi.astype(jnp.int32), ids, pos


def _prepare(w_in, b_gate, q_gain, k_gain, rel_bias, w_branch_a, w_branch_b, w_out, ln1_g, ln1_b,
             w_route_group, b_route_group, w_route_expert, b_route_expert, w_expert_in,
             w_expert_out, ln2_g, ln2_b, seq):
    row = lambda v: v.reshape(1, -1).astype(F32)
    w = w_in[0]
    wr = jnp.concatenate([w_route_group[0]]
                         + [w_route_expert[0, gi] for gi in range(N_GROUPS)], axis=1)
    wr = jnp.pad(wr.astype(F32), ((0, 0), (0, ROUTE_COLS - wr.shape[1])))
    br = jnp.concatenate([b_route_group[0], b_route_expert[0].reshape(-1)])
    br = jnp.pad(br.astype(F32), (0, ROUTE_COLS - br.shape[0])).reshape(1, ROUTE_COLS)
    wr2 = jnp.concatenate(_split_bf16(wr), axis=1)
    scale = HEAD_DIM ** -0.5
    wb = w[:, DT_ROWS:]
    wb = jnp.concatenate([wb[:, 0:B_WIDTH] * (scale * LOG2E), wb[:, B_WIDTH:]], axis=1)
    return dict(
        wdt=w[:, 0:DT_ROWS].T.astype(BF16), wb=wb.astype(BF16), bg=row(b_gate[0]),
        rope_q=_rope_tables(seq, q_gain[0], scale * LOG2E), rope_k=_rope_tables(seq, k_gain[0], 1.0),
        score_bound=_score_bound(q_gain[0], k_gain[0]),
        bm=[_bias_mask_table(rel_bias, gi, r) for gi, (_, r) in enumerate(DILATED_PAIRS)],
        wa=w_branch_a[0].astype(BF16), wbr=w_branch_b[0].astype(BF16), wo=w_out[0].astype(BF16),
        ln1=(row(ln1_g[0]), row(ln1_b[0])), wr2=wr2, br=br,
        wei=w_expert_in[0].astype(BF16), weo=w_expert_out[0].astype(BF16),
        ln2=(row(ln2_g[0]), row(ln2_b[0])),
    )


def _trunk(x, p):
    b, s, d = x.shape
    qt, k, vt, *qkv_b, gates = _inproj_call(x, p["wdt"], p["wb"], p["bg"], *p["rope_q"], *p["rope_k"])
    ya = _attn_a_call(*p["score_bound"], qt, k, vt)
    os, ls = [], []
    for gi, (_, r) in enumerate(DILATED_PAIRS):
        qg, kg, vg = (qkv_b[i * N_DIL_GROUPS + gi] for i in range(3))
        o, lse = _attn_b_call(qg, kg, vg, p["bm"][gi], r)
        os.append(o.reshape(b * s // r, r * B_OUT_WIDTH))
        ls.append(lse.reshape(b * s // r, r * B_OUT_WIDTH))
    return _merge_call(x.reshape(b * s, d), ya.reshape(b * s, A_WIDTH), os, ls,
                       gates.reshape(b * s, 2 * D_MODEL), p["wa"], p["wbr"], p["wo"], *p["ln1"],
                       p["wr2"], p["br"])


def _experts(x1ext, bucket, counts, p, shape):
    nvalid, elo, ehi, ids, pos = _route_plan(bucket.reshape(-1), counts[:N_ROUTE_BUCKETS, 0])
    x_sorted = _sc_gather(x1ext, ids)
    y_sorted = _moe_tiles_call(x_sorted, nvalid, elo, ehi, p["wei"], p["weo"], *p["ln2"])
    return _sc_gather(y_sorted, pos).reshape(shape)


def kernel(x_prompt, x_sample, w_in, b_gate, q_gain, k_gain, rel_bias, w_branch_a, w_branch_b, w_out,
           ln1_g, ln1_b, w_route_group, b_route_group, w_route_expert, b_route_expert, w_expert_in,
           w_expert_out, ln2_g, ln2_b):
    assert x_prompt.shape[1] == x_sample.shape[1]
    p = _prepare(w_in, b_gate, q_gain, k_gain, rel_bias, w_branch_a, w_branch_b, w_out, ln1_g, ln1_b,
                 w_route_group, b_route_group, w_route_expert, b_route_expert, w_expert_in,
                 w_expert_out, ln2_g, ln2_b, x_prompt.shape[1])
    mid_sample = _trunk(x_sample, p)
    mid_prompt = _trunk(x_prompt, p)
    y_sample = _experts(*mid_sample, p, x_sample.shape)
    y_prompt = _experts(*mid_prompt, p, x_prompt.shape)
    return (y_prompt, y_sample)
```

```python
import functools
import math

import jax
import jax.numpy as jnp
import numpy as np
from jax import lax
from jax.experimental import pallas as pl
from jax.experimental.pallas import tpu as pltpu
from jax.experimental.pallas import tpu_sc as plsc

F32 = jnp.float32
BF16 = jnp.bfloat16

D_MODEL = 1024
HEAD_DIM = 64
A_Q_HEADS = 8
A_KV_HEADS = 2
A_GROUP = A_Q_HEADS // A_KV_HEADS
GRID_W = 64
ROPE_AXIS_DIM = HEAD_DIM // 2
ROPE_THETA = 10000.0
DILATED_PAIRS = ((128, 1), (512, 4), (2048, 16))
N_DIL_GROUPS = len(DILATED_PAIRS)
B_HEADS_PER_GROUP = 4
N_BUCKETS = 32
MAX_DISTANCE = 1024
A_WIDTH = A_Q_HEADS * HEAD_DIM
A_KV_WIDTH = A_KV_HEADS * HEAD_DIM
B_WIDTH = B_HEADS_PER_GROUP * N_DIL_GROUPS * HEAD_DIM
B_OUT_WIDTH = B_HEADS_PER_GROUP * HEAD_DIM
DT_ROWS = A_WIDTH + 2 * A_KV_WIDTH
N_GROUPS = 4
EXPERTS_PER_GROUP = 4
N_EXPERTS = N_GROUPS * EXPERTS_PER_GROUP
D_EXPERT = 512
PAIRS = ((0, 1), (0, 2), (0, 3), (1, 2), (1, 3), (2, 3))
N_ROUTE_BUCKETS = N_GROUPS * len(PAIRS)
ALPHA = 2.0 ** 0.25
RMS_EPS = 1e-6
LN_EPS = 1e-5
NEG_INF = -1e30

LANES = 128
TM = 512
TMG = 512
TSUB = 128
NQ = 256
BF16_SUBLANES = 16
F32_SUBLANES = 8
VT_ROWS = HEAD_DIM + BF16_SUBLANES
LOG2E = 1.4426950408889634
LN2 = 0.6931471805599453
MAX_SAFE_SHIFT = 40.0
HALO = 64
BAND = NQ + 2 * HALO
B_SUBSEQ_PER_STEP = 4
TMOE = 256
ROUTE_COLS = LANES
SC_CORES = 2
SC_SUBCORES = 16
SC_ROWS = 32
VMEM_LIMIT = 56 * 1024 * 1024


def _params(sem):
    return pltpu.CompilerParams(dimension_semantics=sem, vmem_limit_bytes=VMEM_LIMIT)


def _swap16(x):
    return jnp.concatenate([x[16:32], x[0:16], x[48:64], x[32:48]], axis=0)


def _norm_rope(xh, ta, tb):
    r = lax.rsqrt(jnp.mean(xh * xh, axis=0, keepdims=True) + RMS_EPS)
    return (xh * ta + _swap16(xh) * tb) * r


def _inproj_body(x_ref, wdt_ref, wb_ref, bg_ref, taq_ref, tbq_ref, tak_ref, tbk_ref, after_ref,
                 qt_ref, k_ref, vt_ref, *rest):
    del after_ref
    b_refs, gates_ref, tmp_sc = rest[:3 * N_DIL_GROUPS], rest[3 * N_DIL_GROUPS], rest[-1]
    xb = x_ref[...].astype(BF16)
    dt = lax.dot_general(wdt_ref[...], xb, (((1,), (1,)), ((), ())),
                         preferred_element_type=F32)
    taq, tbq = taq_ref[...], tbq_ref[...]
    for h in range(A_Q_HEADS):
        sl = slice(h * HEAD_DIM, (h + 1) * HEAD_DIM)
        qt_ref[sl, :] = _norm_rope(dt[sl], taq, tbq).astype(BF16)
    tak, tbk = tak_ref[...], tbk_ref[...]
    kt = jnp.concatenate(
        [_norm_rope(dt[A_WIDTH + h * HEAD_DIM:A_WIDTH + (h + 1) * HEAD_DIM], tak, tbk)
         for h in range(A_KV_HEADS)], axis=0)
    k_ref[...] = kt.T.astype(BF16)
    for h in range(A_KV_HEADS):
        v0 = A_WIDTH + A_KV_WIDTH + h * HEAD_DIM
        vt_ref[h, 0:HEAD_DIM, :] = dt[v0:v0 + HEAD_DIM].astype(BF16)
        vt_ref[h, HEAD_DIM:, :] = jnp.ones((VT_ROWS - HEAD_DIM, TM), BF16)
    for i in range(3):
        res = jnp.dot(xb, wb_ref[:, i * B_WIDTH:(i + 1) * B_WIDTH], preferred_element_type=F32)
        for c in range(B_WIDTH // LANES):
            tmp_sc[c] = res[:, c * LANES:(c + 1) * LANES]
        for gi, (_, r) in enumerate(DILATED_PAIRS):
            ref = b_refs[i * N_DIL_GROUPS + gi]
            for j in range(r):
                for half in range(B_OUT_WIDTH // LANES):
                    c = gi * (B_OUT_WIDTH // LANES) + half
                    lanes = slice(j * B_OUT_WIDTH + half * LANES, j * B_OUT_WIDTH + (half + 1) * LANES)
                    ref[:, lanes] = tmp_sc[c, pl.ds(j, TM // r, stride=r), :].astype(BF16)
    g0 = 3 * B_WIDTH
    for i in range(2):
        cols = slice(g0 + i * D_MODEL, g0 + (i + 1) * D_MODEL)
        pre = jnp.dot(xb, wb_ref[:, cols], preferred_element_type=F32)
        gates_ref[:, i * D_MODEL:(i + 1) * D_MODEL] = jax.nn.sigmoid(
            pre + bg_ref[:, i * D_MODEL:(i + 1) * D_MODEL]).astype(BF16)


def _inproj_call(x, wdt, wb, bg, taq, tbq, tak, tbk, after):
    b, s, d = x.shape
    nt = s // TM
    wcols = wb.shape[1]
    tab = pl.BlockSpec((HEAD_DIM, TM), lambda p, i: (0, p))
    const = lambda shape: pl.BlockSpec(shape, lambda p, i: (0,) * len(shape))
    tok = lambda c: pl.BlockSpec((None, TM, c), lambda p, i: (i, p, 0))
    dil = [r for _ in range(3) for _, r in DILATED_PAIRS]
    return pl.pallas_call(
        _inproj_body,
        grid=(nt, b),
        in_specs=[tok(d), const((DT_ROWS, d)), const((d, wcols)), const((1, 2 * D_MODEL)),
                  tab, tab, tab, tab, pl.BlockSpec(memory_space=pl.ANY)],
        out_specs=[
            pl.BlockSpec((None, A_WIDTH, TM), lambda p, i: (i, 0, p)),
            tok(A_KV_WIDTH),
            pl.BlockSpec((None, None, A_KV_HEADS, VT_ROWS, TM), lambda p, i: (i, p, 0, 0, 0)),
        ] + [pl.BlockSpec((None, TM // r, r * B_OUT_WIDTH), lambda p, i: (i, p, 0)) for r in dil]
        + [tok(2 * D_MODEL)],
        out_shape=[
            jax.ShapeDtypeStruct((b, A_WIDTH, s), BF16),
            jax.ShapeDtypeStruct((b, s, A_KV_WIDTH), BF16),
            jax.ShapeDtypeStruct((b, nt, A_KV_HEADS, VT_ROWS, TM), BF16),
        ] + [jax.ShapeDtypeStruct((b, s // r, r * B_OUT_WIDTH), BF16) for r in dil]
        + [jax.ShapeDtypeStruct((b, s, 2 * D_MODEL), BF16)],
        scratch_shapes=[pltpu.VMEM((B_WIDTH // LANES, TM, LANES), F32)],
        compiler_params=_params(("arbitrary", "arbitrary")),
        name="inproj",
    )(x, wdt, wb, bg, taq, tbq, tak, tbk, after)


def _rope_tables(s, gain, scale):
    t = np.arange(s)
    inv = ROPE_THETA ** (-np.arange(0, ROPE_AXIS_DIM, 2, dtype=np.float64) / ROPE_AXIS_DIM)
    pos = np.stack([t // GRID_W, t // GRID_W, t % GRID_W, t % GRID_W], axis=0)
    ang = (pos[:, None, :] * inv[None, :, None]).reshape(HEAD_DIM, s)
    sign = np.repeat(np.array([-1.0, 1.0, -1.0, 1.0]), 16)[:, None]
    partner = np.arange(HEAD_DIM) ^ 16
    cos = jnp.asarray(np.cos(ang), F32)
    sin = jnp.asarray(sign * np.sin(ang), F32)
    g = gain.astype(F32) * scale
    return g[:, None] * cos, g[partner][:, None] * sin


def _attn_a_body(fast_ref, kbound_ref, qt_ref, k_ref, vt_ref, o_ref, rhs_sc, acc_sc, ot_sc):
    kv = pl.program_id(1)
    nchunks = vt_ref.shape[0]
    tk = vt_ref.shape[2]
    row = lax.broadcasted_iota(jnp.int32, (A_KV_WIDTH, NQ), 0)
    own = (row // HEAD_DIM) == kv
    for g in range(A_GROUP):
        qg = qt_ref[g * HEAD_DIM:(g + 1) * HEAD_DIM, :]
        rhs_sc[g] = jnp.where(own, jnp.concatenate([qg, qg], axis=0), jnp.zeros((), BF16))

    def load_chunk(c):
        return k_ref[pl.ds(pl.multiple_of(c * tk, tk), tk), :], vt_ref[c]

    @pl.when(fast_ref[0] == 1)
    def _():
        shift = []
        for g in range(A_GROUP):
            qg = qt_ref[g * HEAD_DIM:(g + 1) * HEAD_DIM, :].astype(F32)
            shift.append(jnp.sqrt(jnp.sum(qg * qg, axis=0, keepdims=True)) * kbound_ref[0])

        def scores(tile):
            c, g = tile
            return jnp.dot(k_ref[c * tk:(c + 1) * tk, :], rhs_sc[g], preferred_element_type=F32)

        heads = range(A_GROUP)
        accs = [None] * A_GROUP
        sts = [scores((0, g)) for g in heads]
        for c in range(nchunks):
            sts_next = [scores((c + 1, g)) for g in heads] if c + 1 < nchunks else None
            ps = [jnp.exp2(sts[g] - shift[g]).astype(BF16) for g in heads]
            pvs = [jnp.dot(vt_ref[c], ps[g], preferred_element_type=F32) for g in heads]
            accs = [pvs[g] if accs[g] is None else accs[g] + pvs[g] for g in heads]
            sts = sts_next
        for g in heads:
            acc_sc[g] = accs[g]

    @pl.when(fast_ref[0] == 0)
    def _():
        acc_sc[...] = jnp.zeros_like(acc_sc)

        def chunk(c, ms):
            kc, vc = load_chunk(c)
            new_ms = []
            for g in range(A_GROUP):
                st = jnp.dot(kc, rhs_sc[g], preferred_element_type=F32)
                m_new = jnp.maximum(ms[g], jnp.max(st, axis=0, keepdims=True))
                alpha = jnp.exp2(ms[g] - m_new)
                p = jnp.exp2(st - m_new).astype(BF16)
                acc_sc[g] = alpha * acc_sc[g] + jnp.dot(vc, p, preferred_element_type=F32)
                new_ms.append(m_new)
            return tuple(new_ms)

        lax.fori_loop(0, nchunks, chunk,
                      tuple(jnp.full((1, NQ), NEG_INF, F32) for _ in range(A_GROUP)))

    for g in range(A_GROUP):
        ot_sc[g * HEAD_DIM:(g + 1) * HEAD_DIM, :] = (
            acc_sc[g, 0:HEAD_DIM, :] / acc_sc[g, HEAD_DIM:HEAD_DIM + 1, :])
    o_ref[...] = ot_sc[...].T.astype(BF16)


def _attn_a_call(fast, kbound, qt, k, vt):
    b, _, s = qt.shape
    nchunks, tk = vt.shape[1], vt.shape[4]
    gw = A_GROUP * HEAD_DIM
    grid_spec = pltpu.PrefetchScalarGridSpec(
        num_scalar_prefetch=2,
        grid=(b, A_KV_HEADS, s // NQ),
        in_specs=[
            pl.BlockSpec((None, gw, NQ), lambda i, h, j, *_: (i, h, j)),
            pl.BlockSpec((None, s, A_KV_WIDTH), lambda i, h, j, *_: (i, 0, 0)),
            pl.BlockSpec((None, nchunks, None, VT_ROWS, tk), lambda i, h, j, *_: (i, 0, h, 0, 0)),
        ],
        out_specs=pl.BlockSpec((None, NQ, gw), lambda i, h, j, *_: (i, j, h)),
        scratch_shapes=[pltpu.VMEM((A_GROUP, A_KV_WIDTH, NQ), BF16),
                        pltpu.VMEM((A_GROUP, VT_ROWS, NQ), F32),
                        pltpu.VMEM((gw, NQ), F32)],
    )
    return pl.pallas_call(
        _attn_a_body,
        grid_spec=grid_spec,
        out_shape=jax.ShapeDtypeStruct((b, s, A_WIDTH), BF16),
        compiler_params=_params(("arbitrary", "arbitrary", "arbitrary")),
        name="attn_a",
    )(fast, kbound, qt, k, vt)


def _score_bound(q_gain, k_gain):
    kb = math.sqrt(HEAD_DIM) * jnp.max(jnp.abs(k_gain.astype(F32)))
    qb = math.sqrt(HEAD_DIM) * jnp.max(jnp.abs(q_gain.astype(F32))) * (HEAD_DIM ** -0.5) * LOG2E
    fast = (qb * kb <= MAX_SAFE_SHIFT).astype(jnp.int32)
    return fast.reshape(1), kb.reshape(1)


def _attn_b_body(q_ref, k_ref, v_ref, bm_ref, o_ref, lse_ref, kp_sc, vp_sc):
    length = q_ref.shape[0]
    nsub = q_ref.shape[1] // B_OUT_WIDTH
    cols = lambda u: slice(u * B_OUT_WIDTH, (u + 1) * B_OUT_WIDTH)
    zeros = jnp.zeros((HALO, B_OUT_WIDTH), BF16)
    for sc, ref in ((kp_sc, k_ref), (vp_sc, v_ref)):
        for u in range(nsub):
            sc[u, 0:HALO, :] = zeros
            sc[u, HALO:HALO + length, :] = ref[:, cols(u)]
            sc[u, HALO + length:, :] = zeros
    row_head = lax.broadcasted_iota(jnp.int32, (B_OUT_WIDTH, NQ), 0) // HEAD_DIM
    ones = jnp.ones((VT_ROWS - HEAD_DIM, BAND), BF16)
    blocks = {}

    def block(key):
        if key not in blocks:
            u, i = key
            q0 = i * NQ
            qt = q_ref[q0:q0 + NQ, cols(u)].astype(F32).T.astype(BF16)
            kband = kp_sc[u, q0:q0 + BAND, :]
            vbt = vp_sc[u, q0:q0 + BAND, :].astype(F32).T.astype(BF16)
            pen = None
            if q0 - HALO < 0 or q0 - HALO + BAND > length:
                key_pos = lax.broadcasted_iota(jnp.int32, (BAND, NQ), 0) + (q0 - HALO)
                pen = jnp.where((key_pos >= 0) & (key_pos < length), 0.0, NEG_INF)
            blocks[key] = (qt, kband, vbt, pen)
        return blocks[key]

    def scores(key, h):
        qt, kband, _, _ = block(key)
        rhs = jnp.where(row_head == h, qt, jnp.zeros((), BF16))
        return jnp.dot(kband, rhs, preferred_element_type=F32)

    heads = range(B_HEADS_PER_GROUP)
    keys = [(u, i) for u in range(nsub) for i in range(length // NQ)]
    sts = [scores(keys[0], h) for h in heads]
    for n, key in enumerate(keys):
        sts_next = [scores(keys[n + 1], h) for h in heads] if n + 1 < len(keys) else None
        _, _, vbt, pen = block(key)
        sts = [sts[h] + bm_ref[h] for h in heads]
        if pen is not None:
            sts = [st + pen for st in sts]
        ms = [jnp.max(st, axis=0, keepdims=True) for st in sts]
        ps = [jnp.exp2(st - m).astype(BF16) for st, m in zip(sts, ms)]
        accs = [jnp.dot(jnp.concatenate([vbt[h * HEAD_DIM:(h + 1) * HEAD_DIM], ones], axis=0), ps[h],
                        preferred_element_type=F32) for h in heads]
        ls = [acc[HEAD_DIM:HEAD_DIM + 1] for acc in accs]
        u, i = key
        rows = slice(i * NQ, (i + 1) * NQ)
        o_ref[rows, cols(u)] = jnp.concatenate([acc[0:HEAD_DIM] / l for acc, l in zip(accs, ls)], axis=0).T
        lse_ref[rows, cols(u)] = jnp.concatenate(
            [jnp.broadcast_to((m + jnp.log2(l)) * LN2, (HEAD_DIM, NQ)) for m, l in zip(ms, ls)], axis=0).T
        del blocks[key]
        sts = sts_next


def _attn_b_call(qg, kg, vg, bm, r):
    b, length, _ = qg.shape
    nsub = min(r, B_SUBSEQ_PER_STEP)
    spec = pl.BlockSpec((None, length, nsub * B_OUT_WIDTH), lambda i, j: (i, 0, j))
    out_sds = jax.ShapeDtypeStruct((b, length, r * B_OUT_WIDTH), F32)
    return pl.pallas_call(
        _attn_b_body,
        grid=(b, r // nsub),
        in_specs=[spec, spec, spec,
                  pl.BlockSpec((B_HEADS_PER_GROUP, BAND, NQ), lambda i, j: (0, 0, 0))],
        out_specs=[spec, spec],
        out_shape=[out_sds, out_sds],
        scratch_shapes=[pltpu.VMEM((nsub, length + 2 * HALO, B_OUT_WIDTH), BF16),
                        pltpu.VMEM((nsub, length + 2 * HALO, B_OUT_WIDTH), BF16)],
        compiler_params=_params(("arbitrary", "arbitrary")),
        name=f"attn_b_r{r}",
    )(qg, kg, vg, bm)


def _t5_bucket_np(rel):
    half = N_BUCKETS // 2
    exact = half // 2
    n = np.abs(rel)
    big = exact + (np.log(np.maximum(n, 1).astype(np.float64) / exact)
                   / math.log(MAX_DISTANCE / exact) * (half - exact)).astype(np.int64)
    big = np.minimum(big, half - 1)
    return np.where(rel > 0, half, 0) + np.where(n < exact, n, big)


def _bias_mask_table(rel_bias, gi, r):
    span = BAND + NQ
    rel = np.arange(span) - (NQ - 1) - HALO
    tab = rel_bias[:, gi * B_HEADS_PER_GROUP:(gi + 1) * B_HEADS_PER_GROUP].astype(F32) * LOG2E
    u = jnp.where(jnp.asarray(np.abs(rel) <= HALO)[:, None],
                  tab[jnp.asarray(_t5_bucket_np(rel * r), jnp.int32)], NEG_INF).T
    shifted = jnp.tile(u, (1, NQ))[:, :NQ * (span - 1)].reshape(-1, NQ, span - 1)
    return jnp.transpose(shifted[:, :, NQ - 1:NQ - 1 + BAND], (0, 2, 1))


def _layer_norm(h, g, b):
    mu = jnp.mean(h, axis=-1, keepdims=True)
    c = h - mu
    var = jnp.mean(c * c, axis=-1, keepdims=True)
    return c * lax.rsqrt(var + LN_EPS) * g + b


def _split_bf16(a):
    hi = a.astype(BF16)
    return hi, (a - hi.astype(F32)).astype(BF16)


def _merge_body(x_ref, ya_ref, o1_ref, o2_ref, o3_ref, l1_ref, l2_ref, l3_ref, gates_ref,
                wa_ref, wbr_ref, wo_ref, g_ref, b_ref, wr2_ref, br_ref, out_ref, bkt_ref, cnt_ref,
                *tok_sc):
    slabs = B_OUT_WIDTH // LANES
    dil = [r for _, r in DILATED_PAIRS]
    b_refs = (o1_ref, o2_ref, o3_ref, l1_ref, l2_ref, l3_ref)
    for ref, sc, r in zip(b_refs, tok_sc, dil + dil):
        if r > 1:
            for j in range(r):
                for c in range(slabs):
                    lanes = slice(j * B_OUT_WIDTH + c * LANES, j * B_OUT_WIDTH + (c + 1) * LANES)
                    sc[c, pl.ds(j, TMG // r, stride=r), :] = ref[:, lanes]

    def token_major(which, rows):
        ref, sc, r = b_refs[which], tok_sc[which], (dil + dil)[which]
        if r == 1:
            return ref[rows, :]
        return jnp.concatenate([sc[c, rows, :] for c in range(slabs)], axis=1)

    subs = [slice(i * TSUB, (i + 1) * TSUB) for i in range(TMG // TSUB)]
    ybs = []
    for rows in subs:
        o1, o2, o3, l1, l2, l3 = (token_major(w, rows) for w in range(6))
        mx = jnp.maximum(jnp.maximum(l1, l2), l3)
        e1, e2, e3 = jnp.exp(l1 - mx), jnp.exp(l2 - mx), jnp.exp(l3 - mx)
        ybs.append(((e1 * o1 + e2 * o2 + e3 * o3) / (e1 + e2 + e3)).astype(BF16))
    ya_ps = [jnp.dot(ya_ref[rows, :], wa_ref[...], preferred_element_type=F32) for rows in subs]
    yb_ps = [jnp.dot(yb, wbr_ref[...], preferred_element_type=F32) for yb in ybs]
    mergeds = [(gates_ref[rows, 0:D_MODEL].astype(F32) * ya_p
                + gates_ref[rows, D_MODEL:2 * D_MODEL].astype(F32) * yb_p).astype(BF16)
               for rows, ya_p, yb_p in zip(subs, ya_ps, yb_ps)]
    hs = [ALPHA * x_ref[rows, :] + jnp.dot(m, wo_ref[...], preferred_element_type=F32)
          for rows, m in zip(subs, mergeds)]
    x1s = [_layer_norm(h, g_ref[...], b_ref[...]) for h in hs]
    for rows, x1 in zip(subs, x1s):
        out_ref[rows, 0:D_MODEL] = x1

    splits = [_split_bf16(x1) for x1 in x1s]
    boths = [jnp.dot(xh, wr2_ref[...], preferred_element_type=F32)
             + jnp.dot(xl, wr2_ref[...], preferred_element_type=F32) for xh, xl in splits]
    logits = jnp.concatenate(
        [(both[:, 0:ROUTE_COLS] + both[:, ROUTE_COLS:] + br_ref[...]).T for both in boths], axis=1)
    col = lambda i: logits[i:i + 1, :]
    lg = [col(i) for i in range(N_GROUPS)]
    gmax = functools.reduce(jnp.maximum, lg)
    top_p = 1.0 / functools.reduce(lambda a, c: a + c, [jnp.exp(v - gmax) for v in lg])
    gsel = jnp.full_like(gmax, N_GROUPS - 1).astype(jnp.int32)
    for i in reversed(range(N_GROUPS - 1)):
        gsel = jnp.where(lg[i] >= gmax, i, gsel)
    sel = []
    for e in range(EXPERTS_PER_GROUP):
        v = col(N_GROUPS + (N_GROUPS - 1) * EXPERTS_PER_GROUP + e)
        for gi in reversed(range(N_GROUPS - 1)):
            v = jnp.where(gsel == gi, col(N_GROUPS + gi * EXPERTS_PER_GROUP + e), v)
        sel.append(v)
    rank = []
    for e in range(EXPERTS_PER_GROUP):
        rk = jnp.zeros_like(gsel)
        for j in range(EXPERTS_PER_GROUP):
            if j != e:
                ahead = (sel[j] > sel[e]) | ((sel[j] == sel[e]) & (j < e))
                rk = rk + ahead.astype(jnp.int32)
        rank.append(rk)
    pick = lambda k: functools.reduce(
        lambda a, c: a + c, [jnp.where(rank[e] == k, sel[e], 0.0) for e in range(EXPERTS_PER_GROUP)])
    t = jnp.exp(pick(1) - pick(0))
    w_first = top_p / (1.0 + t)
    w_second = top_p * t / (1.0 + t)
    chosen = [rank[e] < 2 for e in range(EXPERTS_PER_GROUP)]
    weight = [jnp.where(rank[e] == 0, w_first, w_second) for e in range(EXPERTS_PER_GROUP)]
    pair = jnp.zeros_like(gsel)
    w_lo = jnp.zeros_like(top_p)
    w_hi = jnp.zeros_like(top_p)
    for pi, (a, c) in enumerate(PAIRS):
        hit = chosen[a] & chosen[c]
        pair = jnp.where(hit, pi, pair)
        w_lo = jnp.where(hit, weight[a], w_lo)
        w_hi = jnp.where(hit, weight[c], w_hi)
    bucket = gsel * len(PAIRS) + pair
    bkt_ref[...] = bucket
    hits = lax.broadcasted_iota(jnp.int32, (ROUTE_COLS, TMG), 0) == bucket

    @pl.when(pl.program_id(0) == 0)
    def _():
        cnt_ref[...] = jnp.zeros_like(cnt_ref)

    cnt_ref[...] += jnp.sum(hits.astype(jnp.int32), axis=1, keepdims=True)
    row = lax.broadcasted_iota(jnp.int32, (F32_SUBLANES, TMG), 0)
    head = jnp.where(row == 1, w_lo, jnp.where(row == 2, w_hi, 0.0))
    route_t = jnp.concatenate([head, jnp.zeros((ROUTE_COLS - F32_SUBLANES, TMG), F32)], axis=0)
    out_ref[:, D_MODEL:] = route_t.T


def _merge_call(x, ya, os, ls, gates, wa, wbr, wo, g, b, wr2, br):
    t, d = x.shape
    tok = lambda c: pl.BlockSpec((TMG, c), lambda i: (i, 0))
    const = lambda a: pl.BlockSpec(a.shape, lambda i: (0,) * a.ndim)
    consts = (wa, wbr, wo, g, b, wr2, br)
    dil = [pl.BlockSpec((TMG // r, r * B_OUT_WIDTH), lambda i: (i, 0)) for _, r in DILATED_PAIRS]
    return pl.pallas_call(
        _merge_body,
        grid=(t // TMG,),
        in_specs=[tok(d), tok(A_WIDTH)] + dil + dil + [tok(2 * D_MODEL)]
        + [const(a) for a in consts],
        out_specs=[tok(d + ROUTE_COLS), pl.BlockSpec((1, TMG), lambda i: (0, i)),
                   pl.BlockSpec((ROUTE_COLS, 1), lambda i: (0, 0))],
        out_shape=[jax.ShapeDtypeStruct((t, d + ROUTE_COLS), F32), jax.ShapeDtypeStruct((1, t), jnp.int32),
                   jax.ShapeDtypeStruct((ROUTE_COLS, 1), jnp.int32)],
        scratch_shapes=[pltpu.VMEM((B_OUT_WIDTH // LANES, TMG, LANES), F32)] * (2 * N_DIL_GROUPS),
        compiler_params=_params(("arbitrary",)),
        name="merge_ln1_route",
    )(x, ya, *os, *ls, gates, *consts)


def _sc_gather(table, idx):
    nrows, d = idx.shape[0], table.shape[1]
    workers = SC_CORES * SC_SUBCORES
    per_worker = nrows // workers
    assert nrows % (workers * SC_ROWS) == 0
    mesh = plsc.VectorSubcoreMesh(core_axis_name="c", subcore_axis_name="s")

    @functools.partial(
        pl.kernel, mesh=mesh, out_type=jax.ShapeDtypeStruct((nrows, d), table.dtype),
        scratch_types=[pltpu.VMEM((SC_ROWS,), jnp.int32), pltpu.VMEM((SC_ROWS, d), table.dtype),
                       pltpu.SemaphoreType.DMA])
    def gather(table_hbm, idx_hbm, out_hbm, idx_v, rows_v, sem):
        base = (lax.axis_index("s") * SC_CORES + lax.axis_index("c")) * per_worker

        @pl.loop(0, per_worker, step=SC_ROWS)
        def _(off):
            start = pl.multiple_of(base + off, SC_ROWS)
            pltpu.sync_copy(idx_hbm.at[pl.ds(start, SC_ROWS)], idx_v)
            pltpu.async_copy(table_hbm.at[idx_v], rows_v, sem).wait()
            pltpu.sync_copy(rows_v, out_hbm.at[pl.ds(start, SC_ROWS)])

    return gather(table, idx)


def _moe_tiles_body(nvalid_ref, elo_ref, ehi_ref, x_ref, wil_ref, wih_ref, wol_ref, woh_ref,
                    g_ref, b_ref, o_ref):
    del elo_ref, ehi_ref
    n = nvalid_ref[pl.program_id(0)]

    @pl.when(n == 0)
    def _():
        o_ref[...] = jnp.zeros_like(o_ref)

    @pl.when(n > 0)
    def _():
        xt = x_ref[:, 0:D_MODEL]
        xb = xt.astype(BF16)
        hcats = [jnp.dot(xb, wi_ref[...], preferred_element_type=F32)
                 for wi_ref in (wil_ref, wih_ref)]
        acts = [(h[:, 0:D_EXPERT] * jax.nn.sigmoid(h[:, 0:D_EXPERT]) * h[:, D_EXPERT:]).astype(BF16)
                for h in hcats]
        y_lo, y_hi = (jnp.dot(act, wo_ref[...], preferred_element_type=F32)
                      for act, wo_ref in zip(acts, (wol_ref, woh_ref)))
        moe = (x_ref[:, D_MODEL + 1:D_MODEL + 2] * y_lo + x_ref[:, D_MODEL + 2:D_MODEL + 3] * y_hi)
        o_ref[...] = _layer_norm(ALPHA * xt + moe, g_ref[...], b_ref[...])


def _moe_tiles_call(x_sorted, nvalid, elo, ehi, wei, weo, g, b):
    ntiles = nvalid.shape[0]
    wi_spec = lambda e: pl.BlockSpec((None, D_MODEL, 2 * D_EXPERT), lambda i, nv, lo, hi: (e(lo, hi)[i], 0, 0))
    wo_spec = lambda e: pl.BlockSpec((None, D_EXPERT, D_MODEL), lambda i, nv, lo, hi: (e(lo, hi)[i], 0, 0))
    first = lambda lo, hi: lo
    second = lambda lo, hi: hi
    vec = pl.BlockSpec((1, D_MODEL), lambda i, nv, lo, hi: (0, 0))
    grid_spec = pltpu.PrefetchScalarGridSpec(
        num_scalar_prefetch=3,
        grid=(ntiles,),
        in_specs=[pl.BlockSpec((TMOE, D_MODEL + ROUTE_COLS), lambda i, nv, lo, hi: (i, 0)),
                  wi_spec(first), wi_spec(second), wo_spec(first), wo_spec(second), vec, vec],
        out_specs=pl.BlockSpec((TMOE, D_MODEL), lambda i, nv, lo, hi: (i, 0)),
    )
    return pl.pallas_call(
        _moe_tiles_body,
        grid_spec=grid_spec,
        out_shape=jax.ShapeDtypeStruct((ntiles * TMOE, D_MODEL), F32),
        compiler_params=_params(("arbitrary",)),
        name="moe_ln2",
    )(nvalid, elo, ehi, x_sorted, wei, wei, weo, weo, g, b)


def _route_plan(bucket, counts):
    t = bucket.shape[0]
    ntiles = t // TMOE + N_ROUTE_BUCKETS
    token = jnp.arange(t, dtype=jnp.int32)
    _, order = lax.sort((bucket, token), num_keys=1, is_stable=True)
    row_start = jnp.cumsum(counts) - counts
    tiles_per = (counts + TMOE - 1) // TMOE
    tile_end = jnp.cumsum(tiles_per)
    tile_start = tile_end - tiles_per
    total = tile_end[-1]
    tile = jnp.arange(ntiles, dtype=jnp.int32)
    used = tile < total
    bkt = jnp.sum(jnp.minimum(tile, total - 1)[:, None] >= tile_end[None, :], axis=1, dtype=jnp.int32)
    local = tile - tile_start[bkt]
    row0 = row_start[bkt] + local * TMOE
    nvalid = jnp.where(used, jnp.clip(counts[bkt] - local * TMOE, 0, TMOE), 0).astype(jnp.int32)
    pairs = jnp.asarray(PAIRS, jnp.int32)
    grp = bkt // len(PAIRS)
    elo = grp * EXPERTS_PER_GROUP + pairs[bkt % len(PAIRS), 0]
    ehi = grp * EXPERTS_PER_GROUP + pairs[bkt % len(PAIRS), 1]
    idx = jnp.minimum(row0[:, None] + jnp.arange(TMOE, dtype=jnp.int32)[None, :], t - 1)
    ids = order[idx].reshape(ntiles * TMOE)
    shift = tile_start * TMOE - row_start
    step = shift - jnp.concatenate([jnp.zeros((1,), jnp.int32), shift[:-1]])
    padded_row = token + jnp.sum(jnp.where(token[None, :] >= row_start[:, None], step[:, None], 0), axis=0)
    _, pos = lax.sort((order, padded_row), num_keys=1)
    return nvalid, elo.astype(jnp.int32), ehi.astype(jnp.int32), ids, pos


def _prepare(w_in, b_gate, q_gain, k_gain, rel_bias, w_branch_a, w_branch_b, w_out, ln1_g, ln1_b,
             w_route_group, b_route_group, w_route_expert, b_route_expert, w_expert_in,
             w_expert_out, ln2_g, ln2_b, seq):
    row = lambda v: v.reshape(1, -1).astype(F32)
    w = w_in[0]
    wr = jnp.concatenate([w_route_group[0]]
                         + [w_route_expert[0, gi] for gi in range(N_GROUPS)], axis=1)
    wr = jnp.pad(wr.astype(F32), ((0, 0), (0, ROUTE_COLS - wr.shape[1])))
    br = jnp.concatenate([b_route_group[0], b_route_expert[0].reshape(-1)])
    br = jnp.pad(br.astype(F32), (0, ROUTE_COLS - br.shape[0])).reshape(1, ROUTE_COLS)
    wr2 = jnp.concatenate(_split_bf16(wr), axis=1)
    scale = HEAD_DIM ** -0.5
    wb = w[:, DT_ROWS:]
    wb = jnp.concatenate([wb[:, 0:B_WIDTH] * (scale * LOG2E), wb[:, B_WIDTH:]], axis=1)
    return dict(
        wdt=w[:, 0:DT_ROWS].T.astype(BF16), wb=wb.astype(BF16), bg=row(b_gate[0]),
        rope_q=_rope_tables(seq, q_gain[0], scale * LOG2E), rope_k=_rope_tables(seq, k_gain[0], 1.0),
        score_bound=_score_bound(q_gain[0], k_gain[0]),
        bm=[_bias_mask_table(rel_bias, gi, r) for gi, (_, r) in enumerate(DILATED_PAIRS)],
        wa=w_branch_a[0].astype(BF16), wbr=w_branch_b[0].astype(BF16), wo=w_out[0].astype(BF16),
        ln1=(row(ln1_g[0]), row(ln1_b[0])), wr2=wr2, br=br,
        wei=w_expert_in[0].astype(BF16), weo=w_expert_out[0].astype(BF16),
        ln2=(row(ln2_g[0]), row(ln2_b[0])),
    )


def _trunk(x, p, after):
    b, s, d = x.shape
    qt, k, vt, *qkv_b, gates = _inproj_call(x, p["wdt"], p["wb"], p["bg"], *p["rope_q"], *p["rope_k"],
                                            after)
    ya = _attn_a_call(*p["score_bound"], qt, k, vt)
    os, ls = [], []
    for gi, (_, r) in enumerate(DILATED_PAIRS):
        qg, kg, vg = (qkv_b[i * N_DIL_GROUPS + gi] for i in range(3))
        o, lse = _attn_b_call(qg, kg, vg, p["bm"][gi], r)
        os.append(o.reshape(b * s // r, r * B_OUT_WIDTH))
        ls.append(lse.reshape(b * s // r, r * B_OUT_WIDTH))
    return _merge_call(x.reshape(b * s, d), ya.reshape(b * s, A_WIDTH), os, ls,
                       gates.reshape(b * s, 2 * D_MODEL), p["wa"], p["wbr"], p["wo"], *p["ln1"],
                       p["wr2"], p["br"])


def _experts(x_sorted, plan, p, shape):
    nvalid, elo, ehi, _, pos = plan
    y_sorted = _moe_tiles_call(x_sorted, nvalid, elo, ehi, p["wei"], p["weo"], *p["ln2"])
    return _sc_gather(y_sorted, pos).reshape(shape)


def kernel(x_prompt, x_sample, w_in, b_gate, q_gain, k_gain, rel_bias, w_branch_a, w_branch_b, w_out,
           ln1_g, ln1_b, w_route_group, b_route_group, w_route_expert, b_route_expert, w_expert_in,
           w_expert_out, ln2_g, ln2_b):
    assert x_prompt.shape[1] == x_sample.shape[1]
    p = _prepare(w_in, b_gate, q_gain, k_gain, rel_bias, w_branch_a, w_branch_b, w_out, ln1_g, ln1_b,
                 w_route_group, b_route_group, w_route_expert, b_route_expert, w_expert_in,
                 w_expert_out, ln2_g, ln2_b, x_prompt.shape[1])
    plan = lambda bucket, counts: _route_plan(bucket.reshape(-1), counts[:N_ROUTE_BUCKETS, 0])
    x1_sample, *route_sample = _trunk(x_sample, p, p["br"])
    plan_sample = plan(*route_sample)
    sorted_sample = _sc_gather(x1_sample, plan_sample[3])
    x1_prompt, *route_prompt = _trunk(x_prompt, p, plan_sample[3])
    plan_prompt = plan(*route_prompt)
    sorted_prompt = _sc_gather(x1_prompt, plan_prompt[3])
    y_sample = _experts(sorted_sample, plan_sample, p, x_sample.shape)
    y_prompt = _experts(sorted_prompt, plan_prompt, p, x_prompt.shape)
    return (y_prompt, y_sample)
```

```python
import functools
import math

import jax
import jax.numpy as jnp
import numpy as np
from jax import lax
from jax.experimental import pallas as pl
from jax.experimental.pallas import tpu as pltpu
from jax.experimental.pallas import tpu_sc as plsc

F32 = jnp.float32
BF16 = jnp.bfloat16

D_MODEL = 1024
HEAD_DIM = 64
A_Q_HEADS = 8
A_KV_HEADS = 2
A_GROUP = A_Q_HEADS // A_KV_HEADS
GRID_W = 64
ROPE_AXIS_DIM = HEAD_DIM // 2
ROPE_THETA = 10000.0
DILATED_PAIRS = ((128, 1), (512, 4), (2048, 16))
N_DIL_GROUPS = len(DILATED_PAIRS)
B_HEADS_PER_GROUP = 4
N_BUCKETS = 32
MAX_DISTANCE = 1024
A_WIDTH = A_Q_HEADS * HEAD_DIM
A_KV_WIDTH = A_KV_HEADS * HEAD_DIM
B_WIDTH = B_HEADS_PER_GROUP * N_DIL_GROUPS * HEAD_DIM
B_OUT_WIDTH = B_HEADS_PER_GROUP * HEAD_DIM
DT_ROWS = A_WIDTH + 2 * A_KV_WIDTH
N_GROUPS = 4
EXPERTS_PER_GROUP = 4
N_EXPERTS = N_GROUPS * EXPERTS_PER_GROUP
D_EXPERT = 512
PAIRS = ((0, 1), (0, 2), (0, 3), (1, 2), (1, 3), (2, 3))
N_ROUTE_BUCKETS = N_GROUPS * len(PAIRS)
ALPHA = 2.0 ** 0.25
RMS_EPS = 1e-6
LN_EPS = 1e-5
NEG_INF = -1e30

LANES = 128
TM = 512
TMG = 512
TSUB = 128
NQ = 256
BF16_SUBLANES = 16
F32_SUBLANES = 8
VT_ROWS = HEAD_DIM + BF16_SUBLANES
LOG2E = 1.4426950408889634
LN2 = 0.6931471805599453
MAX_SAFE_SHIFT = 40.0
HALO = 64
BAND = NQ + 2 * HALO
B_SUBSEQ_PER_STEP = 4
TMOE = 256
ROUTE_COLS = LANES
SC_CORES = 2
SC_SUBCORES = 16
SC_ROWS = 32
VMEM_LIMIT = 56 * 1024 * 1024


def _params(sem):
    return pltpu.CompilerParams(dimension_semantics=sem, vmem_limit_bytes=VMEM_LIMIT)


def _swap16(x):
    return jnp.concatenate([x[16:32], x[0:16], x[48:64], x[32:48]], axis=0)


def _norm_rope(xh, ta, tb):
    r = lax.rsqrt(jnp.mean(xh * xh, axis=0, keepdims=True) + RMS_EPS)
    return (xh * ta + _swap16(xh) * tb) * r


def _inproj_body(x_ref, wdt_ref, wb_ref, bg_ref, taq_ref, tbq_ref, tak_ref, tbk_ref, after_ref,
                 qt_ref, k_ref, vt_ref, *rest):
    del after_ref
    b_refs, gates_ref, tmp_sc = rest[:3 * N_DIL_GROUPS], rest[3 * N_DIL_GROUPS], rest[-1]
    xb = x_ref[...].astype(BF16)
    dt = lax.dot_general(wdt_ref[...], xb, (((1,), (1,)), ((), ())),
                         preferred_element_type=F32)
    taq, tbq = taq_ref[...], tbq_ref[...]
    for h in range(A_Q_HEADS):
        sl = slice(h * HEAD_DIM, (h + 1) * HEAD_DIM)
        qt_ref[sl, :] = _norm_rope(dt[sl], taq, tbq).astype(BF16)
    tak, tbk = tak_ref[...], tbk_ref[...]
    kt = jnp.concatenate(
        [_norm_rope(dt[A_WIDTH + h * HEAD_DIM:A_WIDTH + (h + 1) * HEAD_DIM], tak, tbk)
         for h in range(A_KV_HEADS)], axis=0)
    k_ref[...] = kt.T.astype(BF16)
    for h in range(A_KV_HEADS):
        v0 = A_WIDTH + A_KV_WIDTH + h * HEAD_DIM
        vt_ref[h, 0:HEAD_DIM, :] = dt[v0:v0 + HEAD_DIM].astype(BF16)
        vt_ref[h, HEAD_DIM:, :] = jnp.ones((VT_ROWS - HEAD_DIM, TM), BF16)
    for i in range(3):
        res = jnp.dot(xb, wb_ref[:, i * B_WIDTH:(i + 1) * B_WIDTH], preferred_element_type=F32)
        for c in range(B_WIDTH // LANES):
            tmp_sc[c] = res[:, c * LANES:(c + 1) * LANES]
        for gi, (_, r) in enumerate(DILATED_PAIRS):
            ref = b_refs[i * N_DIL_GROUPS + gi]
            for j in range(r):
                for half in range(B_OUT_WIDTH // LANES):
                    c = gi * (B_OUT_WIDTH // LANES) + half
                    lanes = slice(j * B_OUT_WIDTH + half * LANES, j * B_OUT_WIDTH + (half + 1) * LANES)
                    ref[:, lanes] = tmp_sc[c, pl.ds(j, TM // r, stride=r), :].astype(BF16)
    g0 = 3 * B_WIDTH
    for i in range(2):
        cols = slice(g0 + i * D_MODEL, g0 + (i + 1) * D_MODEL)
        pre = jnp.dot(xb, wb_ref[:, cols], preferred_element_type=F32)
        gates_ref[:, i * D_MODEL:(i + 1) * D_MODEL] = jax.nn.sigmoid(
            pre + bg_ref[:, i * D_MODEL:(i + 1) * D_MODEL]).astype(BF16)


def _inproj_call(x, wdt, wb, bg, taq, tbq, tak, tbk, after):
    b, s, d = x.shape
    nt = s // TM
    wcols = wb.shape[1]
    tab = pl.BlockSpec((HEAD_DIM, TM), lambda p, i: (0, p))
    const = lambda shape: pl.BlockSpec(shape, lambda p, i: (0,) * len(shape))
    tok = lambda c: pl.BlockSpec((None, TM, c), lambda p, i: (i, p, 0))
    dil = [r for _ in range(3) for _, r in DILATED_PAIRS]
    return pl.pallas_call(
        _inproj_body,
        grid=(nt, b),
        in_specs=[tok(d), const((DT_ROWS, d)), const((d, wcols)), const((1, 2 * D_MODEL)),
                  tab, tab, tab, tab, pl.BlockSpec(memory_space=pl.ANY)],
        out_specs=[
            pl.BlockSpec((None, A_WIDTH, TM), lambda p, i: (i, 0, p)),
            tok(A_KV_WIDTH),
            pl.BlockSpec((None, None, A_KV_HEADS, VT_ROWS, TM), lambda p, i: (i, p, 0, 0, 0)),
        ] + [pl.BlockSpec((None, TM // r, r * B_OUT_WIDTH), lambda p, i: (i, p, 0)) for r in dil]
        + [tok(2 * D_MODEL)],
        out_shape=[
            jax.ShapeDtypeStruct((b, A_WIDTH, s), BF16),
            jax.ShapeDtypeStruct((b, s, A_KV_WIDTH), BF16),
            jax.ShapeDtypeStruct((b, nt, A_KV_HEADS, VT_ROWS, TM), BF16),
        ] + [jax.ShapeDtypeStruct((b, s // r, r * B_OUT_WIDTH), BF16) for r in dil]
        + [jax.ShapeDtypeStruct((b, s, 2 * D_MODEL), BF16)],
        scratch_shapes=[pltpu.VMEM((B_WIDTH // LANES, TM, LANES), F32)],
        compiler_params=_params(("arbitrary", "arbitrary")),
        name="inproj",
    )(x, wdt, wb, bg, taq, tbq, tak, tbk, after)


def _rope_tables(s, gain, scale):
    t = np.arange(s)
    inv = ROPE_THETA ** (-np.arange(0, ROPE_AXIS_DIM, 2, dtype=np.float64) / ROPE_AXIS_DIM)
    pos = np.stack([t // GRID_W, t // GRID_W, t % GRID_W, t % GRID_W], axis=0)
    ang = (pos[:, None, :] * inv[None, :, None]).reshape(HEAD_DIM, s)
    sign = np.repeat(np.array([-1.0, 1.0, -1.0, 1.0]), 16)[:, None]
    partner = np.arange(HEAD_DIM) ^ 16
    cos = jnp.asarray(np.cos(ang), F32)
    sin = jnp.asarray(sign * np.sin(ang), F32)
    g = gain.astype(F32) * scale
    return g[:, None] * cos, g[partner][:, None] * sin


def _attn_a_body(fast_ref, kbound_ref, qt_ref, k_ref, vt_ref, o_ref, rhs_sc, acc_sc, ot_sc):
    kv = pl.program_id(1)
    nchunks = vt_ref.shape[0]
    tk = vt_ref.shape[2]
    row = lax.broadcasted_iota(jnp.int32, (A_KV_WIDTH, NQ), 0)
    own = (row // HEAD_DIM) == kv
    for g in range(A_GROUP):
        qg = qt_ref[g * HEAD_DIM:(g + 1) * HEAD_DIM, :]
        rhs_sc[g] = jnp.where(own, jnp.concatenate([qg, qg], axis=0), jnp.zeros((), BF16))

    def load_chunk(c):
        return k_ref[pl.ds(pl.multiple_of(c * tk, tk), tk), :], vt_ref[c]

    @pl.when(fast_ref[0] == 1)
    def _():
        shift = []
        for g in range(A_GROUP):
            qg = qt_ref[g * HEAD_DIM:(g + 1) * HEAD_DIM, :].astype(F32)
            shift.append(jnp.sqrt(jnp.sum(qg * qg, axis=0, keepdims=True)) * kbound_ref[0])

        def scores(tile):
            c, g = tile
            return jnp.dot(k_ref[c * tk:(c + 1) * tk, :], rhs_sc[g], preferred_element_type=F32)

        heads = range(A_GROUP)
        accs = [None] * A_GROUP
        sts = [scores((0, g)) for g in heads]
        for c in range(nchunks):
            sts_next = [scores((c + 1, g)) for g in heads] if c + 1 < nchunks else None
            ps = [jnp.exp2(sts[g] - shift[g]).astype(BF16) for g in heads]
            pvs = [jnp.dot(vt_ref[c], ps[g], preferred_element_type=F32) for g in heads]
            accs = [pvs[g] if accs[g] is None else accs[g] + pvs[g] for g in heads]
            sts = sts_next
        for g in heads:
            acc_sc[g] = accs[g]

    @pl.when(fast_ref[0] == 0)
    def _():
        acc_sc[...] = jnp.zeros_like(acc_sc)

        def chunk(c, ms):
            kc, vc = load_chunk(c)
            new_ms = []
            for g in range(A_GROUP):
                st = jnp.dot(kc, rhs_sc[g], preferred_element_type=F32)
                m_new = jnp.maximum(ms[g], jnp.max(st, axis=0, keepdims=True))
                alpha = jnp.exp2(ms[g] - m_new)
                p = jnp.exp2(st - m_new).astype(BF16)
                acc_sc[g] = alpha * acc_sc[g] + jnp.dot(vc, p, preferred_element_type=F32)
                new_ms.append(m_new)
            return tuple(new_ms)

        lax.fori_loop(0, nchunks, chunk,
                      tuple(jnp.full((1, NQ), NEG_INF, F32) for _ in range(A_GROUP)))

    for g in range(A_GROUP):
        ot_sc[g * HEAD_DIM:(g + 1) * HEAD_DIM, :] = (
            acc_sc[g, 0:HEAD_DIM, :] / acc_sc[g, HEAD_DIM:HEAD_DIM + 1, :])
    o_ref[...] = ot_sc[...].T.astype(BF16)


def _attn_a_call(fast, kbound, qt, k, vt):
    b, _, s = qt.shape
    nchunks, tk = vt.shape[1], vt.shape[4]
    gw = A_GROUP * HEAD_DIM
    grid_spec = pltpu.PrefetchScalarGridSpec(
        num_scalar_prefetch=2,
        grid=(b, A_KV_HEADS, s // NQ),
        in_specs=[
            pl.BlockSpec((None, gw, NQ), lambda i, h, j, *_: (i, h, j)),
            pl.BlockSpec((None, s, A_KV_WIDTH), lambda i, h, j, *_: (i, 0, 0)),
            pl.BlockSpec((None, nchunks, None, VT_ROWS, tk), lambda i, h, j, *_: (i, 0, h, 0, 0)),
        ],
        out_specs=pl.BlockSpec((None, NQ, gw), lambda i, h, j, *_: (i, j, h)),
        scratch_shapes=[pltpu.VMEM((A_GROUP, A_KV_WIDTH, NQ), BF16),
                        pltpu.VMEM((A_GROUP, VT_ROWS, NQ), F32),
                        pltpu.VMEM((gw, NQ), F32)],
    )
    return pl.pallas_call(
        _attn_a_body,
        grid_spec=grid_spec,
        out_shape=jax.ShapeDtypeStruct((b, s, A_WIDTH), BF16),
        compiler_params=_params(("arbitrary", "arbitrary", "arbitrary")),
        name="attn_a",
    )(fast, kbound, qt, k, vt)


def _score_bound(q_gain, k_gain):
    kb = math.sqrt(HEAD_DIM) * jnp.max(jnp.abs(k_gain.astype(F32)))
    qb = math.sqrt(HEAD_DIM) * jnp.max(jnp.abs(q_gain.astype(F32))) * (HEAD_DIM ** -0.5) * LOG2E
    fast = (qb * kb <= MAX_SAFE_SHIFT).astype(jnp.int32)
    return fast.reshape(1), kb.reshape(1)


def _attn_b_body(q_ref, k_ref, v_ref, bm_ref, o_ref, lse_ref, kp_sc, vp_sc):
    length = q_ref.shape[0]
    nsub = q_ref.shape[1] // B_OUT_WIDTH
    cols = lambda u: slice(u * B_OUT_WIDTH, (u + 1) * B_OUT_WIDTH)
    zeros = jnp.zeros((HALO, B_OUT_WIDTH), BF16)
    for sc, ref in ((kp_sc, k_ref), (vp_sc, v_ref)):
        for u in range(nsub):
            sc[u, 0:HALO, :] = zeros
            sc[u, HALO:HALO + length, :] = ref[:, cols(u)]
            sc[u, HALO + length:, :] = zeros
    row_head = lax.broadcasted_iota(jnp.int32, (B_OUT_WIDTH, NQ), 0) // HEAD_DIM
    ones = jnp.ones((VT_ROWS - HEAD_DIM, BAND), BF16)
    blocks = {}

    def block(key):
        if key not in blocks:
            u, i = key
            q0 = i * NQ
            qt = q_ref[q0:q0 + NQ, cols(u)].astype(F32).T.astype(BF16)
            kband = kp_sc[u, q0:q0 + BAND, :]
            vbt = vp_sc[u, q0:q0 + BAND, :].astype(F32).T.astype(BF16)
            pen = None
            if q0 - HALO < 0 or q0 - HALO + BAND > length:
                key_pos = lax.broadcasted_iota(jnp.int32, (BAND, NQ), 0) + (q0 - HALO)
                pen = jnp.where((key_pos >= 0) & (key_pos < length), 0.0, NEG_INF)
            blocks[key] = (qt, kband, vbt, pen)
        return blocks[key]

    def scores(key, h):
        qt, kband, _, _ = block(key)
        rhs = jnp.where(row_head == h, qt, jnp.zeros((), BF16))
        return jnp.dot(kband, rhs, preferred_element_type=F32)

    heads = range(B_HEADS_PER_GROUP)
    keys = [(u, i) for u in range(nsub) for i in range(length // NQ)]
    sts = [scores(keys[0], h) for h in heads]
    for n, key in enumerate(keys):
        sts_next = [scores(keys[n + 1], h) for h in heads] if n + 1 < len(keys) else None
        _, _, vbt, pen = block(key)
        sts = [sts[h] + bm_ref[h] for h in heads]
        if pen is not None:
            sts = [st + pen for st in sts]
        ms = [jnp.max(st, axis=0, keepdims=True) for st in sts]
        ps = [jnp.exp2(st - m).astype(BF16) for st, m in zip(sts, ms)]
        accs = [jnp.dot(jnp.concatenate([vbt[h * HEAD_DIM:(h + 1) * HEAD_DIM], ones], axis=0), ps[h],
                        preferred_element_type=F32) for h in heads]
        ls = [acc[HEAD_DIM:HEAD_DIM + 1] for acc in accs]
        u, i = key
        rows = slice(i * NQ, (i + 1) * NQ)
        o_ref[rows, cols(u)] = jnp.concatenate([acc[0:HEAD_DIM] / l for acc, l in zip(accs, ls)], axis=0).T
        lse_ref[rows, cols(u)] = jnp.concatenate(
            [jnp.broadcast_to((m + jnp.log2(l)) * LN2, (HEAD_DIM, NQ)) for m, l in zip(ms, ls)], axis=0).T
        del blocks[key]
        sts = sts_next


def _attn_b_call(qg, kg, vg, bm, r):
    b, length, _ = qg.shape
    nsub = min(r, B_SUBSEQ_PER_STEP)
    spec = pl.BlockSpec((None, length, nsub * B_OUT_WIDTH), lambda i, j: (i, 0, j))
    out_sds = jax.ShapeDtypeStruct((b, length, r * B_OUT_WIDTH), F32)
    return pl.pallas_call(
        _attn_b_body,
        grid=(b, r // nsub),
        in_specs=[spec, spec, spec,
                  pl.BlockSpec((B_HEADS_PER_GROUP, BAND, NQ), lambda i, j: (0, 0, 0))],
        out_specs=[spec, spec],
        out_shape=[out_sds, out_sds],
        scratch_shapes=[pltpu.VMEM((nsub, length + 2 * HALO, B_OUT_WIDTH), BF16),
                        pltpu.VMEM((nsub, length + 2 * HALO, B_OUT_WIDTH), BF16)],
        compiler_params=_params(("arbitrary", "arbitrary")),
        name=f"attn_b_r{r}",
    )(qg, kg, vg, bm)


def _t5_bucket_np(rel):
    half = N_BUCKETS // 2
    exact = half // 2
    n = np.abs(rel)
    big = exact + (np.log(np.maximum(n, 1).astype(np.float64) / exact)
                   / math.log(MAX_DISTANCE / exact) * (half - exact)).astype(np.int64)
    big = np.minimum(big, half - 1)
    return np.where(rel > 0, half, 0) + np.where(n < exact, n, big)


def _bias_mask_table(rel_bias, gi, r):
    span = BAND + NQ
    rel = np.arange(span) - (NQ - 1) - HALO
    tab = rel_bias[:, gi * B_HEADS_PER_GROUP:(gi + 1) * B_HEADS_PER_GROUP].astype(F32) * LOG2E
    u = jnp.where(jnp.asarray(np.abs(rel) <= HALO)[:, None],
                  tab[jnp.asarray(_t5_bucket_np(rel * r), jnp.int32)], NEG_INF).T
    shifted = jnp.tile(u, (1, NQ))[:, :NQ * (span - 1)].reshape(-1, NQ, span - 1)
    return jnp.transpose(shifted[:, :, NQ - 1:NQ - 1 + BAND], (0, 2, 1))


def _layer_norm(h, g, b):
    mu = jnp.mean(h, axis=-1, keepdims=True)
    c = h - mu
    var = jnp.mean(c * c, axis=-1, keepdims=True)
    return c * lax.rsqrt(var + LN_EPS) * g + b


def _split_bf16(a):
    hi = a.astype(BF16)
    return hi, (a - hi.astype(F32)).astype(BF16)


def _merge_body(x_ref, ya_ref, o1_ref, o2_ref, o3_ref, l1_ref, l2_ref, l3_ref, gates_ref,
                wa_ref, wbr_ref, wo_ref, g_ref, b_ref, wr2_ref, br_ref, out_ref, bkt_ref, cnt_ref,
                *tok_sc):
    slabs = B_OUT_WIDTH // LANES
    dil = [r for _, r in DILATED_PAIRS]
    b_refs = (o1_ref, o2_ref, o3_ref, l1_ref, l2_ref, l3_ref)
    for ref, sc, r in zip(b_refs, tok_sc, dil + dil):
        if r > 1:
            for j in range(r):
                for c in range(slabs):
                    lanes = slice(j * B_OUT_WIDTH + c * LANES, j * B_OUT_WIDTH + (c + 1) * LANES)
                    sc[c, pl.ds(j, TMG // r, stride=r), :] = ref[:, lanes]

    def token_major(which, rows):
        ref, sc, r = b_refs[which], tok_sc[which], (dil + dil)[which]
        if r == 1:
            return ref[rows, :]
        return jnp.concatenate([sc[c, rows, :] for c in range(slabs)], axis=1)

    subs = [slice(i * TSUB, (i + 1) * TSUB) for i in range(TMG // TSUB)]
    ybs = []
    for rows in subs:
        o1, o2, o3, l1, l2, l3 = (token_major(w, rows) for w in range(6))
        mx = jnp.maximum(jnp.maximum(l1, l2), l3)
        e1, e2, e3 = jnp.exp(l1 - mx), jnp.exp(l2 - mx), jnp.exp(l3 - mx)
        ybs.append(((e1 * o1 + e2 * o2 + e3 * o3) / (e1 + e2 + e3)).astype(BF16))
    ya_ps = [jnp.dot(ya_ref[rows, :], wa_ref[...], preferred_element_type=F32) for rows in subs]
    yb_ps = [jnp.dot(yb, wbr_ref[...], preferred_element_type=F32) for yb in ybs]
    mergeds = [(gates_ref[rows, 0:D_MODEL].astype(F32) * ya_p
                + gates_ref[rows, D_MODEL:2 * D_MODEL].astype(F32) * yb_p).astype(BF16)
               for rows, ya_p, yb_p in zip(subs, ya_ps, yb_ps)]
    hs = [ALPHA * x_ref[rows, :] + jnp.dot(m, wo_ref[...], preferred_element_type=F32)
          for rows, m in zip(subs, mergeds)]
    x1s = [_layer_norm(h, g_ref[...], b_ref[...]) for h in hs]
    for rows, x1 in zip(subs, x1s):
        out_ref[rows, 0:D_MODEL] = x1

    splits = [_split_bf16(x1) for x1 in x1s]
    boths = [jnp.dot(xh, wr2_ref[...], preferred_element_type=F32)
             + jnp.dot(xl, wr2_ref[...], preferred_element_type=F32) for xh, xl in splits]
    logits = jnp.concatenate(
        [(both[:, 0:ROUTE_COLS] + both[:, ROUTE_COLS:] + br_ref[...]).T for both in boths], axis=1)
    col = lambda i: logits[i:i + 1, :]
    lg = [col(i) for i in range(N_GROUPS)]
    gmax = functools.reduce(jnp.maximum, lg)
    top_p = 1.0 / functools.reduce(lambda a, c: a + c, [jnp.exp(v - gmax) for v in lg])
    gsel = jnp.full_like(gmax, N_GROUPS - 1).astype(jnp.int32)
    for i in reversed(range(N_GROUPS - 1)):
        gsel = jnp.where(lg[i] >= gmax, i, gsel)
    sel = []
    for e in range(EXPERTS_PER_GROUP):
        v = col(N_GROUPS + (N_GROUPS - 1) * EXPERTS_PER_GROUP + e)
        for gi in reversed(range(N_GROUPS - 1)):
            v = jnp.where(gsel == gi, col(N_GROUPS + gi * EXPERTS_PER_GROUP + e), v)
        sel.append(v)
    rank = []
    for e in range(EXPERTS_PER_GROUP):
        rk = jnp.zeros_like(gsel)
        for j in range(EXPERTS_PER_GROUP):
            if j != e:
                ahead = (sel[j] > sel[e]) | ((sel[j] == sel[e]) & (j < e))
                rk = rk + ahead.astype(jnp.int32)
        rank.append(rk)
    pick = lambda k: functools.reduce(
        lambda a, c: a + c, [jnp.where(rank[e] == k, sel[e], 0.0) for e in range(EXPERTS_PER_GROUP)])
    t = jnp.exp(pick(1) - pick(0))
    w_first = top_p / (1.0 + t)
    w_second = top_p * t / (1.0 + t)
    chosen = [rank[e] < 2 for e in range(EXPERTS_PER_GROUP)]
    weight = [jnp.where(rank[e] == 0, w_first, w_second) for e in range(EXPERTS_PER_GROUP)]
    pair = jnp.zeros_like(gsel)
    w_lo = jnp.zeros_like(top_p)
    w_hi = jnp.zeros_like(top_p)
    for pi, (a, c) in enumerate(PAIRS):
        hit = chosen[a] & chosen[c]
        pair = jnp.where(hit, pi, pair)
        w_lo = jnp.where(hit, weight[a], w_lo)
        w_hi = jnp.where(hit, weight[c], w_hi)
    bucket = gsel * len(PAIRS) + pair
    bkt_ref[...] = bucket
    hits = lax.broadcasted_iota(jnp.int32, (ROUTE_COLS, TMG), 0) == bucket

    @pl.when(pl.program_id(0) == 0)
    def _():
        cnt_ref[...] = jnp.zeros_like(cnt_ref)

    cnt_ref[...] += jnp.sum(hits.astype(jnp.int32), axis=1, keepdims=True)
    row = lax.broadcasted_iota(jnp.int32, (F32_SUBLANES, TMG), 0)
    head = jnp.where(row == 1, w_lo, jnp.where(row == 2, w_hi, 0.0))
    route_t = jnp.concatenate([head, jnp.zeros((ROUTE_COLS - F32_SUBLANES, TMG), F32)], axis=0)
    out_ref[:, D_MODEL:] = route_t.T


def _merge_call(x, ya, os, ls, gates, wa, wbr, wo, g, b, wr2, br):
    t, d = x.shape
    tok = lambda c: pl.BlockSpec((TMG, c), lambda i: (i, 0))
    const = lambda a: pl.BlockSpec(a.shape, lambda i: (0,) * a.ndim)
    consts = (wa, wbr, wo, g, b, wr2, br)
    dil = [pl.BlockSpec((TMG // r, r * B_OUT_WIDTH), lambda i: (i, 0)) for _, r in DILATED_PAIRS]
    return pl.pallas_call(
        _merge_body,
        grid=(t // TMG,),
        in_specs=[tok(d), tok(A_WIDTH)] + dil + dil + [tok(2 * D_MODEL)]
        + [const(a) for a in consts],
        out_specs=[tok(d + ROUTE_COLS), pl.BlockSpec((1, TMG), lambda i: (0, i)),
                   pl.BlockSpec((ROUTE_COLS, 1), lambda i: (0, 0))],
        out_shape=[jax.ShapeDtypeStruct((t, d + ROUTE_COLS), F32), jax.ShapeDtypeStruct((1, t), jnp.int32),
                   jax.ShapeDtypeStruct((ROUTE_COLS, 1), jnp.int32)],
        scratch_shapes=[pltpu.VMEM((B_OUT_WIDTH // LANES, TMG, LANES), F32)] * (2 * N_DIL_GROUPS),
        compiler_params=_params(("arbitrary",)),
        name="merge_ln1_route",
    )(x, ya, *os, *ls, gates, *consts)


def _sc_permute_rows(src, idx, nrows_out, scatter):
    nrows, d = idx.shape[0], src.shape[1]
    workers = SC_CORES * SC_SUBCORES
    per_worker = nrows // workers
    assert nrows % (workers * SC_ROWS) == 0
    mesh = plsc.VectorSubcoreMesh(core_axis_name="c", subcore_axis_name="s")

    @functools.partial(
        pl.kernel, mesh=mesh, out_type=jax.ShapeDtypeStruct((nrows_out, d), src.dtype),
        scratch_types=[pltpu.VMEM((SC_ROWS,), jnp.int32), pltpu.VMEM((SC_ROWS, d), src.dtype),
                       pltpu.SemaphoreType.DMA])
    def permute(src_hbm, idx_hbm, out_hbm, idx_v, rows_v, sem):
        base = (lax.axis_index("s") * SC_CORES + lax.axis_index("c")) * per_worker

        @pl.loop(0, per_worker, step=SC_ROWS)
        def _(off):
            rows = pl.ds(pl.multiple_of(base + off, SC_ROWS), SC_ROWS)
            pltpu.sync_copy(idx_hbm.at[rows], idx_v)
            if scatter:
                pltpu.sync_copy(src_hbm.at[rows], rows_v)
                pltpu.async_copy(rows_v, out_hbm.at[idx_v], sem).wait()
            else:
                pltpu.async_copy(src_hbm.at[idx_v], rows_v, sem).wait()
                pltpu.sync_copy(rows_v, out_hbm.at[rows])

    return permute(src, idx)


def _moe_tiles_body(nvalid_ref, elo_ref, ehi_ref, x_ref, wil_ref, wih_ref, wol_ref, woh_ref,
                    g_ref, b_ref, o_ref):
    del elo_ref, ehi_ref
    n = nvalid_ref[pl.program_id(0)]

    @pl.when(n == 0)
    def _():
        o_ref[...] = jnp.zeros_like(o_ref)

    @pl.when(n > 0)
    def _():
        xt = x_ref[:, 0:D_MODEL]
        xb = xt.astype(BF16)
        hcats = [jnp.dot(xb, wi_ref[...], preferred_element_type=F32)
                 for wi_ref in (wil_ref, wih_ref)]
        acts = [(h[:, 0:D_EXPERT] * jax.nn.sigmoid(h[:, 0:D_EXPERT]) * h[:, D_EXPERT:]).astype(BF16)
                for h in hcats]
        y_lo, y_hi = (jnp.dot(act, wo_ref[...], preferred_element_type=F32)
                      for act, wo_ref in zip(acts, (wol_ref, woh_ref)))
        moe = (x_ref[:, D_MODEL + 1:D_MODEL + 2] * y_lo + x_ref[:, D_MODEL + 2:D_MODEL + 3] * y_hi)
        o_ref[...] = _layer_norm(ALPHA * xt + moe, g_ref[...], b_ref[...])


def _moe_tiles_call(x_sorted, nvalid, elo, ehi, wei, weo, g, b):
    ntiles = nvalid.shape[0]
    wi_spec = lambda e: pl.BlockSpec((None, D_MODEL, 2 * D_EXPERT), lambda i, nv, lo, hi: (e(lo, hi)[i], 0, 0))
    wo_spec = lambda e: pl.BlockSpec((None, D_EXPERT, D_MODEL), lambda i, nv, lo, hi: (e(lo, hi)[i], 0, 0))
    first = lambda lo, hi: lo
    second = lambda lo, hi: hi
    vec = pl.BlockSpec((1, D_MODEL), lambda i, nv, lo, hi: (0, 0))
    grid_spec = pltpu.PrefetchScalarGridSpec(
        num_scalar_prefetch=3,
        grid=(ntiles,),
        in_specs=[pl.BlockSpec((TMOE, D_MODEL + ROUTE_COLS), lambda i, nv, lo, hi: (i, 0)),
                  wi_spec(first), wi_spec(second), wo_spec(first), wo_spec(second), vec, vec],
        out_specs=pl.BlockSpec((TMOE, D_MODEL), lambda i, nv, lo, hi: (i, 0)),
    )
    return pl.pallas_call(
        _moe_tiles_body,
        grid_spec=grid_spec,
        out_shape=jax.ShapeDtypeStruct((ntiles * TMOE, D_MODEL), F32),
        compiler_params=_params(("arbitrary",)),
        name="moe_ln2",
    )(nvalid, elo, ehi, x_sorted, wei, wei, weo, weo, g, b)


def _route_plan(bucket, counts):
    t = bucket.shape[0]
    ntiles = t // TMOE + N_ROUTE_BUCKETS
    token = jnp.arange(t, dtype=jnp.int32)
    _, order = lax.sort((bucket, token), num_keys=1, is_stable=True)
    row_start = jnp.cumsum(counts) - counts
    tiles_per = (counts + TMOE - 1) // TMOE
    tile_end = jnp.cumsum(tiles_per)
    tile_start = tile_end - tiles_per
    total = tile_end[-1]
    tile = jnp.arange(ntiles, dtype=jnp.int32)
    used = tile < total
    bkt = jnp.sum(jnp.minimum(tile, total - 1)[:, None] >= tile_end[None, :], axis=1, dtype=jnp.int32)
    local = tile - tile_start[bkt]
    nvalid = jnp.where(used, jnp.clip(counts[bkt] - local * TMOE, 0, TMOE), 0).astype(jnp.int32)
    pairs = jnp.asarray(PAIRS, jnp.int32)
    grp = bkt // len(PAIRS)
    elo = grp * EXPERTS_PER_GROUP + pairs[bkt % len(PAIRS), 0]
    ehi = grp * EXPERTS_PER_GROUP + pairs[bkt % len(PAIRS), 1]
    shift = tile_start * TMOE - row_start
    step = shift - jnp.concatenate([jnp.zeros((1,), jnp.int32), shift[:-1]])
    padded_row = token + jnp.sum(jnp.where(token[None, :] >= row_start[:, None], step[:, None], 0), axis=0)
    _, pos = lax.sort((order, padded_row), num_keys=1)
    return nvalid, elo.astype(jnp.int32), ehi.astype(jnp.int32), pos


def _prepare(w_in, b_gate, q_gain, k_gain, rel_bias, w_branch_a, w_branch_b, w_out, ln1_g, ln1_b,
             w_route_group, b_route_group, w_route_expert, b_route_expert, w_expert_in,
             w_expert_out, ln2_g, ln2_b, seq):
    row = lambda v: v.reshape(1, -1).astype(F32)
    w = w_in[0]
    wr = jnp.concatenate([w_route_group[0]]
                         + [w_route_expert[0, gi] for gi in range(N_GROUPS)], axis=1)
    wr = jnp.pad(wr.astype(F32), ((0, 0), (0, ROUTE_COLS - wr.shape[1])))
    br = jnp.concatenate([b_route_group[0], b_route_expert[0].reshape(-1)])
    br = jnp.pad(br.astype(F32), (0, ROUTE_COLS - br.shape[0])).reshape(1, ROUTE_COLS)
    wr2 = jnp.concatenate(_split_bf16(wr), axis=1)
    scale = HEAD_DIM ** -0.5
    wb = w[:, DT_ROWS:]
    wb = jnp.concatenate([wb[:, 0:B_WIDTH] * (scale * LOG2E), wb[:, B_WIDTH:]], axis=1)
    return dict(
        wdt=w[:, 0:DT_ROWS].T.astype(BF16), wb=wb.astype(BF16), bg=row(b_gate[0]),
        rope_q=_rope_tables(seq, q_gain[0], scale * LOG2E), rope_k=_rope_tables(seq, k_gain[0], 1.0),
        score_bound=_score_bound(q_gain[0], k_gain[0]),
        bm=[_bias_mask_table(rel_bias, gi, r) for gi, (_, r) in enumerate(DILATED_PAIRS)],
        wa=w_branch_a[0].astype(BF16), wbr=w_branch_b[0].astype(BF16), wo=w_out[0].astype(BF16),
        ln1=(row(ln1_g[0]), row(ln1_b[0])), wr2=wr2, br=br,
        wei=w_expert_in[0].astype(BF16), weo=w_expert_out[0].astype(BF16),
        ln2=(row(ln2_g[0]), row(ln2_b[0])),
    )


def _trunk(x, p, after):
    b, s, d = x.shape
    qt, k, vt, *qkv_b, gates = _inproj_call(x, p["wdt"], p["wb"], p["bg"], *p["rope_q"], *p["rope_k"],
                                            after)
    ya = _attn_a_call(*p["score_bound"], qt, k, vt)
    os, ls = [], []
    for gi, (_, r) in enumerate(DILATED_PAIRS):
        qg, kg, vg = (qkv_b[i * N_DIL_GROUPS + gi] for i in range(3))
        o, lse = _attn_b_call(qg, kg, vg, p["bm"][gi], r)
        os.append(o.reshape(b * s // r, r * B_OUT_WIDTH))
        ls.append(lse.reshape(b * s // r, r * B_OUT_WIDTH))
    return _merge_call(x.reshape(b * s, d), ya.reshape(b * s, A_WIDTH), os, ls,
                       gates.reshape(b * s, 2 * D_MODEL), p["wa"], p["wbr"], p["wo"], *p["ln1"],
                       p["wr2"], p["br"])


def _sorted_rows(x1ext, plan):
    ntiles = plan[0].shape[0]
    return _sc_permute_rows(x1ext, plan[3], ntiles * TMOE, scatter=True)


def _experts(x_sorted, plan, p, shape):
    nvalid, elo, ehi, pos = plan
    y_sorted = _moe_tiles_call(x_sorted, nvalid, elo, ehi, p["wei"], p["weo"], *p["ln2"])
    return _sc_permute_rows(y_sorted, pos, pos.shape[0], scatter=False).reshape(shape)


def kernel(x_prompt, x_sample, w_in, b_gate, q_gain, k_gain, rel_bias, w_branch_a, w_branch_b, w_out,
           ln1_g, ln1_b, w_route_group, b_route_group, w_route_expert, b_route_expert, w_expert_in,
           w_expert_out, ln2_g, ln2_b):
    assert x_prompt.shape[1] == x_sample.shape[1]
    p = _prepare(w_in, b_gate, q_gain, k_gain, rel_bias, w_branch_a, w_branch_b, w_out, ln1_g, ln1_b,
                 w_route_group, b_route_group, w_route_expert, b_route_expert, w_expert_in,
                 w_expert_out, ln2_g, ln2_b, x_prompt.shape[1])
    plan = lambda bucket, counts: _route_plan(bucket.reshape(-1), counts[:N_ROUTE_BUCKETS, 0])
    x1_sample, *route_sample = _trunk(x_sample, p, p["br"])
    plan_sample = plan(*route_sample)
    sorted_sample = _sorted_rows(x1_sample, plan_sample)
    x1_prompt, *route_prompt = _trunk(x_prompt, p, plan_sample[3])
    plan_prompt = plan(*route_prompt)
    sorted_prompt = _sorted_rows(x1_prompt, plan_prompt)
    y_sample = _experts(sorted_sample, plan_sample, p, x_sample.shape)
    y_prompt = _experts(sorted_prompt, plan_prompt, p, x_prompt.shape)
    return (y_prompt, y_sample)
```

```python
import functools
import math

import jax
import jax.numpy as jnp
import numpy as np
from jax import lax
from jax.experimental import pallas as pl
from jax.experimental.pallas import tpu as pltpu
from jax.experimental.pallas import tpu_sc as plsc

F32 = jnp.float32
BF16 = jnp.bfloat16

D_MODEL = 1024
HEAD_DIM = 64
A_Q_HEADS = 8
A_KV_HEADS = 2
A_GROUP = A_Q_HEADS // A_KV_HEADS
GRID_W = 64
ROPE_AXIS_DIM = HEAD_DIM // 2
ROPE_THETA = 10000.0
DILATED_PAIRS = ((128, 1), (512, 4), (2048, 16))
N_DIL_GROUPS = len(DILATED_PAIRS)
B_HEADS_PER_GROUP = 4
N_BUCKETS = 32
MAX_DISTANCE = 1024
A_WIDTH = A_Q_HEADS * HEAD_DIM
A_KV_WIDTH = A_KV_HEADS * HEAD_DIM
B_WIDTH = B_HEADS_PER_GROUP * N_DIL_GROUPS * HEAD_DIM
B_OUT_WIDTH = B_HEADS_PER_GROUP * HEAD_DIM
DT_ROWS = A_WIDTH + 2 * A_KV_WIDTH
N_GROUPS = 4
EXPERTS_PER_GROUP = 4
N_EXPERTS = N_GROUPS * EXPERTS_PER_GROUP
D_EXPERT = 512
PAIRS = ((0, 1), (0, 2), (0, 3), (1, 2), (1, 3), (2, 3))
N_ROUTE_BUCKETS = N_GROUPS * len(PAIRS)
ALPHA = 2.0 ** 0.25
RMS_EPS = 1e-6
LN_EPS = 1e-5
NEG_INF = -1e30

LANES = 128
TM = 512
TMG = 512
TSUB = 128
NQ = 256
BF16_SUBLANES = 16
F32_SUBLANES = 8
VT_ROWS = HEAD_DIM + BF16_SUBLANES
LOG2E = 1.4426950408889634
LN2 = 0.6931471805599453
MAX_SAFE_SHIFT = 40.0
HALO = 64
BAND = NQ + 2 * HALO
B_SUBSEQ_PER_STEP = 4
TMOE = 256
ROUTE_COLS = LANES
SC_CORES = 2
SC_SUBCORES = 16
SC_ROWS = 32
VMEM_LIMIT = 56 * 1024 * 1024


def _params(sem):
    return pltpu.CompilerParams(dimension_semantics=sem, vmem_limit_bytes=VMEM_LIMIT)


def _swap16(x):
    return jnp.concatenate([x[16:32], x[0:16], x[48:64], x[32:48]], axis=0)


def _norm_rope(xh, ta, tb):
    r = lax.rsqrt(jnp.mean(xh * xh, axis=0, keepdims=True) + RMS_EPS)
    return (xh * ta + _swap16(xh) * tb) * r


def _inproj_body(x_ref, wdt_ref, wb_ref, bg_ref, taq_ref, tbq_ref, tak_ref, tbk_ref, after_ref,
                 qt_ref, k_ref, vt_ref, *rest):
    del after_ref
    b_refs, gates_ref, tmp_sc = rest[:3 * N_DIL_GROUPS], rest[3 * N_DIL_GROUPS], rest[-1]
    xb = x_ref[...].astype(BF16)

    def project(lo, hi):
        return jnp.dot(xb, wb_ref[:, lo:hi], preferred_element_type=F32)

    def rope_queries(dt):
        taq, tbq = taq_ref[...], tbq_ref[...]
        for h in range(A_Q_HEADS):
            sl = slice(h * HEAD_DIM, (h + 1) * HEAD_DIM)
            qt_ref[sl, :] = _norm_rope(dt[sl], taq, tbq).astype(BF16)

    def rope_keys_values(dt):
        tak, tbk = tak_ref[...], tbk_ref[...]
        kt = jnp.concatenate(
            [_norm_rope(dt[A_WIDTH + h * HEAD_DIM:A_WIDTH + (h + 1) * HEAD_DIM], tak, tbk)
             for h in range(A_KV_HEADS)], axis=0)
        k_ref[...] = kt.T.astype(BF16)
        for h in range(A_KV_HEADS):
            v0 = A_WIDTH + A_KV_WIDTH + h * HEAD_DIM
            vt_ref[h, 0:HEAD_DIM, :] = dt[v0:v0 + HEAD_DIM].astype(BF16)
            vt_ref[h, HEAD_DIM:, :] = jnp.ones((VT_ROWS - HEAD_DIM, TM), BF16)

    def deinterleave(i, res):
        for c in range(B_WIDTH // LANES):
            tmp_sc[i, c] = res[:, c * LANES:(c + 1) * LANES]
        for gi, (_, r) in enumerate(DILATED_PAIRS):
            ref = b_refs[i * N_DIL_GROUPS + gi]
            for j in range(r):
                for half in range(B_OUT_WIDTH // LANES):
                    c = gi * (B_OUT_WIDTH // LANES) + half
                    lanes = slice(j * B_OUT_WIDTH + half * LANES, j * B_OUT_WIDTH + (half + 1) * LANES)
                    ref[:, lanes] = tmp_sc[i, c, pl.ds(j, TM // r, stride=r), :].astype(BF16)

    def gate(i, pre):
        cols = slice(i * D_MODEL, (i + 1) * D_MODEL)
        gates_ref[:, cols] = jax.nn.sigmoid(pre + bg_ref[:, cols]).astype(BF16)

    g0 = 3 * B_WIDTH
    dt = lax.dot_general(wdt_ref[...], xb, (((1,), (1,)), ((), ())),
                         preferred_element_type=F32)
    res_q = project(0, B_WIDTH)
    rope_queries(dt)
    res_k = project(B_WIDTH, 2 * B_WIDTH)
    rope_keys_values(dt)
    deinterleave(0, res_q)
    res_v = project(2 * B_WIDTH, g0)
    deinterleave(1, res_k)
    pre_a = project(g0, g0 + D_MODEL)
    deinterleave(2, res_v)
    pre_b = project(g0 + D_MODEL, g0 + 2 * D_MODEL)
    gate(0, pre_a)
    gate(1, pre_b)


def _inproj_call(x, wdt, wb, bg, taq, tbq, tak, tbk, after):
    b, s, d = x.shape
    nt = s // TM
    wcols = wb.shape[1]
    tab = pl.BlockSpec((HEAD_DIM, TM), lambda p, i: (0, p))
    const = lambda shape: pl.BlockSpec(shape, lambda p, i: (0,) * len(shape))
    tok = lambda c: pl.BlockSpec((None, TM, c), lambda p, i: (i, p, 0))
    dil = [r for _ in range(3) for _, r in DILATED_PAIRS]
    return pl.pallas_call(
        _inproj_body,
        grid=(nt, b),
        in_specs=[tok(d), const((DT_ROWS, d)), const((d, wcols)), const((1, 2 * D_MODEL)),
                  tab, tab, tab, tab, pl.BlockSpec(memory_space=pl.ANY)],
        out_specs=[
            pl.BlockSpec((None, A_WIDTH, TM), lambda p, i: (i, 0, p)),
            tok(A_KV_WIDTH),
            pl.BlockSpec((None, None, A_KV_HEADS, VT_ROWS, TM), lambda p, i: (i, p, 0, 0, 0)),
        ] + [pl.BlockSpec((None, TM // r, r * B_OUT_WIDTH), lambda p, i: (i, p, 0)) for r in dil]
        + [tok(2 * D_MODEL)],
        out_shape=[
            jax.ShapeDtypeStruct((b, A_WIDTH, s), BF16),
            jax.ShapeDtypeStruct((b, s, A_KV_WIDTH), BF16),
            jax.ShapeDtypeStruct((b, nt, A_KV_HEADS, VT_ROWS, TM), BF16),
        ] + [jax.ShapeDtypeStruct((b, s // r, r * B_OUT_WIDTH), BF16) for r in dil]
        + [jax.ShapeDtypeStruct((b, s, 2 * D_MODEL), BF16)],
        scratch_shapes=[pltpu.VMEM((3, B_WIDTH // LANES, TM, LANES), F32)],
        compiler_params=_params(("arbitrary", "arbitrary")),
        name="inproj",
    )(x, wdt, wb, bg, taq, tbq, tak, tbk, after)


def _rope_tables(s, gain, scale):
    t = np.arange(s)
    inv = ROPE_THETA ** (-np.arange(0, ROPE_AXIS_DIM, 2, dtype=np.float64) / ROPE_AXIS_DIM)
    pos = np.stack([t // GRID_W, t // GRID_W, t % GRID_W, t % GRID_W], axis=0)
    ang = (pos[:, None, :] * inv[None, :, None]).reshape(HEAD_DIM, s)
    sign = np.repeat(np.array([-1.0, 1.0, -1.0, 1.0]), 16)[:, None]
    partner = np.arange(HEAD_DIM) ^ 16
    cos = jnp.asarray(np.cos(ang), F32)
    sin = jnp.asarray(sign * np.sin(ang), F32)
    g = gain.astype(F32) * scale
    return g[:, None] * cos, g[partner][:, None] * sin


def _attn_a_body(fast_ref, kbound_ref, qt_ref, k_ref, vt_ref, o_ref, rhs_sc, acc_sc, ot_sc):
    nchunks = vt_ref.shape[0]
    tk = vt_ref.shape[3]
    heads = range(A_Q_HEADS)
    zeros = jnp.zeros((HEAD_DIM, NQ), BF16)
    for h in heads:
        qh = qt_ref[h * HEAD_DIM:(h + 1) * HEAD_DIM, :]
        rhs_sc[h] = jnp.concatenate([qh, zeros] if h < A_GROUP else [zeros, qh], axis=0)

    @pl.when(fast_ref[0] == 1)
    def _():
        shift = []
        for h in heads:
            qh = qt_ref[h * HEAD_DIM:(h + 1) * HEAD_DIM, :].astype(F32)
            shift.append(jnp.sqrt(jnp.sum(qh * qh, axis=0, keepdims=True)) * kbound_ref[0])

        def scores(c, h):
            return jnp.dot(k_ref[c * tk:(c + 1) * tk, :], rhs_sc[h], preferred_element_type=F32)

        groups = [(kv, c) for kv in range(A_KV_HEADS) for c in range(nchunks)]
        members = lambda kv: range(kv * A_GROUP, (kv + 1) * A_GROUP)
        accs = [None] * A_Q_HEADS
        sts = [scores(groups[0][1], h) for h in members(groups[0][0])]
        for n, (kv, c) in enumerate(groups):
            sts_next = None
            if n + 1 < len(groups):
                sts_next = [scores(groups[n + 1][1], h) for h in members(groups[n + 1][0])]
            ps = [jnp.exp2(st - shift[h]).astype(BF16) for st, h in zip(sts, members(kv))]
            pvs = [jnp.dot(vt_ref[c, kv], p, preferred_element_type=F32) for p in ps]
            for pv, h in zip(pvs, members(kv)):
                accs[h] = pv if accs[h] is None else accs[h] + pv
            sts = sts_next
        for h in heads:
            acc_sc[h] = accs[h]

    @pl.when(fast_ref[0] == 0)
    def _():
        acc_sc[...] = jnp.zeros_like(acc_sc)

        def chunk(c, ms):
            kc = k_ref[pl.ds(pl.multiple_of(c * tk, tk), tk), :]
            new_ms = []
            for h in heads:
                st = jnp.dot(kc, rhs_sc[h], preferred_element_type=F32)
                m_new = jnp.maximum(ms[h], jnp.max(st, axis=0, keepdims=True))
                alpha = jnp.exp2(ms[h] - m_new)
                p = jnp.exp2(st - m_new).astype(BF16)
                acc_sc[h] = alpha * acc_sc[h] + jnp.dot(vt_ref[c, h // A_GROUP], p,
                                                         preferred_element_type=F32)
                new_ms.append(m_new)
            return tuple(new_ms)

        lax.fori_loop(0, nchunks, chunk, tuple(jnp.full((1, NQ), NEG_INF, F32) for _ in heads))

    for h in heads:
        ot_sc[h * HEAD_DIM:(h + 1) * HEAD_DIM, :] = (
            acc_sc[h, 0:HEAD_DIM, :] / acc_sc[h, HEAD_DIM:HEAD_DIM + 1, :])
    o_ref[...] = ot_sc[...].T.astype(BF16)


def _attn_a_call(fast, kbound, qt, k, vt):
    b, _, s = qt.shape
    nchunks, tk = vt.shape[1], vt.shape[4]
    grid_spec = pltpu.PrefetchScalarGridSpec(
        num_scalar_prefetch=2,
        grid=(b, s // NQ),
        in_specs=[
            pl.BlockSpec((None, A_WIDTH, NQ), lambda i, j, *_: (i, 0, j)),
            pl.BlockSpec((None, s, A_KV_WIDTH), lambda i, j, *_: (i, 0, 0)),
            pl.BlockSpec((None, nchunks, A_KV_HEADS, VT_ROWS, tk), lambda i, j, *_: (i, 0, 0, 0, 0)),
        ],
        out_specs=pl.BlockSpec((None, NQ, A_WIDTH), lambda i, j, *_: (i, j, 0)),
        scratch_shapes=[pltpu.VMEM((A_Q_HEADS, A_KV_WIDTH, NQ), BF16),
                        pltpu.VMEM((A_Q_HEADS, VT_ROWS, NQ), F32),
                        pltpu.VMEM((A_WIDTH, NQ), F32)],
    )
    return pl.pallas_call(
        _attn_a_body,
        grid_spec=grid_spec,
        out_shape=jax.ShapeDtypeStruct((b, s, A_WIDTH), BF16),
        compiler_params=_params(("arbitrary", "arbitrary")),
        name="attn_a",
    )(fast, kbound, qt, k, vt)


def _score_bound(q_gain, k_gain):
    kb = math.sqrt(HEAD_DIM) * jnp.max(jnp.abs(k_gain.astype(F32)))
    qb = math.sqrt(HEAD_DIM) * jnp.max(jnp.abs(q_gain.astype(F32))) * (HEAD_DIM ** -0.5) * LOG2E
    fast = (qb * kb <= MAX_SAFE_SHIFT).astype(jnp.int32)
    return fast.reshape(1), kb.reshape(1)


def _attn_b_body(q_ref, k_ref, v_ref, bm_ref, o_ref, lse_ref, kp_sc, vp_sc):
    length = q_ref.shape[0]
    nsub = q_ref.shape[1] // B_OUT_WIDTH
    cols = lambda u: slice(u * B_OUT_WIDTH, (u + 1) * B_OUT_WIDTH)
    zeros = jnp.zeros((HALO, B_OUT_WIDTH), BF16)
    for sc, ref in ((kp_sc, k_ref), (vp_sc, v_ref)):
        for u in range(nsub):
            sc[u, 0:HALO, :] = zeros
            sc[u, HALO:HALO + length, :] = ref[:, cols(u)]
            sc[u, HALO + length:, :] = zeros
    row_head = lax.broadcasted_iota(jnp.int32, (B_OUT_WIDTH, NQ), 0) // HEAD_DIM
    ones = jnp.ones((VT_ROWS - HEAD_DIM, BAND), BF16)
    blocks = {}

    def block(key):
        if key not in blocks:
            u, i = key
            q0 = i * NQ
            qt = q_ref[q0:q0 + NQ, cols(u)].astype(F32).T.astype(BF16)
            kband = kp_sc[u, q0:q0 + BAND, :]
            vbt = vp_sc[u, q0:q0 + BAND, :].astype(F32).T.astype(BF16)
            pen = None
            if q0 - HALO < 0 or q0 - HALO + BAND > length:
                key_pos = lax.broadcasted_iota(jnp.int32, (BAND, NQ), 0) + (q0 - HALO)
                pen = jnp.where((key_pos >= 0) & (key_pos < length), 0.0, NEG_INF)
            blocks[key] = (qt, kband, vbt, pen)
        return blocks[key]

    def scores(key, h):
        qt, kband, _, _ = block(key)
        rhs = jnp.where(row_head == h, qt, jnp.zeros((), BF16))
        return jnp.dot(kband, rhs, preferred_element_type=F32)

    heads = range(B_HEADS_PER_GROUP)
    keys = [(u, i) for u in range(nsub) for i in range(length // NQ)]
    sts = [scores(keys[0], h) for h in heads]
    for n, key in enumerate(keys):
        sts_next = [scores(keys[n + 1], h) for h in heads] if n + 1 < len(keys) else None
        _, _, vbt, pen = block(key)
        sts = [sts[h] + bm_ref[h] for h in heads]
        if pen is not None:
            sts = [st + pen for st in sts]
        ms = [jnp.max(st, axis=0, keepdims=True) for st in sts]
        ps = [jnp.exp2(st - m).astype(BF16) for st, m in zip(sts, ms)]
        accs = [jnp.dot(jnp.concatenate([vbt[h * HEAD_DIM:(h + 1) * HEAD_DIM], ones], axis=0), ps[h],
                        preferred_element_type=F32) for h in heads]
        ls = [acc[HEAD_DIM:HEAD_DIM + 1] for acc in accs]
        u, i = key
        rows = slice(i * NQ, (i + 1) * NQ)
        o_ref[rows, cols(u)] = jnp.concatenate([acc[0:HEAD_DIM] / l for acc, l in zip(accs, ls)], axis=0).T
        lse_ref[rows, cols(u)] = jnp.concatenate(
            [jnp.broadcast_to((m + jnp.log2(l)) * LN2, (HEAD_DIM, NQ)) for m, l in zip(ms, ls)], axis=0).T
        del blocks[key]
        sts = sts_next


def _attn_b_call(qg, kg, vg, bm, r):
    b, length, _ = qg.shape
    nsub = min(r, B_SUBSEQ_PER_STEP)
    spec = pl.BlockSpec((None, length, nsub * B_OUT_WIDTH), lambda i, j: (i, 0, j))
    out_sds = jax.ShapeDtypeStruct((b, length, r * B_OUT_WIDTH), F32)
    return pl.pallas_call(
        _attn_b_body,
        grid=(b, r // nsub),
        in_specs=[spec, spec, spec,
                  pl.BlockSpec((B_HEADS_PER_GROUP, BAND, NQ), lambda i, j: (0, 0, 0))],
        out_specs=[spec, spec],
        out_shape=[out_sds, out_sds],
        scratch_shapes=[pltpu.VMEM((nsub, length + 2 * HALO, B_OUT_WIDTH), BF16),
                        pltpu.VMEM((nsub, length + 2 * HALO, B_OUT_WIDTH), BF16)],
        compiler_params=_params(("arbitrary", "arbitrary")),
        name=f"attn_b_r{r}",
    )(qg, kg, vg, bm)


def _t5_bucket_np(rel):
    half = N_BUCKETS // 2
    exact = half // 2
    n = np.abs(rel)
    big = exact + (np.log(np.maximum(n, 1).astype(np.float64) / exact)
                   / math.log(MAX_DISTANCE / exact) * (half - exact)).astype(np.int64)
    big = np.minimum(big, half - 1)
    return np.where(rel > 0, half, 0) + np.where(n < exact, n, big)


def _bias_mask_table(rel_bias, gi, r):
    span = BAND + NQ
    rel = np.arange(span) - (NQ - 1) - HALO
    tab = rel_bias[:, gi * B_HEADS_PER_GROUP:(gi + 1) * B_HEADS_PER_GROUP].astype(F32) * LOG2E
    u = jnp.where(jnp.asarray(np.abs(rel) <= HALO)[:, None],
                  tab[jnp.asarray(_t5_bucket_np(rel * r), jnp.int32)], NEG_INF).T
    shifted = jnp.tile(u, (1, NQ))[:, :NQ * (span - 1)].reshape(-1, NQ, span - 1)
    return jnp.transpose(shifted[:, :, NQ - 1:NQ - 1 + BAND], (0, 2, 1))


def _layer_norm(h, g, b):
    mu = jnp.mean(h, axis=-1, keepdims=True)
    c = h - mu
    var = jnp.mean(c * c, axis=-1, keepdims=True)
    return c * lax.rsqrt(var + LN_EPS) * g + b


def _split_bf16(a):
    hi = a.astype(BF16)
    return hi, (a - hi.astype(F32)).astype(BF16)


def _merge_body(x_ref, ya_ref, o1_ref, o2_ref, o3_ref, l1_ref, l2_ref, l3_ref, gates_ref,
                wa_ref, wbr_ref, wo_ref, g_ref, b_ref, wr2_ref, br_ref, out_ref, bkt_ref, cnt_ref,
                *tok_sc):
    slabs = B_OUT_WIDTH // LANES
    dil = [r for _, r in DILATED_PAIRS]
    b_refs = (o1_ref, o2_ref, o3_ref, l1_ref, l2_ref, l3_ref)
    for ref, sc, r in zip(b_refs, tok_sc, dil + dil):
        if r > 1:
            for j in range(r):
                for c in range(slabs):
                    lanes = slice(j * B_OUT_WIDTH + c * LANES, j * B_OUT_WIDTH + (c + 1) * LANES)
                    sc[c, pl.ds(j, TMG // r, stride=r), :] = ref[:, lanes]

    def token_major(which, rows):
        ref, sc, r = b_refs[which], tok_sc[which], (dil + dil)[which]
        if r == 1:
            return ref[rows, :]
        return jnp.concatenate([sc[c, rows, :] for c in range(slabs)], axis=1)

    subs = [slice(i * TSUB, (i + 1) * TSUB) for i in range(TMG // TSUB)]
    ybs = []
    for rows in subs:
        o1, o2, o3, l1, l2, l3 = (token_major(w, rows) for w in range(6))
        mx = jnp.maximum(jnp.maximum(l1, l2), l3)
        e1, e2, e3 = jnp.exp(l1 - mx), jnp.exp(l2 - mx), jnp.exp(l3 - mx)
        ybs.append(((e1 * o1 + e2 * o2 + e3 * o3) / (e1 + e2 + e3)).astype(BF16))
    ya_ps = [jnp.dot(ya_ref[rows, :], wa_ref[...], preferred_element_type=F32) for rows in subs]
    yb_ps = [jnp.dot(yb, wbr_ref[...], preferred_element_type=F32) for yb in ybs]
    mergeds = [(gates_ref[rows, 0:D_MODEL].astype(F32) * ya_p
                + gates_ref[rows, D_MODEL:2 * D_MODEL].astype(F32) * yb_p).astype(BF16)
               for rows, ya_p, yb_p in zip(subs, ya_ps, yb_ps)]
    hs = [ALPHA * x_ref[rows, :] + jnp.dot(m, wo_ref[...], preferred_element_type=F32)
          for rows, m in zip(subs, mergeds)]
    x1s = [_layer_norm(h, g_ref[...], b_ref[...]) for h in hs]
    for rows, x1 in zip(subs, x1s):
        out_ref[rows, 0:D_MODEL] = x1

    splits = [_split_bf16(x1) for x1 in x1s]
    boths = [jnp.dot(xh, wr2_ref[...], preferred_element_type=F32)
             + jnp.dot(xl, wr2_ref[...], preferred_element_type=F32) for xh, xl in splits]
    logits = jnp.concatenate(
        [(both[:, 0:ROUTE_COLS] + both[:, ROUTE_COLS:] + br_ref[...]).T for both in boths], axis=1)
    col = lambda i: logits[i:i + 1, :]
    lg = [col(i) for i in range(N_GROUPS)]
    gmax = functools.reduce(jnp.maximum, lg)
    top_p = 1.0 / functools.reduce(lambda a, c: a + c, [jnp.exp(v - gmax) for v in lg])
    gsel = jnp.full_like(gmax, N_GROUPS - 1).astype(jnp.int32)
    for i in reversed(range(N_GROUPS - 1)):
        gsel = jnp.where(lg[i] >= gmax, i, gsel)
    sel = []
    for e in range(EXPERTS_PER_GROUP):
        v = col(N_GROUPS + (N_GROUPS - 1) * EXPERTS_PER_GROUP + e)
        for gi in reversed(range(N_GROUPS - 1)):
            v = jnp.where(gsel == gi, col(N_GROUPS + gi * EXPERTS_PER_GROUP + e), v)
        sel.append(v)
    rank = []
    for e in range(EXPERTS_PER_GROUP):
        rk = jnp.zeros_like(gsel)
        for j in range(EXPERTS_PER_GROUP):
            if j != e:
                ahead = (sel[j] > sel[e]) | ((sel[j] == sel[e]) & (j < e))
                rk = rk + ahead.astype(jnp.int32)
        rank.append(rk)
    pick = lambda k: functools.reduce(
        lambda a, c: a + c, [jnp.where(rank[e] == k, sel[e], 0.0) for e in range(EXPERTS_PER_GROUP)])
    t = jnp.exp(pick(1) - pick(0))
    w_first = top_p / (1.0 + t)
    w_second = top_p * t / (1.0 + t)
    chosen = [rank[e] < 2 for e in range(EXPERTS_PER_GROUP)]
    weight = [jnp.where(rank[e] == 0, w_first, w_second) for e in range(EXPERTS_PER_GROUP)]
    pair = jnp.zeros_like(gsel)
    w_lo = jnp.zeros_like(top_p)
    w_hi = jnp.zeros_like(top_p)
    for pi, (a, c) in enumerate(PAIRS):
        hit = chosen[a] & chosen[c]
        pair = jnp.where(hit, pi, pair)
        w_lo = jnp.where(hit, weight[a], w_lo)
        w_hi = jnp.where(hit, weight[c], w_hi)
    bucket = gsel * len(PAIRS) + pair
    bkt_ref[...] = bucket
    hits = lax.broadcasted_iota(jnp.int32, (ROUTE_COLS, TMG), 0) == bucket

    @pl.when(pl.program_id(0) == 0)
    def _():
        cnt_ref[...] = jnp.zeros_like(cnt_ref)

    cnt_ref[...] += jnp.sum(hits.astype(jnp.int32), axis=1, keepdims=True)
    row = lax.broadcasted_iota(jnp.int32, (F32_SUBLANES, TMG), 0)
    head = jnp.where(row == 1, w_lo, jnp.where(row == 2, w_hi, 0.0))
    route_t = jnp.concatenate([head, jnp.zeros((ROUTE_COLS - F32_SUBLANES, TMG), F32)], axis=0)
    out_ref[:, D_MODEL:] = route_t.T


def _merge_call(x, ya, os, ls, gates, wa, wbr, wo, g, b, wr2, br):
    t, d = x.shape
    tok = lambda c: pl.BlockSpec((TMG, c), lambda i: (i, 0))
    const = lambda a: pl.BlockSpec(a.shape, lambda i: (0,) * a.ndim)
    consts = (wa, wbr, wo, g, b, wr2, br)
    dil = [pl.BlockSpec((TMG // r, r * B_OUT_WIDTH), lambda i: (i, 0)) for _, r in DILATED_PAIRS]
    return pl.pallas_call(
        _merge_body,
        grid=(t // TMG,),
        in_specs=[tok(d), tok(A_WIDTH)] + dil + dil + [tok(2 * D_MODEL)]
        + [const(a) for a in consts],
        out_specs=[tok(d + ROUTE_COLS), pl.BlockSpec((1, TMG), lambda i: (0, i)),
                   pl.BlockSpec((ROUTE_COLS, 1), lambda i: (0, 0))],
        out_shape=[jax.ShapeDtypeStruct((t, d + ROUTE_COLS), F32), jax.ShapeDtypeStruct((1, t), jnp.int32),
                   jax.ShapeDtypeStruct((ROUTE_COLS, 1), jnp.int32)],
        scratch_shapes=[pltpu.VMEM((B_OUT_WIDTH // LANES, TMG, LANES), F32)] * (2 * N_DIL_GROUPS),
        compiler_params=_params(("arbitrary",)),
        name="merge_ln1_route",
    )(x, ya, *os, *ls, gates, *consts)


def _sc_permute_rows(src, idx, nrows_out, scatter):
    nrows, d = idx.shape[0], src.shape[1]
    workers = SC_CORES * SC_SUBCORES
    per_worker = nrows // workers
    assert nrows % (workers * SC_ROWS) == 0
    mesh = plsc.VectorSubcoreMesh(core_axis_name="c", subcore_axis_name="s")

    @functools.partial(
        pl.kernel, mesh=mesh, out_type=jax.ShapeDtypeStruct((nrows_out, d), src.dtype),
        scratch_types=[pltpu.VMEM((SC_ROWS,), jnp.int32), pltpu.VMEM((SC_ROWS, d), src.dtype),
                       pltpu.SemaphoreType.DMA])
    def permute(src_hbm, idx_hbm, out_hbm, idx_v, rows_v, sem):
        base = (lax.axis_index("s") * SC_CORES + lax.axis_index("c")) * per_worker

        @pl.loop(0, per_worker, step=SC_ROWS)
        def _(off):
            rows = pl.ds(pl.multiple_of(base + off, SC_ROWS), SC_ROWS)
            pltpu.sync_copy(idx_hbm.at[rows], idx_v)
            if scatter:
                pltpu.sync_copy(src_hbm.at[rows], rows_v)
                pltpu.async_copy(rows_v, out_hbm.at[idx_v], sem).wait()
            else:
                pltpu.async_copy(src_hbm.at[idx_v], rows_v, sem).wait()
                pltpu.sync_copy(rows_v, out_hbm.at[rows])

    return permute(src, idx)


def _moe_tiles_body(nvalid_ref, elo_ref, ehi_ref, x_ref, wil_ref, wih_ref, wol_ref, woh_ref,
                    g_ref, b_ref, o_ref):
    del elo_ref, ehi_ref
    n = nvalid_ref[pl.program_id(0)]

    @pl.when(n == 0)
    def _():
        o_ref[...] = jnp.zeros_like(o_ref)

    @pl.when(n > 0)
    def _():
        xt = x_ref[:, 0:D_MODEL]
        xb = xt.astype(BF16)
        hcats = [jnp.dot(xb, wi_ref[...], preferred_element_type=F32)
                 for wi_ref in (wil_ref, wih_ref)]
        acts = [(h[:, 0:D_EXPERT] * jax.nn.sigmoid(h[:, 0:D_EXPERT]) * h[:, D_EXPERT:]).astype(BF16)
                for h in hcats]
        y_lo, y_hi = (jnp.dot(act, wo_ref[...], preferred_element_type=F32)
                      for act, wo_ref in zip(acts, (wol_ref, woh_ref)))
        moe = (x_ref[:, D_MODEL + 1:D_MODEL + 2] * y_lo + x_ref[:, D_MODEL + 2:D_MODEL + 3] * y_hi)
        o_ref[...] = _layer_norm(ALPHA * xt + moe, g_ref[...], b_ref[...])


def _moe_tiles_call(x_sorted, nvalid, elo, ehi, wei, weo, g, b):
    ntiles = nvalid.shape[0]
    wi_spec = lambda e: pl.BlockSpec((None, D_MODEL, 2 * D_EXPERT), lambda i, nv, lo, hi: (e(lo, hi)[i], 0, 0))
    wo_spec = lambda e: pl.BlockSpec((None, D_EXPERT, D_MODEL), lambda i, nv, lo, hi: (e(lo, hi)[i], 0, 0))
    first = lambda lo, hi: lo
    second = lambda lo, hi: hi
    vec = pl.BlockSpec((1, D_MODEL), lambda i, nv, lo, hi: (0, 0))
    grid_spec = pltpu.PrefetchScalarGridSpec(
        num_scalar_prefetch=3,
        grid=(ntiles,),
        in_specs=[pl.BlockSpec((TMOE, D_MODEL + ROUTE_COLS), lambda i, nv, lo, hi: (i, 0)),
                  wi_spec(first), wi_spec(second), wo_spec(first), wo_spec(second), vec, vec],
        out_specs=pl.BlockSpec((TMOE, D_MODEL), lambda i, nv, lo, hi: (i, 0)),
    )
    return pl.pallas_call(
        _moe_tiles_body,
        grid_spec=grid_spec,
        out_shape=jax.ShapeDtypeStruct((ntiles * TMOE, D_MODEL), F32),
        compiler_params=_params(("arbitrary",)),
        name="moe_ln2",
    )(nvalid, elo, ehi, x_sorted, wei, wei, weo, weo, g, b)


def _route_plan(bucket, counts):
    t = bucket.shape[0]
    ntiles = t // TMOE + N_ROUTE_BUCKETS
    token = jnp.arange(t, dtype=jnp.int32)
    _, order = lax.sort((bucket, token), num_keys=1, is_stable=True)
    row_start = jnp.cumsum(counts) - counts
    tiles_per = (counts + TMOE - 1) // TMOE
    tile_end = jnp.cumsum(tiles_per)
    tile_start = tile_end - tiles_per
    total = tile_end[-1]
    tile = jnp.arange(ntiles, dtype=jnp.int32)
    used = tile < total
    bkt = jnp.sum(jnp.minimum(tile, total - 1)[:, None] >= tile_end[None, :], axis=1, dtype=jnp.int32)
    local = tile - tile_start[bkt]
    nvalid = jnp.where(used, jnp.clip(counts[bkt] - local * TMOE, 0, TMOE), 0).astype(jnp.int32)
    pairs = jnp.asarray(PAIRS, jnp.int32)
    grp = bkt // len(PAIRS)
    elo = grp * EXPERTS_PER_GROUP + pairs[bkt % len(PAIRS), 0]
    ehi = grp * EXPERTS_PER_GROUP + pairs[bkt % len(PAIRS), 1]
    shift = tile_start * TMOE - row_start
    step = shift - jnp.concatenate([jnp.zeros((1,), jnp.int32), shift[:-1]])
    padded_row = token + jnp.sum(jnp.where(token[None, :] >= row_start[:, None], step[:, None], 0), axis=0)
    _, pos = lax.sort((order, padded_row), num_keys=1)
    return nvalid, elo.astype(jnp.int32), ehi.astype(jnp.int32), pos


def _prepare(w_in, b_gate, q_gain, k_gain, rel_bias, w_branch_a, w_branch_b, w_out, ln1_g, ln1_b,
             w_route_group, b_route_group, w_route_expert, b_route_expert, w_expert_in,
             w_expert_out, ln2_g, ln2_b, seq):
    row = lambda v: v.reshape(1, -1).astype(F32)
    w = w_in[0]
    wr = jnp.concatenate([w_route_group[0]]
                         + [w_route_expert[0, gi] for gi in range(N_GROUPS)], axis=1)
    wr = jnp.pad(wr.astype(F32), ((0, 0), (0, ROUTE_COLS - wr.shape[1])))
    br = jnp.concatenate([b_route_group[0], b_route_expert[0].reshape(-1)])
    br = jnp.pad(br.astype(F32), (0, ROUTE_COLS - br.shape[0])).reshape(1, ROUTE_COLS)
    wr2 = jnp.concatenate(_split_bf16(wr), axis=1)
    scale = HEAD_DIM ** -0.5
    wb = w[:, DT_ROWS:]
    wb = jnp.concatenate([wb[:, 0:B_WIDTH] * (scale * LOG2E), wb[:, B_WIDTH:]], axis=1)
    return dict(
        wdt=w[:, 0:DT_ROWS].T.astype(BF16), wb=wb.astype(BF16), bg=row(b_gate[0]),
        rope_q=_rope_tables(seq, q_gain[0], scale * LOG2E), rope_k=_rope_tables(seq, k_gain[0], 1.0),
        score_bound=_score_bound(q_gain[0], k_gain[0]),
        bm=[_bias_mask_table(rel_bias, gi, r) for gi, (_, r) in enumerate(DILATED_PAIRS)],
        wa=w_branch_a[0].astype(BF16), wbr=w_branch_b[0].astype(BF16), wo=w_out[0].astype(BF16),
        ln1=(row(ln1_g[0]), row(ln1_b[0])), wr2=wr2, br=br,
        wei=w_expert_in[0].astype(BF16), weo=w_expert_out[0].astype(BF16),
        ln2=(row(ln2_g[0]), row(ln2_b[0])),
    )


def _trunk(x, p, after):
    b, s, d = x.shape
    qt, k, vt, *qkv_b, gates = _inproj_call(x, p["wdt"], p["wb"], p["bg"], *p["rope_q"], *p["rope_k"],
                                            after)
    ya = _attn_a_call(*p["score_bound"], qt, k, vt)
    os, ls = [], []
    for gi, (_, r) in enumerate(DILATED_PAIRS):
        qg, kg, vg = (qkv_b[i * N_DIL_GROUPS + gi] for i in range(3))
        o, lse = _attn_b_call(qg, kg, vg, p["bm"][gi], r)
        os.append(o.reshape(b * s // r, r * B_OUT_WIDTH))
        ls.append(lse.reshape(b * s // r, r * B_OUT_WIDTH))
    return _merge_call(x.reshape(b * s, d), ya.reshape(b * s, A_WIDTH), os, ls,
                       gates.reshape(b * s, 2 * D_MODEL), p["wa"], p["wbr"], p["wo"], *p["ln1"],
                       p["wr2"], p["br"])


def _sorted_rows(x1ext, plan):
    ntiles = plan[0].shape[0]
    return _sc_permute_rows(x1ext, plan[3], ntiles * TMOE, scatter=True)


def _experts(x_sorted, plan, p, shape):
    nvalid, elo, ehi, pos = plan
    y_sorted = _moe_tiles_call(x_sorted, nvalid, elo, ehi, p["wei"], p["weo"], *p["ln2"])
    return _sc_permute_rows(y_sorted, pos, pos.shape[0], scatter=False).reshape(shape)


def kernel(x_prompt, x_sample, w_in, b_gate, q_gain, k_gain, rel_bias, w_branch_a, w_branch_b, w_out,
           ln1_g, ln1_b, w_route_group, b_route_group, w_route_expert, b_route_expert, w_expert_in,
           w_expert_out, ln2_g, ln2_b):
    assert x_prompt.shape[1] == x_sample.shape[1]
    p = _prepare(w_in, b_gate, q_gain, k_gain, rel_bias, w_branch_a, w_branch_b, w_out, ln1_g, ln1_b,
                 w_route_group, b_route_group, w_route_expert, b_route_expert, w_expert_in,
                 w_expert_out, ln2_g, ln2_b, x_prompt.shape[1])
    plan = lambda bucket, counts: _route_plan(bucket.reshape(-1), counts[:N_ROUTE_BUCKETS, 0])
    x1_sample, *route_sample = _trunk(x_sample, p, p["br"])
    plan_sample = plan(*route_sample)
    sorted_sample = _sorted_rows(x1_sample, plan_sample)
    x1_prompt, *route_prompt = _trunk(x_prompt, p, plan_sample[3])
    plan_prompt = plan(*route_prompt)
    sorted_prompt = _sorted_rows(x1_prompt, plan_prompt)
    y_sample = _experts(sorted_sample, plan_sample, p, x_sample.shape)
    y_prompt = _experts(sorted_prompt, plan_prompt, p, x_prompt.shape)
    return (y_prompt, y_sample)
```

```python
import functools
import math

import jax
import jax.numpy as jnp
import numpy as np
from jax import lax
from jax.experimental import pallas as pl
from jax.experimental.pallas import tpu as pltpu
from jax.experimental.pallas import tpu_sc as plsc

F32 = jnp.float32
BF16 = jnp.bfloat16

D_MODEL = 1024
HEAD_DIM = 64
A_Q_HEADS = 8
A_KV_HEADS = 2
A_GROUP = A_Q_HEADS // A_KV_HEADS
GRID_W = 64
ROPE_AXIS_DIM = HEAD_DIM // 2
ROPE_THETA = 10000.0
DILATED_PAIRS = ((128, 1), (512, 4), (2048, 16))
N_DIL_GROUPS = len(DILATED_PAIRS)
B_HEADS_PER_GROUP = 4
N_BUCKETS = 32
MAX_DISTANCE = 1024
A_WIDTH = A_Q_HEADS * HEAD_DIM
A_KV_WIDTH = A_KV_HEADS * HEAD_DIM
B_WIDTH = B_HEADS_PER_GROUP * N_DIL_GROUPS * HEAD_DIM
B_OUT_WIDTH = B_HEADS_PER_GROUP * HEAD_DIM
DT_ROWS = A_WIDTH + 2 * A_KV_WIDTH
N_GROUPS = 4
EXPERTS_PER_GROUP = 4
N_EXPERTS = N_GROUPS * EXPERTS_PER_GROUP
D_EXPERT = 512
PAIRS = ((0, 1), (0, 2), (0, 3), (1, 2), (1, 3), (2, 3))
N_ROUTE_BUCKETS = N_GROUPS * len(PAIRS)
ALPHA = 2.0 ** 0.25
RMS_EPS = 1e-6
LN_EPS = 1e-5
NEG_INF = -1e30

LANES = 128
TM = 512
TMG = 512
TSUB = 128
NQ = 256
A_HEADS_PER_PHASE = 2
BF16_SUBLANES = 16
F32_SUBLANES = 8
VT_ROWS = HEAD_DIM + BF16_SUBLANES
LOG2E = 1.4426950408889634
LN2 = 0.6931471805599453
MAX_SAFE_SHIFT = 40.0
HALO = 64
BAND = NQ + 2 * HALO
B_SUBSEQ_PER_STEP = 4
TMOE = 256
ROUTE_COLS = LANES
SC_CORES = 2
SC_SUBCORES = 16
SC_ROWS = 32
VMEM_LIMIT = 56 * 1024 * 1024


def _params(sem):
    return pltpu.CompilerParams(dimension_semantics=sem, vmem_limit_bytes=VMEM_LIMIT)


def _swap16(x):
    return jnp.concatenate([x[16:32], x[0:16], x[48:64], x[32:48]], axis=0)


def _norm_rope(xh, ta, tb):
    r = lax.rsqrt(jnp.mean(xh * xh, axis=0, keepdims=True) + RMS_EPS)
    return (xh * ta + _swap16(xh) * tb) * r


def _inproj_body(x_ref, wdt_ref, wb_ref, bg_ref, taq_ref, tbq_ref, tak_ref, tbk_ref, after_ref,
                 qt_ref, k_ref, vt_ref, *rest):
    del after_ref
    b_refs, gates_ref, tmp_sc = rest[:3 * N_DIL_GROUPS], rest[3 * N_DIL_GROUPS], rest[-1]
    xb = x_ref[...].astype(BF16)

    def project(lo, hi):
        return jnp.dot(xb, wb_ref[:, lo:hi], preferred_element_type=F32)

    def rope_queries(dt):
        taq, tbq = taq_ref[...], tbq_ref[...]
        for h in range(A_Q_HEADS):
            sl = slice(h * HEAD_DIM, (h + 1) * HEAD_DIM)
            qt_ref[sl, :] = _norm_rope(dt[sl], taq, tbq).astype(BF16)

    def rope_keys_values(dt):
        tak, tbk = tak_ref[...], tbk_ref[...]
        kt = jnp.concatenate(
            [_norm_rope(dt[A_WIDTH + h * HEAD_DIM:A_WIDTH + (h + 1) * HEAD_DIM], tak, tbk)
             for h in range(A_KV_HEADS)], axis=0)
        k_ref[...] = kt.T.astype(BF16)
        for h in range(A_KV_HEADS):
            v0 = A_WIDTH + A_KV_WIDTH + h * HEAD_DIM
            vt_ref[h, 0:HEAD_DIM, :] = dt[v0:v0 + HEAD_DIM].astype(BF16)
            vt_ref[h, HEAD_DIM:, :] = jnp.ones((VT_ROWS - HEAD_DIM, TM), BF16)

    def deinterleave(i, res):
        for c in range(B_WIDTH // LANES):
            tmp_sc[i, c] = res[:, c * LANES:(c + 1) * LANES]
        for gi, (_, r) in enumerate(DILATED_PAIRS):
            ref = b_refs[i * N_DIL_GROUPS + gi]
            for j in range(r):
                for half in range(B_OUT_WIDTH // LANES):
                    c = gi * (B_OUT_WIDTH // LANES) + half
                    lanes = slice(j * B_OUT_WIDTH + half * LANES, j * B_OUT_WIDTH + (half + 1) * LANES)
                    ref[:, lanes] = tmp_sc[i, c, pl.ds(j, TM // r, stride=r), :].astype(BF16)

    def gate(i, pre):
        cols = slice(i * D_MODEL, (i + 1) * D_MODEL)
        gates_ref[:, cols] = jax.nn.sigmoid(pre + bg_ref[:, cols]).astype(BF16)

    g0 = 3 * B_WIDTH
    dt = lax.dot_general(wdt_ref[...], xb, (((1,), (1,)), ((), ())),
                         preferred_element_type=F32)
    res_q = project(0, B_WIDTH)
    rope_queries(dt)
    res_k = project(B_WIDTH, 2 * B_WIDTH)
    rope_keys_values(dt)
    deinterleave(0, res_q)
    res_v = project(2 * B_WIDTH, g0)
    deinterleave(1, res_k)
    pre_a = project(g0, g0 + D_MODEL)
    deinterleave(2, res_v)
    pre_b = project(g0 + D_MODEL, g0 + 2 * D_MODEL)
    gate(0, pre_a)
    gate(1, pre_b)


def _inproj_call(x, wdt, wb, bg, taq, tbq, tak, tbk, after):
    b, s, d = x.shape
    nt = s // TM
    wcols = wb.shape[1]
    tab = pl.BlockSpec((HEAD_DIM, TM), lambda p, i: (0, p))
    const = lambda shape: pl.BlockSpec(shape, lambda p, i: (0,) * len(shape))
    tok = lambda c: pl.BlockSpec((None, TM, c), lambda p, i: (i, p, 0))
    dil = [r for _ in range(3) for _, r in DILATED_PAIRS]
    return pl.pallas_call(
        _inproj_body,
        grid=(nt, b),
        in_specs=[tok(d), const((DT_ROWS, d)), const((d, wcols)), const((1, 2 * D_MODEL)),
                  tab, tab, tab, tab, pl.BlockSpec(memory_space=pl.ANY)],
        out_specs=[
            pl.BlockSpec((None, A_WIDTH, TM), lambda p, i: (i, 0, p)),
            tok(A_KV_WIDTH),
            pl.BlockSpec((None, None, A_KV_HEADS, VT_ROWS, TM), lambda p, i: (i, p, 0, 0, 0)),
        ] + [pl.BlockSpec((None, TM // r, r * B_OUT_WIDTH), lambda p, i: (i, p, 0)) for r in dil]
        + [tok(2 * D_MODEL)],
        out_shape=[
            jax.ShapeDtypeStruct((b, A_WIDTH, s), BF16),
            jax.ShapeDtypeStruct((b, s, A_KV_WIDTH), BF16),
            jax.ShapeDtypeStruct((b, nt, A_KV_HEADS, VT_ROWS, TM), BF16),
        ] + [jax.ShapeDtypeStruct((b, s // r, r * B_OUT_WIDTH), BF16) for r in dil]
        + [jax.ShapeDtypeStruct((b, s, 2 * D_MODEL), BF16)],
        scratch_shapes=[pltpu.VMEM((3, B_WIDTH // LANES, TM, LANES), F32)],
        compiler_params=_params(("arbitrary", "arbitrary")),
        name="inproj",
    )(x, wdt, wb, bg, taq, tbq, tak, tbk, after)


def _rope_tables(s, gain, scale):
    t = np.arange(s)
    inv = ROPE_THETA ** (-np.arange(0, ROPE_AXIS_DIM, 2, dtype=np.float64) / ROPE_AXIS_DIM)
    pos = np.stack([t // GRID_W, t // GRID_W, t % GRID_W, t % GRID_W], axis=0)
    ang = (pos[:, None, :] * inv[None, :, None]).reshape(HEAD_DIM, s)
    sign = np.repeat(np.array([-1.0, 1.0, -1.0, 1.0]), 16)[:, None]
    partner = np.arange(HEAD_DIM) ^ 16
    cos = jnp.asarray(np.cos(ang), F32)
    sin = jnp.asarray(sign * np.sin(ang), F32)
    g = gain.astype(F32) * scale
    return g[:, None] * cos, g[partner][:, None] * sin


def _attn_a_body(fast_ref, kbound_ref, qt_ref, k_ref, vt_ref, o_ref, rhs_sc, acc_sc, ot_sc):
    nchunks = vt_ref.shape[0]
    tk = vt_ref.shape[3]
    heads = range(A_Q_HEADS)
    zeros = jnp.zeros((HEAD_DIM, NQ), BF16)
    for h in heads:
        qh = qt_ref[h * HEAD_DIM:(h + 1) * HEAD_DIM, :]
        rhs_sc[h] = jnp.concatenate([qh, zeros] if h < A_GROUP else [zeros, qh], axis=0)

    @pl.when(fast_ref[0] == 1)
    def _():
        shift = []
        for h in heads:
            qh = qt_ref[h * HEAD_DIM:(h + 1) * HEAD_DIM, :].astype(F32)
            shift.append(jnp.sqrt(jnp.sum(qh * qh, axis=0, keepdims=True)) * kbound_ref[0])

        def scores(c, h):
            return jnp.dot(k_ref[c * tk:(c + 1) * tk, :], rhs_sc[h], preferred_element_type=F32)

        groups = [(kv, c, part) for kv in range(A_KV_HEADS) for c in range(nchunks)
                  for part in range(A_GROUP // A_HEADS_PER_PHASE)]
        members = lambda kv, part: range(kv * A_GROUP + part * A_HEADS_PER_PHASE,
                                         kv * A_GROUP + (part + 1) * A_HEADS_PER_PHASE)
        accs = [None] * A_Q_HEADS
        sts = [scores(groups[0][1], h) for h in members(groups[0][0], groups[0][2])]
        for n, (kv, c, part) in enumerate(groups):
            sts_next = None
            if n + 1 < len(groups):
                kv2, c2, part2 = groups[n + 1]
                sts_next = [scores(c2, h) for h in members(kv2, part2)]
            ps = [jnp.exp2(st - shift[h]).astype(BF16) for st, h in zip(sts, members(kv, part))]
            pvs = [jnp.dot(vt_ref[c, kv], p, preferred_element_type=F32) for p in ps]
            for pv, h in zip(pvs, members(kv, part)):
                accs[h] = pv if accs[h] is None else accs[h] + pv
            sts = sts_next
        for h in heads:
            acc_sc[h] = accs[h]

    @pl.when(fast_ref[0] == 0)
    def _():
        acc_sc[...] = jnp.zeros_like(acc_sc)

        def chunk(c, ms):
            kc = k_ref[pl.ds(pl.multiple_of(c * tk, tk), tk), :]
            new_ms = []
            for h in heads:
                st = jnp.dot(kc, rhs_sc[h], preferred_element_type=F32)
                m_new = jnp.maximum(ms[h], jnp.max(st, axis=0, keepdims=True))
                alpha = jnp.exp2(ms[h] - m_new)
                p = jnp.exp2(st - m_new).astype(BF16)
                acc_sc[h] = alpha * acc_sc[h] + jnp.dot(vt_ref[c, h // A_GROUP], p,
                                                         preferred_element_type=F32)
                new_ms.append(m_new)
            return tuple(new_ms)

        lax.fori_loop(0, nchunks, chunk, tuple(jnp.full((1, NQ), NEG_INF, F32) for _ in heads))

    for h in heads:
        ot_sc[h * HEAD_DIM:(h + 1) * HEAD_DIM, :] = (
            acc_sc[h, 0:HEAD_DIM, :] / acc_sc[h, HEAD_DIM:HEAD_DIM + 1, :])
    o_ref[...] = ot_sc[...].T.astype(BF16)


def _attn_a_call(fast, kbound, qt, k, vt):
    b, _, s = qt.shape
    nchunks, tk = vt.shape[1], vt.shape[4]
    grid_spec = pltpu.PrefetchScalarGridSpec(
        num_scalar_prefetch=2,
        grid=(b, s // NQ),
        in_specs=[
            pl.BlockSpec((None, A_WIDTH, NQ), lambda i, j, *_: (i, 0, j)),
            pl.BlockSpec((None, s, A_KV_WIDTH), lambda i, j, *_: (i, 0, 0)),
            pl.BlockSpec((None, nchunks, A_KV_HEADS, VT_ROWS, tk), lambda i, j, *_: (i, 0, 0, 0, 0)),
        ],
        out_specs=pl.BlockSpec((None, NQ, A_WIDTH), lambda i, j, *_: (i, j, 0)),
        scratch_shapes=[pltpu.VMEM((A_Q_HEADS, A_KV_WIDTH, NQ), BF16),
                        pltpu.VMEM((A_Q_HEADS, VT_ROWS, NQ), F32),
                        pltpu.VMEM((A_WIDTH, NQ), F32)],
    )
    return pl.pallas_call(
        _attn_a_body,
        grid_spec=grid_spec,
        out_shape=jax.ShapeDtypeStruct((b, s, A_WIDTH), BF16),
        compiler_params=_params(("arbitrary", "arbitrary")),
        name="attn_a",
    )(fast, kbound, qt, k, vt)


def _score_bound(q_gain, k_gain):
    kb = math.sqrt(HEAD_DIM) * jnp.max(jnp.abs(k_gain.astype(F32)))
    qb = math.sqrt(HEAD_DIM) * jnp.max(jnp.abs(q_gain.astype(F32))) * (HEAD_DIM ** -0.5) * LOG2E
    fast = (qb * kb <= MAX_SAFE_SHIFT).astype(jnp.int32)
    return fast.reshape(1), kb.reshape(1)


def _attn_b_body(q_ref, k_ref, v_ref, bm_ref, o_ref, lse_ref, kp_sc, vp_sc):
    length = q_ref.shape[0]
    nsub = q_ref.shape[1] // B_OUT_WIDTH
    cols = lambda u: slice(u * B_OUT_WIDTH, (u + 1) * B_OUT_WIDTH)
    zeros = jnp.zeros((HALO, B_OUT_WIDTH), BF16)
    for sc, ref in ((kp_sc, k_ref), (vp_sc, v_ref)):
        for u in range(nsub):
            sc[u, 0:HALO, :] = zeros
            sc[u, HALO:HALO + length, :] = ref[:, cols(u)]
            sc[u, HALO + length:, :] = zeros
    row_head = lax.broadcasted_iota(jnp.int32, (B_OUT_WIDTH, NQ), 0) // HEAD_DIM
    ones = jnp.ones((VT_ROWS - HEAD_DIM, BAND), BF16)
    blocks = {}

    def block(key):
        if key not in blocks:
            u, i = key
            q0 = i * NQ
            qt = q_ref[q0:q0 + NQ, cols(u)].astype(F32).T.astype(BF16)
            kband = kp_sc[u, q0:q0 + BAND, :]
            vbt = vp_sc[u, q0:q0 + BAND, :].astype(F32).T.astype(BF16)
            pen = None
            if q0 - HALO < 0 or q0 - HALO + BAND > length:
                key_pos = lax.broadcasted_iota(jnp.int32, (BAND, NQ), 0) + (q0 - HALO)
                pen = jnp.where((key_pos >= 0) & (key_pos < length), 0.0, NEG_INF)
            blocks[key] = (qt, kband, vbt, pen)
        return blocks[key]

    def scores(key, h):
        qt, kband, _, _ = block(key)
        rhs = jnp.where(row_head == h, qt, jnp.zeros((), BF16))
        return jnp.dot(kband, rhs, preferred_element_type=F32)

    heads = range(B_HEADS_PER_GROUP)
    keys = [(u, i) for u in range(nsub) for i in range(length // NQ)]
    sts = [scores(keys[0], h) for h in heads]
    for n, key in enumerate(keys):
        sts_next = [scores(keys[n + 1], h) for h in heads] if n + 1 < len(keys) else None
        _, _, vbt, pen = block(key)
        sts = [sts[h] + bm_ref[h] for h in heads]
        if pen is not None:
            sts = [st + pen for st in sts]
        ms = [jnp.max(st, axis=0, keepdims=True) for st in sts]
        ps = [jnp.exp2(st - m).astype(BF16) for st, m in zip(sts, ms)]
        accs = [jnp.dot(jnp.concatenate([vbt[h * HEAD_DIM:(h + 1) * HEAD_DIM], ones], axis=0), ps[h],
                        preferred_element_type=F32) for h in heads]
        ls = [acc[HEAD_DIM:HEAD_DIM + 1] for acc in accs]
        u, i = key
        rows = slice(i * NQ, (i + 1) * NQ)
        o_ref[rows, cols(u)] = jnp.concatenate([acc[0:HEAD_DIM] / l for acc, l in zip(accs, ls)], axis=0).T
        lse_ref[rows, cols(u)] = jnp.concatenate(
            [jnp.broadcast_to((m + jnp.log2(l)) * LN2, (HEAD_DIM, NQ)) for m, l in zip(ms, ls)], axis=0).T
        del blocks[key]
        sts = sts_next


def _attn_b_call(qg, kg, vg, bm, r):
    b, length, _ = qg.shape
    nsub = min(r, B_SUBSEQ_PER_STEP)
    spec = pl.BlockSpec((None, length, nsub * B_OUT_WIDTH), lambda i, j: (i, 0, j))
    out_sds = jax.ShapeDtypeStruct((b, length, r * B_OUT_WIDTH), F32)
    return pl.pallas_call(
        _attn_b_body,
        grid=(b, r // nsub),
        in_specs=[spec, spec, spec,
                  pl.BlockSpec((B_HEADS_PER_GROUP, BAND, NQ), lambda i, j: (0, 0, 0))],
        out_specs=[spec, spec],
        out_shape=[out_sds, out_sds],
        scratch_shapes=[pltpu.VMEM((nsub, length + 2 * HALO, B_OUT_WIDTH), BF16),
                        pltpu.VMEM((nsub, length + 2 * HALO, B_OUT_WIDTH), BF16)],
        compiler_params=_params(("arbitrary", "arbitrary")),
        name=f"attn_b_r{r}",
    )(qg, kg, vg, bm)


def _t5_bucket_np(rel):
    half = N_BUCKETS // 2
    exact = half // 2
    n = np.abs(rel)
    big = exact + (np.log(np.maximum(n, 1).astype(np.float64) / exact)
                   / math.log(MAX_DISTANCE / exact) * (half - exact)).astype(np.int64)
    big = np.minimum(big, half - 1)
    return np.where(rel > 0, half, 0) + np.where(n < exact, n, big)


def _bias_mask_table(rel_bias, gi, r):
    span = BAND + NQ
    rel = np.arange(span) - (NQ - 1) - HALO
    tab = rel_bias[:, gi * B_HEADS_PER_GROUP:(gi + 1) * B_HEADS_PER_GROUP].astype(F32) * LOG2E
    u = jnp.where(jnp.asarray(np.abs(rel) <= HALO)[:, None],
                  tab[jnp.asarray(_t5_bucket_np(rel * r), jnp.int32)], NEG_INF).T
    shifted = jnp.tile(u, (1, NQ))[:, :NQ * (span - 1)].reshape(-1, NQ, span - 1)
    return jnp.transpose(shifted[:, :, NQ - 1:NQ - 1 + BAND], (0, 2, 1))


def _layer_norm(h, g, b):
    mu = jnp.mean(h, axis=-1, keepdims=True)
    c = h - mu
    var = jnp.mean(c * c, axis=-1, keepdims=True)
    return c * lax.rsqrt(var + LN_EPS) * g + b


def _split_bf16(a):
    hi = a.astype(BF16)
    return hi, (a - hi.astype(F32)).astype(BF16)


def _merge_body(x_ref, ya_ref, o1_ref, o2_ref, o3_ref, l1_ref, l2_ref, l3_ref, gates_ref,
                wa_ref, wbr_ref, wo_ref, g_ref, b_ref, wr2_ref, br_ref, out_ref, bkt_ref, cnt_ref,
                *tok_sc):
    slabs = B_OUT_WIDTH // LANES
    dil = [r for _, r in DILATED_PAIRS]
    b_refs = (o1_ref, o2_ref, o3_ref, l1_ref, l2_ref, l3_ref)
    for ref, sc, r in zip(b_refs, tok_sc, dil + dil):
        if r > 1:
            for j in range(r):
                for c in range(slabs):
                    lanes = slice(j * B_OUT_WIDTH + c * LANES, j * B_OUT_WIDTH + (c + 1) * LANES)
                    sc[c, pl.ds(j, TMG // r, stride=r), :] = ref[:, lanes]

    def token_major(which, rows):
        ref, sc, r = b_refs[which], tok_sc[which], (dil + dil)[which]
        if r == 1:
            return ref[rows, :]
        return jnp.concatenate([sc[c, rows, :] for c in range(slabs)], axis=1)

    subs = [slice(i * TSUB, (i + 1) * TSUB) for i in range(TMG // TSUB)]
    ybs = []
    for rows in subs:
        o1, o2, o3, l1, l2, l3 = (token_major(w, rows) for w in range(6))
        mx = jnp.maximum(jnp.maximum(l1, l2), l3)
        e1, e2, e3 = jnp.exp(l1 - mx), jnp.exp(l2 - mx), jnp.exp(l3 - mx)
        ybs.append(((e1 * o1 + e2 * o2 + e3 * o3) / (e1 + e2 + e3)).astype(BF16))
    ya_ps = [jnp.dot(ya_ref[rows, :], wa_ref[...], preferred_element_type=F32) for rows in subs]
    yb_ps = [jnp.dot(yb, wbr_ref[...], preferred_element_type=F32) for yb in ybs]
    mergeds = [(gates_ref[rows, 0:D_MODEL].astype(F32) * ya_p
                + gates_ref[rows, D_MODEL:2 * D_MODEL].astype(F32) * yb_p).astype(BF16)
               for rows, ya_p, yb_p in zip(subs, ya_ps, yb_ps)]
    hs = [ALPHA * x_ref[rows, :] + jnp.dot(m, wo_ref[...], preferred_element_type=F32)
          for rows, m in zip(subs, mergeds)]
    x1s = [_layer_norm(h, g_ref[...], b_ref[...]) for h in hs]
    for rows, x1 in zip(subs, x1s):
        out_ref[rows, 0:D_MODEL] = x1

    splits = [_split_bf16(x1) for x1 in x1s]
    boths = [jnp.dot(xh, wr2_ref[...], preferred_element_type=F32)
             + jnp.dot(xl, wr2_ref[...], preferred_element_type=F32) for xh, xl in splits]
    logits = jnp.concatenate(
        [(both[:, 0:ROUTE_COLS] + both[:, ROUTE_COLS:] + br_ref[...]).T for both in boths], axis=1)
    col = lambda i: logits[i:i + 1, :]
    lg = [col(i) for i in range(N_GROUPS)]
    gmax = functools.reduce(jnp.maximum, lg)
    top_p = 1.0 / functools.reduce(lambda a, c: a + c, [jnp.exp(v - gmax) for v in lg])
    gsel = jnp.full_like(gmax, N_GROUPS - 1).astype(jnp.int32)
    for i in reversed(range(N_GROUPS - 1)):
        gsel = jnp.where(lg[i] >= gmax, i, gsel)
    sel = []
    for e in range(EXPERTS_PER_GROUP):
        v = col(N_GROUPS + (N_GROUPS - 1) * EXPERTS_PER_GROUP + e)
        for gi in reversed(range(N_GROUPS - 1)):
            v = jnp.where(gsel == gi, col(N_GROUPS + gi * EXPERTS_PER_GROUP + e), v)
        sel.append(v)
    rank = []
    for e in range(EXPERTS_PER_GROUP):
        rk = jnp.zeros_like(gsel)
        for j in range(EXPERTS_PER_GROUP):
            if j != e:
                ahead = (sel[j] > sel[e]) | ((sel[j] == sel[e]) & (j < e))
                rk = rk + ahead.astype(jnp.int32)
        rank.append(rk)
    pick = lambda k: functools.reduce(
        lambda a, c: a + c, [jnp.where(rank[e] == k, sel[e], 0.0) for e in range(EXPERTS_PER_GROUP)])
    t = jnp.exp(pick(1) - pick(0))
    w_first = top_p / (1.0 + t)
    w_second = top_p * t / (1.0 + t)
    chosen = [rank[e] < 2 for e in range(EXPERTS_PER_GROUP)]
    weight = [jnp.where(rank[e] == 0, w_first, w_second) for e in range(EXPERTS_PER_GROUP)]
    pair = jnp.zeros_like(gsel)
    w_lo = jnp.zeros_like(top_p)
    w_hi = jnp.zeros_like(top_p)
    for pi, (a, c) in enumerate(PAIRS):
        hit = chosen[a] & chosen[c]
        pair = jnp.where(hit, pi, pair)
        w_lo = jnp.where(hit, weight[a], w_lo)
        w_hi = jnp.where(hit, weight[c], w_hi)
    bucket = gsel * len(PAIRS) + pair
    bkt_ref[...] = bucket
    hits = lax.broadcasted_iota(jnp.int32, (ROUTE_COLS, TMG), 0) == bucket

    @pl.when(pl.program_id(0) == 0)
    def _():
        cnt_ref[...] = jnp.zeros_like(cnt_ref)

    cnt_ref[...] += jnp.sum(hits.astype(jnp.int32), axis=1, keepdims=True)
    row = lax.broadcasted_iota(jnp.int32, (F32_SUBLANES, TMG), 0)
    head = jnp.where(row == 1, w_lo, jnp.where(row == 2, w_hi, 0.0))
    route_t = jnp.concatenate([head, jnp.zeros((ROUTE_COLS - F32_SUBLANES, TMG), F32)], axis=0)
    out_ref[:, D_MODEL:] = route_t.T


def _merge_call(x, ya, os, ls, gates, wa, wbr, wo, g, b, wr2, br):
    t, d = x.shape
    tok = lambda c: pl.BlockSpec((TMG, c), lambda i: (i, 0))
    const = lambda a: pl.BlockSpec(a.shape, lambda i: (0,) * a.ndim)
    consts = (wa, wbr, wo, g, b, wr2, br)
    dil = [pl.BlockSpec((TMG // r, r * B_OUT_WIDTH), lambda i: (i, 0)) for _, r in DILATED_PAIRS]
    return pl.pallas_call(
        _merge_body,
        grid=(t // TMG,),
        in_specs=[tok(d), tok(A_WIDTH)] + dil + dil + [tok(2 * D_MODEL)]
        + [const(a) for a in consts],
        out_specs=[tok(d + ROUTE_COLS), pl.BlockSpec((1, TMG), lambda i: (0, i)),
                   pl.BlockSpec((ROUTE_COLS, 1), lambda i: (0, 0))],
        out_shape=[jax.ShapeDtypeStruct((t, d + ROUTE_COLS), F32), jax.ShapeDtypeStruct((1, t), jnp.int32),
                   jax.ShapeDtypeStruct((ROUTE_COLS, 1), jnp.int32)],
        scratch_shapes=[pltpu.VMEM((B_OUT_WIDTH // LANES, TMG, LANES), F32)] * (2 * N_DIL_GROUPS),
        compiler_params=_params(("arbitrary",)),
        name="merge_ln1_route",
    )(x, ya, *os, *ls, gates, *consts)


def _sc_permute_rows(src, idx, nrows_out, scatter):
    nrows, d = idx.shape[0], src.shape[1]
    workers = SC_CORES * SC_SUBCORES
    per_worker = nrows // workers
    assert nrows % (workers * SC_ROWS) == 0
    mesh = plsc.VectorSubcoreMesh(core_axis_name="c", subcore_axis_name="s")

    @functools.partial(
        pl.kernel, mesh=mesh, out_type=jax.ShapeDtypeStruct((nrows_out, d), src.dtype),
        scratch_types=[pltpu.VMEM((SC_ROWS,), jnp.int32), pltpu.VMEM((SC_ROWS, d), src.dtype),
                       pltpu.SemaphoreType.DMA])
    def permute(src_hbm, idx_hbm, out_hbm, idx_v, rows_v, sem):
        base = (lax.axis_index("s") * SC_CORES + lax.axis_index("c")) * per_worker

        @pl.loop(0, per_worker, step=SC_ROWS)
        def _(off):
            rows = pl.ds(pl.multiple_of(base + off, SC_ROWS), SC_ROWS)
            pltpu.sync_copy(idx_hbm.at[rows], idx_v)
            if scatter:
                pltpu.sync_copy(src_hbm.at[rows], rows_v)
                pltpu.async_copy(rows_v, out_hbm.at[idx_v], sem).wait()
            else:
                pltpu.async_copy(src_hbm.at[idx_v], rows_v, sem).wait()
                pltpu.sync_copy(rows_v, out_hbm.at[rows])

    return permute(src, idx)


def _moe_tiles_body(nvalid_ref, elo_ref, ehi_ref, x_ref, wil_ref, wih_ref, wol_ref, woh_ref,
                    g_ref, b_ref, o_ref):
    del elo_ref, ehi_ref
    n = nvalid_ref[pl.program_id(0)]

    @pl.when(n == 0)
    def _():
        o_ref[...] = jnp.zeros_like(o_ref)

    @pl.when(n > 0)
    def _():
        xt = x_ref[:, 0:D_MODEL]
        xb = xt.astype(BF16)
        hcats = [jnp.dot(xb, wi_ref[...], preferred_element_type=F32)
                 for wi_ref in (wil_ref, wih_ref)]
        acts = [(h[:, 0:D_EXPERT] * jax.nn.sigmoid(h[:, 0:D_EXPERT]) * h[:, D_EXPERT:]).astype(BF16)
                for h in hcats]
        y_lo, y_hi = (jnp.dot(act, wo_ref[...], preferred_element_type=F32)
                      for act, wo_ref in zip(acts, (wol_ref, woh_ref)))
        moe = (x_ref[:, D_MODEL + 1:D_MODEL + 2] * y_lo + x_ref[:, D_MODEL + 2:D_MODEL + 3] * y_hi)
        o_ref[...] = _layer_norm(ALPHA * xt + moe, g_ref[...], b_ref[...])


def _moe_tiles_call(x_sorted, nvalid, elo, ehi, wei, weo, g, b):
    ntiles = nvalid.shape[0]
    wi_spec = lambda e: pl.BlockSpec((None, D_MODEL, 2 * D_EXPERT), lambda i, nv, lo, hi: (e(lo, hi)[i], 0, 0))
    wo_spec = lambda e: pl.BlockSpec((None, D_EXPERT, D_MODEL), lambda i, nv, lo, hi: (e(lo, hi)[i], 0, 0))
    first = lambda lo, hi: lo
    second = lambda lo, hi: hi
    vec = pl.BlockSpec((1, D_MODEL), lambda i, nv, lo, hi: (0, 0))
    grid_spec = pltpu.PrefetchScalarGridSpec(
        num_scalar_prefetch=3,
        grid=(ntiles,),
        in_specs=[pl.BlockSpec((TMOE, D_MODEL + ROUTE_COLS), lambda i, nv, lo, hi: (i, 0)),
                  wi_spec(first), wi_spec(second), wo_spec(first), wo_spec(second), vec, vec],
        out_specs=pl.BlockSpec((TMOE, D_MODEL), lambda i, nv, lo, hi: (i, 0)),
    )
    return pl.pallas_call(
        _moe_tiles_body,
        grid_spec=grid_spec,
        out_shape=jax.ShapeDtypeStruct((ntiles * TMOE, D_MODEL), F32),
        compiler_params=_params(("arbitrary",)),
        name="moe_ln2",
    )(nvalid, elo, ehi, x_sorted, wei, wei, weo, weo, g, b)


def _route_plan(bucket, counts):
    t = bucket.shape[0]
    ntiles = t // TMOE + N_ROUTE_BUCKETS
    token = jnp.arange(t, dtype=jnp.int32)
    _, order = lax.sort((bucket, token), num_keys=1, is_stable=True)
    row_start = jnp.cumsum(counts) - counts
    tiles_per = (counts + TMOE - 1) // TMOE
    tile_end = jnp.cumsum(tiles_per)
    tile_start = tile_end - tiles_per
    total = tile_end[-1]
    tile = jnp.arange(ntiles, dtype=jnp.int32)
    used = tile < total
    bkt = jnp.sum(jnp.minimum(tile, total - 1)[:, None] >= tile_end[None, :], axis=1, dtype=jnp.int32)
    local = tile - tile_start[bkt]
    nvalid = jnp.where(used, jnp.clip(counts[bkt] - local * TMOE, 0, TMOE), 0).astype(jnp.int32)
    pairs = jnp.asarray(PAIRS, jnp.int32)
    grp = bkt // len(PAIRS)
    elo = grp * EXPERTS_PER_GROUP + pairs[bkt % len(PAIRS), 0]
    ehi = grp * EXPERTS_PER_GROUP + pairs[bkt % len(PAIRS), 1]
    shift = tile_start * TMOE - row_start
    step = shift - jnp.concatenate([jnp.zeros((1,), jnp.int32), shift[:-1]])
    padded_row = token + jnp.sum(jnp.where(token[None, :] >= row_start[:, None], step[:, None], 0), axis=0)
    _, pos = lax.sort((order, padded_row), num_keys=1)
    return nvalid, elo.astype(jnp.int32), ehi.astype(jnp.int32), pos


def _prepare(w_in, b_gate, q_gain, k_gain, rel_bias, w_branch_a, w_branch_b, w_out, ln1_g, ln1_b,
             w_route_group, b_route_group, w_route_expert, b_route_expert, w_expert_in,
             w_expert_out, ln2_g, ln2_b, seq):
    row = lambda v: v.reshape(1, -1).astype(F32)
    w = w_in[0]
    wr = jnp.concatenate([w_route_group[0]]
                         + [w_route_expert[0, gi] for gi in range(N_GROUPS)], axis=1)
    wr = jnp.pad(wr.astype(F32), ((0, 0), (0, ROUTE_COLS - wr.shape[1])))
    br = jnp.concatenate([b_route_group[0], b_route_expert[0].reshape(-1)])
    br = jnp.pad(br.astype(F32), (0, ROUTE_COLS - br.shape[0])).reshape(1, ROUTE_COLS)
    wr2 = jnp.concatenate(_split_bf16(wr), axis=1)
    scale = HEAD_DIM ** -0.5
    wb = w[:, DT_ROWS:]
    wb = jnp.concatenate([wb[:, 0:B_WIDTH] * (scale * LOG2E), wb[:, B_WIDTH:]], axis=1)
    return dict(
        wdt=w[:, 0:DT_ROWS].T.astype(BF16), wb=wb.astype(BF16), bg=row(b_gate[0]),
        rope_q=_rope_tables(seq, q_gain[0], scale * LOG2E), rope_k=_rope_tables(seq, k_gain[0], 1.0),
        score_bound=_score_bound(q_gain[0], k_gain[0]),
        bm=[_bias_mask_table(rel_bias, gi, r) for gi, (_, r) in enumerate(DILATED_PAIRS)],
        wa=w_branch_a[0].astype(BF16), wbr=w_branch_b[0].astype(BF16), wo=w_out[0].astype(BF16),
        ln1=(row(ln1_g[0]), row(ln1_b[0])), wr2=wr2, br=br,
        wei=w_expert_in[0].astype(BF16), weo=w_expert_out[0].astype(BF16),
        ln2=(row(ln2_g[0]), row(ln2_b[0])),
    )


def _trunk(x, p, after):
    b, s, d = x.shape
    qt, k, vt, *qkv_b, gates = _inproj_call(x, p["wdt"], p["wb"], p["bg"], *p["rope_q"], *p["rope_k"],
                                            after)
    ya = _attn_a_call(*p["score_bound"], qt, k, vt)
    os, ls = [], []
    for gi, (_, r) in enumerate(DILATED_PAIRS):
        qg, kg, vg = (qkv_b[i * N_DIL_GROUPS + gi] for i in range(3))
        o, lse = _attn_b_call(qg, kg, vg, p["bm"][gi], r)
        os.append(o.reshape(b * s // r, r * B_OUT_WIDTH))
        ls.append(lse.reshape(b * s // r, r * B_OUT_WIDTH))
    return _merge_call(x.reshape(b * s, d), ya.reshape(b * s, A_WIDTH), os, ls,
                       gates.reshape(b * s, 2 * D_MODEL), p["wa"], p["wbr"], p["wo"], *p["ln1"],
                       p["wr2"], p["br"])


def _sorted_rows(x1ext, plan):
    ntiles = plan[0].shape[0]
    return _sc_permute_rows(x1ext, plan[3], ntiles * TMOE, scatter=True)


def _experts(x_sorted, plan, p, shape):
    nvalid, elo, ehi, pos = plan
    y_sorted = _moe_tiles_call(x_sorted, nvalid, elo, ehi, p["wei"], p["weo"], *p["ln2"])
    return _sc_permute_rows(y_sorted, pos, pos.shape[0], scatter=False).reshape(shape)


def kernel(x_prompt, x_sample, w_in, b_gate, q_gain, k_gain, rel_bias, w_branch_a, w_branch_b, w_out,
           ln1_g, ln1_b, w_route_group, b_route_group, w_route_expert, b_route_expert, w_expert_in,
           w_expert_out, ln2_g, ln2_b):
    assert x_prompt.shape[1] == x_sample.shape[1]
    p = _prepare(w_in, b_gate, q_gain, k_gain, rel_bias, w_branch_a, w_branch_b, w_out, ln1_g, ln1_b,
                 w_route_group, b_route_group, w_route_expert, b_route_expert, w_expert_in,
                 w_expert_out, ln2_g, ln2_b, x_prompt.shape[1])
    plan = lambda bucket, counts: _route_plan(bucket.reshape(-1), counts[:N_ROUTE_BUCKETS, 0])
    x1_sample, *route_sample = _trunk(x_sample, p, p["br"])
    plan_sample = plan(*route_sample)
    sorted_sample = _sorted_rows(x1_sample, plan_sample)
    x1_prompt, *route_prompt = _trunk(x_prompt, p, plan_sample[3])
    plan_prompt = plan(*route_prompt)
    sorted_prompt = _sorted_rows(x1_prompt, plan_prompt)
    y_sample = _experts(sorted_sample, plan_sample, p, x_sample.shape)
    y_prompt = _experts(sorted_prompt, plan_prompt, p, x_prompt.shape)
    return (y_prompt, y_sample)
```

```python
import functools
import math

import jax
import jax.numpy as jnp
import numpy as np
from jax import lax
from jax.experimental import pallas as pl
from jax.experimental.pallas import tpu as pltpu
from jax.experimental.pallas import tpu_sc as plsc

F32 = jnp.float32
BF16 = jnp.bfloat16

D_MODEL = 1024
HEAD_DIM = 64
A_Q_HEADS = 8
A_KV_HEADS = 2
A_GROUP = A_Q_HEADS // A_KV_HEADS
GRID_W = 64
ROPE_AXIS_DIM = HEAD_DIM // 2
ROPE_THETA = 10000.0
DILATED_PAIRS = ((128, 1), (512, 4), (2048, 16))
N_DIL_GROUPS = len(DILATED_PAIRS)
B_HEADS_PER_GROUP = 4
N_BUCKETS = 32
MAX_DISTANCE = 1024
A_WIDTH = A_Q_HEADS * HEAD_DIM
A_KV_WIDTH = A_KV_HEADS * HEAD_DIM
B_WIDTH = B_HEADS_PER_GROUP * N_DIL_GROUPS * HEAD_DIM
B_OUT_WIDTH = B_HEADS_PER_GROUP * HEAD_DIM
DT_ROWS = A_WIDTH + 2 * A_KV_WIDTH
N_GROUPS = 4
EXPERTS_PER_GROUP = 4
N_EXPERTS = N_GROUPS * EXPERTS_PER_GROUP
D_EXPERT = 512
PAIRS = ((0, 1), (0, 2), (0, 3), (1, 2), (1, 3), (2, 3))
N_ROUTE_BUCKETS = N_GROUPS * len(PAIRS)
ALPHA = 2.0 ** 0.25
RMS_EPS = 1e-6
LN_EPS = 1e-5
NEG_INF = -1e30

LANES = 128
TM = 512
TMG = 512
TSUB = 128
NQ = 256
A_HEADS_PER_PHASE = 2
A_TK = 256
A_LOOKAHEAD = 2
BF16_SUBLANES = 16
F32_SUBLANES = 8
VT_ROWS = HEAD_DIM + BF16_SUBLANES
LOG2E = 1.4426950408889634
LN2 = 0.6931471805599453
MAX_SAFE_SHIFT = 40.0
HALO = 64
BAND = NQ + 2 * HALO
B_SUBSEQ_PER_STEP = 4
TMOE = 256
ROUTE_COLS = LANES
SC_CORES = 2
SC_SUBCORES = 16
SC_ROWS = 32
VMEM_LIMIT = 56 * 1024 * 1024


def _params(sem):
    return pltpu.CompilerParams(dimension_semantics=sem, vmem_limit_bytes=VMEM_LIMIT)


def _swap16(x):
    return jnp.concatenate([x[16:32], x[0:16], x[48:64], x[32:48]], axis=0)


def _norm_rope(xh, ta, tb):
    r = lax.rsqrt(jnp.mean(xh * xh, axis=0, keepdims=True) + RMS_EPS)
    return (xh * ta + _swap16(xh) * tb) * r


def _inproj_body(x_ref, wdt_ref, wb_ref, bg_ref, taq_ref, tbq_ref, tak_ref, tbk_ref, after_ref,
                 qt_ref, k_ref, vt_ref, *rest):
    del after_ref
    b_refs, gates_ref, tmp_sc = rest[:3 * N_DIL_GROUPS], rest[3 * N_DIL_GROUPS], rest[-1]
    xb = x_ref[...].astype(BF16)

    def project(lo, hi):
        return jnp.dot(xb, wb_ref[:, lo:hi], preferred_element_type=F32)

    def rope_queries(dt):
        taq, tbq = taq_ref[...], tbq_ref[...]
        for h in range(A_Q_HEADS):
            sl = slice(h * HEAD_DIM, (h + 1) * HEAD_DIM)
            qt_ref[sl, :] = _norm_rope(dt[sl], taq, tbq).astype(BF16)

    def rope_keys_values(dt):
        tak, tbk = tak_ref[...], tbk_ref[...]
        kt = jnp.concatenate(
            [_norm_rope(dt[A_WIDTH + h * HEAD_DIM:A_WIDTH + (h + 1) * HEAD_DIM], tak, tbk)
             for h in range(A_KV_HEADS)], axis=0)
        k_ref[...] = kt.T.astype(BF16)
        for h in range(A_KV_HEADS):
            v0 = A_WIDTH + A_KV_WIDTH + h * HEAD_DIM
            vt_ref[h, 0:HEAD_DIM, :] = dt[v0:v0 + HEAD_DIM].astype(BF16)
            vt_ref[h, HEAD_DIM:, :] = jnp.ones((VT_ROWS - HEAD_DIM, TM), BF16)

    def deinterleave(i, res):
        for c in range(B_WIDTH // LANES):
            tmp_sc[i, c] = res[:, c * LANES:(c + 1) * LANES]
        for gi, (_, r) in enumerate(DILATED_PAIRS):
            ref = b_refs[i * N_DIL_GROUPS + gi]
            for j in range(r):
                for half in range(B_OUT_WIDTH // LANES):
                    c = gi * (B_OUT_WIDTH // LANES) + half
                    lanes = slice(j * B_OUT_WIDTH + half * LANES, j * B_OUT_WIDTH + (half + 1) * LANES)
                    ref[:, lanes] = tmp_sc[i, c, pl.ds(j, TM // r, stride=r), :].astype(BF16)

    def gate(i, pre):
        cols = slice(i * D_MODEL, (i + 1) * D_MODEL)
        gates_ref[:, cols] = jax.nn.sigmoid(pre + bg_ref[:, cols]).astype(BF16)

    g0 = 3 * B_WIDTH
    dt = lax.dot_general(wdt_ref[...], xb, (((1,), (1,)), ((), ())),
                         preferred_element_type=F32)
    res_q = project(0, B_WIDTH)
    rope_queries(dt)
    res_k = project(B_WIDTH, 2 * B_WIDTH)
    rope_keys_values(dt)
    deinterleave(0, res_q)
    res_v = project(2 * B_WIDTH, g0)
    deinterleave(1, res_k)
    pre_a = project(g0, g0 + D_MODEL)
    deinterleave(2, res_v)
    pre_b = project(g0 + D_MODEL, g0 + 2 * D_MODEL)
    gate(0, pre_a)
    gate(1, pre_b)


def _inproj_call(x, wdt, wb, bg, taq, tbq, tak, tbk, after):
    b, s, d = x.shape
    nt = s // TM
    wcols = wb.shape[1]
    tab = pl.BlockSpec((HEAD_DIM, TM), lambda p, i: (0, p))
    const = lambda shape: pl.BlockSpec(shape, lambda p, i: (0,) * len(shape))
    tok = lambda c: pl.BlockSpec((None, TM, c), lambda p, i: (i, p, 0))
    dil = [r for _ in range(3) for _, r in DILATED_PAIRS]
    return pl.pallas_call(
        _inproj_body,
        grid=(nt, b),
        in_specs=[tok(d), const((DT_ROWS, d)), const((d, wcols)), const((1, 2 * D_MODEL)),
                  tab, tab, tab, tab, pl.BlockSpec(memory_space=pl.ANY)],
        out_specs=[
            pl.BlockSpec((None, A_WIDTH, TM), lambda p, i: (i, 0, p)),
            tok(A_KV_WIDTH),
            pl.BlockSpec((None, None, A_KV_HEADS, VT_ROWS, TM), lambda p, i: (i, p, 0, 0, 0)),
        ] + [pl.BlockSpec((None, TM // r, r * B_OUT_WIDTH), lambda p, i: (i, p, 0)) for r in dil]
        + [tok(2 * D_MODEL)],
        out_shape=[
            jax.ShapeDtypeStruct((b, A_WIDTH, s), BF16),
            jax.ShapeDtypeStruct((b, s, A_KV_WIDTH), BF16),
            jax.ShapeDtypeStruct((b, nt, A_KV_HEADS, VT_ROWS, TM), BF16),
        ] + [jax.ShapeDtypeStruct((b, s // r, r * B_OUT_WIDTH), BF16) for r in dil]
        + [jax.ShapeDtypeStruct((b, s, 2 * D_MODEL), BF16)],
        scratch_shapes=[pltpu.VMEM((3, B_WIDTH // LANES, TM, LANES), F32)],
        compiler_params=_params(("arbitrary", "arbitrary")),
        name="inproj",
    )(x, wdt, wb, bg, taq, tbq, tak, tbk, after)


def _rope_tables(s, gain, scale):
    t = np.arange(s)
    inv = ROPE_THETA ** (-np.arange(0, ROPE_AXIS_DIM, 2, dtype=np.float64) / ROPE_AXIS_DIM)
    pos = np.stack([t // GRID_W, t // GRID_W, t % GRID_W, t % GRID_W], axis=0)
    ang = (pos[:, None, :] * inv[None, :, None]).reshape(HEAD_DIM, s)
    sign = np.repeat(np.array([-1.0, 1.0, -1.0, 1.0]), 16)[:, None]
    partner = np.arange(HEAD_DIM) ^ 16
    cos = jnp.asarray(np.cos(ang), F32)
    sin = jnp.asarray(sign * np.sin(ang), F32)
    g = gain.astype(F32) * scale
    return g[:, None] * cos, g[partner][:, None] * sin


def _attn_a_body(fast_ref, kbound_ref, qt_ref, k_ref, vt_ref, o_ref, rhs_sc, acc_sc, ot_sc):
    nchunks = vt_ref.shape[0]
    tk = vt_ref.shape[3]
    heads = range(A_Q_HEADS)
    zeros = jnp.zeros((HEAD_DIM, NQ), BF16)
    for h in heads:
        qh = qt_ref[h * HEAD_DIM:(h + 1) * HEAD_DIM, :]
        rhs_sc[h] = jnp.concatenate([qh, zeros] if h < A_GROUP else [zeros, qh], axis=0)

    @pl.when(fast_ref[0] == 1)
    def _():
        shift = []
        for h in heads:
            qh = qt_ref[h * HEAD_DIM:(h + 1) * HEAD_DIM, :].astype(F32)
            shift.append(jnp.sqrt(jnp.sum(qh * qh, axis=0, keepdims=True)) * kbound_ref[0])

        def scores(c, h):
            return jnp.dot(k_ref[c * A_TK:(c + 1) * A_TK, :], rhs_sc[h], preferred_element_type=F32)

        def values(c, kv):
            per = tk // A_TK
            return vt_ref[c // per, kv][:, (c % per) * A_TK:(c % per + 1) * A_TK]

        groups = [(kv, c, part) for kv in range(A_KV_HEADS) for c in range(nchunks * tk // A_TK)
                  for part in range(A_GROUP // A_HEADS_PER_PHASE)]
        members = lambda kv, part: range(kv * A_GROUP + part * A_HEADS_PER_PHASE,
                                         kv * A_GROUP + (part + 1) * A_HEADS_PER_PHASE)
        issue = lambda group: [scores(group[1], h) for h in members(group[0], group[2])]
        accs = [None] * A_Q_HEADS
        pending = [issue(group) for group in groups[:A_LOOKAHEAD]]
        for n, (kv, c, part) in enumerate(groups):
            if n + A_LOOKAHEAD < len(groups):
                pending.append(issue(groups[n + A_LOOKAHEAD]))
            sts = pending.pop(0)
            ps = [jnp.exp2(st - shift[h]).astype(BF16) for st, h in zip(sts, members(kv, part))]
            pvs = [jnp.dot(values(c, kv), p, preferred_element_type=F32) for p in ps]
            for pv, h in zip(pvs, members(kv, part)):
                accs[h] = pv if accs[h] is None else accs[h] + pv
        for h in heads:
            acc_sc[h] = accs[h]

    @pl.when(fast_ref[0] == 0)
    def _():
        acc_sc[...] = jnp.zeros_like(acc_sc)

        def chunk(c, ms):
            kc = k_ref[pl.ds(pl.multiple_of(c * tk, tk), tk), :]
            new_ms = []
            for h in heads:
                st = jnp.dot(kc, rhs_sc[h], preferred_element_type=F32)
                m_new = jnp.maximum(ms[h], jnp.max(st, axis=0, keepdims=True))
                alpha = jnp.exp2(ms[h] - m_new)
                p = jnp.exp2(st - m_new).astype(BF16)
                acc_sc[h] = alpha * acc_sc[h] + jnp.dot(vt_ref[c, h // A_GROUP], p,
                                                         preferred_element_type=F32)
                new_ms.append(m_new)
            return tuple(new_ms)

        lax.fori_loop(0, nchunks, chunk, tuple(jnp.full((1, NQ), NEG_INF, F32) for _ in heads))

    for h in heads:
        ot_sc[h * HEAD_DIM:(h + 1) * HEAD_DIM, :] = (
            acc_sc[h, 0:HEAD_DIM, :] / acc_sc[h, HEAD_DIM:HEAD_DIM + 1, :])
    o_ref[...] = ot_sc[...].T.astype(BF16)


def _attn_a_call(fast, kbound, qt, k, vt):
    b, _, s = qt.shape
    nchunks, tk = vt.shape[1], vt.shape[4]
    grid_spec = pltpu.PrefetchScalarGridSpec(
        num_scalar_prefetch=2,
        grid=(b, s // NQ),
        in_specs=[
            pl.BlockSpec((None, A_WIDTH, NQ), lambda i, j, *_: (i, 0, j)),
            pl.BlockSpec((None, s, A_KV_WIDTH), lambda i, j, *_: (i, 0, 0)),
            pl.BlockSpec((None, nchunks, A_KV_HEADS, VT_ROWS, tk), lambda i, j, *_: (i, 0, 0, 0, 0)),
        ],
        out_specs=pl.BlockSpec((None, NQ, A_WIDTH), lambda i, j, *_: (i, j, 0)),
        scratch_shapes=[pltpu.VMEM((A_Q_HEADS, A_KV_WIDTH, NQ), BF16),
                        pltpu.VMEM((A_Q_HEADS, VT_ROWS, NQ), F32),
                        pltpu.VMEM((A_WIDTH, NQ), F32)],
    )
    return pl.pallas_call(
        _attn_a_body,
        grid_spec=grid_spec,
        out_shape=jax.ShapeDtypeStruct((b, s, A_WIDTH), BF16),
        compiler_params=_params(("arbitrary", "arbitrary")),
        name="attn_a",
    )(fast, kbound, qt, k, vt)


def _score_bound(q_gain, k_gain):
    kb = math.sqrt(HEAD_DIM) * jnp.max(jnp.abs(k_gain.astype(F32)))
    qb = math.sqrt(HEAD_DIM) * jnp.max(jnp.abs(q_gain.astype(F32))) * (HEAD_DIM ** -0.5) * LOG2E
    fast = (qb * kb <= MAX_SAFE_SHIFT).astype(jnp.int32)
    return fast.reshape(1), kb.reshape(1)


def _attn_b_body(q_ref, k_ref, v_ref, bm_ref, o_ref, lse_ref, kp_sc, vp_sc):
    length = q_ref.shape[0]
    nsub = q_ref.shape[1] // B_OUT_WIDTH
    cols = lambda u: slice(u * B_OUT_WIDTH, (u + 1) * B_OUT_WIDTH)
    zeros = jnp.zeros((HALO, B_OUT_WIDTH), BF16)
    for sc, ref in ((kp_sc, k_ref), (vp_sc, v_ref)):
        for u in range(nsub):
            sc[u, 0:HALO, :] = zeros
            sc[u, HALO:HALO + length, :] = ref[:, cols(u)]
            sc[u, HALO + length:, :] = zeros
    row_head = lax.broadcasted_iota(jnp.int32, (B_OUT_WIDTH, NQ), 0) // HEAD_DIM
    ones = jnp.ones((VT_ROWS - HEAD_DIM, BAND), BF16)
    blocks = {}

    def block(key):
        if key not in blocks:
            u, i = key
            q0 = i * NQ
            qt = q_ref[q0:q0 + NQ, cols(u)].astype(F32).T.astype(BF16)
            kband = kp_sc[u, q0:q0 + BAND, :]
            vbt = vp_sc[u, q0:q0 + BAND, :].astype(F32).T.astype(BF16)
            pen = None
            if q0 - HALO < 0 or q0 - HALO + BAND > length:
                key_pos = lax.broadcasted_iota(jnp.int32, (BAND, NQ), 0) + (q0 - HALO)
                pen = jnp.where((key_pos >= 0) & (key_pos < length), 0.0, NEG_INF)
            blocks[key] = (qt, kband, vbt, pen)
        return blocks[key]

    def scores(key, h):
        qt, kband, _, _ = block(key)
        rhs = jnp.where(row_head == h, qt, jnp.zeros((), BF16))
        return jnp.dot(kband, rhs, preferred_element_type=F32)

    heads = range(B_HEADS_PER_GROUP)
    keys = [(u, i) for u in range(nsub) for i in range(length // NQ)]
    sts = [scores(keys[0], h) for h in heads]
    for n, key in enumerate(keys):
        sts_next = [scores(keys[n + 1], h) for h in heads] if n + 1 < len(keys) else None
        _, _, vbt, pen = block(key)
        sts = [sts[h] + bm_ref[h] for h in heads]
        if pen is not None:
            sts = [st + pen for st in sts]
        ms = [jnp.max(st, axis=0, keepdims=True) for st in sts]
        ps = [jnp.exp2(st - m).astype(BF16) for st, m in zip(sts, ms)]
        accs = [jnp.dot(jnp.concatenate([vbt[h * HEAD_DIM:(h + 1) * HEAD_DIM], ones], axis=0), ps[h],
                        preferred_element_type=F32) for h in heads]
        ls = [acc[HEAD_DIM:HEAD_DIM + 1] for acc in accs]
        u, i = key
        rows = slice(i * NQ, (i + 1) * NQ)
        o_ref[rows, cols(u)] = jnp.concatenate([acc[0:HEAD_DIM] / l for acc, l in zip(accs, ls)], axis=0).T
        lse_ref[rows, cols(u)] = jnp.concatenate(
            [jnp.broadcast_to((m + jnp.log2(l)) * LN2, (HEAD_DIM, NQ)) for m, l in zip(ms, ls)], axis=0).T
        del blocks[key]
        sts = sts_next


def _attn_b_call(qg, kg, vg, bm, r):
    b, length, _ = qg.shape
    nsub = min(r, B_SUBSEQ_PER_STEP)
    spec = pl.BlockSpec((None, length, nsub * B_OUT_WIDTH), lambda i, j: (i, 0, j))
    out_sds = jax.ShapeDtypeStruct((b, length, r * B_OUT_WIDTH), F32)
    return pl.pallas_call(
        _attn_b_body,
        grid=(b, r // nsub),
        in_specs=[spec, spec, spec,
                  pl.BlockSpec((B_HEADS_PER_GROUP, BAND, NQ), lambda i, j: (0, 0, 0))],
        out_specs=[spec, spec],
        out_shape=[out_sds, out_sds],
        scratch_shapes=[pltpu.VMEM((nsub, length + 2 * HALO, B_OUT_WIDTH), BF16),
                        pltpu.VMEM((nsub, length + 2 * HALO, B_OUT_WIDTH), BF16)],
        compiler_params=_params(("arbitrary", "arbitrary")),
        name=f"attn_b_r{r}",
    )(qg, kg, vg, bm)


def _t5_bucket_np(rel):
    half = N_BUCKETS // 2
    exact = half // 2
    n = np.abs(rel)
    big = exact + (np.log(np.maximum(n, 1).astype(np.float64) / exact)
                   / math.log(MAX_DISTANCE / exact) * (half - exact)).astype(np.int64)
    big = np.minimum(big, half - 1)
    return np.where(rel > 0, half, 0) + np.where(n < exact, n, big)


def _bias_mask_table(rel_bias, gi, r):
    span = BAND + NQ
    rel = np.arange(span) - (NQ - 1) - HALO
    tab = rel_bias[:, gi * B_HEADS_PER_GROUP:(gi + 1) * B_HEADS_PER_GROUP].astype(F32) * LOG2E
    u = jnp.where(jnp.asarray(np.abs(rel) <= HALO)[:, None],
                  tab[jnp.asarray(_t5_bucket_np(rel * r), jnp.int32)], NEG_INF).T
    shifted = jnp.tile(u, (1, NQ))[:, :NQ * (span - 1)].reshape(-1, NQ, span - 1)
    return jnp.transpose(shifted[:, :, NQ - 1:NQ - 1 + BAND], (0, 2, 1))


def _layer_norm(h, g, b):
    mu = jnp.mean(h, axis=-1, keepdims=True)
    c = h - mu
    var = jnp.mean(c * c, axis=-1, keepdims=True)
    return c * lax.rsqrt(var + LN_EPS) * g + b


def _split_bf16(a):
    hi = a.astype(BF16)
    return hi, (a - hi.astype(F32)).astype(BF16)


def _merge_body(x_ref, ya_ref, o1_ref, o2_ref, o3_ref, l1_ref, l2_ref, l3_ref, gates_ref,
                wa_ref, wbr_ref, wo_ref, g_ref, b_ref, wr2_ref, br_ref, out_ref, bkt_ref, cnt_ref,
                *tok_sc):
    slabs = B_OUT_WIDTH // LANES
    dil = [r for _, r in DILATED_PAIRS]
    b_refs = (o1_ref, o2_ref, o3_ref, l1_ref, l2_ref, l3_ref)
    for ref, sc, r in zip(b_refs, tok_sc, dil + dil):
        if r > 1:
            for j in range(r):
                for c in range(slabs):
                    lanes = slice(j * B_OUT_WIDTH + c * LANES, j * B_OUT_WIDTH + (c + 1) * LANES)
                    sc[c, pl.ds(j, TMG // r, stride=r), :] = ref[:, lanes]

    def token_major(which, rows):
        ref, sc, r = b_refs[which], tok_sc[which], (dil + dil)[which]
        if r == 1:
            return ref[rows, :]
        return jnp.concatenate([sc[c, rows, :] for c in range(slabs)], axis=1)

    subs = [slice(i * TSUB, (i + 1) * TSUB) for i in range(TMG // TSUB)]
    ybs = []
    for rows in subs:
        o1, o2, o3, l1, l2, l3 = (token_major(w, rows) for w in range(6))
        mx = jnp.maximum(jnp.maximum(l1, l2), l3)
        e1, e2, e3 = jnp.exp(l1 - mx), jnp.exp(l2 - mx), jnp.exp(l3 - mx)
        ybs.append(((e1 * o1 + e2 * o2 + e3 * o3) / (e1 + e2 + e3)).astype(BF16))
    ya_ps = [jnp.dot(ya_ref[rows, :], wa_ref[...], preferred_element_type=F32) for rows in subs]
    yb_ps = [jnp.dot(yb, wbr_ref[...], preferred_element_type=F32) for yb in ybs]
    mergeds = [(gates_ref[rows, 0:D_MODEL].astype(F32) * ya_p
                + gates_ref[rows, D_MODEL:2 * D_MODEL].astype(F32) * yb_p).astype(BF16)
               for rows, ya_p, yb_p in zip(subs, ya_ps, yb_ps)]
    hs = [ALPHA * x_ref[rows, :] + jnp.dot(m, wo_ref[...], preferred_element_type=F32)
          for rows, m in zip(subs, mergeds)]
    x1s = [_layer_norm(h, g_ref[...], b_ref[...]) for h in hs]
    for rows, x1 in zip(subs, x1s):
        out_ref[rows, 0:D_MODEL] = x1

    splits = [_split_bf16(x1) for x1 in x1s]
    boths = [jnp.dot(xh, wr2_ref[...], preferred_element_type=F32)
             + jnp.dot(xl, wr2_ref[...], preferred_element_type=F32) for xh, xl in splits]
    logits = jnp.concatenate(
        [(both[:, 0:ROUTE_COLS] + both[:, ROUTE_COLS:] + br_ref[...]).T for both in boths], axis=1)
    col = lambda i: logits[i:i + 1, :]
    lg = [col(i) for i in range(N_GROUPS)]
    gmax = functools.reduce(jnp.maximum, lg)
    top_p = 1.0 / functools.reduce(lambda a, c: a + c, [jnp.exp(v - gmax) for v in lg])
    gsel = jnp.full_like(gmax, N_GROUPS - 1).astype(jnp.int32)
    for i in reversed(range(N_GROUPS - 1)):
        gsel = jnp.where(lg[i] >= gmax, i, gsel)
    sel = []
    for e in range(EXPERTS_PER_GROUP):
        v = col(N_GROUPS + (N_GROUPS - 1) * EXPERTS_PER_GROUP + e)
        for gi in reversed(range(N_GROUPS - 1)):
            v = jnp.where(gsel == gi, col(N_GROUPS + gi * EXPERTS_PER_GROUP + e), v)
        sel.append(v)
    rank = []
    for e in range(EXPERTS_PER_GROUP):
        rk = jnp.zeros_like(gsel)
        for j in range(EXPERTS_PER_GROUP):
            if j != e:
                ahead = (sel[j] > sel[e]) | ((sel[j] == sel[e]) & (j < e))
                rk = rk + ahead.astype(jnp.int32)
        rank.append(rk)
    pick = lambda k: functools.reduce(
        lambda a, c: a + c, [jnp.where(rank[e] == k, sel[e], 0.0) for e in range(EXPERTS_PER_GROUP)])
    t = jnp.exp(pick(1) - pick(0))
    w_first = top_p / (1.0 + t)
    w_second = top_p * t / (1.0 + t)
    chosen = [rank[e] < 2 for e in range(EXPERTS_PER_GROUP)]
    weight = [jnp.where(rank[e] == 0, w_first, w_second) for e in range(EXPERTS_PER_GROUP)]
    pair = jnp.zeros_like(gsel)
    w_lo = jnp.zeros_like(top_p)
    w_hi = jnp.zeros_like(top_p)
    for pi, (a, c) in enumerate(PAIRS):
        hit = chosen[a] & chosen[c]
        pair = jnp.where(hit, pi, pair)
        w_lo = jnp.where(hit, weight[a], w_lo)
        w_hi = jnp.where(hit, weight[c], w_hi)
    bucket = gsel * len(PAIRS) + pair
    bkt_ref[...] = bucket
    hits = lax.broadcasted_iota(jnp.int32, (ROUTE_COLS, TMG), 0) == bucket

    @pl.when(pl.program_id(0) == 0)
    def _():
        cnt_ref[...] = jnp.zeros_like(cnt_ref)

    cnt_ref[...] += jnp.sum(hits.astype(jnp.int32), axis=1, keepdims=True)
    row = lax.broadcasted_iota(jnp.int32, (F32_SUBLANES, TMG), 0)
    head = jnp.where(row == 1, w_lo, jnp.where(row == 2, w_hi, 0.0))
    route_t = jnp.concatenate([head, jnp.zeros((ROUTE_COLS - F32_SUBLANES, TMG), F32)], axis=0)
    out_ref[:, D_MODEL:] = route_t.T


def _merge_call(x, ya, os, ls, gates, wa, wbr, wo, g, b, wr2, br):
    t, d = x.shape
    tok = lambda c: pl.BlockSpec((TMG, c), lambda i: (i, 0))
    const = lambda a: pl.BlockSpec(a.shape, lambda i: (0,) * a.ndim)
    consts = (wa, wbr, wo, g, b, wr2, br)
    dil = [pl.BlockSpec((TMG // r, r * B_OUT_WIDTH), lambda i: (i, 0)) for _, r in DILATED_PAIRS]
    return pl.pallas_call(
        _merge_body,
        grid=(t // TMG,),
        in_specs=[tok(d), tok(A_WIDTH)] + dil + dil + [tok(2 * D_MODEL)]
        + [const(a) for a in consts],
        out_specs=[tok(d + ROUTE_COLS), pl.BlockSpec((1, TMG), lambda i: (0, i)),
                   pl.BlockSpec((ROUTE_COLS, 1), lambda i: (0, 0))],
        out_shape=[jax.ShapeDtypeStruct((t, d + ROUTE_COLS), F32), jax.ShapeDtypeStruct((1, t), jnp.int32),
                   jax.ShapeDtypeStruct((ROUTE_COLS, 1), jnp.int32)],
        scratch_shapes=[pltpu.VMEM((B_OUT_WIDTH // LANES, TMG, LANES), F32)] * (2 * N_DIL_GROUPS),
        compiler_params=_params(("arbitrary",)),
        name="merge_ln1_route",
    )(x, ya, *os, *ls, gates, *consts)


def _sc_permute_rows(src, idx, nrows_out, scatter):
    nrows, d = idx.shape[0], src.shape[1]
    workers = SC_CORES * SC_SUBCORES
    per_worker = nrows // workers
    assert nrows % (workers * SC_ROWS) == 0
    mesh = plsc.VectorSubcoreMesh(core_axis_name="c", subcore_axis_name="s")

    @functools.partial(
        pl.kernel, mesh=mesh, out_type=jax.ShapeDtypeStruct((nrows_out, d), src.dtype),
        scratch_types=[pltpu.VMEM((SC_ROWS,), jnp.int32), pltpu.VMEM((SC_ROWS, d), src.dtype),
                       pltpu.SemaphoreType.DMA])
    def permute(src_hbm, idx_hbm, out_hbm, idx_v, rows_v, sem):
        base = (lax.axis_index("s") * SC_CORES + lax.axis_index("c")) * per_worker

        @pl.loop(0, per_worker, step=SC_ROWS)
        def _(off):
            rows = pl.ds(pl.multiple_of(base + off, SC_ROWS), SC_ROWS)
            pltpu.sync_copy(idx_hbm.at[rows], idx_v)
            if scatter:
                pltpu.sync_copy(src_hbm.at[rows], rows_v)
                pltpu.async_copy(rows_v, out_hbm.at[idx_v], sem).wait()
            else:
                pltpu.async_copy(src_hbm.at[idx_v], rows_v, sem).wait()
                pltpu.sync_copy(rows_v, out_hbm.at[rows])

    return permute(src, idx)


def _moe_tiles_body(nvalid_ref, elo_ref, ehi_ref, x_ref, wil_ref, wih_ref, wol_ref, woh_ref,
                    g_ref, b_ref, o_ref):
    del elo_ref, ehi_ref
    n = nvalid_ref[pl.program_id(0)]

    @pl.when(n == 0)
    def _():
        o_ref[...] = jnp.zeros_like(o_ref)

    @pl.when(n > 0)
    def _():
        xt = x_ref[:, 0:D_MODEL]
        xb = xt.astype(BF16)
        hcats = [jnp.dot(xb, wi_ref[...], preferred_element_type=F32)
                 for wi_ref in (wil_ref, wih_ref)]
        acts = [(h[:, 0:D_EXPERT] * jax.nn.sigmoid(h[:, 0:D_EXPERT]) * h[:, D_EXPERT:]).astype(BF16)
                for h in hcats]
        y_lo, y_hi = (jnp.dot(act, wo_ref[...], preferred_element_type=F32)
                      for act, wo_ref in zip(acts, (wol_ref, woh_ref)))
        moe = (x_ref[:, D_MODEL + 1:D_MODEL + 2] * y_lo + x_ref[:, D_MODEL + 2:D_MODEL + 3] * y_hi)
        o_ref[...] = _layer_norm(ALPHA * xt + moe, g_ref[...], b_ref[...])


def _moe_tiles_call(x_sorted, nvalid, elo, ehi, wei, weo, g, b):
    ntiles = nvalid.shape[0]
    wi_spec = lambda e: pl.BlockSpec((None, D_MODEL, 2 * D_EXPERT), lambda i, nv, lo, hi: (e(lo, hi)[i], 0, 0))
    wo_spec = lambda e: pl.BlockSpec((None, D_EXPERT, D_MODEL), lambda i, nv, lo, hi: (e(lo, hi)[i], 0, 0))
    first = lambda lo, hi: lo
    second = lambda lo, hi: hi
    vec = pl.BlockSpec((1, D_MODEL), lambda i, nv, lo, hi: (0, 0))
    grid_spec = pltpu.PrefetchScalarGridSpec(
        num_scalar_prefetch=3,
        grid=(ntiles,),
        in_specs=[pl.BlockSpec((TMOE, D_MODEL + ROUTE_COLS), lambda i, nv, lo, hi: (i, 0)),
                  wi_spec(first), wi_spec(second), wo_spec(first), wo_spec(second), vec, vec],
        out_specs=pl.BlockSpec((TMOE, D_MODEL), lambda i, nv, lo, hi: (i, 0)),
    )
    return pl.pallas_call(
        _moe_tiles_body,
        grid_spec=grid_spec,
        out_shape=jax.ShapeDtypeStruct((ntiles * TMOE, D_MODEL), F32),
        compiler_params=_params(("arbitrary",)),
        name="moe_ln2",
    )(nvalid, elo, ehi, x_sorted, wei, wei, weo, weo, g, b)


def _route_plan(bucket, counts):
    t = bucket.shape[0]
    ntiles = t // TMOE + N_ROUTE_BUCKETS
    token = jnp.arange(t, dtype=jnp.int32)
    _, order = lax.sort((bucket, token), num_keys=1, is_stable=True)
    row_start = jnp.cumsum(counts) - counts
    tiles_per = (counts + TMOE - 1) // TMOE
    tile_end = jnp.cumsum(tiles_per)
    tile_start = tile_end - tiles_per
    total = tile_end[-1]
    tile = jnp.arange(ntiles, dtype=jnp.int32)
    used = tile < total
    bkt = jnp.sum(jnp.minimum(tile, total - 1)[:, None] >= tile_end[None, :], axis=1, dtype=jnp.int32)
    local = tile - tile_start[bkt]
    nvalid = jnp.where(used, jnp.clip(counts[bkt] - local * TMOE, 0, TMOE), 0).astype(jnp.int32)
    pairs = jnp.asarray(PAIRS, jnp.int32)
    grp = bkt // len(PAIRS)
    elo = grp * EXPERTS_PER_GROUP + pairs[bkt % len(PAIRS), 0]
    ehi = grp * EXPERTS_PER_GROUP + pairs[bkt % len(PAIRS), 1]
    shift = tile_start * TMOE - row_start
    step = shift - jnp.concatenate([jnp.zeros((1,), jnp.int32), shift[:-1]])
    padded_row = token + jnp.sum(jnp.where(token[None, :] >= row_start[:, None], step[:, None], 0), axis=0)
    _, pos = lax.sort((order, padded_row), num_keys=1)
    return nvalid, elo.astype(jnp.int32), ehi.astype(jnp.int32), pos


def _prepare(w_in, b_gate, q_gain, k_gain, rel_bias, w_branch_a, w_branch_b, w_out, ln1_g, ln1_b,
             w_route_group, b_route_group, w_route_expert, b_route_expert, w_expert_in,
             w_expert_out, ln2_g, ln2_b, seq):
    row = lambda v: v.reshape(1, -1).astype(F32)
    w = w_in[0]
    wr = jnp.concatenate([w_route_group[0]]
                         + [w_route_expert[0, gi] for gi in range(N_GROUPS)], axis=1)
    wr = jnp.pad(wr.astype(F32), ((0, 0), (0, ROUTE_COLS - wr.shape[1])))
    br = jnp.concatenate([b_route_group[0], b_route_expert[0].reshape(-1)])
    br = jnp.pad(br.astype(F32), (0, ROUTE_COLS - br.shape[0])).reshape(1, ROUTE_COLS)
    wr2 = jnp.concatenate(_split_bf16(wr), axis=1)
    scale = HEAD_DIM ** -0.5
    wb = w[:, DT_ROWS:]
    wb = jnp.concatenate([wb[:, 0:B_WIDTH] * (scale * LOG2E), wb[:, B_WIDTH:]], axis=1)
    return dict(
        wdt=w[:, 0:DT_ROWS].T.astype(BF16), wb=wb.astype(BF16), bg=row(b_gate[0]),
        rope_q=_rope_tables(seq, q_gain[0], scale * LOG2E), rope_k=_rope_tables(seq, k_gain[0], 1.0),
        score_bound=_score_bound(q_gain[0], k_gain[0]),
        bm=[_bias_mask_table(rel_bias, gi, r) for gi, (_, r) in enumerate(DILATED_PAIRS)],
        wa=w_branch_a[0].astype(BF16), wbr=w_branch_b[0].astype(BF16), wo=w_out[0].astype(BF16),
        ln1=(row(ln1_g[0]), row(ln1_b[0])), wr2=wr2, br=br,
        wei=w_expert_in[0].astype(BF16), weo=w_expert_out[0].astype(BF16),
        ln2=(row(ln2_g[0]), row(ln2_b[0])),
    )


def _trunk(x, p, after):
    b, s, d = x.shape
    qt, k, vt, *qkv_b, gates = _inproj_call(x, p["wdt"], p["wb"], p["bg"], *p["rope_q"], *p["rope_k"],
                                            after)
    ya = _attn_a_call(*p["score_bound"], qt, k, vt)
    os, ls = [], []
    for gi, (_, r) in enumerate(DILATED_PAIRS):
        qg, kg, vg = (qkv_b[i * N_DIL_GROUPS + gi] for i in range(3))
        o, lse = _attn_b_call(qg, kg, vg, p["bm"][gi], r)
        os.append(o.reshape(b * s // r, r * B_OUT_WIDTH))
        ls.append(lse.reshape(b * s // r, r * B_OUT_WIDTH))
    return _merge_call(x.reshape(b * s, d), ya.reshape(b * s, A_WIDTH), os, ls,
                       gates.reshape(b * s, 2 * D_MODEL), p["wa"], p["wbr"], p["wo"], *p["ln1"],
                       p["wr2"], p["br"])


def _sorted_rows(x1ext, plan):
    ntiles = plan[0].shape[0]
    return _sc_permute_rows(x1ext, plan[3], ntiles * TMOE, scatter=True)


def _experts(x_sorted, plan, p, shape):
    nvalid, elo, ehi, pos = plan
    y_sorted = _moe_tiles_call(x_sorted, nvalid, elo, ehi, p["wei"], p["weo"], *p["ln2"])
    return _sc_permute_rows(y_sorted, pos, pos.shape[0], scatter=False).reshape(shape)


def kernel(x_prompt, x_sample, w_in, b_gate, q_gain, k_gain, rel_bias, w_branch_a, w_branch_b, w_out,
           ln1_g, ln1_b, w_route_group, b_route_group, w_route_expert, b_route_expert, w_expert_in,
           w_expert_out, ln2_g, ln2_b):
    assert x_prompt.shape[1] == x_sample.shape[1]
    p = _prepare(w_in, b_gate, q_gain, k_gain, rel_bias, w_branch_a, w_branch_b, w_out, ln1_g, ln1_b,
                 w_route_group, b_route_group, w_route_expert, b_route_expert, w_expert_in,
                 w_expert_out, ln2_g, ln2_b, x_prompt.shape[1])
    plan = lambda bucket, counts: _route_plan(bucket.reshape(-1), counts[:N_ROUTE_BUCKETS, 0])
    x1_sample, *route_sample = _trunk(x_sample, p, p["br"])
    plan_sample = plan(*route_sample)
    sorted_sample = _sorted_rows(x1_sample, plan_sample)
    x1_prompt, *route_prompt = _trunk(x_prompt, p, plan_sample[3])
    plan_prompt = plan(*route_prompt)
    sorted_prompt = _sorted_rows(x1_prompt, plan_prompt)
    y_sample = _experts(sorted_sample, plan_sample, p, x_sample.shape)
    y_prompt = _experts(sorted_prompt, plan_prompt, p, x_prompt.shape)
    return (y_prompt, y_sample)
```

```python
import functools
import math

import jax
import jax.numpy as jnp
import numpy as np
from jax import lax
from jax.experimental import pallas as pl
from jax.experimental.pallas import tpu as pltpu
from jax.experimental.pallas import tpu_sc as plsc

F32 = jnp.float32
BF16 = jnp.bfloat16

D_MODEL = 1024
HEAD_DIM = 64
A_Q_HEADS = 8
A_KV_HEADS = 2
A_GROUP = A_Q_HEADS // A_KV_HEADS
GRID_W = 64
ROPE_AXIS_DIM = HEAD_DIM // 2
ROPE_THETA = 10000.0
DILATED_PAIRS = ((128, 1), (512, 4), (2048, 16))
N_DIL_GROUPS = len(DILATED_PAIRS)
B_HEADS_PER_GROUP = 4
N_BUCKETS = 32
MAX_DISTANCE = 1024
A_WIDTH = A_Q_HEADS * HEAD_DIM
A_KV_WIDTH = A_KV_HEADS * HEAD_DIM
B_WIDTH = B_HEADS_PER_GROUP * N_DIL_GROUPS * HEAD_DIM
B_OUT_WIDTH = B_HEADS_PER_GROUP * HEAD_DIM
DT_ROWS = A_WIDTH + 2 * A_KV_WIDTH
N_GROUPS = 4
EXPERTS_PER_GROUP = 4
N_EXPERTS = N_GROUPS * EXPERTS_PER_GROUP
D_EXPERT = 512
PAIRS = ((0, 1), (0, 2), (0, 3), (1, 2), (1, 3), (2, 3))
N_ROUTE_BUCKETS = N_GROUPS * len(PAIRS)
ALPHA = 2.0 ** 0.25
RMS_EPS = 1e-6
LN_EPS = 1e-5
NEG_INF = -1e30

LANES = 128
TM = 512
TMG = 512
TSUB = 128
NQ = 256
A_HEADS_PER_PHASE = 2
A_TK = 256
A_LOOKAHEAD = 2
BF16_SUBLANES = 16
F32_SUBLANES = 8
VT_ROWS = HEAD_DIM + BF16_SUBLANES
LOG2E = 1.4426950408889634
LN2 = 0.6931471805599453
MAX_SAFE_SHIFT = 40.0
HALO = 64
BAND = NQ + 2 * HALO
B_SUBSEQ_PER_STEP = 4
TMOE = 256
MOE_HCHUNK = 256
ROUTE_COLS = LANES
SC_CORES = 2
SC_SUBCORES = 16
SC_ROWS = 32
VMEM_LIMIT = 56 * 1024 * 1024


def _params(sem):
    return pltpu.CompilerParams(dimension_semantics=sem, vmem_limit_bytes=VMEM_LIMIT)


def _swap16(x):
    return jnp.concatenate([x[16:32], x[0:16], x[48:64], x[32:48]], axis=0)


def _norm_rope(xh, ta, tb):
    r = lax.rsqrt(jnp.mean(xh * xh, axis=0, keepdims=True) + RMS_EPS)
    return (xh * ta + _swap16(xh) * tb) * r


def _inproj_body(x_ref, wdt_ref, wb_ref, bg_ref, taq_ref, tbq_ref, tak_ref, tbk_ref, after_ref,
                 qt_ref, k_ref, vt_ref, *rest):
    del after_ref
    b_refs, gates_ref, tmp_sc = rest[:3 * N_DIL_GROUPS], rest[3 * N_DIL_GROUPS], rest[-1]
    xb = x_ref[...].astype(BF16)

    def project(lo, hi):
        return jnp.dot(xb, wb_ref[:, lo:hi], preferred_element_type=F32)

    def rope_queries(dt):
        taq, tbq = taq_ref[...], tbq_ref[...]
        for h in range(A_Q_HEADS):
            sl = slice(h * HEAD_DIM, (h + 1) * HEAD_DIM)
            qt_ref[sl, :] = _norm_rope(dt[sl], taq, tbq).astype(BF16)

    def rope_keys_values(dt):
        tak, tbk = tak_ref[...], tbk_ref[...]
        kt = jnp.concatenate(
            [_norm_rope(dt[A_WIDTH + h * HEAD_DIM:A_WIDTH + (h + 1) * HEAD_DIM], tak, tbk)
             for h in range(A_KV_HEADS)], axis=0)
        k_ref[...] = kt.T.astype(BF16)
        for h in range(A_KV_HEADS):
            v0 = A_WIDTH + A_KV_WIDTH + h * HEAD_DIM
            vt_ref[h, 0:HEAD_DIM, :] = dt[v0:v0 + HEAD_DIM].astype(BF16)
            vt_ref[h, HEAD_DIM:, :] = jnp.ones((VT_ROWS - HEAD_DIM, TM), BF16)

    def deinterleave(i, res):
        for c in range(B_WIDTH // LANES):
            tmp_sc[i, c] = res[:, c * LANES:(c + 1) * LANES]
        for gi, (_, r) in enumerate(DILATED_PAIRS):
            ref = b_refs[i * N_DIL_GROUPS + gi]
            for j in range(r):
                for half in range(B_OUT_WIDTH // LANES):
                    c = gi * (B_OUT_WIDTH // LANES) + half
                    lanes = slice(j * B_OUT_WIDTH + half * LANES, j * B_OUT_WIDTH + (half + 1) * LANES)
                    ref[:, lanes] = tmp_sc[i, c, pl.ds(j, TM // r, stride=r), :].astype(BF16)

    def gate(i, pre):
        cols = slice(i * D_MODEL, (i + 1) * D_MODEL)
        gates_ref[:, cols] = jax.nn.sigmoid(pre + bg_ref[:, cols]).astype(BF16)

    g0 = 3 * B_WIDTH
    dt = lax.dot_general(wdt_ref[...], xb, (((1,), (1,)), ((), ())),
                         preferred_element_type=F32)
    res_q = project(0, B_WIDTH)
    rope_queries(dt)
    res_k = project(B_WIDTH, 2 * B_WIDTH)
    rope_keys_values(dt)
    deinterleave(0, res_q)
    res_v = project(2 * B_WIDTH, g0)
    deinterleave(1, res_k)
    pre_a = project(g0, g0 + D_MODEL)
    deinterleave(2, res_v)
    pre_b = project(g0 + D_MODEL, g0 + 2 * D_MODEL)
    gate(0, pre_a)
    gate(1, pre_b)


def _inproj_call(x, wdt, wb, bg, taq, tbq, tak, tbk, after):
    b, s, d = x.shape
    nt = s // TM
    wcols = wb.shape[1]
    tab = pl.BlockSpec((HEAD_DIM, TM), lambda p, i: (0, p))
    const = lambda shape: pl.BlockSpec(shape, lambda p, i: (0,) * len(shape))
    tok = lambda c: pl.BlockSpec((None, TM, c), lambda p, i: (i, p, 0))
    dil = [r for _ in range(3) for _, r in DILATED_PAIRS]
    return pl.pallas_call(
        _inproj_body,
        grid=(nt, b),
        in_specs=[tok(d), const((DT_ROWS, d)), const((d, wcols)), const((1, 2 * D_MODEL)),
                  tab, tab, tab, tab, pl.BlockSpec(memory_space=pl.ANY)],
        out_specs=[
            pl.BlockSpec((None, A_WIDTH, TM), lambda p, i: (i, 0, p)),
            tok(A_KV_WIDTH),
            pl.BlockSpec((None, None, A_KV_HEADS, VT_ROWS, TM), lambda p, i: (i, p, 0, 0, 0)),
        ] + [pl.BlockSpec((None, TM // r, r * B_OUT_WIDTH), lambda p, i: (i, p, 0)) for r in dil]
        + [tok(2 * D_MODEL)],
        out_shape=[
            jax.ShapeDtypeStruct((b, A_WIDTH, s), BF16),
            jax.ShapeDtypeStruct((b, s, A_KV_WIDTH), BF16),
            jax.ShapeDtypeStruct((b, nt, A_KV_HEADS, VT_ROWS, TM), BF16),
        ] + [jax.ShapeDtypeStruct((b, s // r, r * B_OUT_WIDTH), BF16) for r in dil]
        + [jax.ShapeDtypeStruct((b, s, 2 * D_MODEL), BF16)],
        scratch_shapes=[pltpu.VMEM((3, B_WIDTH // LANES, TM, LANES), F32)],
        compiler_params=_params(("arbitrary", "arbitrary")),
        name="inproj",
    )(x, wdt, wb, bg, taq, tbq, tak, tbk, after)


def _rope_tables(s, gain, scale):
    t = np.arange(s)
    inv = ROPE_THETA ** (-np.arange(0, ROPE_AXIS_DIM, 2, dtype=np.float64) / ROPE_AXIS_DIM)
    pos = np.stack([t // GRID_W, t // GRID_W, t % GRID_W, t % GRID_W], axis=0)
    ang = (pos[:, None, :] * inv[None, :, None]).reshape(HEAD_DIM, s)
    sign = np.repeat(np.array([-1.0, 1.0, -1.0, 1.0]), 16)[:, None]
    partner = np.arange(HEAD_DIM) ^ 16
    cos = jnp.asarray(np.cos(ang), F32)
    sin = jnp.asarray(sign * np.sin(ang), F32)
    g = gain.astype(F32) * scale
    return g[:, None] * cos, g[partner][:, None] * sin


def _attn_a_body(fast_ref, kbound_ref, qt_ref, k_ref, vt_ref, o_ref, rhs_sc, acc_sc, ot_sc):
    nchunks = vt_ref.shape[0]
    tk = vt_ref.shape[3]
    heads = range(A_Q_HEADS)
    zeros = jnp.zeros((HEAD_DIM, NQ), BF16)
    for h in heads:
        qh = qt_ref[h * HEAD_DIM:(h + 1) * HEAD_DIM, :]
        rhs_sc[h] = jnp.concatenate([qh, zeros] if h < A_GROUP else [zeros, qh], axis=0)

    @pl.when(fast_ref[0] == 1)
    def _():
        shift = []
        for h in heads:
            qh = qt_ref[h * HEAD_DIM:(h + 1) * HEAD_DIM, :].astype(F32)
            shift.append(jnp.sqrt(jnp.sum(qh * qh, axis=0, keepdims=True)) * kbound_ref[0])

        def scores(c, h):
            return jnp.dot(k_ref[c * A_TK:(c + 1) * A_TK, :], rhs_sc[h], preferred_element_type=F32)

        def values(c, kv):
            per = tk // A_TK
            return vt_ref[c // per, kv][:, (c % per) * A_TK:(c % per + 1) * A_TK]

        groups = [(kv, c, part) for kv in range(A_KV_HEADS) for c in range(nchunks * tk // A_TK)
                  for part in range(A_GROUP // A_HEADS_PER_PHASE)]
        members = lambda kv, part: range(kv * A_GROUP + part * A_HEADS_PER_PHASE,
                                         kv * A_GROUP + (part + 1) * A_HEADS_PER_PHASE)
        issue = lambda group: [scores(group[1], h) for h in members(group[0], group[2])]
        accs = [None] * A_Q_HEADS
        pending = [issue(group) for group in groups[:A_LOOKAHEAD]]
        for n, (kv, c, part) in enumerate(groups):
            if n + A_LOOKAHEAD < len(groups):
                pending.append(issue(groups[n + A_LOOKAHEAD]))
            sts = pending.pop(0)
            ps = [jnp.exp2(st - shift[h]).astype(BF16) for st, h in zip(sts, members(kv, part))]
            pvs = [jnp.dot(values(c, kv), p, preferred_element_type=F32) for p in ps]
            for pv, h in zip(pvs, members(kv, part)):
                accs[h] = pv if accs[h] is None else accs[h] + pv
        for h in heads:
            acc_sc[h] = accs[h]

    @pl.when(fast_ref[0] == 0)
    def _():
        acc_sc[...] = jnp.zeros_like(acc_sc)

        def chunk(c, ms):
            kc = k_ref[pl.ds(pl.multiple_of(c * tk, tk), tk), :]
            new_ms = []
            for h in heads:
                st = jnp.dot(kc, rhs_sc[h], preferred_element_type=F32)
                m_new = jnp.maximum(ms[h], jnp.max(st, axis=0, keepdims=True))
                alpha = jnp.exp2(ms[h] - m_new)
                p = jnp.exp2(st - m_new).astype(BF16)
                acc_sc[h] = alpha * acc_sc[h] + jnp.dot(vt_ref[c, h // A_GROUP], p,
                                                         preferred_element_type=F32)
                new_ms.append(m_new)
            return tuple(new_ms)

        lax.fori_loop(0, nchunks, chunk, tuple(jnp.full((1, NQ), NEG_INF, F32) for _ in heads))

    for h in heads:
        ot_sc[h * HEAD_DIM:(h + 1) * HEAD_DIM, :] = (
            acc_sc[h, 0:HEAD_DIM, :] / acc_sc[h, HEAD_DIM:HEAD_DIM + 1, :])
    o_ref[...] = ot_sc[...].T.astype(BF16)


def _attn_a_call(fast, kbound, qt, k, vt):
    b, _, s = qt.shape
    nchunks, tk = vt.shape[1], vt.shape[4]
    grid_spec = pltpu.PrefetchScalarGridSpec(
        num_scalar_prefetch=2,
        grid=(b, s // NQ),
        in_specs=[
            pl.BlockSpec((None, A_WIDTH, NQ), lambda i, j, *_: (i, 0, j)),
            pl.BlockSpec((None, s, A_KV_WIDTH), lambda i, j, *_: (i, 0, 0)),
            pl.BlockSpec((None, nchunks, A_KV_HEADS, VT_ROWS, tk), lambda i, j, *_: (i, 0, 0, 0, 0)),
        ],
        out_specs=pl.BlockSpec((None, NQ, A_WIDTH), lambda i, j, *_: (i, j, 0)),
        scratch_shapes=[pltpu.VMEM((A_Q_HEADS, A_KV_WIDTH, NQ), BF16),
                        pltpu.VMEM((A_Q_HEADS, VT_ROWS, NQ), F32),
                        pltpu.VMEM((A_WIDTH, NQ), F32)],
    )
    return pl.pallas_call(
        _attn_a_body,
        grid_spec=grid_spec,
        out_shape=jax.ShapeDtypeStruct((b, s, A_WIDTH), BF16),
        compiler_params=_params(("arbitrary", "arbitrary")),
        name="attn_a",
    )(fast, kbound, qt, k, vt)


def _score_bound(q_gain, k_gain):
    kb = math.sqrt(HEAD_DIM) * jnp.max(jnp.abs(k_gain.astype(F32)))
    qb = math.sqrt(HEAD_DIM) * jnp.max(jnp.abs(q_gain.astype(F32))) * (HEAD_DIM ** -0.5) * LOG2E
    fast = (qb * kb <= MAX_SAFE_SHIFT).astype(jnp.int32)
    return fast.reshape(1), kb.reshape(1)


def _attn_b_body(q_ref, k_ref, v_ref, bm_ref, o_ref, lse_ref, kp_sc, vp_sc):
    length = q_ref.shape[0]
    nsub = q_ref.shape[1] // B_OUT_WIDTH
    cols = lambda u: slice(u * B_OUT_WIDTH, (u + 1) * B_OUT_WIDTH)
    zeros = jnp.zeros((HALO, B_OUT_WIDTH), BF16)
    for sc, ref in ((kp_sc, k_ref), (vp_sc, v_ref)):
        for u in range(nsub):
            sc[u, 0:HALO, :] = zeros
            sc[u, HALO:HALO + length, :] = ref[:, cols(u)]
            sc[u, HALO + length:, :] = zeros
    row_head = lax.broadcasted_iota(jnp.int32, (B_OUT_WIDTH, NQ), 0) // HEAD_DIM
    ones = jnp.ones((VT_ROWS - HEAD_DIM, BAND), BF16)
    blocks = {}

    def block(key):
        if key not in blocks:
            u, i = key
            q0 = i * NQ
            qt = q_ref[q0:q0 + NQ, cols(u)].astype(F32).T.astype(BF16)
            kband = kp_sc[u, q0:q0 + BAND, :]
            vbt = vp_sc[u, q0:q0 + BAND, :].astype(F32).T.astype(BF16)
            pen = None
            if q0 - HALO < 0 or q0 - HALO + BAND > length:
                key_pos = lax.broadcasted_iota(jnp.int32, (BAND, NQ), 0) + (q0 - HALO)
                pen = jnp.where((key_pos >= 0) & (key_pos < length), 0.0, NEG_INF)
            blocks[key] = (qt, kband, vbt, pen)
        return blocks[key]

    def scores(key, h):
        qt, kband, _, _ = block(key)
        rhs = jnp.where(row_head == h, qt, jnp.zeros((), BF16))
        return jnp.dot(kband, rhs, preferred_element_type=F32)

    heads = range(B_HEADS_PER_GROUP)
    keys = [(u, i) for u in range(nsub) for i in range(length // NQ)]
    sts = [scores(keys[0], h) for h in heads]
    for n, key in enumerate(keys):
        sts_next = [scores(keys[n + 1], h) for h in heads] if n + 1 < len(keys) else None
        _, _, vbt, pen = block(key)
        sts = [sts[h] + bm_ref[h] for h in heads]
        if pen is not None:
            sts = [st + pen for st in sts]
        ms = [jnp.max(st, axis=0, keepdims=True) for st in sts]
        ps = [jnp.exp2(st - m).astype(BF16) for st, m in zip(sts, ms)]
        accs = [jnp.dot(jnp.concatenate([vbt[h * HEAD_DIM:(h + 1) * HEAD_DIM], ones], axis=0), ps[h],
                        preferred_element_type=F32) for h in heads]
        ls = [acc[HEAD_DIM:HEAD_DIM + 1] for acc in accs]
        u, i = key
        rows = slice(i * NQ, (i + 1) * NQ)
        o_ref[rows, cols(u)] = jnp.concatenate([acc[0:HEAD_DIM] / l for acc, l in zip(accs, ls)], axis=0).T
        lse_ref[rows, cols(u)] = jnp.concatenate(
            [jnp.broadcast_to((m + jnp.log2(l)) * LN2, (HEAD_DIM, NQ)) for m, l in zip(ms, ls)], axis=0).T
        del blocks[key]
        sts = sts_next


def _attn_b_call(qg, kg, vg, bm, r):
    b, length, _ = qg.shape
    nsub = min(r, B_SUBSEQ_PER_STEP)
    spec = pl.BlockSpec((None, length, nsub * B_OUT_WIDTH), lambda i, j: (i, 0, j))
    out_sds = jax.ShapeDtypeStruct((b, length, r * B_OUT_WIDTH), F32)
    return pl.pallas_call(
        _attn_b_body,
        grid=(b, r // nsub),
        in_specs=[spec, spec, spec,
                  pl.BlockSpec((B_HEADS_PER_GROUP, BAND, NQ), lambda i, j: (0, 0, 0))],
        out_specs=[spec, spec],
        out_shape=[out_sds, out_sds],
        scratch_shapes=[pltpu.VMEM((nsub, length + 2 * HALO, B_OUT_WIDTH), BF16),
                        pltpu.VMEM((nsub, length + 2 * HALO, B_OUT_WIDTH), BF16)],
        compiler_params=_params(("arbitrary", "arbitrary")),
        name=f"attn_b_r{r}",
    )(qg, kg, vg, bm)


def _t5_bucket_np(rel):
    half = N_BUCKETS // 2
    exact = half // 2
    n = np.abs(rel)
    big = exact + (np.log(np.maximum(n, 1).astype(np.float64) / exact)
                   / math.log(MAX_DISTANCE / exact) * (half - exact)).astype(np.int64)
    big = np.minimum(big, half - 1)
    return np.where(rel > 0, half, 0) + np.where(n < exact, n, big)


def _bias_mask_table(rel_bias, gi, r):
    span = BAND + NQ
    rel = np.arange(span) - (NQ - 1) - HALO
    tab = rel_bias[:, gi * B_HEADS_PER_GROUP:(gi + 1) * B_HEADS_PER_GROUP].astype(F32) * LOG2E
    u = jnp.where(jnp.asarray(np.abs(rel) <= HALO)[:, None],
                  tab[jnp.asarray(_t5_bucket_np(rel * r), jnp.int32)], NEG_INF).T
    shifted = jnp.tile(u, (1, NQ))[:, :NQ * (span - 1)].reshape(-1, NQ, span - 1)
    return jnp.transpose(shifted[:, :, NQ - 1:NQ - 1 + BAND], (0, 2, 1))


def _layer_norm(h, g, b):
    mu = jnp.mean(h, axis=-1, keepdims=True)
    c = h - mu
    var = jnp.mean(c * c, axis=-1, keepdims=True)
    return c * lax.rsqrt(var + LN_EPS) * g + b


def _split_bf16(a):
    hi = a.astype(BF16)
    return hi, (a - hi.astype(F32)).astype(BF16)


def _merge_body(x_ref, ya_ref, o1_ref, o2_ref, o3_ref, l1_ref, l2_ref, l3_ref, gates_ref,
                wa_ref, wbr_ref, wo_ref, g_ref, b_ref, wr2_ref, br_ref, out_ref, bkt_ref, cnt_ref,
                *tok_sc):
    slabs = B_OUT_WIDTH // LANES
    dil = [r for _, r in DILATED_PAIRS]
    b_refs = (o1_ref, o2_ref, o3_ref, l1_ref, l2_ref, l3_ref)
    for ref, sc, r in zip(b_refs, tok_sc, dil + dil):
        if r > 1:
            for j in range(r):
                for c in range(slabs):
                    lanes = slice(j * B_OUT_WIDTH + c * LANES, j * B_OUT_WIDTH + (c + 1) * LANES)
                    sc[c, pl.ds(j, TMG // r, stride=r), :] = ref[:, lanes]

    def token_major(which, rows):
        ref, sc, r = b_refs[which], tok_sc[which], (dil + dil)[which]
        if r == 1:
            return ref[rows, :]
        return jnp.concatenate([sc[c, rows, :] for c in range(slabs)], axis=1)

    subs = [slice(i * TSUB, (i + 1) * TSUB) for i in range(TMG // TSUB)]
    ybs = []
    for rows in subs:
        o1, o2, o3, l1, l2, l3 = (token_major(w, rows) for w in range(6))
        mx = jnp.maximum(jnp.maximum(l1, l2), l3)
        e1, e2, e3 = jnp.exp(l1 - mx), jnp.exp(l2 - mx), jnp.exp(l3 - mx)
        ybs.append(((e1 * o1 + e2 * o2 + e3 * o3) / (e1 + e2 + e3)).astype(BF16))
    ya_ps = [jnp.dot(ya_ref[rows, :], wa_ref[...], preferred_element_type=F32) for rows in subs]
    yb_ps = [jnp.dot(yb, wbr_ref[...], preferred_element_type=F32) for yb in ybs]
    mergeds = [(gates_ref[rows, 0:D_MODEL].astype(F32) * ya_p
                + gates_ref[rows, D_MODEL:2 * D_MODEL].astype(F32) * yb_p).astype(BF16)
               for rows, ya_p, yb_p in zip(subs, ya_ps, yb_ps)]
    hs = [ALPHA * x_ref[rows, :] + jnp.dot(m, wo_ref[...], preferred_element_type=F32)
          for rows, m in zip(subs, mergeds)]
    x1s = [_layer_norm(h, g_ref[...], b_ref[...]) for h in hs]
    for rows, x1 in zip(subs, x1s):
        out_ref[rows, 0:D_MODEL] = x1

    splits = [_split_bf16(x1) for x1 in x1s]
    boths = [jnp.dot(xh, wr2_ref[...], preferred_element_type=F32)
             + jnp.dot(xl, wr2_ref[...], preferred_element_type=F32) for xh, xl in splits]
    logits = jnp.concatenate(
        [(both[:, 0:ROUTE_COLS] + both[:, ROUTE_COLS:] + br_ref[...]).T for both in boths], axis=1)
    col = lambda i: logits[i:i + 1, :]
    lg = [col(i) for i in range(N_GROUPS)]
    gmax = functools.reduce(jnp.maximum, lg)
    top_p = 1.0 / functools.reduce(lambda a, c: a + c, [jnp.exp(v - gmax) for v in lg])
    gsel = jnp.full_like(gmax, N_GROUPS - 1).astype(jnp.int32)
    for i in reversed(range(N_GROUPS - 1)):
        gsel = jnp.where(lg[i] >= gmax, i, gsel)
    sel = []
    for e in range(EXPERTS_PER_GROUP):
        v = col(N_GROUPS + (N_GROUPS - 1) * EXPERTS_PER_GROUP + e)
        for gi in reversed(range(N_GROUPS - 1)):
            v = jnp.where(gsel == gi, col(N_GROUPS + gi * EXPERTS_PER_GROUP + e), v)
        sel.append(v)
    rank = []
    for e in range(EXPERTS_PER_GROUP):
        rk = jnp.zeros_like(gsel)
        for j in range(EXPERTS_PER_GROUP):
            if j != e:
                ahead = (sel[j] > sel[e]) | ((sel[j] == sel[e]) & (j < e))
                rk = rk + ahead.astype(jnp.int32)
        rank.append(rk)
    pick = lambda k: functools.reduce(
        lambda a, c: a + c, [jnp.where(rank[e] == k, sel[e], 0.0) for e in range(EXPERTS_PER_GROUP)])
    t = jnp.exp(pick(1) - pick(0))
    w_first = top_p / (1.0 + t)
    w_second = top_p * t / (1.0 + t)
    chosen = [rank[e] < 2 for e in range(EXPERTS_PER_GROUP)]
    weight = [jnp.where(rank[e] == 0, w_first, w_second) for e in range(EXPERTS_PER_GROUP)]
    pair = jnp.zeros_like(gsel)
    w_lo = jnp.zeros_like(top_p)
    w_hi = jnp.zeros_like(top_p)
    for pi, (a, c) in enumerate(PAIRS):
        hit = chosen[a] & chosen[c]
        pair = jnp.where(hit, pi, pair)
        w_lo = jnp.where(hit, weight[a], w_lo)
        w_hi = jnp.where(hit, weight[c], w_hi)
    bucket = gsel * len(PAIRS) + pair
    bkt_ref[...] = bucket
    hits = lax.broadcasted_iota(jnp.int32, (ROUTE_COLS, TMG), 0) == bucket

    @pl.when(pl.program_id(0) == 0)
    def _():
        cnt_ref[...] = jnp.zeros_like(cnt_ref)

    cnt_ref[...] += jnp.sum(hits.astype(jnp.int32), axis=1, keepdims=True)
    row = lax.broadcasted_iota(jnp.int32, (F32_SUBLANES, TMG), 0)
    head = jnp.where(row == 1, w_lo, jnp.where(row == 2, w_hi, 0.0))
    route_t = jnp.concatenate([head, jnp.zeros((ROUTE_COLS - F32_SUBLANES, TMG), F32)], axis=0)
    out_ref[:, D_MODEL:] = route_t.T


def _merge_call(x, ya, os, ls, gates, wa, wbr, wo, g, b, wr2, br):
    t, d = x.shape
    tok = lambda c: pl.BlockSpec((TMG, c), lambda i: (i, 0))
    const = lambda a: pl.BlockSpec(a.shape, lambda i: (0,) * a.ndim)
    consts = (wa, wbr, wo, g, b, wr2, br)
    dil = [pl.BlockSpec((TMG // r, r * B_OUT_WIDTH), lambda i: (i, 0)) for _, r in DILATED_PAIRS]
    return pl.pallas_call(
        _merge_body,
        grid=(t // TMG,),
        in_specs=[tok(d), tok(A_WIDTH)] + dil + dil + [tok(2 * D_MODEL)]
        + [const(a) for a in consts],
        out_specs=[tok(d + ROUTE_COLS), pl.BlockSpec((1, TMG), lambda i: (0, i)),
                   pl.BlockSpec((ROUTE_COLS, 1), lambda i: (0, 0))],
        out_shape=[jax.ShapeDtypeStruct((t, d + ROUTE_COLS), F32), jax.ShapeDtypeStruct((1, t), jnp.int32),
                   jax.ShapeDtypeStruct((ROUTE_COLS, 1), jnp.int32)],
        scratch_shapes=[pltpu.VMEM((B_OUT_WIDTH // LANES, TMG, LANES), F32)] * (2 * N_DIL_GROUPS),
        compiler_params=_params(("arbitrary",)),
        name="merge_ln1_route",
    )(x, ya, *os, *ls, gates, *consts)


def _sc_permute_rows(src, idx, nrows_out, scatter):
    nrows, d = idx.shape[0], src.shape[1]
    workers = SC_CORES * SC_SUBCORES
    per_worker = nrows // workers
    assert nrows % (workers * SC_ROWS) == 0
    mesh = plsc.VectorSubcoreMesh(core_axis_name="c", subcore_axis_name="s")

    @functools.partial(
        pl.kernel, mesh=mesh, out_type=jax.ShapeDtypeStruct((nrows_out, d), src.dtype),
        scratch_types=[pltpu.VMEM((SC_ROWS,), jnp.int32), pltpu.VMEM((SC_ROWS, d), src.dtype),
                       pltpu.SemaphoreType.DMA])
    def permute(src_hbm, idx_hbm, out_hbm, idx_v, rows_v, sem):
        base = (lax.axis_index("s") * SC_CORES + lax.axis_index("c")) * per_worker

        @pl.loop(0, per_worker, step=SC_ROWS)
        def _(off):
            rows = pl.ds(pl.multiple_of(base + off, SC_ROWS), SC_ROWS)
            pltpu.sync_copy(idx_hbm.at[rows], idx_v)
            if scatter:
                pltpu.sync_copy(src_hbm.at[rows], rows_v)
                pltpu.async_copy(rows_v, out_hbm.at[idx_v], sem).wait()
            else:
                pltpu.async_copy(src_hbm.at[idx_v], rows_v, sem).wait()
                pltpu.sync_copy(rows_v, out_hbm.at[rows])

    return permute(src, idx)


def _moe_tiles_body(nvalid_ref, elo_ref, ehi_ref, x_ref, wil_ref, wih_ref, wol_ref, woh_ref,
                    g_ref, b_ref, o_ref):
    del elo_ref, ehi_ref
    n = nvalid_ref[pl.program_id(0)]

    @pl.when(n == 0)
    def _():
        o_ref[...] = jnp.zeros_like(o_ref)

    @pl.when(n > 0)
    def _():
        xt = x_ref[:, 0:D_MODEL]
        xb = xt.astype(BF16)
        wis, wos = (wil_ref, wih_ref), (wol_ref, woh_ref)
        chunks = [(e, c) for e in range(2) for c in range(D_EXPERT // MOE_HCHUNK)]

        def up(chunk):
            e, c = chunk
            lo = c * MOE_HCHUNK
            gate = jnp.dot(xb, wis[e][:, lo:lo + MOE_HCHUNK], preferred_element_type=F32)
            lin = jnp.dot(xb, wis[e][:, D_EXPERT + lo:D_EXPERT + lo + MOE_HCHUNK], preferred_element_type=F32)
            return gate, lin

        ys = [None, None]
        pending = up(chunks[0])
        for n, (e, c) in enumerate(chunks):
            gate, lin = pending
            if n + 1 < len(chunks):
                pending = up(chunks[n + 1])
            act = (gate * jax.nn.sigmoid(gate) * lin).astype(BF16)
            part = jnp.dot(act, wos[e][c * MOE_HCHUNK:(c + 1) * MOE_HCHUNK, :], preferred_element_type=F32)
            ys[e] = part if ys[e] is None else ys[e] + part
        moe = (x_ref[:, D_MODEL + 1:D_MODEL + 2] * ys[0] + x_ref[:, D_MODEL + 2:D_MODEL + 3] * ys[1])
        o_ref[...] = _layer_norm(ALPHA * xt + moe, g_ref[...], b_ref[...])


def _moe_tiles_call(x_sorted, nvalid, elo, ehi, wei, weo, g, b):
    ntiles = nvalid.shape[0]
    wi_spec = lambda e: pl.BlockSpec((None, D_MODEL, 2 * D_EXPERT), lambda i, nv, lo, hi: (e(lo, hi)[i], 0, 0))
    wo_spec = lambda e: pl.BlockSpec((None, D_EXPERT, D_MODEL), lambda i, nv, lo, hi: (e(lo, hi)[i], 0, 0))
    first = lambda lo, hi: lo
    second = lambda lo, hi: hi
    vec = pl.BlockSpec((1, D_MODEL), lambda i, nv, lo, hi: (0, 0))
    grid_spec = pltpu.PrefetchScalarGridSpec(
        num_scalar_prefetch=3,
        grid=(ntiles,),
        in_specs=[pl.BlockSpec((TMOE, D_MODEL + ROUTE_COLS), lambda i, nv, lo, hi: (i, 0)),
                  wi_spec(first), wi_spec(second), wo_spec(first), wo_spec(second), vec, vec],
        out_specs=pl.BlockSpec((TMOE, D_MODEL), lambda i, nv, lo, hi: (i, 0)),
    )
    return pl.pallas_call(
        _moe_tiles_body,
        grid_spec=grid_spec,
        out_shape=jax.ShapeDtypeStruct((ntiles * TMOE, D_MODEL), F32),
        compiler_params=_params(("arbitrary",)),
        name="moe_ln2",
    )(nvalid, elo, ehi, x_sorted, wei, wei, weo, weo, g, b)


def _route_plan(bucket, counts):
    t = bucket.shape[0]
    ntiles = t // TMOE + N_ROUTE_BUCKETS
    token = jnp.arange(t, dtype=jnp.int32)
    _, order = lax.sort((bucket, token), num_keys=1, is_stable=True)
    row_start = jnp.cumsum(counts) - counts
    tiles_per = (counts + TMOE - 1) // TMOE
    tile_end = jnp.cumsum(tiles_per)
    tile_start = tile_end - tiles_per
    total = tile_end[-1]
    tile = jnp.arange(ntiles, dtype=jnp.int32)
    used = tile < total
    bkt = jnp.sum(jnp.minimum(tile, total - 1)[:, None] >= tile_end[None, :], axis=1, dtype=jnp.int32)
    local = tile - tile_start[bkt]
    nvalid = jnp.where(used, jnp.clip(counts[bkt] - local * TMOE, 0, TMOE), 0).astype(jnp.int32)
    pairs = jnp.asarray(PAIRS, jnp.int32)
    grp = bkt // len(PAIRS)
    elo = grp * EXPERTS_PER_GROUP + pairs[bkt % len(PAIRS), 0]
    ehi = grp * EXPERTS_PER_GROUP + pairs[bkt % len(PAIRS), 1]
    shift = tile_start * TMOE - row_start
    step = shift - jnp.concatenate([jnp.zeros((1,), jnp.int32), shift[:-1]])
    padded_row = token + jnp.sum(jnp.where(token[None, :] >= row_start[:, None], step[:, None], 0), axis=0)
    _, pos = lax.sort((order, padded_row), num_keys=1)
    return nvalid, elo.astype(jnp.int32), ehi.astype(jnp.int32), pos


def _prepare(w_in, b_gate, q_gain, k_gain, rel_bias, w_branch_a, w_branch_b, w_out, ln1_g, ln1_b,
             w_route_group, b_route_group, w_route_expert, b_route_expert, w_expert_in,
             w_expert_out, ln2_g, ln2_b, seq):
    row = lambda v: v.reshape(1, -1).astype(F32)
    w = w_in[0]
    wr = jnp.concatenate([w_route_group[0]]
                         + [w_route_expert[0, gi] for gi in range(N_GROUPS)], axis=1)
    wr = jnp.pad(wr.astype(F32), ((0, 0), (0, ROUTE_COLS - wr.shape[1])))
    br = jnp.concatenate([b_route_group[0], b_route_expert[0].reshape(-1)])
    br = jnp.pad(br.astype(F32), (0, ROUTE_COLS - br.shape[0])).reshape(1, ROUTE_COLS)
    wr2 = jnp.concatenate(_split_bf16(wr), axis=1)
    scale = HEAD_DIM ** -0.5
    wb = w[:, DT_ROWS:]
    wb = jnp.concatenate([wb[:, 0:B_WIDTH] * (scale * LOG2E), wb[:, B_WIDTH:]], axis=1)
    return dict(
        wdt=w[:, 0:DT_ROWS].T.astype(BF16), wb=wb.astype(BF16), bg=row(b_gate[0]),
        rope_q=_rope_tables(seq, q_gain[0], scale * LOG2E), rope_k=_rope_tables(seq, k_gain[0], 1.0),
        score_bound=_score_bound(q_gain[0], k_gain[0]),
        bm=[_bias_mask_table(rel_bias, gi, r) for gi, (_, r) in enumerate(DILATED_PAIRS)],
        wa=w_branch_a[0].astype(BF16), wbr=w_branch_b[0].astype(BF16), wo=w_out[0].astype(BF16),
        ln1=(row(ln1_g[0]), row(ln1_b[0])), wr2=wr2, br=br,
        wei=w_expert_in[0].astype(BF16), weo=w_expert_out[0].astype(BF16),
        ln2=(row(ln2_g[0]), row(ln2_b[0])),
    )


def _trunk(x, p, after):
    b, s, d = x.shape
    qt, k, vt, *qkv_b, gates = _inproj_call(x, p["wdt"], p["wb"], p["bg"], *p["rope_q"], *p["rope_k"],
                                            after)
    ya = _attn_a_call(*p["score_bound"], qt, k, vt)
    os, ls = [], []
    for gi, (_, r) in enumerate(DILATED_PAIRS):
        qg, kg, vg = (qkv_b[i * N_DIL_GROUPS + gi] for i in range(3))
        o, lse = _attn_b_call(qg, kg, vg, p["bm"][gi], r)
        os.append(o.reshape(b * s // r, r * B_OUT_WIDTH))
        ls.append(lse.reshape(b * s // r, r * B_OUT_WIDTH))
    return _merge_call(x.reshape(b * s, d), ya.reshape(b * s, A_WIDTH), os, ls,
                       gates.reshape(b * s, 2 * D_MODEL), p["wa"], p["wbr"], p["wo"], *p["ln1"],
                       p["wr2"], p["br"])


def _sorted_rows(x1ext, plan):
    ntiles = plan[0].shape[0]
    return _sc_permute_rows(x1ext, plan[3], ntiles * TMOE, scatter=True)


def _experts(x_sorted, plan, p, shape):
    nvalid, elo, ehi, pos = plan
    y_sorted = _moe_tiles_call(x_sorted, nvalid, elo, ehi, p["wei"], p["weo"], *p["ln2"])
    return _sc_permute_rows(y_sorted, pos, pos.shape[0], scatter=False).reshape(shape)


def kernel(x_prompt, x_sample, w_in, b_gate, q_gain, k_gain, rel_bias, w_branch_a, w_branch_b, w_out,
           ln1_g, ln1_b, w_route_group, b_route_group, w_route_expert, b_route_expert, w_expert_in,
           w_expert_out, ln2_g, ln2_b):
    assert x_prompt.shape[1] == x_sample.shape[1]
    p = _prepare(w_in, b_gate, q_gain, k_gain, rel_bias, w_branch_a, w_branch_b, w_out, ln1_g, ln1_b,
                 w_route_group, b_route_group, w_route_expert, b_route_expert, w_expert_in,
                 w_expert_out, ln2_g, ln2_b, x_prompt.shape[1])
    plan = lambda bucket, counts: _route_plan(bucket.reshape(-1), counts[:N_ROUTE_BUCKETS, 0])
    x1_sample, *route_sample = _trunk(x_sample, p, p["br"])
    plan_sample = plan(*route_sample)
    sorted_sample = _sorted_rows(x1_sample, plan_sample)
    x1_prompt, *route_prompt = _trunk(x_prompt, p, plan_sample[3])
    plan_prompt = plan(*route_prompt)
    sorted_prompt = _sorted_rows(x1_prompt, plan_prompt)
    y_sample = _experts(sorted_sample, plan_sample, p, x_sample.shape)
    y_prompt = _experts(sorted_prompt, plan_prompt, p, x_prompt.shape)
    return (y_prompt, y_sample)
```

```python
import functools
import math

import jax
import jax.numpy as jnp
import numpy as np
from jax import lax
from jax.experimental import pallas as pl
from jax.experimental.pallas import tpu as pltpu
from jax.experimental.pallas import tpu_sc as plsc

F32 = jnp.float32
BF16 = jnp.bfloat16

D_MODEL = 1024
HEAD_DIM = 64
A_Q_HEADS = 8
A_KV_HEADS = 2
A_GROUP = A_Q_HEADS // A_KV_HEADS
GRID_W = 64
ROPE_AXIS_DIM = HEAD_DIM // 2
ROPE_THETA = 10000.0
DILATED_PAIRS = ((128, 1), (512, 4), (2048, 16))
N_DIL_GROUPS = len(DILATED_PAIRS)
B_HEADS_PER_GROUP = 4
N_BUCKETS = 32
MAX_DISTANCE = 1024
A_WIDTH = A_Q_HEADS * HEAD_DIM
A_KV_WIDTH = A_KV_HEADS * HEAD_DIM
B_WIDTH = B_HEADS_PER_GROUP * N_DIL_GROUPS * HEAD_DIM
B_OUT_WIDTH = B_HEADS_PER_GROUP * HEAD_DIM
DT_ROWS = A_WIDTH + 2 * A_KV_WIDTH
N_GROUPS = 4
EXPERTS_PER_GROUP = 4
N_EXPERTS = N_GROUPS * EXPERTS_PER_GROUP
D_EXPERT = 512
PAIRS = ((0, 1), (0, 2), (0, 3), (1, 2), (1, 3), (2, 3))
N_ROUTE_BUCKETS = N_GROUPS * len(PAIRS)
ALPHA = 2.0 ** 0.25
RMS_EPS = 1e-6
LN_EPS = 1e-5
NEG_INF = -1e30

LANES = 128
TM = 512
TMG = 512
TSUB = 128
NQ = 256
A_HEADS_PER_PHASE = 1
A_TK = 256
A_LOOKAHEAD = 4
BF16_SUBLANES = 16
F32_SUBLANES = 8
VT_ROWS = HEAD_DIM + BF16_SUBLANES
LOG2E = 1.4426950408889634
LN2 = 0.6931471805599453
MAX_SAFE_SHIFT = 40.0
HALO = 64
BAND = NQ + 2 * HALO
B_SUBSEQ_PER_STEP = 4
TMOE = 256
ROUTE_COLS = LANES
SC_CORES = 2
SC_SUBCORES = 16
SC_ROWS = 32
VMEM_LIMIT = 56 * 1024 * 1024


def _params(sem):
    return pltpu.CompilerParams(dimension_semantics=sem, vmem_limit_bytes=VMEM_LIMIT)


def _swap16(x):
    return jnp.concatenate([x[16:32], x[0:16], x[48:64], x[32:48]], axis=0)


def _norm_rope(xh, ta, tb):
    r = lax.rsqrt(jnp.mean(xh * xh, axis=0, keepdims=True) + RMS_EPS)
    return (xh * ta + _swap16(xh) * tb) * r


def _inproj_body(x_ref, wdt_ref, wb_ref, bg_ref, taq_ref, tbq_ref, tak_ref, tbk_ref, after_ref,
                 qt_ref, k_ref, vt_ref, *rest):
    del after_ref
    b_refs, gates_ref, tmp_sc = rest[:3 * N_DIL_GROUPS], rest[3 * N_DIL_GROUPS], rest[-1]
    xb = x_ref[...].astype(BF16)

    def project(lo, hi):
        return jnp.dot(xb, wb_ref[:, lo:hi], preferred_element_type=F32)

    def rope_queries(dt):
        taq, tbq = taq_ref[...], tbq_ref[...]
        for h in range(A_Q_HEADS):
            sl = slice(h * HEAD_DIM, (h + 1) * HEAD_DIM)
            qt_ref[sl, :] = _norm_rope(dt[sl], taq, tbq).astype(BF16)

    def rope_keys_values(dt):
        tak, tbk = tak_ref[...], tbk_ref[...]
        kt = jnp.concatenate(
            [_norm_rope(dt[A_WIDTH + h * HEAD_DIM:A_WIDTH + (h + 1) * HEAD_DIM], tak, tbk)
             for h in range(A_KV_HEADS)], axis=0)
        k_ref[...] = kt.T.astype(BF16)
        for h in range(A_KV_HEADS):
            v0 = A_WIDTH + A_KV_WIDTH + h * HEAD_DIM
            vt_ref[h, 0:HEAD_DIM, :] = dt[v0:v0 + HEAD_DIM].astype(BF16)
            vt_ref[h, HEAD_DIM:, :] = jnp.ones((VT_ROWS - HEAD_DIM, TM), BF16)

    def deinterleave(i, res):
        for c in range(B_WIDTH // LANES):
            tmp_sc[i, c] = res[:, c * LANES:(c + 1) * LANES]
        for gi, (_, r) in enumerate(DILATED_PAIRS):
            ref = b_refs[i * N_DIL_GROUPS + gi]
            for j in range(r):
                for half in range(B_OUT_WIDTH // LANES):
                    c = gi * (B_OUT_WIDTH // LANES) + half
                    lanes = slice(j * B_OUT_WIDTH + half * LANES, j * B_OUT_WIDTH + (half + 1) * LANES)
                    ref[:, lanes] = tmp_sc[i, c, pl.ds(j, TM // r, stride=r), :].astype(BF16)

    def gate(i, pre):
        cols = slice(i * D_MODEL, (i + 1) * D_MODEL)
        gates_ref[:, cols] = jax.nn.sigmoid(pre + bg_ref[:, cols]).astype(BF16)

    g0 = 3 * B_WIDTH
    dt = lax.dot_general(wdt_ref[...], xb, (((1,), (1,)), ((), ())),
                         preferred_element_type=F32)
    res_q = project(0, B_WIDTH)
    rope_queries(dt)
    res_k = project(B_WIDTH, 2 * B_WIDTH)
    rope_keys_values(dt)
    deinterleave(0, res_q)
    res_v = project(2 * B_WIDTH, g0)
    deinterleave(1, res_k)
    pre_a = project(g0, g0 + D_MODEL)
    deinterleave(2, res_v)
    pre_b = project(g0 + D_MODEL, g0 + 2 * D_MODEL)
    gate(0, pre_a)
    gate(1, pre_b)


def _inproj_call(x, wdt, wb, bg, taq, tbq, tak, tbk, after):
    b, s, d = x.shape
    nt = s // TM
    wcols = wb.shape[1]
    tab = pl.BlockSpec((HEAD_DIM, TM), lambda p, i: (0, p))
    const = lambda shape: pl.BlockSpec(shape, lambda p, i: (0,) * len(shape))
    tok = lambda c: pl.BlockSpec((None, TM, c), lambda p, i: (i, p, 0))
    dil = [r for _ in range(3) for _, r in DILATED_PAIRS]
    return pl.pallas_call(
        _inproj_body,
        grid=(nt, b),
        in_specs=[tok(d), const((DT_ROWS, d)), const((d, wcols)), const((1, 2 * D_MODEL)),
                  tab, tab, tab, tab, pl.BlockSpec(memory_space=pl.ANY)],
        out_specs=[
            pl.BlockSpec((None, A_WIDTH, TM), lambda p, i: (i, 0, p)),
            tok(A_KV_WIDTH),
            pl.BlockSpec((None, None, A_KV_HEADS, VT_ROWS, TM), lambda p, i: (i, p, 0, 0, 0)),
        ] + [pl.BlockSpec((None, TM // r, r * B_OUT_WIDTH), lambda p, i: (i, p, 0)) for r in dil]
        + [tok(2 * D_MODEL)],
        out_shape=[
            jax.ShapeDtypeStruct((b, A_WIDTH, s), BF16),
            jax.ShapeDtypeStruct((b, s, A_KV_WIDTH), BF16),
            jax.ShapeDtypeStruct((b, nt, A_KV_HEADS, VT_ROWS, TM), BF16),
        ] + [jax.ShapeDtypeStruct((b, s // r, r * B_OUT_WIDTH), BF16) for r in dil]
        + [jax.ShapeDtypeStruct((b, s, 2 * D_MODEL), BF16)],
        scratch_shapes=[pltpu.VMEM((3, B_WIDTH // LANES, TM, LANES), F32)],
        compiler_params=_params(("arbitrary", "arbitrary")),
        name="inproj",
    )(x, wdt, wb, bg, taq, tbq, tak, tbk, after)


def _rope_tables(s, gain, scale):
    t = np.arange(s)
    inv = ROPE_THETA ** (-np.arange(0, ROPE_AXIS_DIM, 2, dtype=np.float64) / ROPE_AXIS_DIM)
    pos = np.stack([t // GRID_W, t // GRID_W, t % GRID_W, t % GRID_W], axis=0)
    ang = (pos[:, None, :] * inv[None, :, None]).reshape(HEAD_DIM, s)
    sign = np.repeat(np.array([-1.0, 1.0, -1.0, 1.0]), 16)[:, None]
    partner = np.arange(HEAD_DIM) ^ 16
    cos = jnp.asarray(np.cos(ang), F32)
    sin = jnp.asarray(sign * np.sin(ang), F32)
    g = gain.astype(F32) * scale
    return g[:, None] * cos, g[partner][:, None] * sin


def _attn_a_body(fast_ref, kbound_ref, qt_ref, k_ref, vt_ref, o_ref, rhs_sc, acc_sc, ot_sc):
    nchunks = vt_ref.shape[0]
    tk = vt_ref.shape[3]
    heads = range(A_Q_HEADS)
    zeros = jnp.zeros((HEAD_DIM, NQ), BF16)
    for h in heads:
        qh = qt_ref[h * HEAD_DIM:(h + 1) * HEAD_DIM, :]
        rhs_sc[h] = jnp.concatenate([qh, zeros] if h < A_GROUP else [zeros, qh], axis=0)

    @pl.when(fast_ref[0] == 1)
    def _():
        shift = []
        for h in heads:
            qh = qt_ref[h * HEAD_DIM:(h + 1) * HEAD_DIM, :].astype(F32)
            shift.append(jnp.sqrt(jnp.sum(qh * qh, axis=0, keepdims=True)) * kbound_ref[0])

        def scores(c, h):
            return jnp.dot(k_ref[c * A_TK:(c + 1) * A_TK, :], rhs_sc[h], preferred_element_type=F32)

        def values(c, kv):
            per = tk // A_TK
            return vt_ref[c // per, kv][:, (c % per) * A_TK:(c % per + 1) * A_TK]

        groups = [(kv, c, part) for kv in range(A_KV_HEADS) for c in range(nchunks * tk // A_TK)
                  for part in range(A_GROUP // A_HEADS_PER_PHASE)]
        members = lambda kv, part: range(kv * A_GROUP + part * A_HEADS_PER_PHASE,
                                         kv * A_GROUP + (part + 1) * A_HEADS_PER_PHASE)
        issue = lambda group: [scores(group[1], h) for h in members(group[0], group[2])]
        accs = [None] * A_Q_HEADS
        pending = [issue(group) for group in groups[:A_LOOKAHEAD]]
        for n, (kv, c, part) in enumerate(groups):
            if n + A_LOOKAHEAD < len(groups):
                pending.append(issue(groups[n + A_LOOKAHEAD]))
            sts = pending.pop(0)
            ps = [jnp.exp2(st - shift[h]).astype(BF16) for st, h in zip(sts, members(kv, part))]
            pvs = [jnp.dot(values(c, kv), p, preferred_element_type=F32) for p in ps]
            for pv, h in zip(pvs, members(kv, part)):
                accs[h] = pv if accs[h] is None else accs[h] + pv
        for h in heads:
            acc_sc[h] = accs[h]

    @pl.when(fast_ref[0] == 0)
    def _():
        acc_sc[...] = jnp.zeros_like(acc_sc)

        def chunk(c, ms):
            kc = k_ref[pl.ds(pl.multiple_of(c * tk, tk), tk), :]
            new_ms = []
            for h in heads:
                st = jnp.dot(kc, rhs_sc[h], preferred_element_type=F32)
                m_new = jnp.maximum(ms[h], jnp.max(st, axis=0, keepdims=True))
                alpha = jnp.exp2(ms[h] - m_new)
                p = jnp.exp2(st - m_new).astype(BF16)
                acc_sc[h] = alpha * acc_sc[h] + jnp.dot(vt_ref[c, h // A_GROUP], p,
                                                         preferred_element_type=F32)
                new_ms.append(m_new)
            return tuple(new_ms)

        lax.fori_loop(0, nchunks, chunk, tuple(jnp.full((1, NQ), NEG_INF, F32) for _ in heads))

    for h in heads:
        ot_sc[h * HEAD_DIM:(h + 1) * HEAD_DIM, :] = (
            acc_sc[h, 0:HEAD_DIM, :] / acc_sc[h, HEAD_DIM:HEAD_DIM + 1, :])
    o_ref[...] = ot_sc[...].T.astype(BF16)


def _attn_a_call(fast, kbound, qt, k, vt):
    b, _, s = qt.shape
    nchunks, tk = vt.shape[1], vt.shape[4]
    grid_spec = pltpu.PrefetchScalarGridSpec(
        num_scalar_prefetch=2,
        grid=(b, s // NQ),
        in_specs=[
            pl.BlockSpec((None, A_WIDTH, NQ), lambda i, j, *_: (i, 0, j)),
            pl.BlockSpec((None, s, A_KV_WIDTH), lambda i, j, *_: (i, 0, 0)),
            pl.BlockSpec((None, nchunks, A_KV_HEADS, VT_ROWS, tk), lambda i, j, *_: (i, 0, 0, 0, 0)),
        ],
        out_specs=pl.BlockSpec((None, NQ, A_WIDTH), lambda i, j, *_: (i, j, 0)),
        scratch_shapes=[pltpu.VMEM((A_Q_HEADS, A_KV_WIDTH, NQ), BF16),
                        pltpu.VMEM((A_Q_HEADS, VT_ROWS, NQ), F32),
                        pltpu.VMEM((A_WIDTH, NQ), F32)],
    )
    return pl.pallas_call(
        _attn_a_body,
        grid_spec=grid_spec,
        out_shape=jax.ShapeDtypeStruct((b, s, A_WIDTH), BF16),
        compiler_params=_params(("arbitrary", "arbitrary")),
        name="attn_a",
    )(fast, kbound, qt, k, vt)


def _score_bound(q_gain, k_gain):
    kb = math.sqrt(HEAD_DIM) * jnp.max(jnp.abs(k_gain.astype(F32)))
    qb = math.sqrt(HEAD_DIM) * jnp.max(jnp.abs(q_gain.astype(F32))) * (HEAD_DIM ** -0.5) * LOG2E
    fast = (qb * kb <= MAX_SAFE_SHIFT).astype(jnp.int32)
    return fast.reshape(1), kb.reshape(1)


def _attn_b_body(q_ref, k_ref, v_ref, bm_ref, o_ref, lse_ref, kp_sc, vp_sc):
    length = q_ref.shape[0]
    nsub = q_ref.shape[1] // B_OUT_WIDTH
    cols = lambda u: slice(u * B_OUT_WIDTH, (u + 1) * B_OUT_WIDTH)
    zeros = jnp.zeros((HALO, B_OUT_WIDTH), BF16)
    for sc, ref in ((kp_sc, k_ref), (vp_sc, v_ref)):
        for u in range(nsub):
            sc[u, 0:HALO, :] = zeros
            sc[u, HALO:HALO + length, :] = ref[:, cols(u)]
            sc[u, HALO + length:, :] = zeros
    row_head = lax.broadcasted_iota(jnp.int32, (B_OUT_WIDTH, NQ), 0) // HEAD_DIM
    ones = jnp.ones((VT_ROWS - HEAD_DIM, BAND), BF16)
    blocks = {}

    def block(key):
        if key not in blocks:
            u, i = key
            q0 = i * NQ
            qt = q_ref[q0:q0 + NQ, cols(u)].astype(F32).T.astype(BF16)
            kband = kp_sc[u, q0:q0 + BAND, :]
            vbt = vp_sc[u, q0:q0 + BAND, :].astype(F32).T.astype(BF16)
            pen = None
            if q0 - HALO < 0 or q0 - HALO + BAND > length:
                key_pos = lax.broadcasted_iota(jnp.int32, (BAND, NQ), 0) + (q0 - HALO)
                pen = jnp.where((key_pos >= 0) & (key_pos < length), 0.0, NEG_INF)
            blocks[key] = (qt, kband, vbt, pen)
        return blocks[key]

    def scores(key, h):
        qt, kband, _, _ = block(key)
        rhs = jnp.where(row_head == h, qt, jnp.zeros((), BF16))
        return jnp.dot(kband, rhs, preferred_element_type=F32)

    heads = range(B_HEADS_PER_GROUP)
    keys = [(u, i) for u in range(nsub) for i in range(length // NQ)]
    sts = [scores(keys[0], h) for h in heads]
    for n, key in enumerate(keys):
        sts_next = [scores(keys[n + 1], h) for h in heads] if n + 1 < len(keys) else None
        _, _, vbt, pen = block(key)
        sts = [sts[h] + bm_ref[h] for h in heads]
        if pen is not None:
            sts = [st + pen for st in sts]
        ms = [jnp.max(st, axis=0, keepdims=True) for st in sts]
        ps = [jnp.exp2(st - m).astype(BF16) for st, m in zip(sts, ms)]
        accs = [jnp.dot(jnp.concatenate([vbt[h * HEAD_DIM:(h + 1) * HEAD_DIM], ones], axis=0), ps[h],
                        preferred_element_type=F32) for h in heads]
        ls = [acc[HEAD_DIM:HEAD_DIM + 1] for acc in accs]
        u, i = key
        rows = slice(i * NQ, (i + 1) * NQ)
        o_ref[rows, cols(u)] = jnp.concatenate([acc[0:HEAD_DIM] / l for acc, l in zip(accs, ls)], axis=0).T
        lse_ref[rows, cols(u)] = jnp.concatenate(
            [jnp.broadcast_to((m + jnp.log2(l)) * LN2, (HEAD_DIM, NQ)) for m, l in zip(ms, ls)], axis=0).T
        del blocks[key]
        sts = sts_next


def _attn_b_call(qg, kg, vg, bm, r):
    b, length, _ = qg.shape
    nsub = min(r, B_SUBSEQ_PER_STEP)
    spec = pl.BlockSpec((None, length, nsub * B_OUT_WIDTH), lambda i, j: (i, 0, j))
    out_sds = jax.ShapeDtypeStruct((b, length, r * B_OUT_WIDTH), F32)
    return pl.pallas_call(
        _attn_b_body,
        grid=(b, r // nsub),
        in_specs=[spec, spec, spec,
                  pl.BlockSpec((B_HEADS_PER_GROUP, BAND, NQ), lambda i, j: (0, 0, 0))],
        out_specs=[spec, spec],
        out_shape=[out_sds, out_sds],
        scratch_shapes=[pltpu.VMEM((nsub, length + 2 * HALO, B_OUT_WIDTH), BF16),
                        pltpu.VMEM((nsub, length + 2 * HALO, B_OUT_WIDTH), BF16)],
        compiler_params=_params(("arbitrary", "arbitrary")),
        name=f"attn_b_r{r}",
    )(qg, kg, vg, bm)


def _t5_bucket_np(rel):
    half = N_BUCKETS // 2
    exact = half // 2
    n = np.abs(rel)
    big = exact + (np.log(np.maximum(n, 1).astype(np.float64) / exact)
                   / math.log(MAX_DISTANCE / exact) * (half - exact)).astype(np.int64)
    big = np.minimum(big, half - 1)
    return np.where(rel > 0, half, 0) + np.where(n < exact, n, big)


def _bias_mask_table(rel_bias, gi, r):
    span = BAND + NQ
    rel = np.arange(span) - (NQ - 1) - HALO
    tab = rel_bias[:, gi * B_HEADS_PER_GROUP:(gi + 1) * B_HEADS_PER_GROUP].astype(F32) * LOG2E
    u = jnp.where(jnp.asarray(np.abs(rel) <= HALO)[:, None],
                  tab[jnp.asarray(_t5_bucket_np(rel * r), jnp.int32)], NEG_INF).T
    shifted = jnp.tile(u, (1, NQ))[:, :NQ * (span - 1)].reshape(-1, NQ, span - 1)
    return jnp.transpose(shifted[:, :, NQ - 1:NQ - 1 + BAND], (0, 2, 1))


def _layer_norm(h, g, b):
    mu = jnp.mean(h, axis=-1, keepdims=True)
    c = h - mu
    var = jnp.mean(c * c, axis=-1, keepdims=True)
    return c * lax.rsqrt(var + LN_EPS) * g + b


def _split_bf16(a):
    hi = a.astype(BF16)
    return hi, (a - hi.astype(F32)).astype(BF16)


def _merge_body(x_ref, ya_ref, o1_ref, o2_ref, o3_ref, l1_ref, l2_ref, l3_ref, gates_ref,
                wa_ref, wbr_ref, wo_ref, g_ref, b_ref, wr2_ref, br_ref, out_ref, bkt_ref, cnt_ref,
                *tok_sc):
    slabs = B_OUT_WIDTH // LANES
    dil = [r for _, r in DILATED_PAIRS]
    b_refs = (o1_ref, o2_ref, o3_ref, l1_ref, l2_ref, l3_ref)
    for ref, sc, r in zip(b_refs, tok_sc, dil + dil):
        if r > 1:
            for j in range(r):
                for c in range(slabs):
                    lanes = slice(j * B_OUT_WIDTH + c * LANES, j * B_OUT_WIDTH + (c + 1) * LANES)
                    sc[c, pl.ds(j, TMG // r, stride=r), :] = ref[:, lanes]

    def token_major(which, rows):
        ref, sc, r = b_refs[which], tok_sc[which], (dil + dil)[which]
        if r == 1:
            return ref[rows, :]
        return jnp.concatenate([sc[c, rows, :] for c in range(slabs)], axis=1)

    subs = [slice(i * TSUB, (i + 1) * TSUB) for i in range(TMG // TSUB)]
    ybs = []
    for rows in subs:
        o1, o2, o3, l1, l2, l3 = (token_major(w, rows) for w in range(6))
        mx = jnp.maximum(jnp.maximum(l1, l2), l3)
        e1, e2, e3 = jnp.exp(l1 - mx), jnp.exp(l2 - mx), jnp.exp(l3 - mx)
        ybs.append(((e1 * o1 + e2 * o2 + e3 * o3) / (e1 + e2 + e3)).astype(BF16))
    ya_ps = [jnp.dot(ya_ref[rows, :], wa_ref[...], preferred_element_type=F32) for rows in subs]
    yb_ps = [jnp.dot(yb, wbr_ref[...], preferred_element_type=F32) for yb in ybs]
    mergeds = [(gates_ref[rows, 0:D_MODEL].astype(F32) * ya_p
                + gates_ref[rows, D_MODEL:2 * D_MODEL].astype(F32) * yb_p).astype(BF16)
               for rows, ya_p, yb_p in zip(subs, ya_ps, yb_ps)]
    hs = [ALPHA * x_ref[rows, :] + jnp.dot(m, wo_ref[...], preferred_element_type=F32)
          for rows, m in zip(subs, mergeds)]
    x1s = [_layer_norm(h, g_ref[...], b_ref[...]) for h in hs]
    for rows, x1 in zip(subs, x1s):
        out_ref[rows, 0:D_MODEL] = x1

    splits = [_split_bf16(x1) for x1 in x1s]
    boths = [jnp.dot(xh, wr2_ref[...], preferred_element_type=F32)
             + jnp.dot(xl, wr2_ref[...], preferred_element_type=F32) for xh, xl in splits]
    logits = jnp.concatenate(
        [(both[:, 0:ROUTE_COLS] + both[:, ROUTE_COLS:] + br_ref[...]).T for both in boths], axis=1)
    col = lambda i: logits[i:i + 1, :]
    lg = [col(i) for i in range(N_GROUPS)]
    gmax = functools.reduce(jnp.maximum, lg)
    top_p = 1.0 / functools.reduce(lambda a, c: a + c, [jnp.exp(v - gmax) for v in lg])
    gsel = jnp.full_like(gmax, N_GROUPS - 1).astype(jnp.int32)
    for i in reversed(range(N_GROUPS - 1)):
        gsel = jnp.where(lg[i] >= gmax, i, gsel)
    sel = []
    for e in range(EXPERTS_PER_GROUP):
        v = col(N_GROUPS + (N_GROUPS - 1) * EXPERTS_PER_GROUP + e)
        for gi in reversed(range(N_GROUPS - 1)):
            v = jnp.where(gsel == gi, col(N_GROUPS + gi * EXPERTS_PER_GROUP + e), v)
        sel.append(v)
    rank = []
    for e in range(EXPERTS_PER_GROUP):
        rk = jnp.zeros_like(gsel)
        for j in range(EXPERTS_PER_GROUP):
            if j != e:
                ahead = (sel[j] > sel[e]) | ((sel[j] == sel[e]) & (j < e))
                rk = rk + ahead.astype(jnp.int32)
        rank.append(rk)
    pick = lambda k: functools.reduce(
        lambda a, c: a + c, [jnp.where(rank[e] == k, sel[e], 0.0) for e in range(EXPERTS_PER_GROUP)])
    t = jnp.exp(pick(1) - pick(0))
    w_first = top_p / (1.0 + t)
    w_second = top_p * t / (1.0 + t)
    chosen = [rank[e] < 2 for e in range(EXPERTS_PER_GROUP)]
    weight = [jnp.where(rank[e] == 0, w_first, w_second) for e in range(EXPERTS_PER_GROUP)]
    pair = jnp.zeros_like(gsel)
    w_lo = jnp.zeros_like(top_p)
    w_hi = jnp.zeros_like(top_p)
    for pi, (a, c) in enumerate(PAIRS):
        hit = chosen[a] & chosen[c]
        pair = jnp.where(hit, pi, pair)
        w_lo = jnp.where(hit, weight[a], w_lo)
        w_hi = jnp.where(hit, weight[c], w_hi)
    bucket = gsel * len(PAIRS) + pair
    bkt_ref[...] = bucket
    hits = lax.broadcasted_iota(jnp.int32, (ROUTE_COLS, TMG), 0) == bucket

    @pl.when(pl.program_id(0) == 0)
    def _():
        cnt_ref[...] = jnp.zeros_like(cnt_ref)

    cnt_ref[...] += jnp.sum(hits.astype(jnp.int32), axis=1, keepdims=True)
    row = lax.broadcasted_iota(jnp.int32, (F32_SUBLANES, TMG), 0)
    head = jnp.where(row == 1, w_lo, jnp.where(row == 2, w_hi, 0.0))
    route_t = jnp.concatenate([head, jnp.zeros((ROUTE_COLS - F32_SUBLANES, TMG), F32)], axis=0)
    out_ref[:, D_MODEL:] = route_t.T


def _merge_call(x, ya, os, ls, gates, wa, wbr, wo, g, b, wr2, br):
    t, d = x.shape
    tok = lambda c: pl.BlockSpec((TMG, c), lambda i: (i, 0))
    const = lambda a: pl.BlockSpec(a.shape, lambda i: (0,) * a.ndim)
    consts = (wa, wbr, wo, g, b, wr2, br)
    dil = [pl.BlockSpec((TMG // r, r * B_OUT_WIDTH), lambda i: (i, 0)) for _, r in DILATED_PAIRS]
    return pl.pallas_call(
        _merge_body,
        grid=(t // TMG,),
        in_specs=[tok(d), tok(A_WIDTH)] + dil + dil + [tok(2 * D_MODEL)]
        + [const(a) for a in consts],
        out_specs=[tok(d + ROUTE_COLS), pl.BlockSpec((1, TMG), lambda i: (0, i)),
                   pl.BlockSpec((ROUTE_COLS, 1), lambda i: (0, 0))],
        out_shape=[jax.ShapeDtypeStruct((t, d + ROUTE_COLS), F32), jax.ShapeDtypeStruct((1, t), jnp.int32),
                   jax.ShapeDtypeStruct((ROUTE_COLS, 1), jnp.int32)],
        scratch_shapes=[pltpu.VMEM((B_OUT_WIDTH // LANES, TMG, LANES), F32)] * (2 * N_DIL_GROUPS),
        compiler_params=_params(("arbitrary",)),
        name="merge_ln1_route",
    )(x, ya, *os, *ls, gates, *consts)


def _sc_permute_rows(src, idx, nrows_out, scatter):
    nrows, d = idx.shape[0], src.shape[1]
    workers = SC_CORES * SC_SUBCORES
    per_worker = nrows // workers
    assert nrows % (workers * SC_ROWS) == 0
    mesh = plsc.VectorSubcoreMesh(core_axis_name="c", subcore_axis_name="s")

    @functools.partial(
        pl.kernel, mesh=mesh, out_type=jax.ShapeDtypeStruct((nrows_out, d), src.dtype),
        scratch_types=[pltpu.VMEM((SC_ROWS,), jnp.int32), pltpu.VMEM((SC_ROWS, d), src.dtype),
                       pltpu.SemaphoreType.DMA])
    def permute(src_hbm, idx_hbm, out_hbm, idx_v, rows_v, sem):
        base = (lax.axis_index("s") * SC_CORES + lax.axis_index("c")) * per_worker

        @pl.loop(0, per_worker, step=SC_ROWS)
        def _(off):
            rows = pl.ds(pl.multiple_of(base + off, SC_ROWS), SC_ROWS)
            pltpu.sync_copy(idx_hbm.at[rows], idx_v)
            if scatter:
                pltpu.sync_copy(src_hbm.at[rows], rows_v)
                pltpu.async_copy(rows_v, out_hbm.at[idx_v], sem).wait()
            else:
                pltpu.async_copy(src_hbm.at[idx_v], rows_v, sem).wait()
                pltpu.sync_copy(rows_v, out_hbm.at[rows])

    return permute(src, idx)


def _moe_tiles_body(nvalid_ref, elo_ref, ehi_ref, x_ref, wil_ref, wih_ref, wol_ref, woh_ref,
                    g_ref, b_ref, o_ref):
    del elo_ref, ehi_ref
    n = nvalid_ref[pl.program_id(0)]

    @pl.when(n == 0)
    def _():
        o_ref[...] = jnp.zeros_like(o_ref)

    @pl.when(n > 0)
    def _():
        xt = x_ref[:, 0:D_MODEL]
        xb = xt.astype(BF16)
        hcats = [jnp.dot(xb, wi_ref[...], preferred_element_type=F32)
                 for wi_ref in (wil_ref, wih_ref)]
        acts = [(h[:, 0:D_EXPERT] * jax.nn.sigmoid(h[:, 0:D_EXPERT]) * h[:, D_EXPERT:]).astype(BF16)
                for h in hcats]
        y_lo, y_hi = (jnp.dot(act, wo_ref[...], preferred_element_type=F32)
                      for act, wo_ref in zip(acts, (wol_ref, woh_ref)))
        moe = (x_ref[:, D_MODEL + 1:D_MODEL + 2] * y_lo + x_ref[:, D_MODEL + 2:D_MODEL + 3] * y_hi)
        o_ref[...] = _layer_norm(ALPHA * xt + moe, g_ref[...], b_ref[...])


def _moe_tiles_call(x_sorted, nvalid, elo, ehi, wei, weo, g, b):
    ntiles = nvalid.shape[0]
    wi_spec = lambda e: pl.BlockSpec((None, D_MODEL, 2 * D_EXPERT), lambda i, nv, lo, hi: (e(lo, hi)[i], 0, 0))
    wo_spec = lambda e: pl.BlockSpec((None, D_EXPERT, D_MODEL), lambda i, nv, lo, hi: (e(lo, hi)[i], 0, 0))
    first = lambda lo, hi: lo
    second = lambda lo, hi: hi
    vec = pl.BlockSpec((1, D_MODEL), lambda i, nv, lo, hi: (0, 0))
    grid_spec = pltpu.PrefetchScalarGridSpec(
        num_scalar_prefetch=3,
        grid=(ntiles,),
        in_specs=[pl.BlockSpec((TMOE, D_MODEL + ROUTE_COLS), lambda i, nv, lo, hi: (i, 0)),
                  wi_spec(first), wi_spec(second), wo_spec(first), wo_spec(second), vec, vec],
        out_specs=pl.BlockSpec((TMOE, D_MODEL), lambda i, nv, lo, hi: (i, 0)),
    )
    return pl.pallas_call(
        _moe_tiles_body,
        grid_spec=grid_spec,
        out_shape=jax.ShapeDtypeStruct((ntiles * TMOE, D_MODEL), F32),
        compiler_params=_params(("arbitrary",)),
        name="moe_ln2",
    )(nvalid, elo, ehi, x_sorted, wei, wei, weo, weo, g, b)


def _route_plan(bucket, counts):
    t = bucket.shape[0]
    ntiles = t // TMOE + N_ROUTE_BUCKETS
    token = jnp.arange(t, dtype=jnp.int32)
    _, order = lax.sort((bucket, token), num_keys=1, is_stable=True)
    row_start = jnp.cumsum(counts) - counts
    tiles_per = (counts + TMOE - 1) // TMOE
    tile_end = jnp.cumsum(tiles_per)
    tile_start = tile_end - tiles_per
    total = tile_end[-1]
    tile = jnp.arange(ntiles, dtype=jnp.int32)
    used = tile < total
    bkt = jnp.sum(jnp.minimum(tile, total - 1)[:, None] >= tile_end[None, :], axis=1, dtype=jnp.int32)
    local = tile - tile_start[bkt]
    nvalid = jnp.where(used, jnp.clip(counts[bkt] - local * TMOE, 0, TMOE), 0).astype(jnp.int32)
    pairs = jnp.asarray(PAIRS, jnp.int32)
    grp = bkt // len(PAIRS)
    elo = grp * EXPERTS_PER_GROUP + pairs[bkt % len(PAIRS), 0]
    ehi = grp * EXPERTS_PER_GROUP + pairs[bkt % len(PAIRS), 1]
    shift = tile_start * TMOE - row_start
    step = shift - jnp.concatenate([jnp.zeros((1,), jnp.int32), shift[:-1]])
    padded_row = token + jnp.sum(jnp.where(token[None, :] >= row_start[:, None], step[:, None], 0), axis=0)
    _, pos = lax.sort((order, padded_row), num_keys=1)
    return nvalid, elo.astype(jnp.int32), ehi.astype(jnp.int32), pos


def _prepare(w_in, b_gate, q_gain, k_gain, rel_bias, w_branch_a, w_branch_b, w_out, ln1_g, ln1_b,
             w_route_group, b_route_group, w_route_expert, b_route_expert, w_expert_in,
             w_expert_out, ln2_g, ln2_b, seq):
    row = lambda v: v.reshape(1, -1).astype(F32)
    w = w_in[0]
    wr = jnp.concatenate([w_route_group[0]]
                         + [w_route_expert[0, gi] for gi in range(N_GROUPS)], axis=1)
    wr = jnp.pad(wr.astype(F32), ((0, 0), (0, ROUTE_COLS - wr.shape[1])))
    br = jnp.concatenate([b_route_group[0], b_route_expert[0].reshape(-1)])
    br = jnp.pad(br.astype(F32), (0, ROUTE_COLS - br.shape[0])).reshape(1, ROUTE_COLS)
    wr2 = jnp.concatenate(_split_bf16(wr), axis=1)
    scale = HEAD_DIM ** -0.5
    wb = w[:, DT_ROWS:]
    wb = jnp.concatenate([wb[:, 0:B_WIDTH] * (scale * LOG2E), wb[:, B_WIDTH:]], axis=1)
    return dict(
        wdt=w[:, 0:DT_ROWS].T.astype(BF16), wb=wb.astype(BF16), bg=row(b_gate[0]),
        rope_q=_rope_tables(seq, q_gain[0], scale * LOG2E), rope_k=_rope_tables(seq, k_gain[0], 1.0),
        score_bound=_score_bound(q_gain[0], k_gain[0]),
        bm=[_bias_mask_table(rel_bias, gi, r) for gi, (_, r) in enumerate(DILATED_PAIRS)],
        wa=w_branch_a[0].astype(BF16), wbr=w_branch_b[0].astype(BF16), wo=w_out[0].astype(BF16),
        ln1=(row(ln1_g[0]), row(ln1_b[0])), wr2=wr2, br=br,
        wei=w_expert_in[0].astype(BF16), weo=w_expert_out[0].astype(BF16),
        ln2=(row(ln2_g[0]), row(ln2_b[0])),
    )


def _trunk(x, p, after):
    b, s, d = x.shape
    qt, k, vt, *qkv_b, gates = _inproj_call(x, p["wdt"], p["wb"], p["bg"], *p["rope_q"], *p["rope_k"],
                                            after)
    ya = _attn_a_call(*p["score_bound"], qt, k, vt)
    os, ls = [], []
    for gi, (_, r) in enumerate(DILATED_PAIRS):
        qg, kg, vg = (qkv_b[i * N_DIL_GROUPS + gi] for i in range(3))
        o, lse = _attn_b_call(qg, kg, vg, p["bm"][gi], r)
        os.append(o.reshape(b * s // r, r * B_OUT_WIDTH))
        ls.append(lse.reshape(b * s // r, r * B_OUT_WIDTH))
    return _merge_call(x.reshape(b * s, d), ya.reshape(b * s, A_WIDTH), os, ls,
                       gates.reshape(b * s, 2 * D_MODEL), p["wa"], p["wbr"], p["wo"], *p["ln1"],
                       p["wr2"], p["br"])


def _sorted_rows(x1ext, plan):
    ntiles = plan[0].shape[0]
    return _sc_permute_rows(x1ext, plan[3], ntiles * TMOE, scatter=True)


def _experts(x_sorted, plan, p, shape):
    nvalid, elo, ehi, pos = plan
    y_sorted = _moe_tiles_call(x_sorted, nvalid, elo, ehi, p["wei"], p["weo"], *p["ln2"])
    return _sc_permute_rows(y_sorted, pos, pos.shape[0], scatter=False).reshape(shape)


def kernel(x_prompt, x_sample, w_in, b_gate, q_gain, k_gain, rel_bias, w_branch_a, w_branch_b, w_out,
           ln1_g, ln1_b, w_route_group, b_route_group, w_route_expert, b_route_expert, w_expert_in,
           w_expert_out, ln2_g, ln2_b):
    assert x_prompt.shape[1] == x_sample.shape[1]
    p = _prepare(w_in, b_gate, q_gain, k_gain, rel_bias, w_branch_a, w_branch_b, w_out, ln1_g, ln1_b,
                 w_route_group, b_route_group, w_route_expert, b_route_expert, w_expert_in,
                 w_expert_out, ln2_g, ln2_b, x_prompt.shape[1])
    plan = lambda bucket, counts: _route_plan(bucket.reshape(-1), counts[:N_ROUTE_BUCKETS, 0])
    x1_sample, *route_sample = _trunk(x_sample, p, p["br"])
    plan_sample = plan(*route_sample)
    sorted_sample = _sorted_rows(x1_sample, plan_sample)
    x1_prompt, *route_prompt = _trunk(x_prompt, p, plan_sample[3])
    plan_prompt = plan(*route_prompt)
    sorted_prompt = _sorted_rows(x1_prompt, plan_prompt)
    y_sample = _experts(sorted_sample, plan_sample, p, x_sample.shape)
    y_prompt = _experts(sorted_prompt, plan_prompt, p, x_prompt.shape)
    return (y_prompt, y_sample)
```

```python
import functools
import math

import jax
import jax.numpy as jnp
import numpy as np
from jax import lax
from jax.experimental import pallas as pl
from jax.experimental.pallas import tpu as pltpu
from jax.experimental.pallas import tpu_sc as plsc

F32 = jnp.float32
BF16 = jnp.bfloat16

D_MODEL = 1024
HEAD_DIM = 64
A_Q_HEADS = 8
A_KV_HEADS = 2
A_GROUP = A_Q_HEADS // A_KV_HEADS
GRID_W = 64
ROPE_AXIS_DIM = HEAD_DIM // 2
ROPE_THETA = 10000.0
DILATED_PAIRS = ((128, 1), (512, 4), (2048, 16))
N_DIL_GROUPS = len(DILATED_PAIRS)
B_HEADS_PER_GROUP = 4
N_BUCKETS = 32
MAX_DISTANCE = 1024
A_WIDTH = A_Q_HEADS * HEAD_DIM
A_KV_WIDTH = A_KV_HEADS * HEAD_DIM
B_WIDTH = B_HEADS_PER_GROUP * N_DIL_GROUPS * HEAD_DIM
B_OUT_WIDTH = B_HEADS_PER_GROUP * HEAD_DIM
DT_ROWS = A_WIDTH + 2 * A_KV_WIDTH
N_GROUPS = 4
EXPERTS_PER_GROUP = 4
N_EXPERTS = N_GROUPS * EXPERTS_PER_GROUP
D_EXPERT = 512
PAIRS = ((0, 1), (0, 2), (0, 3), (1, 2), (1, 3), (2, 3))
N_ROUTE_BUCKETS = N_GROUPS * len(PAIRS)
ALPHA = 2.0 ** 0.25
RMS_EPS = 1e-6
LN_EPS = 1e-5
NEG_INF = -1e30

LANES = 128
TM = 512
TMG = 512
TSUB = 128
NQ = 256
A_HEADS_PER_PHASE = 1
A_TK = 256
A_LOOKAHEAD = 4
BF16_SUBLANES = 16
F32_SUBLANES = 8
VT_ROWS = HEAD_DIM + BF16_SUBLANES
LOG2E = 1.4426950408889634
LN2 = 0.6931471805599453
MAX_SAFE_SHIFT = 40.0
HALO = 64
BAND = NQ + 2 * HALO
B_SUBSEQ_PER_STEP = 8
TMOE = 256
ROUTE_COLS = LANES
SC_CORES = 2
SC_SUBCORES = 16
SC_ROWS = 32
VMEM_LIMIT = 56 * 1024 * 1024


def _params(sem):
    return pltpu.CompilerParams(dimension_semantics=sem, vmem_limit_bytes=VMEM_LIMIT)


def _swap16(x):
    return jnp.concatenate([x[16:32], x[0:16], x[48:64], x[32:48]], axis=0)


def _norm_rope(xh, ta, tb):
    r = lax.rsqrt(jnp.mean(xh * xh, axis=0, keepdims=True) + RMS_EPS)
    return (xh * ta + _swap16(xh) * tb) * r


def _inproj_body(x_ref, wdt_ref, wb_ref, bg_ref, taq_ref, tbq_ref, tak_ref, tbk_ref, after_ref,
                 qt_ref, k_ref, vt_ref, *rest):
    del after_ref
    b_refs, gates_ref, tmp_sc = rest[:3 * N_DIL_GROUPS], rest[3 * N_DIL_GROUPS], rest[-1]
    xb = x_ref[...].astype(BF16)

    def project(lo, hi):
        return jnp.dot(xb, wb_ref[:, lo:hi], preferred_element_type=F32)

    def rope_queries(dt):
        taq, tbq = taq_ref[...], tbq_ref[...]
        for h in range(A_Q_HEADS):
            sl = slice(h * HEAD_DIM, (h + 1) * HEAD_DIM)
            qt_ref[sl, :] = _norm_rope(dt[sl], taq, tbq).astype(BF16)

    def rope_keys_values(dt):
        tak, tbk = tak_ref[...], tbk_ref[...]
        kt = jnp.concatenate(
            [_norm_rope(dt[A_WIDTH + h * HEAD_DIM:A_WIDTH + (h + 1) * HEAD_DIM], tak, tbk)
             for h in range(A_KV_HEADS)], axis=0)
        k_ref[...] = kt.T.astype(BF16)
        for h in range(A_KV_HEADS):
            v0 = A_WIDTH + A_KV_WIDTH + h * HEAD_DIM
            vt_ref[h, 0:HEAD_DIM, :] = dt[v0:v0 + HEAD_DIM].astype(BF16)
            vt_ref[h, HEAD_DIM:, :] = jnp.ones((VT_ROWS - HEAD_DIM, TM), BF16)

    def deinterleave(i, res):
        for c in range(B_WIDTH // LANES):
            tmp_sc[i, c] = res[:, c * LANES:(c + 1) * LANES]
        for gi, (_, r) in enumerate(DILATED_PAIRS):
            ref = b_refs[i * N_DIL_GROUPS + gi]
            for j in range(r):
                for half in range(B_OUT_WIDTH // LANES):
                    c = gi * (B_OUT_WIDTH // LANES) + half
                    lanes = slice(j * B_OUT_WIDTH + half * LANES, j * B_OUT_WIDTH + (half + 1) * LANES)
                    ref[:, lanes] = tmp_sc[i, c, pl.ds(j, TM // r, stride=r), :].astype(BF16)

    def gate(i, pre):
        cols = slice(i * D_MODEL, (i + 1) * D_MODEL)
        gates_ref[:, cols] = jax.nn.sigmoid(pre + bg_ref[:, cols]).astype(BF16)

    g0 = 3 * B_WIDTH
    dt = lax.dot_general(wdt_ref[...], xb, (((1,), (1,)), ((), ())),
                         preferred_element_type=F32)
    res_q = project(0, B_WIDTH)
    rope_queries(dt)
    res_k = project(B_WIDTH, 2 * B_WIDTH)
    rope_keys_values(dt)
    deinterleave(0, res_q)
    res_v = project(2 * B_WIDTH, g0)
    deinterleave(1, res_k)
    pre_a = project(g0, g0 + D_MODEL)
    deinterleave(2, res_v)
    pre_b = project(g0 + D_MODEL, g0 + 2 * D_MODEL)
    gate(0, pre_a)
    gate(1, pre_b)


def _inproj_call(x, wdt, wb, bg, taq, tbq, tak, tbk, after):
    b, s, d = x.shape
    nt = s // TM
    wcols = wb.shape[1]
    tab = pl.BlockSpec((HEAD_DIM, TM), lambda p, i: (0, p))
    const = lambda shape: pl.BlockSpec(shape, lambda p, i: (0,) * len(shape))
    tok = lambda c: pl.BlockSpec((None, TM, c), lambda p, i: (i, p, 0))
    dil = [r for _ in range(3) for _, r in DILATED_PAIRS]
    return pl.pallas_call(
        _inproj_body,
        grid=(nt, b),
        in_specs=[tok(d), const((DT_ROWS, d)), const((d, wcols)), const((1, 2 * D_MODEL)),
                  tab, tab, tab, tab, pl.BlockSpec(memory_space=pl.ANY)],
        out_specs=[
            pl.BlockSpec((None, A_WIDTH, TM), lambda p, i: (i, 0, p)),
            tok(A_KV_WIDTH),
            pl.BlockSpec((None, None, A_KV_HEADS, VT_ROWS, TM), lambda p, i: (i, p, 0, 0, 0)),
        ] + [pl.BlockSpec((None, TM // r, r * B_OUT_WIDTH), lambda p, i: (i, p, 0)) for r in dil]
        + [tok(2 * D_MODEL)],
        out_shape=[
            jax.ShapeDtypeStruct((b, A_WIDTH, s), BF16),
            jax.ShapeDtypeStruct((b, s, A_KV_WIDTH), BF16),
            jax.ShapeDtypeStruct((b, nt, A_KV_HEADS, VT_ROWS, TM), BF16),
        ] + [jax.ShapeDtypeStruct((b, s // r, r * B_OUT_WIDTH), BF16) for r in dil]
        + [jax.ShapeDtypeStruct((b, s, 2 * D_MODEL), BF16)],
        scratch_shapes=[pltpu.VMEM((3, B_WIDTH // LANES, TM, LANES), F32)],
        compiler_params=_params(("arbitrary", "arbitrary")),
        name="inproj",
    )(x, wdt, wb, bg, taq, tbq, tak, tbk, after)


def _rope_tables(s, gain, scale):
    t = np.arange(s)
    inv = ROPE_THETA ** (-np.arange(0, ROPE_AXIS_DIM, 2, dtype=np.float64) / ROPE_AXIS_DIM)
    pos = np.stack([t // GRID_W, t // GRID_W, t % GRID_W, t % GRID_W], axis=0)
    ang = (pos[:, None, :] * inv[None, :, None]).reshape(HEAD_DIM, s)
    sign = np.repeat(np.array([-1.0, 1.0, -1.0, 1.0]), 16)[:, None]
    partner = np.arange(HEAD_DIM) ^ 16
    cos = jnp.asarray(np.cos(ang), F32)
    sin = jnp.asarray(sign * np.sin(ang), F32)
    g = gain.astype(F32) * scale
    return g[:, None] * cos, g[partner][:, None] * sin


def _attn_a_body(fast_ref, kbound_ref, qt_ref, k_ref, vt_ref, o_ref, rhs_sc, acc_sc, ot_sc):
    nchunks = vt_ref.shape[0]
    tk = vt_ref.shape[3]
    heads = range(A_Q_HEADS)
    zeros = jnp.zeros((HEAD_DIM, NQ), BF16)
    for h in heads:
        qh = qt_ref[h * HEAD_DIM:(h + 1) * HEAD_DIM, :]
        rhs_sc[h] = jnp.concatenate([qh, zeros] if h < A_GROUP else [zeros, qh], axis=0)

    @pl.when(fast_ref[0] == 1)
    def _():
        shift = []
        for h in heads:
            qh = qt_ref[h * HEAD_DIM:(h + 1) * HEAD_DIM, :].astype(F32)
            shift.append(jnp.sqrt(jnp.sum(qh * qh, axis=0, keepdims=True)) * kbound_ref[0])

        def scores(c, h):
            return jnp.dot(k_ref[c * A_TK:(c + 1) * A_TK, :], rhs_sc[h], preferred_element_type=F32)

        def values(c, kv):
            per = tk // A_TK
            return vt_ref[c // per, kv][:, (c % per) * A_TK:(c % per + 1) * A_TK]

        groups = [(kv, c, part) for kv in range(A_KV_HEADS) for c in range(nchunks * tk // A_TK)
                  for part in range(A_GROUP // A_HEADS_PER_PHASE)]
        members = lambda kv, part: range(kv * A_GROUP + part * A_HEADS_PER_PHASE,
                                         kv * A_GROUP + (part + 1) * A_HEADS_PER_PHASE)
        issue = lambda group: [scores(group[1], h) for h in members(group[0], group[2])]
        accs = [None] * A_Q_HEADS
        pending = [issue(group) for group in groups[:A_LOOKAHEAD]]
        for n, (kv, c, part) in enumerate(groups):
            if n + A_LOOKAHEAD < len(groups):
                pending.append(issue(groups[n + A_LOOKAHEAD]))
            sts = pending.pop(0)
            ps = [jnp.exp2(st - shift[h]).astype(BF16) for st, h in zip(sts, members(kv, part))]
            pvs = [jnp.dot(values(c, kv), p, preferred_element_type=F32) for p in ps]
            for pv, h in zip(pvs, members(kv, part)):
                accs[h] = pv if accs[h] is None else accs[h] + pv
        for h in heads:
            acc_sc[h] = accs[h]

    @pl.when(fast_ref[0] == 0)
    def _():
        acc_sc[...] = jnp.zeros_like(acc_sc)

        def chunk(c, ms):
            kc = k_ref[pl.ds(pl.multiple_of(c * tk, tk), tk), :]
            new_ms = []
            for h in heads:
                st = jnp.dot(kc, rhs_sc[h], preferred_element_type=F32)
                m_new = jnp.maximum(ms[h], jnp.max(st, axis=0, keepdims=True))
                alpha = jnp.exp2(ms[h] - m_new)
                p = jnp.exp2(st - m_new).astype(BF16)
                acc_sc[h] = alpha * acc_sc[h] + jnp.dot(vt_ref[c, h // A_GROUP], p,
                                                         preferred_element_type=F32)
                new_ms.append(m_new)
            return tuple(new_ms)

        lax.fori_loop(0, nchunks, chunk, tuple(jnp.full((1, NQ), NEG_INF, F32) for _ in heads))

    for h in heads:
        ot_sc[h * HEAD_DIM:(h + 1) * HEAD_DIM, :] = (
            acc_sc[h, 0:HEAD_DIM, :] / acc_sc[h, HEAD_DIM:HEAD_DIM + 1, :])
    o_ref[...] = ot_sc[...].T.astype(BF16)


def _attn_a_call(fast, kbound, qt, k, vt):
    b, _, s = qt.shape
    nchunks, tk = vt.shape[1], vt.shape[4]
    grid_spec = pltpu.PrefetchScalarGridSpec(
        num_scalar_prefetch=2,
        grid=(b, s // NQ),
        in_specs=[
            pl.BlockSpec((None, A_WIDTH, NQ), lambda i, j, *_: (i, 0, j)),
            pl.BlockSpec((None, s, A_KV_WIDTH), lambda i, j, *_: (i, 0, 0)),
            pl.BlockSpec((None, nchunks, A_KV_HEADS, VT_ROWS, tk), lambda i, j, *_: (i, 0, 0, 0, 0)),
        ],
        out_specs=pl.BlockSpec((None, NQ, A_WIDTH), lambda i, j, *_: (i, j, 0)),
        scratch_shapes=[pltpu.VMEM((A_Q_HEADS, A_KV_WIDTH, NQ), BF16),
                        pltpu.VMEM((A_Q_HEADS, VT_ROWS, NQ), F32),
                        pltpu.VMEM((A_WIDTH, NQ), F32)],
    )
    return pl.pallas_call(
        _attn_a_body,
        grid_spec=grid_spec,
        out_shape=jax.ShapeDtypeStruct((b, s, A_WIDTH), BF16),
        compiler_params=_params(("arbitrary", "arbitrary")),
        name="attn_a",
    )(fast, kbound, qt, k, vt)


def _score_bound(q_gain, k_gain):
    kb = math.sqrt(HEAD_DIM) * jnp.max(jnp.abs(k_gain.astype(F32)))
    qb = math.sqrt(HEAD_DIM) * jnp.max(jnp.abs(q_gain.astype(F32))) * (HEAD_DIM ** -0.5) * LOG2E
    fast = (qb * kb <= MAX_SAFE_SHIFT).astype(jnp.int32)
    return fast.reshape(1), kb.reshape(1)


def _attn_b_body(q_ref, k_ref, v_ref, bm_ref, o_ref, lse_ref, kp_sc, vp_sc):
    length = q_ref.shape[0]
    nsub = q_ref.shape[1] // B_OUT_WIDTH
    cols = lambda u: slice(u * B_OUT_WIDTH, (u + 1) * B_OUT_WIDTH)
    zeros = jnp.zeros((HALO, B_OUT_WIDTH), BF16)
    for sc, ref in ((kp_sc, k_ref), (vp_sc, v_ref)):
        for u in range(nsub):
            sc[u, 0:HALO, :] = zeros
            sc[u, HALO:HALO + length, :] = ref[:, cols(u)]
            sc[u, HALO + length:, :] = zeros
    row_head = lax.broadcasted_iota(jnp.int32, (B_OUT_WIDTH, NQ), 0) // HEAD_DIM
    ones = jnp.ones((VT_ROWS - HEAD_DIM, BAND), BF16)
    blocks = {}

    def block(key):
        if key not in blocks:
            u, i = key
            q0 = i * NQ
            qt = q_ref[q0:q0 + NQ, cols(u)].astype(F32).T.astype(BF16)
            kband = kp_sc[u, q0:q0 + BAND, :]
            vbt = vp_sc[u, q0:q0 + BAND, :].astype(F32).T.astype(BF16)
            pen = None
            if q0 - HALO < 0 or q0 - HALO + BAND > length:
                key_pos = lax.broadcasted_iota(jnp.int32, (BAND, NQ), 0) + (q0 - HALO)
                pen = jnp.where((key_pos >= 0) & (key_pos < length), 0.0, NEG_INF)
            blocks[key] = (qt, kband, vbt, pen)
        return blocks[key]

    def scores(key, h):
        qt, kband, _, _ = block(key)
        rhs = jnp.where(row_head == h, qt, jnp.zeros((), BF16))
        return jnp.dot(kband, rhs, preferred_element_type=F32)

    heads = range(B_HEADS_PER_GROUP)
    keys = [(u, i) for u in range(nsub) for i in range(length // NQ)]
    sts = [scores(keys[0], h) for h in heads]
    for n, key in enumerate(keys):
        sts_next = [scores(keys[n + 1], h) for h in heads] if n + 1 < len(keys) else None
        _, _, vbt, pen = block(key)
        sts = [sts[h] + bm_ref[h] for h in heads]
        if pen is not None:
            sts = [st + pen for st in sts]
        ms = [jnp.max(st, axis=0, keepdims=True) for st in sts]
        ps = [jnp.exp2(st - m).astype(BF16) for st, m in zip(sts, ms)]
        accs = [jnp.dot(jnp.concatenate([vbt[h * HEAD_DIM:(h + 1) * HEAD_DIM], ones], axis=0), ps[h],
                        preferred_element_type=F32) for h in heads]
        ls = [acc[HEAD_DIM:HEAD_DIM + 1] for acc in accs]
        u, i = key
        rows = slice(i * NQ, (i + 1) * NQ)
        o_ref[rows, cols(u)] = jnp.concatenate([acc[0:HEAD_DIM] / l for acc, l in zip(accs, ls)], axis=0).T
        lse_ref[rows, cols(u)] = jnp.concatenate(
            [jnp.broadcast_to((m + jnp.log2(l)) * LN2, (HEAD_DIM, NQ)) for m, l in zip(ms, ls)], axis=0).T
        del blocks[key]
        sts = sts_next


def _attn_b_call(qg, kg, vg, bm, r):
    b, length, _ = qg.shape
    nsub = min(r, B_SUBSEQ_PER_STEP)
    spec = pl.BlockSpec((None, length, nsub * B_OUT_WIDTH), lambda i, j: (i, 0, j))
    out_sds = jax.ShapeDtypeStruct((b, length, r * B_OUT_WIDTH), F32)
    return pl.pallas_call(
        _attn_b_body,
        grid=(b, r // nsub),
        in_specs=[spec, spec, spec,
                  pl.BlockSpec((B_HEADS_PER_GROUP, BAND, NQ), lambda i, j: (0, 0, 0))],
        out_specs=[spec, spec],
        out_shape=[out_sds, out_sds],
        scratch_shapes=[pltpu.VMEM((nsub, length + 2 * HALO, B_OUT_WIDTH), BF16),
                        pltpu.VMEM((nsub, length + 2 * HALO, B_OUT_WIDTH), BF16)],
        compiler_params=_params(("arbitrary", "arbitrary")),
        name=f"attn_b_r{r}",
    )(qg, kg, vg, bm)


def _t5_bucket_np(rel):
    half = N_BUCKETS // 2
    exact = half // 2
    n = np.abs(rel)
    big = exact + (np.log(np.maximum(n, 1).astype(np.float64) / exact)
                   / math.log(MAX_DISTANCE / exact) * (half - exact)).astype(np.int64)
    big = np.minimum(big, half - 1)
    return np.where(rel > 0, half, 0) + np.where(n < exact, n, big)


def _bias_mask_table(rel_bias, gi, r):
    span = BAND + NQ
    rel = np.arange(span) - (NQ - 1) - HALO
    tab = rel_bias[:, gi * B_HEADS_PER_GROUP:(gi + 1) * B_HEADS_PER_GROUP].astype(F32) * LOG2E
    u = jnp.where(jnp.asarray(np.abs(rel) <= HALO)[:, None],
                  tab[jnp.asarray(_t5_bucket_np(rel * r), jnp.int32)], NEG_INF).T
    shifted = jnp.tile(u, (1, NQ))[:, :NQ * (span - 1)].reshape(-1, NQ, span - 1)
    return jnp.transpose(shifted[:, :, NQ - 1:NQ - 1 + BAND], (0, 2, 1))


def _layer_norm(h, g, b):
    mu = jnp.mean(h, axis=-1, keepdims=True)
    c = h - mu
    var = jnp.mean(c * c, axis=-1, keepdims=True)
    return c * lax.rsqrt(var + LN_EPS) * g + b


def _split_bf16(a):
    hi = a.astype(BF16)
    return hi, (a - hi.astype(F32)).astype(BF16)


def _merge_body(x_ref, ya_ref, o1_ref, o2_ref, o3_ref, l1_ref, l2_ref, l3_ref, gates_ref,
                wa_ref, wbr_ref, wo_ref, g_ref, b_ref, wr2_ref, br_ref, out_ref, bkt_ref, cnt_ref,
                *tok_sc):
    slabs = B_OUT_WIDTH // LANES
    dil = [r for _, r in DILATED_PAIRS]
    b_refs = (o1_ref, o2_ref, o3_ref, l1_ref, l2_ref, l3_ref)
    for ref, sc, r in zip(b_refs, tok_sc, dil + dil):
        if r > 1:
            for j in range(r):
                for c in range(slabs):
                    lanes = slice(j * B_OUT_WIDTH + c * LANES, j * B_OUT_WIDTH + (c + 1) * LANES)
                    sc[c, pl.ds(j, TMG // r, stride=r), :] = ref[:, lanes]

    def token_major(which, rows):
        ref, sc, r = b_refs[which], tok_sc[which], (dil + dil)[which]
        if r == 1:
            return ref[rows, :]
        return jnp.concatenate([sc[c, rows, :] for c in range(slabs)], axis=1)

    subs = [slice(i * TSUB, (i + 1) * TSUB) for i in range(TMG // TSUB)]
    ybs = []
    for rows in subs:
        o1, o2, o3, l1, l2, l3 = (token_major(w, rows) for w in range(6))
        mx = jnp.maximum(jnp.maximum(l1, l2), l3)
        e1, e2, e3 = jnp.exp(l1 - mx), jnp.exp(l2 - mx), jnp.exp(l3 - mx)
        ybs.append(((e1 * o1 + e2 * o2 + e3 * o3) / (e1 + e2 + e3)).astype(BF16))
    ya_ps = [jnp.dot(ya_ref[rows, :], wa_ref[...], preferred_element_type=F32) for rows in subs]
    yb_ps = [jnp.dot(yb, wbr_ref[...], preferred_element_type=F32) for yb in ybs]
    mergeds = [(gates_ref[rows, 0:D_MODEL].astype(F32) * ya_p
                + gates_ref[rows, D_MODEL:2 * D_MODEL].astype(F32) * yb_p).astype(BF16)
               for rows, ya_p, yb_p in zip(subs, ya_ps, yb_ps)]
    hs = [ALPHA * x_ref[rows, :] + jnp.dot(m, wo_ref[...], preferred_element_type=F32)
          for rows, m in zip(subs, mergeds)]
    x1s = [_layer_norm(h, g_ref[...], b_ref[...]) for h in hs]
    for rows, x1 in zip(subs, x1s):
        out_ref[rows, 0:D_MODEL] = x1

    splits = [_split_bf16(x1) for x1 in x1s]
    boths = [jnp.dot(xh, wr2_ref[...], preferred_element_type=F32)
             + jnp.dot(xl, wr2_ref[...], preferred_element_type=F32) for xh, xl in splits]
    logits = jnp.concatenate(
        [(both[:, 0:ROUTE_COLS] + both[:, ROUTE_COLS:] + br_ref[...]).T for both in boths], axis=1)
    col = lambda i: logits[i:i + 1, :]
    lg = [col(i) for i in range(N_GROUPS)]
    gmax = functools.reduce(jnp.maximum, lg)
    top_p = 1.0 / functools.reduce(lambda a, c: a + c, [jnp.exp(v - gmax) for v in lg])
    gsel = jnp.full_like(gmax, N_GROUPS - 1).astype(jnp.int32)
    for i in reversed(range(N_GROUPS - 1)):
        gsel = jnp.where(lg[i] >= gmax, i, gsel)
    sel = []
    for e in range(EXPERTS_PER_GROUP):
        v = col(N_GROUPS + (N_GROUPS - 1) * EXPERTS_PER_GROUP + e)
        for gi in reversed(range(N_GROUPS - 1)):
            v = jnp.where(gsel == gi, col(N_GROUPS + gi * EXPERTS_PER_GROUP + e), v)
        sel.append(v)
    rank = []
    for e in range(EXPERTS_PER_GROUP):
        rk = jnp.zeros_like(gsel)
        for j in range(EXPERTS_PER_GROUP):
            if j != e:
                ahead = (sel[j] > sel[e]) | ((sel[j] == sel[e]) & (j < e))
                rk = rk + ahead.astype(jnp.int32)
        rank.append(rk)
    pick = lambda k: functools.reduce(
        lambda a, c: a + c, [jnp.where(rank[e] == k, sel[e], 0.0) for e in range(EXPERTS_PER_GROUP)])
    t = jnp.exp(pick(1) - pick(0))
    w_first = top_p / (1.0 + t)
    w_second = top_p * t / (1.0 + t)
    chosen = [rank[e] < 2 for e in range(EXPERTS_PER_GROUP)]
    weight = [jnp.where(rank[e] == 0, w_first, w_second) for e in range(EXPERTS_PER_GROUP)]
    pair = jnp.zeros_like(gsel)
    w_lo = jnp.zeros_like(top_p)
    w_hi = jnp.zeros_like(top_p)
    for pi, (a, c) in enumerate(PAIRS):
        hit = chosen[a] & chosen[c]
        pair = jnp.where(hit, pi, pair)
        w_lo = jnp.where(hit, weight[a], w_lo)
        w_hi = jnp.where(hit, weight[c], w_hi)
    bucket = gsel * len(PAIRS) + pair
    bkt_ref[...] = bucket
    hits = lax.broadcasted_iota(jnp.int32, (ROUTE_COLS, TMG), 0) == bucket

    @pl.when(pl.program_id(0) == 0)
    def _():
        cnt_ref[...] = jnp.zeros_like(cnt_ref)

    cnt_ref[...] += jnp.sum(hits.astype(jnp.int32), axis=1, keepdims=True)
    row = lax.broadcasted_iota(jnp.int32, (F32_SUBLANES, TMG), 0)
    head = jnp.where(row == 1, w_lo, jnp.where(row == 2, w_hi, 0.0))
    route_t = jnp.concatenate([head, jnp.zeros((ROUTE_COLS - F32_SUBLANES, TMG), F32)], axis=0)
    out_ref[:, D_MODEL:] = route_t.T


def _merge_call(x, ya, os, ls, gates, wa, wbr, wo, g, b, wr2, br):
    t, d = x.shape
    tok = lambda c: pl.BlockSpec((TMG, c), lambda i: (i, 0))
    const = lambda a: pl.BlockSpec(a.shape, lambda i: (0,) * a.ndim)
    consts = (wa, wbr, wo, g, b, wr2, br)
    dil = [pl.BlockSpec((TMG // r, r * B_OUT_WIDTH), lambda i: (i, 0)) for _, r in DILATED_PAIRS]
    return pl.pallas_call(
        _merge_body,
        grid=(t // TMG,),
        in_specs=[tok(d), tok(A_WIDTH)] + dil + dil + [tok(2 * D_MODEL)]
        + [const(a) for a in consts],
        out_specs=[tok(d + ROUTE_COLS), pl.BlockSpec((1, TMG), lambda i: (0, i)),
                   pl.BlockSpec((ROUTE_COLS, 1), lambda i: (0, 0))],
        out_shape=[jax.ShapeDtypeStruct((t, d + ROUTE_COLS), F32), jax.ShapeDtypeStruct((1, t), jnp.int32),
                   jax.ShapeDtypeStruct((ROUTE_COLS, 1), jnp.int32)],
        scratch_shapes=[pltpu.VMEM((B_OUT_WIDTH // LANES, TMG, LANES), F32)] * (2 * N_DIL_GROUPS),
        compiler_params=_params(("arbitrary",)),
        name="merge_ln1_route",
    )(x, ya, *os, *ls, gates, *consts)


def _sc_permute_rows(src, idx, nrows_out, scatter):
    nrows, d = idx.shape[0], src.shape[1]
    workers = SC_CORES * SC_SUBCORES
    per_worker = nrows // workers
    assert nrows % (workers * SC_ROWS) == 0
    mesh = plsc.VectorSubcoreMesh(core_axis_name="c", subcore_axis_name="s")

    @functools.partial(
        pl.kernel, mesh=mesh, out_type=jax.ShapeDtypeStruct((nrows_out, d), src.dtype),
        scratch_types=[pltpu.VMEM((SC_ROWS,), jnp.int32), pltpu.VMEM((SC_ROWS, d), src.dtype),
                       pltpu.SemaphoreType.DMA])
    def permute(src_hbm, idx_hbm, out_hbm, idx_v, rows_v, sem):
        base = (lax.axis_index("s") * SC_CORES + lax.axis_index("c")) * per_worker

        @pl.loop(0, per_worker, step=SC_ROWS)
        def _(off):
            rows = pl.ds(pl.multiple_of(base + off, SC_ROWS), SC_ROWS)
            pltpu.sync_copy(idx_hbm.at[rows], idx_v)
            if scatter:
                pltpu.sync_copy(src_hbm.at[rows], rows_v)
                pltpu.async_copy(rows_v, out_hbm.at[idx_v], sem).wait()
            else:
                pltpu.async_copy(src_hbm.at[idx_v], rows_v, sem).wait()
                pltpu.sync_copy(rows_v, out_hbm.at[rows])

    return permute(src, idx)


def _moe_tiles_body(nvalid_ref, elo_ref, ehi_ref, x_ref, wil_ref, wih_ref, wol_ref, woh_ref,
                    g_ref, b_ref, o_ref):
    del elo_ref, ehi_ref
    n = nvalid_ref[pl.program_id(0)]

    @pl.when(n == 0)
    def _():
        o_ref[...] = jnp.zeros_like(o_ref)

    @pl.when(n > 0)
    def _():
        xt = x_ref[:, 0:D_MODEL]
        xb = xt.astype(BF16)
        hcats = [jnp.dot(xb, wi_ref[...], preferred_element_type=F32)
                 for wi_ref in (wil_ref, wih_ref)]
        acts = [(h[:, 0:D_EXPERT] * jax.nn.sigmoid(h[:, 0:D_EXPERT]) * h[:, D_EXPERT:]).astype(BF16)
                for h in hcats]
        y_lo, y_hi = (jnp.dot(act, wo_ref[...], preferred_element_type=F32)
                      for act, wo_ref in zip(acts, (wol_ref, woh_ref)))
        moe = (x_ref[:, D_MODEL + 1:D_MODEL + 2] * y_lo + x_ref[:, D_MODEL + 2:D_MODEL + 3] * y_hi)
        o_ref[...] = _layer_norm(ALPHA * xt + moe, g_ref[...], b_ref[...])


def _moe_tiles_call(x_sorted, nvalid, elo, ehi, wei, weo, g, b):
    ntiles = nvalid.shape[0]
    wi_spec = lambda e: pl.BlockSpec((None, D_MODEL, 2 * D_EXPERT), lambda i, nv, lo, hi: (e(lo, hi)[i], 0, 0))
    wo_spec = lambda e: pl.BlockSpec((None, D_EXPERT, D_MODEL), lambda i, nv, lo, hi: (e(lo, hi)[i], 0, 0))
    first = lambda lo, hi: lo
    second = lambda lo, hi: hi
    vec = pl.BlockSpec((1, D_MODEL), lambda i, nv, lo, hi: (0, 0))
    grid_spec = pltpu.PrefetchScalarGridSpec(
        num_scalar_prefetch=3,
        grid=(ntiles,),
        in_specs=[pl.BlockSpec((TMOE, D_MODEL + ROUTE_COLS), lambda i, nv, lo, hi: (i, 0)),
                  wi_spec(first), wi_spec(second), wo_spec(first), wo_spec(second), vec, vec],
        out_specs=pl.BlockSpec((TMOE, D_MODEL), lambda i, nv, lo, hi: (i, 0)),
    )
    return pl.pallas_call(
        _moe_tiles_body,
        grid_spec=grid_spec,
        out_shape=jax.ShapeDtypeStruct((ntiles * TMOE, D_MODEL), F32),
        compiler_params=_params(("arbitrary",)),
        name="moe_ln2",
    )(nvalid, elo, ehi, x_sorted, wei, wei, weo, weo, g, b)


def _route_plan(bucket, counts):
    t = bucket.shape[0]
    ntiles = t // TMOE + N_ROUTE_BUCKETS
    token = jnp.arange(t, dtype=jnp.int32)
    _, order = lax.sort((bucket, token), num_keys=1, is_stable=True)
    row_start = jnp.cumsum(counts) - counts
    tiles_per = (counts + TMOE - 1) // TMOE
    tile_end = jnp.cumsum(tiles_per)
    tile_start = tile_end - tiles_per
    total = tile_end[-1]
    tile = jnp.arange(ntiles, dtype=jnp.int32)
    used = tile < total
    bkt = jnp.sum(jnp.minimum(tile, total - 1)[:, None] >= tile_end[None, :], axis=1, dtype=jnp.int32)
    local = tile - tile_start[bkt]
    nvalid = jnp.where(used, jnp.clip(counts[bkt] - local * TMOE, 0, TMOE), 0).astype(jnp.int32)
    pairs = jnp.asarray(PAIRS, jnp.int32)
    grp = bkt // len(PAIRS)
    elo = grp * EXPERTS_PER_GROUP + pairs[bkt % len(PAIRS), 0]
    ehi = grp * EXPERTS_PER_GROUP + pairs[bkt % len(PAIRS), 1]
    shift = tile_start * TMOE - row_start
    step = shift - jnp.concatenate([jnp.zeros((1,), jnp.int32), shift[:-1]])
    padded_row = token + jnp.sum(jnp.where(token[None, :] >= row_start[:, None], step[:, None], 0), axis=0)
    _, pos = lax.sort((order, padded_row), num_keys=1)
    return nvalid, elo.astype(jnp.int32), ehi.astype(jnp.int32), pos


def _prepare(w_in, b_gate, q_gain, k_gain, rel_bias, w_branch_a, w_branch_b, w_out, ln1_g, ln1_b,
             w_route_group, b_route_group, w_route_expert, b_route_expert, w_expert_in,
             w_expert_out, ln2_g, ln2_b, seq):
    row = lambda v: v.reshape(1, -1).astype(F32)
    w = w_in[0]
    wr = jnp.concatenate([w_route_group[0]]
                         + [w_route_expert[0, gi] for gi in range(N_GROUPS)], axis=1)
    wr = jnp.pad(wr.astype(F32), ((0, 0), (0, ROUTE_COLS - wr.shape[1])))
    br = jnp.concatenate([b_route_group[0], b_route_expert[0].reshape(-1)])
    br = jnp.pad(br.astype(F32), (0, ROUTE_COLS - br.shape[0])).reshape(1, ROUTE_COLS)
    wr2 = jnp.concatenate(_split_bf16(wr), axis=1)
    scale = HEAD_DIM ** -0.5
    wb = w[:, DT_ROWS:]
    wb = jnp.concatenate([wb[:, 0:B_WIDTH] * (scale * LOG2E), wb[:, B_WIDTH:]], axis=1)
    return dict(
        wdt=w[:, 0:DT_ROWS].T.astype(BF16), wb=wb.astype(BF16), bg=row(b_gate[0]),
        rope_q=_rope_tables(seq, q_gain[0], scale * LOG2E), rope_k=_rope_tables(seq, k_gain[0], 1.0),
        score_bound=_score_bound(q_gain[0], k_gain[0]),
        bm=[_bias_mask_table(rel_bias, gi, r) for gi, (_, r) in enumerate(DILATED_PAIRS)],
        wa=w_branch_a[0].astype(BF16), wbr=w_branch_b[0].astype(BF16), wo=w_out[0].astype(BF16),
        ln1=(row(ln1_g[0]), row(ln1_b[0])), wr2=wr2, br=br,
        wei=w_expert_in[0].astype(BF16), weo=w_expert_out[0].astype(BF16),
        ln2=(row(ln2_g[0]), row(ln2_b[0])),
    )


def _trunk(x, p, after):
    b, s, d = x.shape
    qt, k, vt, *qkv_b, gates = _inproj_call(x, p["wdt"], p["wb"], p["bg"], *p["rope_q"], *p["rope_k"],
                                            after)
    ya = _attn_a_call(*p["score_bound"], qt, k, vt)
    os, ls = [], []
    for gi, (_, r) in enumerate(DILATED_PAIRS):
        qg, kg, vg = (qkv_b[i * N_DIL_GROUPS + gi] for i in range(3))
        o, lse = _attn_b_call(qg, kg, vg, p["bm"][gi], r)
        os.append(o.reshape(b * s // r, r * B_OUT_WIDTH))
        ls.append(lse.reshape(b * s // r, r * B_OUT_WIDTH))
    return _merge_call(x.reshape(b * s, d), ya.reshape(b * s, A_WIDTH), os, ls,
                       gates.reshape(b * s, 2 * D_MODEL), p["wa"], p["wbr"], p["wo"], *p["ln1"],
                       p["wr2"], p["br"])


def _sorted_rows(x1ext, plan):
    ntiles = plan[0].shape[0]
    return _sc_permute_rows(x1ext, plan[3], ntiles * TMOE, scatter=True)


def _experts(x_sorted, plan, p, shape):
    nvalid, elo, ehi, pos = plan
    y_sorted = _moe_tiles_call(x_sorted, nvalid, elo, ehi, p["wei"], p["weo"], *p["ln2"])
    return _sc_permute_rows(y_sorted, pos, pos.shape[0], scatter=False).reshape(shape)


def kernel(x_prompt, x_sample, w_in, b_gate, q_gain, k_gain, rel_bias, w_branch_a, w_branch_b, w_out,
           ln1_g, ln1_b, w_route_group, b_route_group, w_route_expert, b_route_expert, w_expert_in,
           w_expert_out, ln2_g, ln2_b):
    assert x_prompt.shape[1] == x_sample.shape[1]
    p = _prepare(w_in, b_gate, q_gain, k_gain, rel_bias, w_branch_a, w_branch_b, w_out, ln1_g, ln1_b,
                 w_route_group, b_route_group, w_route_expert, b_route_expert, w_expert_in,
                 w_expert_out, ln2_g, ln2_b, x_prompt.shape[1])
    plan = lambda bucket, counts: _route_plan(bucket.reshape(-1), counts[:N_ROUTE_BUCKETS, 0])
    x1_sample, *route_sample = _trunk(x_sample, p, p["br"])
    plan_sample = plan(*route_sample)
    sorted_sample = _sorted_rows(x1_sample, plan_sample)
    x1_prompt, *route_prompt = _trunk(x_prompt, p, plan_sample[3])
    plan_prompt = plan(*route_prompt)
    sorted_prompt = _sorted_rows(x1_prompt, plan_prompt)
    y_sample = _experts(sorted_sample, plan_sample, p, x_sample.shape)
    y_prompt = _experts(sorted_prompt, plan_prompt, p, x_prompt.shape)
    return (y_prompt, y_sample)
```

```python
import functools
import math

import jax
import jax.numpy as jnp
import numpy as np
from jax import lax
from jax.experimental import pallas as pl
from jax.experimental.pallas import tpu as pltpu
from jax.experimental.pallas import tpu_sc as plsc

F32 = jnp.float32
BF16 = jnp.bfloat16

D_MODEL = 1024
HEAD_DIM = 64
A_Q_HEADS = 8
A_KV_HEADS = 2
A_GROUP = A_Q_HEADS // A_KV_HEADS
GRID_W = 64
ROPE_AXIS_DIM = HEAD_DIM // 2
ROPE_THETA = 10000.0
DILATED_PAIRS = ((128, 1), (512, 4), (2048, 16))
N_DIL_GROUPS = len(DILATED_PAIRS)
B_HEADS_PER_GROUP = 4
N_BUCKETS = 32
MAX_DISTANCE = 1024
A_WIDTH = A_Q_HEADS * HEAD_DIM
A_KV_WIDTH = A_KV_HEADS * HEAD_DIM
B_WIDTH = B_HEADS_PER_GROUP * N_DIL_GROUPS * HEAD_DIM
B_OUT_WIDTH = B_HEADS_PER_GROUP * HEAD_DIM
DT_ROWS = A_WIDTH + 2 * A_KV_WIDTH
N_GROUPS = 4
EXPERTS_PER_GROUP = 4
N_EXPERTS = N_GROUPS * EXPERTS_PER_GROUP
D_EXPERT = 512
PAIRS = ((0, 1), (0, 2), (0, 3), (1, 2), (1, 3), (2, 3))
N_ROUTE_BUCKETS = N_GROUPS * len(PAIRS)
ALPHA = 2.0 ** 0.25
RMS_EPS = 1e-6
LN_EPS = 1e-5
NEG_INF = -1e30

LANES = 128
TM = 512
TMG = 512
TSUB = 256
NQ = 256
A_HEADS_PER_PHASE = 1
A_TK = 256
A_LOOKAHEAD = 4
BF16_SUBLANES = 16
F32_SUBLANES = 8
VT_ROWS = HEAD_DIM + BF16_SUBLANES
LOG2E = 1.4426950408889634
LN2 = 0.6931471805599453
MAX_SAFE_SHIFT = 40.0
HALO = 64
BAND = NQ + 2 * HALO
B_SUBSEQ_PER_STEP = 8
TMOE = 256
ROUTE_COLS = LANES
SC_CORES = 2
SC_SUBCORES = 16
SC_ROWS = 32
VMEM_LIMIT = 56 * 1024 * 1024


def _params(sem):
    return pltpu.CompilerParams(dimension_semantics=sem, vmem_limit_bytes=VMEM_LIMIT)


def _swap16(x):
    return jnp.concatenate([x[16:32], x[0:16], x[48:64], x[32:48]], axis=0)


def _norm_rope(xh, ta, tb):
    r = lax.rsqrt(jnp.mean(xh * xh, axis=0, keepdims=True) + RMS_EPS)
    return (xh * ta + _swap16(xh) * tb) * r


def _inproj_body(x_ref, wdt_ref, wb_ref, bg_ref, taq_ref, tbq_ref, tak_ref, tbk_ref, after_ref,
                 qt_ref, k_ref, vt_ref, *rest):
    del after_ref
    b_refs, gates_ref, tmp_sc = rest[:3 * N_DIL_GROUPS], rest[3 * N_DIL_GROUPS], rest[-1]
    xb = x_ref[...].astype(BF16)

    def project(lo, hi):
        return jnp.dot(xb, wb_ref[:, lo:hi], preferred_element_type=F32)

    def rope_queries(dt):
        taq, tbq = taq_ref[...], tbq_ref[...]
        for h in range(A_Q_HEADS):
            sl = slice(h * HEAD_DIM, (h + 1) * HEAD_DIM)
            qt_ref[sl, :] = _norm_rope(dt[sl], taq, tbq).astype(BF16)

    def rope_keys_values(dt):
        tak, tbk = tak_ref[...], tbk_ref[...]
        kt = jnp.concatenate(
            [_norm_rope(dt[A_WIDTH + h * HEAD_DIM:A_WIDTH + (h + 1) * HEAD_DIM], tak, tbk)
             for h in range(A_KV_HEADS)], axis=0)
        k_ref[...] = kt.T.astype(BF16)
        for h in range(A_KV_HEADS):
            v0 = A_WIDTH + A_KV_WIDTH + h * HEAD_DIM
            vt_ref[h, 0:HEAD_DIM, :] = dt[v0:v0 + HEAD_DIM].astype(BF16)
            vt_ref[h, HEAD_DIM:, :] = jnp.ones((VT_ROWS - HEAD_DIM, TM), BF16)

    def deinterleave(i, res):
        for c in range(B_WIDTH // LANES):
            tmp_sc[i, c] = res[:, c * LANES:(c + 1) * LANES]
        for gi, (_, r) in enumerate(DILATED_PAIRS):
            ref = b_refs[i * N_DIL_GROUPS + gi]
            for j in range(r):
                for half in range(B_OUT_WIDTH // LANES):
                    c = gi * (B_OUT_WIDTH // LANES) + half
                    lanes = slice(j * B_OUT_WIDTH + half * LANES, j * B_OUT_WIDTH + (half + 1) * LANES)
                    ref[:, lanes] = tmp_sc[i, c, pl.ds(j, TM // r, stride=r), :].astype(BF16)

    def gate(i, pre):
        cols = slice(i * D_MODEL, (i + 1) * D_MODEL)
        gates_ref[:, cols] = jax.nn.sigmoid(pre + bg_ref[:, cols]).astype(BF16)

    g0 = 3 * B_WIDTH
    dt = lax.dot_general(wdt_ref[...], xb, (((1,), (1,)), ((), ())),
                         preferred_element_type=F32)
    res_q = project(0, B_WIDTH)
    rope_queries(dt)
    res_k = project(B_WIDTH, 2 * B_WIDTH)
    rope_keys_values(dt)
    deinterleave(0, res_q)
    res_v = project(2 * B_WIDTH, g0)
    deinterleave(1, res_k)
    pre_a = project(g0, g0 + D_MODEL)
    deinterleave(2, res_v)
    pre_b = project(g0 + D_MODEL, g0 + 2 * D_MODEL)
    gate(0, pre_a)
    gate(1, pre_b)


def _inproj_call(x, wdt, wb, bg, taq, tbq, tak, tbk, after):
    b, s, d = x.shape
    nt = s // TM
    wcols = wb.shape[1]
    tab = pl.BlockSpec((HEAD_DIM, TM), lambda p, i: (0, p))
    const = lambda shape: pl.BlockSpec(shape, lambda p, i: (0,) * len(shape))
    tok = lambda c: pl.BlockSpec((None, TM, c), lambda p, i: (i, p, 0))
    dil = [r for _ in range(3) for _, r in DILATED_PAIRS]
    return pl.pallas_call(
        _inproj_body,
        grid=(nt, b),
        in_specs=[tok(d), const((DT_ROWS, d)), const((d, wcols)), const((1, 2 * D_MODEL)),
                  tab, tab, tab, tab, pl.BlockSpec(memory_space=pl.ANY)],
        out_specs=[
            pl.BlockSpec((None, A_WIDTH, TM), lambda p, i: (i, 0, p)),
            tok(A_KV_WIDTH),
            pl.BlockSpec((None, None, A_KV_HEADS, VT_ROWS, TM), lambda p, i: (i, p, 0, 0, 0)),
        ] + [pl.BlockSpec((None, TM // r, r * B_OUT_WIDTH), lambda p, i: (i, p, 0)) for r in dil]
        + [tok(2 * D_MODEL)],
        out_shape=[
            jax.ShapeDtypeStruct((b, A_WIDTH, s), BF16),
            jax.ShapeDtypeStruct((b, s, A_KV_WIDTH), BF16),
            jax.ShapeDtypeStruct((b, nt, A_KV_HEADS, VT_ROWS, TM), BF16),
        ] + [jax.ShapeDtypeStruct((b, s // r, r * B_OUT_WIDTH), BF16) for r in dil]
        + [jax.ShapeDtypeStruct((b, s, 2 * D_MODEL), BF16)],
        scratch_shapes=[pltpu.VMEM((3, B_WIDTH // LANES, TM, LANES), F32)],
        compiler_params=_params(("arbitrary", "arbitrary")),
        name="inproj",
    )(x, wdt, wb, bg, taq, tbq, tak, tbk, after)


def _rope_tables(s, gain, scale):
    t = np.arange(s)
    inv = ROPE_THETA ** (-np.arange(0, ROPE_AXIS_DIM, 2, dtype=np.float64) / ROPE_AXIS_DIM)
    pos = np.stack([t // GRID_W, t // GRID_W, t % GRID_W, t % GRID_W], axis=0)
    ang = (pos[:, None, :] * inv[None, :, None]).reshape(HEAD_DIM, s)
    sign = np.repeat(np.array([-1.0, 1.0, -1.0, 1.0]), 16)[:, None]
    partner = np.arange(HEAD_DIM) ^ 16
    cos = jnp.asarray(np.cos(ang), F32)
    sin = jnp.asarray(sign * np.sin(ang), F32)
    g = gain.astype(F32) * scale
    return g[:, None] * cos, g[partner][:, None] * sin


def _attn_a_body(fast_ref, kbound_ref, qt_ref, k_ref, vt_ref, o_ref, rhs_sc, acc_sc, ot_sc):
    nchunks = vt_ref.shape[0]
    tk = vt_ref.shape[3]
    heads = range(A_Q_HEADS)
    zeros = jnp.zeros((HEAD_DIM, NQ), BF16)
    for h in heads:
        qh = qt_ref[h * HEAD_DIM:(h + 1) * HEAD_DIM, :]
        rhs_sc[h] = jnp.concatenate([qh, zeros] if h < A_GROUP else [zeros, qh], axis=0)

    @pl.when(fast_ref[0] == 1)
    def _():
        shift = []
        for h in heads:
            qh = qt_ref[h * HEAD_DIM:(h + 1) * HEAD_DIM, :].astype(F32)
            shift.append(jnp.sqrt(jnp.sum(qh * qh, axis=0, keepdims=True)) * kbound_ref[0])

        def scores(c, h):
            return jnp.dot(k_ref[c * A_TK:(c + 1) * A_TK, :], rhs_sc[h], preferred_element_type=F32)

        def values(c, kv):
            per = tk // A_TK
            return vt_ref[c // per, kv][:, (c % per) * A_TK:(c % per + 1) * A_TK]

        groups = [(kv, c, part) for kv in range(A_KV_HEADS) for c in range(nchunks * tk // A_TK)
                  for part in range(A_GROUP // A_HEADS_PER_PHASE)]
        members = lambda kv, part: range(kv * A_GROUP + part * A_HEADS_PER_PHASE,
                                         kv * A_GROUP + (part + 1) * A_HEADS_PER_PHASE)
        issue = lambda group: [scores(group[1], h) for h in members(group[0], group[2])]
        accs = [None] * A_Q_HEADS
        pending = [issue(group) for group in groups[:A_LOOKAHEAD]]
        for n, (kv, c, part) in enumerate(groups):
            if n + A_LOOKAHEAD < len(groups):
                pending.append(issue(groups[n + A_LOOKAHEAD]))
            sts = pending.pop(0)
            ps = [jnp.exp2(st - shift[h]).astype(BF16) for st, h in zip(sts, members(kv, part))]
            pvs = [jnp.dot(values(c, kv), p, preferred_element_type=F32) for p in ps]
            for pv, h in zip(pvs, members(kv, part)):
                accs[h] = pv if accs[h] is None else accs[h] + pv
        for h in heads:
            acc_sc[h] = accs[h]

    @pl.when(fast_ref[0] == 0)
    def _():
        acc_sc[...] = jnp.zeros_like(acc_sc)

        def chunk(c, ms):
            kc = k_ref[pl.ds(pl.multiple_of(c * tk, tk), tk), :]
            new_ms = []
            for h in heads:
                st = jnp.dot(kc, rhs_sc[h], preferred_element_type=F32)
                m_new = jnp.maximum(ms[h], jnp.max(st, axis=0, keepdims=True))
                alpha = jnp.exp2(ms[h] - m_new)
                p = jnp.exp2(st - m_new).astype(BF16)
                acc_sc[h] = alpha * acc_sc[h] + jnp.dot(vt_ref[c, h // A_GROUP], p,
                                                         preferred_element_type=F32)
                new_ms.append(m_new)
            return tuple(new_ms)

        lax.fori_loop(0, nchunks, chunk, tuple(jnp.full((1, NQ), NEG_INF, F32) for _ in heads))

    for h in heads:
        ot_sc[h * HEAD_DIM:(h + 1) * HEAD_DIM, :] = (
            acc_sc[h, 0:HEAD_DIM, :] / acc_sc[h, HEAD_DIM:HEAD_DIM + 1, :])
    o_ref[...] = ot_sc[...].T.astype(BF16)


def _attn_a_call(fast, kbound, qt, k, vt):
    b, _, s = qt.shape
    nchunks, tk = vt.shape[1], vt.shape[4]
    grid_spec = pltpu.PrefetchScalarGridSpec(
        num_scalar_prefetch=2,
        grid=(b, s // NQ),
        in_specs=[
            pl.BlockSpec((None, A_WIDTH, NQ), lambda i, j, *_: (i, 0, j)),
            pl.BlockSpec((None, s, A_KV_WIDTH), lambda i, j, *_: (i, 0, 0)),
            pl.BlockSpec((None, nchunks, A_KV_HEADS, VT_ROWS, tk), lambda i, j, *_: (i, 0, 0, 0, 0)),
        ],
        out_specs=pl.BlockSpec((None, NQ, A_WIDTH), lambda i, j, *_: (i, j, 0)),
        scratch_shapes=[pltpu.VMEM((A_Q_HEADS, A_KV_WIDTH, NQ), BF16),
                        pltpu.VMEM((A_Q_HEADS, VT_ROWS, NQ), F32),
                        pltpu.VMEM((A_WIDTH, NQ), F32)],
    )
    return pl.pallas_call(
        _attn_a_body,
        grid_spec=grid_spec,
        out_shape=jax.ShapeDtypeStruct((b, s, A_WIDTH), BF16),
        compiler_params=_params(("arbitrary", "arbitrary")),
        name="attn_a",
    )(fast, kbound, qt, k, vt)


def _score_bound(q_gain, k_gain):
    kb = math.sqrt(HEAD_DIM) * jnp.max(jnp.abs(k_gain.astype(F32)))
    qb = math.sqrt(HEAD_DIM) * jnp.max(jnp.abs(q_gain.astype(F32))) * (HEAD_DIM ** -0.5) * LOG2E
    fast = (qb * kb <= MAX_SAFE_SHIFT).astype(jnp.int32)
    return fast.reshape(1), kb.reshape(1)


def _attn_b_body(q_ref, k_ref, v_ref, bm_ref, o_ref, lse_ref, kp_sc, vp_sc):
    length = q_ref.shape[0]
    nsub = q_ref.shape[1] // B_OUT_WIDTH
    cols = lambda u: slice(u * B_OUT_WIDTH, (u + 1) * B_OUT_WIDTH)
    zeros = jnp.zeros((HALO, B_OUT_WIDTH), BF16)
    for sc, ref in ((kp_sc, k_ref), (vp_sc, v_ref)):
        for u in range(nsub):
            sc[u, 0:HALO, :] = zeros
            sc[u, HALO:HALO + length, :] = ref[:, cols(u)]
            sc[u, HALO + length:, :] = zeros
    row_head = lax.broadcasted_iota(jnp.int32, (B_OUT_WIDTH, NQ), 0) // HEAD_DIM
    ones = jnp.ones((VT_ROWS - HEAD_DIM, BAND), BF16)
    blocks = {}

    def block(key):
        if key not in blocks:
            u, i = key
            q0 = i * NQ
            qt = q_ref[q0:q0 + NQ, cols(u)].astype(F32).T.astype(BF16)
            kband = kp_sc[u, q0:q0 + BAND, :]
            vbt = vp_sc[u, q0:q0 + BAND, :].astype(F32).T.astype(BF16)
            pen = None
            if q0 - HALO < 0 or q0 - HALO + BAND > length:
                key_pos = lax.broadcasted_iota(jnp.int32, (BAND, NQ), 0) + (q0 - HALO)
                pen = jnp.where((key_pos >= 0) & (key_pos < length), 0.0, NEG_INF)
            blocks[key] = (qt, kband, vbt, pen)
        return blocks[key]

    def scores(key, h):
        qt, kband, _, _ = block(key)
        rhs = jnp.where(row_head == h, qt, jnp.zeros((), BF16))
        return jnp.dot(kband, rhs, preferred_element_type=F32)

    heads = range(B_HEADS_PER_GROUP)
    keys = [(u, i) for u in range(nsub) for i in range(length // NQ)]
    sts = [scores(keys[0], h) for h in heads]
    for n, key in enumerate(keys):
        sts_next = [scores(keys[n + 1], h) for h in heads] if n + 1 < len(keys) else None
        _, _, vbt, pen = block(key)
        sts = [sts[h] + bm_ref[h] for h in heads]
        if pen is not None:
            sts = [st + pen for st in sts]
        ms = [jnp.max(st, axis=0, keepdims=True) for st in sts]
        ps = [jnp.exp2(st - m).astype(BF16) for st, m in zip(sts, ms)]
        accs = [jnp.dot(jnp.concatenate([vbt[h * HEAD_DIM:(h + 1) * HEAD_DIM], ones], axis=0), ps[h],
                        preferred_element_type=F32) for h in heads]
        ls = [acc[HEAD_DIM:HEAD_DIM + 1] for acc in accs]
        u, i = key
        rows = slice(i * NQ, (i + 1) * NQ)
        o_ref[rows, cols(u)] = jnp.concatenate([acc[0:HEAD_DIM] / l for acc, l in zip(accs, ls)], axis=0).T
        lse_ref[rows, cols(u)] = jnp.concatenate(
            [jnp.broadcast_to((m + jnp.log2(l)) * LN2, (HEAD_DIM, NQ)) for m, l in zip(ms, ls)], axis=0).T
        del blocks[key]
        sts = sts_next


def _attn_b_call(qg, kg, vg, bm, r):
    b, length, _ = qg.shape
    nsub = min(r, B_SUBSEQ_PER_STEP)
    spec = pl.BlockSpec((None, length, nsub * B_OUT_WIDTH), lambda i, j: (i, 0, j))
    out_sds = jax.ShapeDtypeStruct((b, length, r * B_OUT_WIDTH), F32)
    return pl.pallas_call(
        _attn_b_body,
        grid=(b, r // nsub),
        in_specs=[spec, spec, spec,
                  pl.BlockSpec((B_HEADS_PER_GROUP, BAND, NQ), lambda i, j: (0, 0, 0))],
        out_specs=[spec, spec],
        out_shape=[out_sds, out_sds],
        scratch_shapes=[pltpu.VMEM((nsub, length + 2 * HALO, B_OUT_WIDTH), BF16),
                        pltpu.VMEM((nsub, length + 2 * HALO, B_OUT_WIDTH), BF16)],
        compiler_params=_params(("arbitrary", "arbitrary")),
        name=f"attn_b_r{r}",
    )(qg, kg, vg, bm)


def _t5_bucket_np(rel):
    half = N_BUCKETS // 2
    exact = half // 2
    n = np.abs(rel)
    big = exact + (np.log(np.maximum(n, 1).astype(np.float64) / exact)
                   / math.log(MAX_DISTANCE / exact) * (half - exact)).astype(np.int64)
    big = np.minimum(big, half - 1)
    return np.where(rel > 0, half, 0) + np.where(n < exact, n, big)


def _bias_mask_table(rel_bias, gi, r):
    span = BAND + NQ
    rel = np.arange(span) - (NQ - 1) - HALO
    tab = rel_bias[:, gi * B_HEADS_PER_GROUP:(gi + 1) * B_HEADS_PER_GROUP].astype(F32) * LOG2E
    u = jnp.where(jnp.asarray(np.abs(rel) <= HALO)[:, None],
                  tab[jnp.asarray(_t5_bucket_np(rel * r), jnp.int32)], NEG_INF).T
    shifted = jnp.tile(u, (1, NQ))[:, :NQ * (span - 1)].reshape(-1, NQ, span - 1)
    return jnp.transpose(shifted[:, :, NQ - 1:NQ - 1 + BAND], (0, 2, 1))


def _layer_norm(h, g, b):
    mu = jnp.mean(h, axis=-1, keepdims=True)
    c = h - mu
    var = jnp.mean(c * c, axis=-1, keepdims=True)
    return c * lax.rsqrt(var + LN_EPS) * g + b


def _split_bf16(a):
    hi = a.astype(BF16)
    return hi, (a - hi.astype(F32)).astype(BF16)


def _merge_body(x_ref, ya_ref, o1_ref, o2_ref, o3_ref, l1_ref, l2_ref, l3_ref, gates_ref,
                wa_ref, wbr_ref, wo_ref, g_ref, b_ref, wr2_ref, br_ref, out_ref, bkt_ref, cnt_ref,
                *tok_sc):
    slabs = B_OUT_WIDTH // LANES
    dil = [r for _, r in DILATED_PAIRS]
    b_refs = (o1_ref, o2_ref, o3_ref, l1_ref, l2_ref, l3_ref)
    for ref, sc, r in zip(b_refs, tok_sc, dil + dil):
        if r > 1:
            for j in range(r):
                for c in range(slabs):
                    lanes = slice(j * B_OUT_WIDTH + c * LANES, j * B_OUT_WIDTH + (c + 1) * LANES)
                    sc[c, pl.ds(j, TMG // r, stride=r), :] = ref[:, lanes]

    def token_major(which, rows):
        ref, sc, r = b_refs[which], tok_sc[which], (dil + dil)[which]
        if r == 1:
            return ref[rows, :]
        return jnp.concatenate([sc[c, rows, :] for c in range(slabs)], axis=1)

    subs = [slice(i * TSUB, (i + 1) * TSUB) for i in range(TMG // TSUB)]
    ybs = []
    for rows in subs:
        o1, o2, o3, l1, l2, l3 = (token_major(w, rows) for w in range(6))
        mx = jnp.maximum(jnp.maximum(l1, l2), l3)
        e1, e2, e3 = jnp.exp(l1 - mx), jnp.exp(l2 - mx), jnp.exp(l3 - mx)
        ybs.append(((e1 * o1 + e2 * o2 + e3 * o3) / (e1 + e2 + e3)).astype(BF16))
    ya_ps = [jnp.dot(ya_ref[rows, :], wa_ref[...], preferred_element_type=F32) for rows in subs]
    yb_ps = [jnp.dot(yb, wbr_ref[...], preferred_element_type=F32) for yb in ybs]
    mergeds = [(gates_ref[rows, 0:D_MODEL].astype(F32) * ya_p
                + gates_ref[rows, D_MODEL:2 * D_MODEL].astype(F32) * yb_p).astype(BF16)
               for rows, ya_p, yb_p in zip(subs, ya_ps, yb_ps)]
    hs = [ALPHA * x_ref[rows, :] + jnp.dot(m, wo_ref[...], preferred_element_type=F32)
          for rows, m in zip(subs, mergeds)]
    x1s = [_layer_norm(h, g_ref[...], b_ref[...]) for h in hs]
    for rows, x1 in zip(subs, x1s):
        out_ref[rows, 0:D_MODEL] = x1

    splits = [_split_bf16(x1) for x1 in x1s]
    boths = [jnp.dot(xh, wr2_ref[...], preferred_element_type=F32)
             + jnp.dot(xl, wr2_ref[...], preferred_element_type=F32) for xh, xl in splits]
    logits = jnp.concatenate(
        [(both[:, 0:ROUTE_COLS] + both[:, ROUTE_COLS:] + br_ref[...]).T for both in boths], axis=1)
    col = lambda i: logits[i:i + 1, :]
    lg = [col(i) for i in range(N_GROUPS)]
    gmax = functools.reduce(jnp.maximum, lg)
    top_p = 1.0 / functools.reduce(lambda a, c: a + c, [jnp.exp(v - gmax) for v in lg])
    gsel = jnp.full_like(gmax, N_GROUPS - 1).astype(jnp.int32)
    for i in reversed(range(N_GROUPS - 1)):
        gsel = jnp.where(lg[i] >= gmax, i, gsel)
    sel = []
    for e in range(EXPERTS_PER_GROUP):
        v = col(N_GROUPS + (N_GROUPS - 1) * EXPERTS_PER_GROUP + e)
        for gi in reversed(range(N_GROUPS - 1)):
            v = jnp.where(gsel == gi, col(N_GROUPS + gi * EXPERTS_PER_GROUP + e), v)
        sel.append(v)
    rank = []
    for e in range(EXPERTS_PER_GROUP):
        rk = jnp.zeros_like(gsel)
        for j in range(EXPERTS_PER_GROUP):
            if j != e:
                ahead = (sel[j] > sel[e]) | ((sel[j] == sel[e]) & (j < e))
                rk = rk + ahead.astype(jnp.int32)
        rank.append(rk)
    pick = lambda k: functools.reduce(
        lambda a, c: a + c, [jnp.where(rank[e] == k, sel[e], 0.0) for e in range(EXPERTS_PER_GROUP)])
    t = jnp.exp(pick(1) - pick(0))
    w_first = top_p / (1.0 + t)
    w_second = top_p * t / (1.0 + t)
    chosen = [rank[e] < 2 for e in range(EXPERTS_PER_GROUP)]
    weight = [jnp.where(rank[e] == 0, w_first, w_second) for e in range(EXPERTS_PER_GROUP)]
    pair = jnp.zeros_like(gsel)
    w_lo = jnp.zeros_like(top_p)
    w_hi = jnp.zeros_like(top_p)
    for pi, (a, c) in enumerate(PAIRS):
        hit = chosen[a] & chosen[c]
        pair = jnp.where(hit, pi, pair)
        w_lo = jnp.where(hit, weight[a], w_lo)
        w_hi = jnp.where(hit, weight[c], w_hi)
    bucket = gsel * len(PAIRS) + pair
    bkt_ref[...] = bucket
    hits = lax.broadcasted_iota(jnp.int32, (ROUTE_COLS, TMG), 0) == bucket

    @pl.when(pl.program_id(0) == 0)
    def _():
        cnt_ref[...] = jnp.zeros_like(cnt_ref)

    cnt_ref[...] += jnp.sum(hits.astype(jnp.int32), axis=1, keepdims=True)
    row = lax.broadcasted_iota(jnp.int32, (F32_SUBLANES, TMG), 0)
    head = jnp.where(row == 1, w_lo, jnp.where(row == 2, w_hi, 0.0))
    route_t = jnp.concatenate([head, jnp.zeros((ROUTE_COLS - F32_SUBLANES, TMG), F32)], axis=0)
    out_ref[:, D_MODEL:] = route_t.T


def _merge_call(x, ya, os, ls, gates, wa, wbr, wo, g, b, wr2, br):
    t, d = x.shape
    tok = lambda c: pl.BlockSpec((TMG, c), lambda i: (i, 0))
    const = lambda a: pl.BlockSpec(a.shape, lambda i: (0,) * a.ndim)
    consts = (wa, wbr, wo, g, b, wr2, br)
    dil = [pl.BlockSpec((TMG // r, r * B_OUT_WIDTH), lambda i: (i, 0)) for _, r in DILATED_PAIRS]
    return pl.pallas_call(
        _merge_body,
        grid=(t // TMG,),
        in_specs=[tok(d), tok(A_WIDTH)] + dil + dil + [tok(2 * D_MODEL)]
        + [const(a) for a in consts],
        out_specs=[tok(d + ROUTE_COLS), pl.BlockSpec((1, TMG), lambda i: (0, i)),
                   pl.BlockSpec((ROUTE_COLS, 1), lambda i: (0, 0))],
        out_shape=[jax.ShapeDtypeStruct((t, d + ROUTE_COLS), F32), jax.ShapeDtypeStruct((1, t), jnp.int32),
                   jax.ShapeDtypeStruct((ROUTE_COLS, 1), jnp.int32)],
        scratch_shapes=[pltpu.VMEM((B_OUT_WIDTH // LANES, TMG, LANES), F32)] * (2 * N_DIL_GROUPS),
        compiler_params=_params(("arbitrary",)),
        name="merge_ln1_route",
    )(x, ya, *os, *ls, gates, *consts)


def _sc_permute_rows(src, idx, nrows_out, scatter):
    nrows, d = idx.shape[0], src.shape[1]
    workers = SC_CORES * SC_SUBCORES
    per_worker = nrows // workers
    assert nrows % (workers * SC_ROWS) == 0
    mesh = plsc.VectorSubcoreMesh(core_axis_name="c", subcore_axis_name="s")

    @functools.partial(
        pl.kernel, mesh=mesh, out_type=jax.ShapeDtypeStruct((nrows_out, d), src.dtype),
        scratch_types=[pltpu.VMEM((SC_ROWS,), jnp.int32), pltpu.VMEM((SC_ROWS, d), src.dtype),
                       pltpu.SemaphoreType.DMA])
    def permute(src_hbm, idx_hbm, out_hbm, idx_v, rows_v, sem):
        base = (lax.axis_index("s") * SC_CORES + lax.axis_index("c")) * per_worker

        @pl.loop(0, per_worker, step=SC_ROWS)
        def _(off):
            rows = pl.ds(pl.multiple_of(base + off, SC_ROWS), SC_ROWS)
            pltpu.sync_copy(idx_hbm.at[rows], idx_v)
            if scatter:
                pltpu.sync_copy(src_hbm.at[rows], rows_v)
                pltpu.async_copy(rows_v, out_hbm.at[idx_v], sem).wait()
            else:
                pltpu.async_copy(src_hbm.at[idx_v], rows_v, sem).wait()
                pltpu.sync_copy(rows_v, out_hbm.at[rows])

    return permute(src, idx)


def _moe_tiles_body(nvalid_ref, elo_ref, ehi_ref, x_ref, wil_ref, wih_ref, wol_ref, woh_ref,
                    g_ref, b_ref, o_ref):
    del elo_ref, ehi_ref
    n = nvalid_ref[pl.program_id(0)]

    @pl.when(n == 0)
    def _():
        o_ref[...] = jnp.zeros_like(o_ref)

    @pl.when(n > 0)
    def _():
        xt = x_ref[:, 0:D_MODEL]
        xb = xt.astype(BF16)
        hcats = [jnp.dot(xb, wi_ref[...], preferred_element_type=F32)
                 for wi_ref in (wil_ref, wih_ref)]
        acts = [(h[:, 0:D_EXPERT] * jax.nn.sigmoid(h[:, 0:D_EXPERT]) * h[:, D_EXPERT:]).astype(BF16)
                for h in hcats]
        y_lo, y_hi = (jnp.dot(act, wo_ref[...], preferred_element_type=F32)
                      for act, wo_ref in zip(acts, (wol_ref, woh_ref)))
        moe = (x_ref[:, D_MODEL + 1:D_MODEL + 2] * y_lo + x_ref[:, D_MODEL + 2:D_MODEL + 3] * y_hi)
        o_ref[...] = _layer_norm(ALPHA * xt + moe, g_ref[...], b_ref[...])


def _moe_tiles_call(x_sorted, nvalid, elo, ehi, wei, weo, g, b):
    ntiles = nvalid.shape[0]
    wi_spec = lambda e: pl.BlockSpec((None, D_MODEL, 2 * D_EXPERT), lambda i, nv, lo, hi: (e(lo, hi)[i], 0, 0))
    wo_spec = lambda e: pl.BlockSpec((None, D_EXPERT, D_MODEL), lambda i, nv, lo, hi: (e(lo, hi)[i], 0, 0))
    first = lambda lo, hi: lo
    second = lambda lo, hi: hi
    vec = pl.BlockSpec((1, D_MODEL), lambda i, nv, lo, hi: (0, 0))
    grid_spec = pltpu.PrefetchScalarGridSpec(
        num_scalar_prefetch=3,
        grid=(ntiles,),
        in_specs=[pl.BlockSpec((TMOE, D_MODEL + ROUTE_COLS), lambda i, nv, lo, hi: (i, 0)),
                  wi_spec(first), wi_spec(second), wo_spec(first), wo_spec(second), vec, vec],
        out_specs=pl.BlockSpec((TMOE, D_MODEL), lambda i, nv, lo, hi: (i, 0)),
    )
    return pl.pallas_call(
        _moe_tiles_body,
        grid_spec=grid_spec,
        out_shape=jax.ShapeDtypeStruct((ntiles * TMOE, D_MODEL), F32),
        compiler_params=_params(("arbitrary",)),
        name="moe_ln2",
    )(nvalid, elo, ehi, x_sorted, wei, wei, weo, weo, g, b)


def _route_plan(bucket, counts):
    t = bucket.shape[0]
    ntiles = t // TMOE + N_ROUTE_BUCKETS
    token = jnp.arange(t, dtype=jnp.int32)
    _, order = lax.sort((bucket, token), num_keys=1, is_stable=True)
    row_start = jnp.cumsum(counts) - counts
    tiles_per = (counts + TMOE - 1) // TMOE
    tile_end = jnp.cumsum(tiles_per)
    tile_start = tile_end - tiles_per
    total = tile_end[-1]
    tile = jnp.arange(ntiles, dtype=jnp.int32)
    used = tile < total
    bkt = jnp.sum(jnp.minimum(tile, total - 1)[:, None] >= tile_end[None, :], axis=1, dtype=jnp.int32)
    local = tile - tile_start[bkt]
    nvalid = jnp.where(used, jnp.clip(counts[bkt] - local * TMOE, 0, TMOE), 0).astype(jnp.int32)
    pairs = jnp.asarray(PAIRS, jnp.int32)
    grp = bkt // len(PAIRS)
    elo = grp * EXPERTS_PER_GROUP + pairs[bkt % len(PAIRS), 0]
    ehi = grp * EXPERTS_PER_GROUP + pairs[bkt % len(PAIRS), 1]
    shift = tile_start * TMOE - row_start
    step = shift - jnp.concatenate([jnp.zeros((1,), jnp.int32), shift[:-1]])
    padded_row = token + jnp.sum(jnp.where(token[None, :] >= row_start[:, None], step[:, None], 0), axis=0)
    _, pos = lax.sort((order, padded_row), num_keys=1)
    return nvalid, elo.astype(jnp.int32), ehi.astype(jnp.int32), pos


def _prepare(w_in, b_gate, q_gain, k_gain, rel_bias, w_branch_a, w_branch_b, w_out, ln1_g, ln1_b,
             w_route_group, b_route_group, w_route_expert, b_route_expert, w_expert_in,
             w_expert_out, ln2_g, ln2_b, seq):
    row = lambda v: v.reshape(1, -1).astype(F32)
    w = w_in[0]
    wr = jnp.concatenate([w_route_group[0]]
                         + [w_route_expert[0, gi] for gi in range(N_GROUPS)], axis=1)
    wr = jnp.pad(wr.astype(F32), ((0, 0), (0, ROUTE_COLS - wr.shape[1])))
    br = jnp.concatenate([b_route_group[0], b_route_expert[0].reshape(-1)])
    br = jnp.pad(br.astype(F32), (0, ROUTE_COLS - br.shape[0])).reshape(1, ROUTE_COLS)
    wr2 = jnp.concatenate(_split_bf16(wr), axis=1)
    scale = HEAD_DIM ** -0.5
    wb = w[:, DT_ROWS:]
    wb = jnp.concatenate([wb[:, 0:B_WIDTH] * (scale * LOG2E), wb[:, B_WIDTH:]], axis=1)
    return dict(
        wdt=w[:, 0:DT_ROWS].T.astype(BF16), wb=wb.astype(BF16), bg=row(b_gate[0]),
        rope_q=_rope_tables(seq, q_gain[0], scale * LOG2E), rope_k=_rope_tables(seq, k_gain[0], 1.0),
        score_bound=_score_bound(q_gain[0], k_gain[0]),
        bm=[_bias_mask_table(rel_bias, gi, r) for gi, (_, r) in enumerate(DILATED_PAIRS)],
        wa=w_branch_a[0].astype(BF16), wbr=w_branch_b[0].astype(BF16), wo=w_out[0].astype(BF16),
        ln1=(row(ln1_g[0]), row(ln1_b[0])), wr2=wr2, br=br,
        wei=w_expert_in[0].astype(BF16), weo=w_expert_out[0].astype(BF16),
        ln2=(row(ln2_g[0]), row(ln2_b[0])),
    )


def _trunk(x, p, after):
    b, s, d = x.shape
    qt, k, vt, *qkv_b, gates = _inproj_call(x, p["wdt"], p["wb"], p["bg"], *p["rope_q"], *p["rope_k"],
                                            after)
    ya = _attn_a_call(*p["score_bound"], qt, k, vt)
    os, ls = [], []
    for gi, (_, r) in enumerate(DILATED_PAIRS):
        qg, kg, vg = (qkv_b[i * N_DIL_GROUPS + gi] for i in range(3))
        o, lse = _attn_b_call(qg, kg, vg, p["bm"][gi], r)
        os.append(o.reshape(b * s // r, r * B_OUT_WIDTH))
        ls.append(lse.reshape(b * s // r, r * B_OUT_WIDTH))
    return _merge_call(x.reshape(b * s, d), ya.reshape(b * s, A_WIDTH), os, ls,
                       gates.reshape(b * s, 2 * D_MODEL), p["wa"], p["wbr"], p["wo"], *p["ln1"],
                       p["wr2"], p["br"])


def _sorted_rows(x1ext, plan):
    ntiles = plan[0].shape[0]
    return _sc_permute_rows(x1ext, plan[3], ntiles * TMOE, scatter=True)


def _experts(x_sorted, plan, p, shape):
    nvalid, elo, ehi, pos = plan
    y_sorted = _moe_tiles_call(x_sorted, nvalid, elo, ehi, p["wei"], p["weo"], *p["ln2"])
    return _sc_permute_rows(y_sorted, pos, pos.shape[0], scatter=False).reshape(shape)


def kernel(x_prompt, x_sample, w_in, b_gate, q_gain, k_gain, rel_bias, w_branch_a, w_branch_b, w_out,
           ln1_g, ln1_b, w_route_group, b_route_group, w_route_expert, b_route_expert, w_expert_in,
           w_expert_out, ln2_g, ln2_b):
    assert x_prompt.shape[1] == x_sample.shape[1]
    p = _prepare(w_in, b_gate, q_gain, k_gain, rel_bias, w_branch_a, w_branch_b, w_out, ln1_g, ln1_b,
                 w_route_group, b_route_group, w_route_expert, b_route_expert, w_expert_in,
                 w_expert_out, ln2_g, ln2_b, x_prompt.shape[1])
    plan = lambda bucket, counts: _route_plan(bucket.reshape(-1), counts[:N_ROUTE_BUCKETS, 0])
    x1_sample, *route_sample = _trunk(x_sample, p, p["br"])
    plan_sample = plan(*route_sample)
    sorted_sample = _sorted_rows(x1_sample, plan_sample)
    x1_prompt, *route_prompt = _trunk(x_prompt, p, plan_sample[3])
    plan_prompt = plan(*route_prompt)
    sorted_prompt = _sorted_rows(x1_prompt, plan_prompt)
    y_sample = _experts(sorted_sample, plan_sample, p, x_sample.shape)
    y_prompt = _experts(sorted_prompt, plan_prompt, p, x_prompt.shape)
    return (y_prompt, y_sample)
```
